```python
import math
import jax, jax.numpy as jnp
from jax import lax
import numpy as np

D_MODEL = 2048
BATCH = 2
SEQ = 4096
DEPTH = 1
DEC_BATCH = 32
DEC_SEQ = 1
PAST_LEN = 16384
PAGE_SIZE = 128

HEAD_DIM = 64
N_Q_HEADS = 16
N_KV_HEADS = 4
Q_PER_KV = N_Q_HEADS // N_KV_HEADS
D_ATTN = N_Q_HEADS * HEAD_DIM
D_KV = N_KV_HEADS * HEAD_DIM
WINDOW = 128
ATTN_BLOCK = 128
ROPE_THETA = 10000.0
D_RNN = 1024
N_RNN_BLOCKS = 16
RNN_BLOCK = D_RNN // N_RNN_BLOCKS
CONV_WIDTH = 4
LRU_C = 8.0
D_FF = 5632
N_MOD = 9
EPS = 1e-6
D_IN = D_ATTN + 2 * D_KV + 2 * D_RNN + 2 * D_MODEL
SPLITS = (D_ATTN, D_ATTN + D_KV, D_ATTN + 2 * D_KV, D_ATTN + 2 * D_KV + D_RNN,
          D_ATTN + 2 * D_KV + 2 * D_RNN, D_ATTN + 2 * D_KV + 2 * D_RNN + D_MODEL)

kernel_name = 'hybrid_swa_sink_rglru_macaron_adaln_step'


def rmsnorm(x, g):
    xf = x.astype(jnp.float32)
    y = xf * lax.rsqrt(jnp.mean(xf * xf, axis=-1, keepdims=True) + EPS)
    return (y * g.astype(jnp.float32)).astype(x.dtype)


def modulate(x, g, shift, scale):
    return rmsnorm(x, g) * (1 + scale) + shift


def swiglu(x, w1, w3, w2):
    return (jax.nn.silu(x @ w1) * (x @ w3)) @ w2


def rope(x, pos):
    half = HEAD_DIM // 2
    inv = ROPE_THETA ** (-jnp.arange(half, dtype=jnp.float32) * 2.0 / HEAD_DIM)
    ang = pos.astype(jnp.float32)[:, None] * inv[None, :]
    cos = jnp.cos(ang)[:, None, :]
    sin = jnp.sin(ang)[:, None, :]
    xf = x.astype(jnp.float32)
    x1, x2 = xf[..., :half], xf[..., half:]
    return jnp.concatenate([x1 * cos - x2 * sin, x2 * cos + x1 * sin], axis=-1).astype(x.dtype)


def sink_softmax(s, sinks):
    sk = sinks.astype(jnp.float32).reshape(N_KV_HEADS, Q_PER_KV, 1, 1)
    m = jnp.maximum(jnp.max(s, axis=-1, keepdims=True), sk)
    p = jnp.exp(s - m)
    return p / (jnp.sum(p, axis=-1, keepdims=True) + jnp.exp(sk - m))


def window_attn_prompt(q, k, v, sinks):
    B, S = q.shape[0], q.shape[1]
    nb = S // ATTN_BLOCK
    qb = q.reshape(B, nb, ATTN_BLOCK, N_KV_HEADS, Q_PER_KV, HEAD_DIM)
    kb = k.reshape(B, nb, ATTN_BLOCK, N_KV_HEADS, HEAD_DIM)
    vb = v.reshape(B, nb, ATTN_BLOCK, N_KV_HEADS, HEAD_DIM)
    pad = ((0, 0), (1, 0), (0, 0), (0, 0), (0, 0))
    kk = jnp.concatenate([jnp.pad(kb[:, :-1], pad), kb], axis=2)
    vv = jnp.concatenate([jnp.pad(vb[:, :-1], pad), vb], axis=2)
    i = jnp.arange(ATTN_BLOCK)[:, None]
    j = jnp.arange(2 * ATTN_BLOCK)[None, :]
    rel = ATTN_BLOCK + i - j
    band = (rel >= 0) & (rel < WINDOW)
    blk = jnp.arange(nb)[:, None, None]
    mask = band[None] & ((blk > 0) | (j[None] >= ATTN_BLOCK))
    s = jnp.einsum('bnqkgd,bnjkd->bnkgqj', qb, kk).astype(jnp.float32) * (HEAD_DIM ** -0.5)
    s = jnp.where(mask[None, :, None, None], s, -jnp.inf)
    p = sink_softmax(s, sinks).astype(v.dtype)
    o = jnp.einsum('bnkgqj,bnjkd->bnqkgd', p, vv)
    return o.reshape(B, S, D_ATTN)


def window_attn_sample(q, k_new, v_new, k_buf, v_buf, sinks):
    DB, T = q.shape[0], q.shape[1]
    kk = jnp.concatenate([k_buf, k_new], axis=1)
    vv = jnp.concatenate([v_buf, v_new], axis=1)
    nbuf = k_buf.shape[1]
    qpos = PAST_LEN + jnp.arange(T)
    kpos = jnp.concatenate([PAST_LEN - nbuf + jnp.arange(nbuf), PAST_LEN + jnp.arange(T)])
    rel = qpos[:, None] - kpos[None, :]
    mask = (rel >= 0) & (rel < WINDOW)
    qg = q.reshape(DB, T, N_KV_HEADS, Q_PER_KV, HEAD_DIM)
    s = jnp.einsum('btkgd,bjkd->bkgtj', qg, kk).astype(jnp.float32) * (HEAD_DIM ** -0.5)
    s = jnp.where(mask[None, None, None], s, -jnp.inf)
    p = sink_softmax(s, sinks).astype(v_new.dtype)
    o = jnp.einsum('bkgtj,bjkd->btkgd', p, vv)
    return o.reshape(DB, T, D_ATTN)


def causal_conv(x, buf, w, b):
    T = x.shape[1]
    xp = jnp.concatenate([buf, x], axis=1)
    y = b
    for t in range(CONV_WIDTH):
        y = y + xp[:, t:t + T] * w[t]
    return y, xp[:, -(CONV_WIDTH - 1):]


def rglru(x, h0, w_rg, b_rg, w_ig, b_ig, lam):
    B, T, _ = x.shape
    xb = x.reshape(B, T, N_RNN_BLOCKS, RNN_BLOCK)
    r = jax.nn.sigmoid(jnp.einsum('bthi,hij->bthj', xb, w_rg).reshape(B, T, D_RNN) + b_rg)
    ig = jax.nn.sigmoid(jnp.einsum('bthi,hij->bthj', xb, w_ig).reshape(B, T, D_RNN) + b_ig)
    log_a = -LRU_C * r.astype(jnp.float32) * jax.nn.softplus(-lam.astype(jnp.float32))
    a = jnp.exp(log_a)
    u = jnp.sqrt(-jnp.expm1(2.0 * log_a)) * (ig * x).astype(jnp.float32)
    u = u.at[:, 0].add(a[:, 0] * h0.astype(jnp.float32))

    def combine(e, l):
        return e[0] * l[0], l[0] * e[1] + l[1]

    _, h = lax.associative_scan(combine, (a, u), axis=1)
    h = h.astype(x.dtype)
    return h, h[:, -1]


def trunk_layer(x, c, pos, conv_buf, h0, k_buf, v_buf, prompt, p):
    mod = jax.nn.silu(c) @ p['w_ada'] + p['b_ada']
    sh1, sc1, ga1, shm, scm, gam, sh2, sc2, ga2 = jnp.split(mod[:, None, :], N_MOD, axis=-1)
    x = x + 0.5 * ga1 * swiglu(modulate(x, p['g_norm_ffn1'], sh1, sc1), p['ffn1_w1'], p['ffn1_w3'], p['ffn1_w2'])
    h = modulate(x, p['g_norm_mix'], shm, scm)
    z = h @ p['w_in']
    q, k, v, rx, rg, g_att, g_rnn = jnp.split(z, SPLITS, axis=-1)
    B, T = x.shape[0], x.shape[1]
    q = rope(rmsnorm(q.reshape(B, T, N_Q_HEADS, HEAD_DIM), p['g_q']), pos)
    k = rope(rmsnorm(k.reshape(B, T, N_KV_HEADS, HEAD_DIM), p['g_k']), pos)
    v = v.reshape(B, T, N_KV_HEADS, HEAD_DIM)
    if prompt:
        o_att = window_attn_prompt(q, k, v, p['sinks'])
        k_state, v_state = k[:, -WINDOW:], v[:, -WINDOW:]
    else:
        o_att = window_attn_sample(q, k, v, k_buf, v_buf, p['sinks'])
        k_state, v_state = k, v
    xc, conv_state = causal_conv(rx, conv_buf, p['conv_w'], p['conv_b'])
    hr, h_last = rglru(xc, h0, p['w_rg'], p['b_rg'], p['w_ig'], p['b_ig'], p['lru_lambda'])
    o_rnn = hr * jax.nn.gelu(rg)
    mix = jax.nn.sigmoid(g_att) * (o_att @ p['w_pa']) + jax.nn.sigmoid(g_rnn) * (o_rnn @ p['w_pr'])
    x = x + gam * (mix @ p['w_out'])
    x = x + 0.5 * ga2 * swiglu(modulate(x, p['g_norm_ffn2'], sh2, sc2), p['ffn2_w1'], p['ffn2_w3'], p['ffn2_w2'])
    return x, k_state, v_state, h_last, conv_state


def setup_inputs(seed: int = 0) -> dict:
    key = jax.random.key(seed)
    ks = jax.random.split(key, 40)
    f32 = jnp.float32

    def nrm(k, shape, scale):
        return jax.random.normal(k, shape, f32) * scale

    a = jax.random.uniform(ks[30], (DEPTH, D_RNN), f32, minval=0.9, maxval=0.999)
    s = a ** 0.125
    lru_lambda = jnp.log(s) - jnp.log1p(-s)
    return {
        'x_prompt': nrm(ks[0], (BATCH, SEQ, D_MODEL), 1.0),
        'x_sample': nrm(ks[1], (DEC_BATCH, DEC_SEQ, D_MODEL), 1.0),
        'c_prompt': nrm(ks[2], (BATCH, D_MODEL), 1.0),
        'c_sample': nrm(ks[3], (DEC_BATCH, D_MODEL), 1.0),
        'cache_k': nrm(ks[4], (DEPTH, DEC_BATCH, WINDOW, N_KV_HEADS, HEAD_DIM), 1.0),
        'cache_v': nrm(ks[5], (DEPTH, DEC_BATCH, WINDOW, N_KV_HEADS, HEAD_DIM), 1.0),
        'state_h': nrm(ks[6], (DEPTH, DEC_BATCH, D_RNN), 0.5),
        'state_conv': nrm(ks[7], (DEPTH, DEC_BATCH, CONV_WIDTH - 1, D_RNN), 1.0),
        'w_ada': nrm(ks[8], (DEPTH, D_MODEL, N_MOD * D_MODEL), D_MODEL ** -0.5),
        'b_ada': nrm(ks[9], (DEPTH, N_MOD * D_MODEL), 0.02),
        'g_norm_ffn1': 1.0 + nrm(ks[10], (DEPTH, D_MODEL), 0.02),
        'g_norm_mix': 1.0 + nrm(ks[11], (DEPTH, D_MODEL), 0.02),
        'g_norm_ffn2': 1.0 + nrm(ks[12], (DEPTH, D_MODEL), 0.02),
        'ffn1_w1': nrm(ks[13], (DEPTH, D_MODEL, D_FF), D_MODEL ** -0.5),
        'ffn1_w3': nrm(ks[14], (DEPTH, D_MODEL, D_FF), D_MODEL ** -0.5),
        'ffn1_w2': nrm(ks[15], (DEPTH, D_FF, D_MODEL), D_FF ** -0.5),
        'ffn2_w1': nrm(ks[16], (DEPTH, D_MODEL, D_FF), D_MODEL ** -0.5),
        'ffn2_w3': nrm(ks[17], (DEPTH, D_MODEL, D_FF), D_MODEL ** -0.5),
        'ffn2_w2': nrm(ks[18], (DEPTH, D_FF, D_MODEL), D_FF ** -0.5),
        'w_in': nrm(ks[19], (DEPTH, D_MODEL, D_IN), D_MODEL ** -0.5),
        'g_q': 1.0 + nrm(ks[20], (DEPTH, HEAD_DIM), 0.02),
        'g_k': 1.0 + nrm(ks[21], (DEPTH, HEAD_DIM), 0.02),
        'sinks': nrm(ks[22], (DEPTH, N_Q_HEADS), 0.5),
        'conv_w': nrm(ks[23], (DEPTH, CONV_WIDTH, D_RNN), CONV_WIDTH ** -0.5),
        'conv_b': nrm(ks[24], (DEPTH, D_RNN), 0.02),
        'w_rg': nrm(ks[25], (DEPTH, N_RNN_BLOCKS, RNN_BLOCK, RNN_BLOCK), RNN_BLOCK ** -0.5),
        'b_rg': nrm(ks[26], (DEPTH, D_RNN), 0.02),
        'w_ig': nrm(ks[27], (DEPTH, N_RNN_BLOCKS, RNN_BLOCK, RNN_BLOCK), RNN_BLOCK ** -0.5),
        'b_ig': nrm(ks[28], (DEPTH, D_RNN), 0.02),
        'lru_lambda': lru_lambda,
        'w_pa': nrm(ks[31], (DEPTH, D_ATTN, D_MODEL), D_ATTN ** -0.5),
        'w_pr': nrm(ks[32], (DEPTH, D_RNN, D_MODEL), D_RNN ** -0.5),
        'w_out': nrm(ks[33], (DEPTH, D_MODEL, D_MODEL), D_MODEL ** -0.5),
    }


def reference(x_prompt, x_sample, c_prompt, c_sample, cache_k, cache_v, state_h, state_conv,
              w_ada, b_ada, g_norm_ffn1, g_norm_mix, g_norm_ffn2,
              ffn1_w1, ffn1_w3, ffn1_w2, ffn2_w1, ffn2_w3, ffn2_w2,
              w_in, g_q, g_k, sinks, conv_w, conv_b, w_rg, b_rg, w_ig, b_ig, lru_lambda,
              w_pa, w_pr, w_out):
    B, S = x_prompt.shape[0], x_prompt.shape[1]
    DB, T = x_sample.shape[0], x_sample.shape[1]
    pos_p = jnp.arange(S)
    pos_s = PAST_LEN + jnp.arange(T)
    conv0 = jnp.zeros((B, CONV_WIDTH - 1, D_RNN), x_prompt.dtype)
    h_zero = jnp.zeros((B, D_RNN), x_prompt.dtype)
    yp, ys = x_prompt, x_sample
    kp_l, vp_l, ks_l, vs_l, hp_l, hs_l, cp_l, cs_l = [], [], [], [], [], [], [], []
    for l in range(DEPTH):
        p = dict(w_ada=w_ada[l], b_ada=b_ada[l], g_norm_ffn1=g_norm_ffn1[l], g_norm_mix=g_norm_mix[l],
                 g_norm_ffn2=g_norm_ffn2[l], ffn1_w1=ffn1_w1[l], ffn1_w3=ffn1_w3[l], ffn1_w2=ffn1_w2[l],
                 ffn2_w1=ffn2_w1[l], ffn2_w3=ffn2_w3[l], ffn2_w2=ffn2_w2[l], w_in=w_in[l],
                 g_q=g_q[l], g_k=g_k[l], sinks=sinks[l], conv_w=conv_w[l], conv_b=conv_b[l],
                 w_rg=w_rg[l], b_rg=b_rg[l], w_ig=w_ig[l], b_ig=b_ig[l], lru_lambda=lru_lambda[l],
                 w_pa=w_pa[l], w_pr=w_pr[l], w_out=w_out[l])
        yp, kp, vp, hp, cp = trunk_layer(yp, c_prompt, pos_p, conv0, h_zero, None, None, True, p)
        ys, ksn, vsn, hs, cs = trunk_layer(ys, c_sample, pos_s, state_conv[l], state_h[l],
                                           cache_k[l], cache_v[l], False, p)
        kp_l.append(kp); vp_l.append(vp); ks_l.append(ksn); vs_l.append(vsn)
        hp_l.append(hp); hs_l.append(hs); cp_l.append(cp); cs_l.append(cs)
    k_prompt = jnp.stack(kp_l)
    v_prompt = jnp.stack(vp_l)
    k_sample = jnp.stack(ks_l)
    v_sample = jnp.stack(vs_l)
    h_prompt = jnp.stack(hp_l)
    h_sample = jnp.stack(hs_l)
    conv_prompt = jnp.stack(cp_l)
    conv_sample = jnp.stack(cs_l)
    return (yp, ys, k_prompt, v_prompt, k_sample, v_sample, h_prompt, h_sample, conv_prompt, conv_sample)
```

```python
import functools

import jax
import jax.numpy as jnp
from jax import lax
from jax.experimental import pallas as pl
from jax.experimental.pallas import tpu as pltpu

F32 = jnp.float32
BF16 = jnp.bfloat16

HEAD_DIM = 64
N_Q_HEADS = 16
N_KV_HEADS = 4
Q_PER_KV = N_Q_HEADS // N_KV_HEADS
D_ATTN = N_Q_HEADS * HEAD_DIM
D_KV = N_KV_HEADS * HEAD_DIM
D_QK = D_ATTN + D_KV
WINDOW = 128
ROPE_THETA = 10000.0
D_RNN = 1024
N_RNN_BLOCKS = 16
RNN_BLOCK = D_RNN // N_RNN_BLOCKS
CONV_WIDTH = 4
LRU_C = 8.0
N_MOD = 9
EPS = 1e-6
PAST_LEN = 16384

LANES = 128
SUBLANES = 8
MXU_DIM = 256
VMEM_LIMIT = 56 * 1024 * 1024

D_MODEL = 2048
Z_RX = 0
Z_RG = Z_RX + D_RNN
Z_GATT = Z_RG + D_RNN
Z_GRNN = Z_GATT + D_MODEL
Z_V = Z_GRNN + D_MODEL


def _params(*sem):
    return pltpu.CompilerParams(dimension_semantics=sem, vmem_limit_bytes=VMEM_LIMIT)


def _silu(x):
    return x * jax.nn.sigmoid(x)


def _modulate(x, g, shift, scale):
    ms = jnp.mean(x * x, axis=-1, keepdims=True)
    return (x * lax.rsqrt(ms + EPS) * g) * (1.0 + scale) + shift


def _ada_kernel(c_ref, w_ref, b_ref, o_ref):
    s = _silu(c_ref[...]).astype(BF16)
    o_ref[...] = jnp.dot(s, w_ref[...].astype(BF16), preferred_element_type=F32) + b_ref[...]


def _ada(c_all, w_ada, b_ada, tn=1024):
    rows, d = c_all.shape
    n = w_ada.shape[1]
    return pl.pallas_call(
        _ada_kernel,
        grid=(n // tn,),
        in_specs=[pl.BlockSpec((rows, d), lambda j: (0, 0)),
                  pl.BlockSpec((d, tn), lambda j: (0, j)),
                  pl.BlockSpec((1, tn), lambda j: (0, j))],
        out_specs=pl.BlockSpec((rows, tn), lambda j: (0, j)),
        out_shape=jax.ShapeDtypeStruct((rows, n), F32),
        compiler_params=_params("arbitrary"),
        name="ada",
    )(c_all, w_ada, b_ada.reshape(1, n))


def _rope_kernel(inv_ref, cs_ref, sn_ref, *, base, blk):
    r = pl.program_id(0)
    pos = (base + r * blk + lax.broadcasted_iota(jnp.int32, (blk, LANES), 0)).astype(F32)
    ang = pos * inv_ref[...]
    lane = lax.broadcasted_iota(jnp.int32, (blk, LANES), 1)
    first_half = (lane & (HEAD_DIM - 1)) < HEAD_DIM // 2
    sn = jnp.sin(ang)
    cs_ref[...] = jnp.cos(ang)
    sn_ref[...] = jnp.where(first_half, -sn, sn)


def _rope_tables(inv_row, base, rows, blk):
    out = jax.ShapeDtypeStruct((rows, LANES), F32)
    return pl.pallas_call(
        functools.partial(_rope_kernel, base=base, blk=blk),
        grid=(rows // blk,),
        in_specs=[pl.BlockSpec((1, LANES), lambda r: (0, 0))],
        out_specs=[pl.BlockSpec((blk, LANES), lambda r: (r, 0))] * 2,
        out_shape=[out, out],
        compiler_params=_params("arbitrary"),
        name="rope_tables",
    )(inv_row)


def _ffn_kernel(x_ref, mod_ref, g_ref, w1_ref, w3_ref, w2_ref, o_ref, h_ref, *, d, k_shift, nf):
    f = pl.program_id(1)

    @pl.when(f == 0)
    def _():
        shift = mod_ref[:, k_shift * d:(k_shift + 1) * d]
        scale = mod_ref[:, (k_shift + 1) * d:(k_shift + 2) * d]
        h_ref[...] = _modulate(x_ref[...], g_ref[...], shift, scale).astype(BF16)
        o_ref[...] = jnp.zeros_like(o_ref)

    h = h_ref[...]
    a = jnp.dot(h, w1_ref[...], preferred_element_type=F32)
    b = jnp.dot(h, w3_ref[...], preferred_element_type=F32)
    gated = (_silu(a) * b).astype(BF16)
    o_ref[...] += jnp.dot(gated, w2_ref[...], preferred_element_type=F32)

    @pl.when(f == nf - 1)
    def _():
        gate = mod_ref[:, (k_shift + 2) * d:(k_shift + 3) * d]
        o_ref[...] = x_ref[...] + 0.5 * gate * o_ref[...]


def _mod_spec(mod, tm, rows_per_mod):
    if mod.ndim == 3:
        return pl.BlockSpec((None, 1, mod.shape[-1]), lambda m, *_: ((m * tm) // rows_per_mod, 0, 0))
    return pl.BlockSpec((tm, mod.shape[-1]), lambda m, *_: (m, 0))


def _ffn(x, mod, rows_per_mod, g, w1, w3, w2, k_shift, tm, tf):
    m_rows, d = x.shape
    dff = w1.shape[1]
    nf = dff // tf
    return pl.pallas_call(
        functools.partial(_ffn_kernel, d=d, k_shift=k_shift, nf=nf),
        grid=(m_rows // tm, nf),
        in_specs=[pl.BlockSpec((tm, d), lambda m, f: (m, 0)),
                  _mod_spec(mod, tm, rows_per_mod),
                  pl.BlockSpec((1, d), lambda m, f: (0, 0)),
                  pl.BlockSpec((d, tf), lambda m, f: (0, f)),
                  pl.BlockSpec((d, tf), lambda m, f: (0, f)),
                  pl.BlockSpec((tf, d), lambda m, f: (f, 0))],
        out_specs=pl.BlockSpec((tm, d), lambda m, f: (m, 0)),
        out_shape=jax.ShapeDtypeStruct((m_rows, d), F32),
        scratch_shapes=[pltpu.VMEM((tm, d), BF16)],
        compiler_params=_params("parallel", "arbitrary"),
        name="ffn",
    )(x, mod, g.reshape(1, d), w1, w3, w2)


def _inproj_kernel(x_ref, mod_ref, g_ref, w_ref, cs_ref, sn_ref, gqk_ref, bd_ref,
                   qk_ref, kf_ref, z_ref, h_ref, *, d, k_shift):
    n = pl.program_id(1)

    @pl.when(n == 0)
    def _():
        shift = mod_ref[:, k_shift * d:(k_shift + 1) * d]
        scale = mod_ref[:, (k_shift + 1) * d:(k_shift + 2) * d]
        h_ref[...] = _modulate(x_ref[...], g_ref[...], shift, scale).astype(BF16)

    acc = jnp.dot(h_ref[...], w_ref[...], preferred_element_type=F32)

    @pl.when(n == 0)
    def _():
        ms = jnp.dot((acc * acc).astype(BF16), bd_ref[...], preferred_element_type=F32)
        y = acc * lax.rsqrt(ms + EPS) * gqk_ref[...]
        cs = cs_ref[...]
        sn = sn_ref[...]
        lane = lax.broadcasted_iota(jnp.int32, cs.shape, 1)
        first_half = (lane & (HEAD_DIM - 1)) < HEAD_DIM // 2
        for c in range(D_QK // LANES):
            yc = y[:, c * LANES:(c + 1) * LANES]
            partner = jnp.where(first_half,
                                pltpu.roll(yc, LANES - HEAD_DIM // 2, 1),
                                pltpu.roll(yc, HEAD_DIM // 2, 1))
            rot = yc * cs + partner * sn
            if c < D_ATTN // LANES:
                qk_ref[:, c * LANES:(c + 1) * LANES] = (rot * (HEAD_DIM ** -0.5)).astype(BF16)
            else:
                qk_ref[:, c * LANES:(c + 1) * LANES] = rot.astype(BF16)
                kf_ref[:, c * LANES - D_ATTN:(c + 1) * LANES - D_ATTN] = rot

    @pl.when(n > 0)
    def _():
        z_ref[...] = acc


def _inproj(x, mod, rows_per_mod, g, w_in, cs, sn, gqk, bd, k_shift, tm, table_blocks):
    m_rows, d = x.shape
    n_all = w_in.shape[1]
    tn = D_QK
    nn = n_all // tn
    return pl.pallas_call(
        functools.partial(_inproj_kernel, d=d, k_shift=k_shift),
        grid=(m_rows // tm, nn),
        in_specs=[pl.BlockSpec((tm, d), lambda m, n: (m, 0)),
                  _mod_spec(mod, tm, rows_per_mod),
                  pl.BlockSpec((1, d), lambda m, n: (0, 0)),
                  pl.BlockSpec((d, tn), lambda m, n: (0, n)),
                  pl.BlockSpec((tm, LANES), lambda m, n: (m % table_blocks, 0)),
                  pl.BlockSpec((tm, LANES), lambda m, n: (m % table_blocks, 0)),
                  pl.BlockSpec((1, tn), lambda m, n: (0, 0)),
                  pl.BlockSpec((tn, tn), lambda m, n: (0, 0))],
        out_specs=[pl.BlockSpec((tm, tn), lambda m, n: (m, 0)),
                   pl.BlockSpec((tm, D_KV), lambda m, n: (m, 0)),
                   pl.BlockSpec((tm, tn), lambda m, n: (m, jnp.maximum(n - 1, 0)))],
        out_shape=[jax.ShapeDtypeStruct((m_rows, tn), BF16),
                   jax.ShapeDtypeStruct((m_rows, D_KV), F32),
                   jax.ShapeDtypeStruct((m_rows, n_all - tn), F32)],
        scratch_shapes=[pltpu.VMEM((tm, d), BF16)],
        compiler_params=_params("parallel", "arbitrary"),
        name="inproj",
    )(x, mod, g.reshape(1, d), w_in, cs, sn, gqk, bd)


def _attn_kernel(sink_ref, q_ref, kp_ref, kc_ref, vp_ref, vc_ref, o_ref, *, tq, first_block_has_no_past):
    qi = lax.broadcasted_iota(jnp.int32, (tq, WINDOW), 0)
    kj = lax.broadcasted_iota(jnp.int32, (tq, WINDOW), 1)
    if first_block_has_no_past:
        qi = qi + jnp.where(pl.program_id(1) > 0, 0, WINDOW)
    mask_p = kj > qi
    ci = lax.broadcasted_iota(jnp.int32, (tq, tq), 0)
    cj = lax.broadcasted_iota(jnp.int32, (tq, tq), 1)
    mask_c = cj <= ci
    nt = (((1,), (1,)), ((), ()))
    q = q_ref[...]
    kp = kp_ref[...]
    kc = kc_ref[...]
    vp = vp_ref[...].astype(BF16)
    vc = vc_ref[...].astype(BF16)
    for g in range(N_KV_HEADS):
        ks = slice(g * HEAD_DIM, (g + 1) * HEAD_DIM)
        for i in range(Q_PER_KV):
            h = g * Q_PER_KV + i
            qh = q[:, h * HEAD_DIM:(h + 1) * HEAD_DIM]
            sp = lax.dot_general(qh, kp[:, ks], nt, preferred_element_type=F32)
            sc = lax.dot_general(qh, kc[:, ks], nt, preferred_element_type=F32)
            sp = jnp.where(mask_p, sp, -jnp.inf)
            sc = jnp.where(mask_c, sc, -jnp.inf)
            sk = sink_ref[h]
            m = jnp.maximum(jnp.maximum(jnp.max(sp, axis=-1, keepdims=True),
                                        jnp.max(sc, axis=-1, keepdims=True)), sk)
            pp = jnp.exp(sp - m)
            pc = jnp.exp(sc - m)
            denom = (jnp.sum(pp, axis=-1, keepdims=True) + jnp.sum(pc, axis=-1, keepdims=True)
                     + jnp.exp(sk - m))
            o = (jnp.dot(pp.astype(BF16), vp[:, ks], preferred_element_type=F32)
                 + jnp.dot(pc.astype(BF16), vc[:, ks], preferred_element_type=F32))
            o_ref[:, h * HEAD_DIM:(h + 1) * HEAD_DIM] = (o / denom).astype(BF16)


def _attn(sinks, q_arr, kp_arr, kc_arr, vp_arr, vc_arr, nbatch, nblk, tq, specs, first_block_has_no_past):
    q_spec, kp_spec, kc_spec, vp_spec, vc_spec = specs
    return pl.pallas_call(
        functools.partial(_attn_kernel, tq=tq, first_block_has_no_past=first_block_has_no_past),
        grid=(nbatch, nblk),
        in_specs=[pl.BlockSpec(memory_space=pltpu.SMEM), q_spec, kp_spec, kc_spec, vp_spec, vc_spec],
        out_specs=pl.BlockSpec((tq, D_ATTN), lambda b, n: (b * nblk + n, 0)),
        out_shape=jax.ShapeDtypeStruct((nbatch * nblk * tq, D_ATTN), BF16),
        compiler_params=_params("parallel", "arbitrary"),
        name="attn",
    )(sinks, q_arr, kp_arr, kc_arr, vp_arr, vc_arr)


def _softplus(x):
    return jnp.maximum(x, 0.0) + jnp.log1p(jnp.exp(-jnp.abs(x)))


def _lru_coeffs(xc, wrg_ref, wig_ref, brg, big, lam):
    xb = xc.astype(BF16)
    ngroups = D_RNN // MXU_DIM
    rs, igs = [], []
    for c in range(ngroups):
        xg = xb[:, c * MXU_DIM:(c + 1) * MXU_DIM]
        rs.append(jnp.dot(xg, wrg_ref[c], preferred_element_type=F32))
        igs.append(jnp.dot(xg, wig_ref[c], preferred_element_type=F32))
    r = jax.nn.sigmoid(jnp.concatenate(rs, axis=1) + brg)
    ig = jax.nn.sigmoid(jnp.concatenate(igs, axis=1) + big)
    log_a = -LRU_C * r * _softplus(-lam)
    a = jnp.exp(log_a)
    one_minus_a2 = -jnp.tanh(log_a) * (1.0 + a * a)
    u = jnp.sqrt(one_minus_a2) * (ig * xc)
    return a, u


def _rnn_prompt_kernel(rx_ref, rg_ref, cw_ref, cb_ref, wrg_ref, wig_ref, brg_ref, big_ref, lam_ref,
                       o_ref, hlast_ref, conv_ref, xs_ref, a_ref, h_ref, carry_ref, *, tc):
    t = pl.program_id(1)
    pad = SUBLANES

    @pl.when(t == 0)
    def _():
        xs_ref[0:pad, :] = jnp.zeros((pad, D_RNN), F32)
        carry_ref[...] = jnp.zeros_like(carry_ref)

    x = rx_ref[...]
    xs_ref[pad:pad + tc, :] = x
    xc = cb_ref[...] + x * cw_ref[CONV_WIDTH - 1:CONV_WIDTH, :]
    for k in range(1, CONV_WIDTH):
        xc = xc + xs_ref[pad - k:pad - k + tc, :] * cw_ref[CONV_WIDTH - 1 - k:CONV_WIDTH - k, :]
    tail = xs_ref[tc:tc + pad, :]
    xs_ref[0:pad, :] = tail
    conv_ref[...] = tail[pad - (CONV_WIDTH - 1):, :]

    a, u = _lru_coeffs(xc, wrg_ref, wig_ref, brg_ref[...], big_ref[...], lam_ref[...])
    a_ref[...] = a
    h_ref[...] = u

    row = lax.broadcasted_iota(jnp.int32, (SUBLANES, D_RNN), 0)

    def body(r, carry):
        off = pl.multiple_of(r * SUBLANES, SUBLANES)
        av = a_ref[pl.ds(off, SUBLANES), :]
        hv = h_ref[pl.ds(off, SUBLANES), :]
        for sft in (1, 2, 4):
            keep = row >= sft
            a_sh = jnp.where(keep, pltpu.roll(av, sft, 0), 1.0)
            h_sh = jnp.where(keep, pltpu.roll(hv, sft, 0), 0.0)
            hv = av * h_sh + hv
            av = av * a_sh
        hv = hv + av * carry
        h_ref[pl.ds(off, SUBLANES), :] = hv
        return jnp.broadcast_to(hv[SUBLANES - 1:SUBLANES, :], (SUBLANES, D_RNN))

    carry = lax.fori_loop(0, tc // SUBLANES, body, carry_ref[...])
    carry_ref[...] = carry
    hlast_ref[...] = carry[0:1, :]
    o_ref[...] = (h_ref[...] * jax.nn.gelu(rg_ref[...])).astype(BF16)


def _rnn_prompt(z, nbatch, seq, conv_w, conv_b, wrg, wig, b_rg, b_ig, lam, tc):
    nchunk = seq // tc
    rx_blk = Z_RX // D_RNN
    rg_blk = Z_RG // D_RNN
    vec = pl.BlockSpec((1, D_RNN), lambda b, t: (0, 0))
    wspec = pl.BlockSpec(wrg.shape, lambda b, t: (0, 0, 0))
    return pl.pallas_call(
        functools.partial(_rnn_prompt_kernel, tc=tc),
        grid=(nbatch, nchunk),
        in_specs=[pl.BlockSpec((tc, D_RNN), lambda b, t: (b * nchunk + t, rx_blk)),
                  pl.BlockSpec((tc, D_RNN), lambda b, t: (b * nchunk + t, rg_blk)),
                  pl.BlockSpec((CONV_WIDTH, D_RNN), lambda b, t: (0, 0)),
                  vec, wspec, wspec, vec, vec, vec],
        out_specs=[pl.BlockSpec((tc, D_RNN), lambda b, t: (b * nchunk + t, 0)),
                   pl.BlockSpec((None, 1, D_RNN), lambda b, t: (b, 0, 0)),
                   pl.BlockSpec((None, CONV_WIDTH - 1, D_RNN), lambda b, t: (b, 0, 0))],
        out_shape=[jax.ShapeDtypeStruct((nbatch * seq, D_RNN), BF16),
                   jax.ShapeDtypeStruct((nbatch, 1, D_RNN), F32),
                   jax.ShapeDtypeStruct((nbatch, CONV_WIDTH - 1, D_RNN), F32)],
        scratch_shapes=[pltpu.VMEM((tc + SUBLANES, D_RNN), F32),
                        pltpu.VMEM((tc, D_RNN), F32),
                        pltpu.VMEM((tc, D_RNN), F32),
                        pltpu.VMEM((SUBLANES, D_RNN), F32)],
        compiler_params=_params("parallel", "arbitrary"),
        name="rnn_prompt",
    )(z, z, conv_w, conv_b.reshape(1, D_RNN), wrg, wig, b_rg.reshape(1, D_RNN),
      b_ig.reshape(1, D_RNN), lam.reshape(1, D_RNN))


def _rnn_step_kernel(rx_ref, rg_ref, c0_ref, c1_ref, c2_ref, h0_ref, cw_ref, cb_ref, wrg_ref, wig_ref,
                     brg_ref, big_ref, lam_ref, o_ref, h_ref):
    x = rx_ref[...]
    xc = (cb_ref[...] + c0_ref[...] * cw_ref[0:1, :] + c1_ref[...] * cw_ref[1:2, :]
          + c2_ref[...] * cw_ref[2:3, :] + x * cw_ref[3:4, :])
    a, u = _lru_coeffs(xc, wrg_ref, wig_ref, brg_ref[...], big_ref[...], lam_ref[...])
    h = a * h0_ref[...] + u
    h_ref[...] = h
    o_ref[...] = (h * jax.nn.gelu(rg_ref[...])).astype(BF16)


def _rnn_step(z, conv_state, h0, conv_w, conv_b, wrg, wig, b_rg, b_ig, lam):
    rows = z.shape[0]
    full = lambda shape: pl.BlockSpec(shape, lambda i: (0,) * len(shape))
    act = full((rows, D_RNN))
    vec = full((1, D_RNN))
    return pl.pallas_call(
        _rnn_step_kernel,
        grid=(1,),
        in_specs=[pl.BlockSpec((rows, D_RNN), lambda i: (0, Z_RX // D_RNN)),
                  pl.BlockSpec((rows, D_RNN), lambda i: (0, Z_RG // D_RNN)),
                  act, act, act, act, full((CONV_WIDTH, D_RNN)), vec,
                  full(wrg.shape), full(wig.shape), vec, vec, vec],
        out_specs=[act, act],
        out_shape=[jax.ShapeDtypeStruct((rows, D_RNN), BF16),
                   jax.ShapeDtypeStruct((rows, D_RNN), F32)],
        compiler_params=_params("arbitrary"),
        name="rnn_step",
    )(z, z, conv_state[:, 0], conv_state[:, 1], conv_state[:, 2], h0, conv_w,
      conv_b.reshape(1, D_RNN), wrg, wig, b_rg.reshape(1, D_RNN), b_ig.reshape(1, D_RNN),
      lam.reshape(1, D_RNN))


def _mix_kernel(oa_ref, or_ref, ga_ref, gr_ref, x_ref, mod_ref, wpa_ref, wpr_ref, wout_ref, o_ref, *, d, k_gate):
    pa = jnp.dot(oa_ref[...], wpa_ref[...], preferred_element_type=F32)
    pr = jnp.dot(or_ref[...], wpr_ref[...], preferred_element_type=F32)
    mix = jax.nn.sigmoid(ga_ref[...]) * pa + jax.nn.sigmoid(gr_ref[...]) * pr
    gate = mod_ref[:, k_gate * d:(k_gate + 1) * d]
    o_ref[...] = x_ref[...] + gate * jnp.dot(mix.astype(BF16), wout_ref[...], preferred_element_type=F32)


def _mix(o_att, o_rnn, z, x, mod, rows_per_mod, w_pa, w_pr, w_out, k_gate, tm):
    m_rows, d = x.shape
    const = lambda shape: pl.BlockSpec(shape, lambda m: (0, 0))
    return pl.pallas_call(
        functools.partial(_mix_kernel, d=d, k_gate=k_gate),
        grid=(m_rows // tm,),
        in_specs=[pl.BlockSpec((tm, D_ATTN), lambda m: (m, 0)),
                  pl.BlockSpec((tm, D_RNN), lambda m: (m, 0)),
                  pl.BlockSpec((tm, d), lambda m: (m, Z_GATT // d)),
                  pl.BlockSpec((tm, d), lambda m: (m, Z_GRNN // d)),
                  pl.BlockSpec((tm, d), lambda m: (m, 0)),
                  _mod_spec(mod, tm, rows_per_mod),
                  const(w_pa.shape), const(w_pr.shape), const(w_out.shape)],
        out_specs=pl.BlockSpec((tm, d), lambda m: (m, 0)),
        out_shape=jax.ShapeDtypeStruct((m_rows, d), F32),
        compiler_params=_params("parallel"),
        name="mix",
    )(o_att, o_rnn, z, z, x, mod, w_pa, w_pr, w_out)


def _block_diag(w, group):
    n, r, _ = w.shape
    eye = jnp.eye(group, dtype=w.dtype)
    wg = w.reshape(n // group, group, r, r)
    return jnp.einsum("ngij,gh->ngihj", wg, eye).reshape(n // group, group * r, group * r)


def kernel(x_prompt, x_sample, c_prompt, c_sample, cache_k, cache_v, state_h, state_conv, w_ada, b_ada,
           g_norm_ffn1, g_norm_mix, g_norm_ffn2, ffn1_w1, ffn1_w3, ffn1_w2, ffn2_w1, ffn2_w3, ffn2_w2,
           w_in, g_q, g_k, sinks, conv_w, conv_b, w_rg, b_rg, w_ig, b_ig, lru_lambda, w_pa, w_pr, w_out):
    nb, seq, d = x_prompt.shape
    ns = x_sample.shape[0]
    assert d == D_MODEL and w_ada.shape[0] == 1 and x_sample.shape[1] == 1 and cache_k.shape[2] == WINDOW
    assert seq % WINDOW == 0

    bf = lambda w: w[0].astype(BF16)
    wi = w_in[0]
    w_in_p = jnp.concatenate([wi[:, :D_QK], wi[:, D_QK + D_KV:], wi[:, D_QK:D_QK + D_KV]], axis=1).astype(BF16)
    group = MXU_DIM // RNN_BLOCK
    wrg = _block_diag(w_rg[0], group).astype(BF16)
    wig = _block_diag(w_ig[0], group).astype(BF16)
    gqk = jnp.concatenate([jnp.tile(g_q[0], N_Q_HEADS), jnp.tile(g_k[0], N_KV_HEADS)]).reshape(1, D_QK)
    head_mean = jnp.kron(jnp.eye(D_QK // HEAD_DIM, dtype=F32),
                         jnp.full((HEAD_DIM, HEAD_DIM), 1.0 / HEAD_DIM, F32)).astype(BF16)
    inv = ROPE_THETA ** (-jnp.arange(HEAD_DIM // 2, dtype=F32) * 2.0 / HEAD_DIM)
    inv_row = jnp.tile(inv, LANES // (HEAD_DIM // 2)).reshape(1, LANES)

    pad_rows = (-(nb + ns)) % SUBLANES
    c_all = jnp.concatenate([c_prompt, c_sample, jnp.zeros((pad_rows, d), F32)], axis=0)
    mod_all = _ada(c_all, w_ada[0], b_ada[0])
    mod_p = mod_all[:nb].reshape(nb, 1, N_MOD * d)
    mod_s = mod_all[nb:nb + ns]

    tm_in = min(512, seq)
    cs_p, sn_p = _rope_tables(inv_row, 0, seq, tm_in)
    cs_1, sn_1 = _rope_tables(inv_row, PAST_LEN, SUBLANES, SUBLANES)
    cs_s = jnp.broadcast_to(cs_1[0:1], (ns, LANES))
    sn_s = jnp.broadcast_to(sn_1[0:1], (ns, LANES))

    sink_vec = sinks[0]
    lam = lru_lambda[0]

    m_rows = nb * seq
    xp = x_prompt.reshape(m_rows, d)
    tm_ffn = min(512, seq)
    tf = 512 if ffn1_w1.shape[2] % 512 == 0 else ffn1_w1.shape[2]
    w11, w13, w12 = bf(ffn1_w1), bf(ffn1_w3), bf(ffn1_w2)
    w21, w23, w22 = bf(ffn2_w1), bf(ffn2_w3), bf(ffn2_w2)
    wpa, wpr, wout = bf(w_pa), bf(w_pr), bf(w_out)

    x1 = _ffn(xp, mod_p, seq, g_norm_ffn1[0], w11, w13, w12, 0, tm_ffn, tf)
    qk, kf, z = _inproj(x1, mod_p, seq, g_norm_mix[0], w_in_p, cs_p, sn_p, gqk, head_mean, 3,
                        tm_in, seq // tm_in)
    nblk = seq // WINDOW
    kcol = D_ATTN // D_KV
    vcol = Z_V // D_KV
    prev = lambda b, n: b * nblk + jnp.maximum(n - 1, 0)
    specs = (pl.BlockSpec((WINDOW, D_ATTN), lambda b, n: (b * nblk + n, 0)),
             pl.BlockSpec((WINDOW, D_KV), lambda b, n: (prev(b, n), kcol)),
             pl.BlockSpec((WINDOW, D_KV), lambda b, n: (b * nblk + n, kcol)),
             pl.BlockSpec((WINDOW, D_KV), lambda b, n: (prev(b, n), vcol)),
             pl.BlockSpec((WINDOW, D_KV), lambda b, n: (b * nblk + n, vcol)))
    o_att = _attn(sink_vec, qk, qk, qk, z, z, nb, nblk, WINDOW, specs, True)
    o_rnn, h_p, conv_p = _rnn_prompt(z, nb, seq, conv_w[0], conv_b[0], wrg, wig, b_rg[0], b_ig[0], lam,
                                     min(512, seq))
    x2 = _mix(o_att, o_rnn, z, x1, mod_p, seq, wpa, wpr, wout, 5, min(256, seq))
    y_p = _ffn(x2, mod_p, seq, g_norm_ffn2[0], w21, w23, w22, 6, tm_ffn, tf)

    k_prompt = kf.reshape(nb, seq, N_KV_HEADS, HEAD_DIM)[:, seq - WINDOW:][None]
    v_prompt = z[:, Z_V:].reshape(nb, seq, N_KV_HEADS, HEAD_DIM)[:, seq - WINDOW:][None]

    xs = x_sample.reshape(ns, d)
    x1s = _ffn(xs, mod_s, 1, g_norm_ffn1[0], w11, w13, w12, 0, ns, tf)
    qks, kfs, zs = _inproj(x1s, mod_s, 1, g_norm_mix[0], w_in_p, cs_s, sn_s, gqk, head_mean, 3, ns, 1)
    tq = 2 * SUBLANES
    padq = lambda a: jnp.pad(a[:, None, :], ((0, 0), (0, tq - 1), (0, 0))).reshape(ns * tq, a.shape[-1])
    qks_p = padq(qks)
    vs_new = zs[:, Z_V:]
    vs_p = padq(vs_new)
    ck = cache_k[0].reshape(ns * WINDOW, D_KV).astype(BF16)
    cv = cache_v[0].reshape(ns * WINDOW, D_KV)
    specs_s = (pl.BlockSpec((tq, D_ATTN), lambda b, n: (b, 0)),
               pl.BlockSpec((WINDOW, D_KV), lambda b, n: (b, 0)),
               pl.BlockSpec((tq, D_KV), lambda b, n: (b, kcol)),
               pl.BlockSpec((WINDOW, D_KV), lambda b, n: (b, 0)),
               pl.BlockSpec((tq, D_KV), lambda b, n: (b, 0)))
    o_att_s = _attn(sink_vec, qks_p, ck, qks_p, cv, vs_p, ns, 1, tq, specs_s, False)
    o_att_s = o_att_s.reshape(ns, tq, D_ATTN)[:, 0]
    conv_s_in = state_conv[0]
    o_rnn_s, h_s = _rnn_step(zs, conv_s_in, state_h[0], conv_w[0], conv_b[0], wrg, wig, b_rg[0], b_ig[0], lam)
    x2s = _mix(o_att_s, o_rnn_s, zs, x1s, mod_s, 1, wpa, wpr, wout, 5, ns)
    y_s = _ffn(x2s, mod_s, 1, g_norm_ffn2[0], w21, w23, w22, 6, ns, tf)

    k_sample = kfs.reshape(1, ns, 1, N_KV_HEADS, HEAD_DIM)
    v_sample = vs_new.reshape(1, ns, 1, N_KV_HEADS, HEAD_DIM)
    conv_sample = jnp.concatenate([conv_s_in[:, 1:], zs[:, None, Z_RX:Z_RX + D_RNN]], axis=1)[None]

    return (y_p.reshape(nb, seq, d), y_s.reshape(ns, 1, d), k_prompt, v_prompt, k_sample, v_sample,
            h_p.reshape(1, nb, D_RNN), h_s[None], conv_p[None], conv_sample)
```

```python
import functools

import jax
import jax.numpy as jnp
from jax import lax
from jax.experimental import pallas as pl
from jax.experimental.pallas import tpu as pltpu

F32 = jnp.float32
BF16 = jnp.bfloat16

HEAD_DIM = 64
N_Q_HEADS = 16
N_KV_HEADS = 4
Q_PER_KV = N_Q_HEADS // N_KV_HEADS
D_ATTN = N_Q_HEADS * HEAD_DIM
D_KV = N_KV_HEADS * HEAD_DIM
D_QK = D_ATTN + D_KV
WINDOW = 128
ROPE_THETA = 10000.0
D_RNN = 1024
N_RNN_BLOCKS = 16
RNN_BLOCK = D_RNN // N_RNN_BLOCKS
CONV_WIDTH = 4
LRU_C = 8.0
N_MOD = 9
EPS = 1e-6
PAST_LEN = 16384

LANES = 128
SUBLANES = 8
MXU_DIM = 256
VMEM_LIMIT = 56 * 1024 * 1024

D_MODEL = 2048
Z_RX = 0
Z_RG = Z_RX + D_RNN
Z_GATT = Z_RG + D_RNN
Z_GRNN = Z_GATT + D_MODEL
Z_V = Z_GRNN + D_MODEL


def _params(*sem):
    return pltpu.CompilerParams(dimension_semantics=sem, vmem_limit_bytes=VMEM_LIMIT)


def _silu(x):
    return x * jax.nn.sigmoid(x)


def _modulate(x, g, shift, scale):
    ms = jnp.mean(x * x, axis=-1, keepdims=True)
    return (x * lax.rsqrt(ms + EPS) * g) * (1.0 + scale) + shift


def _ada_kernel(c_ref, w_ref, b_ref, o_ref):
    s = _silu(c_ref[...]).astype(BF16)
    o_ref[...] = jnp.dot(s, w_ref[...].astype(BF16), preferred_element_type=F32) + b_ref[...]


def _ada(c_all, w_ada, b_ada, tn=1024):
    rows, d = c_all.shape
    n = w_ada.shape[1]
    return pl.pallas_call(
        _ada_kernel,
        grid=(n // tn,),
        in_specs=[pl.BlockSpec((rows, d), lambda j: (0, 0)),
                  pl.BlockSpec((d, tn), lambda j: (0, j)),
                  pl.BlockSpec((1, tn), lambda j: (0, j))],
        out_specs=pl.BlockSpec((rows, tn), lambda j: (0, j)),
        out_shape=jax.ShapeDtypeStruct((rows, n), F32),
        compiler_params=_params("arbitrary"),
        name="ada",
    )(c_all, w_ada, b_ada.reshape(1, n))


def _rope_kernel(inv_ref, cs_ref, sn_ref, *, base, blk):
    r = pl.program_id(0)
    pos = (base + r * blk + lax.broadcasted_iota(jnp.int32, (blk, LANES), 0)).astype(F32)
    ang = pos * inv_ref[...]
    lane = lax.broadcasted_iota(jnp.int32, (blk, LANES), 1)
    first_half = (lane & (HEAD_DIM - 1)) < HEAD_DIM // 2
    sn = jnp.sin(ang)
    cs_ref[...] = jnp.cos(ang)
    sn_ref[...] = jnp.where(first_half, -sn, sn)


def _rope_tables(inv_row, base, rows, blk):
    out = jax.ShapeDtypeStruct((rows, LANES), F32)
    return pl.pallas_call(
        functools.partial(_rope_kernel, base=base, blk=blk),
        grid=(rows // blk,),
        in_specs=[pl.BlockSpec((1, LANES), lambda r: (0, 0))],
        out_specs=[pl.BlockSpec((blk, LANES), lambda r: (r, 0))] * 2,
        out_shape=[out, out],
        compiler_params=_params("arbitrary"),
        name="rope_tables",
    )(inv_row)


def _ffn_kernel(x_ref, mod_ref, g_ref, w1_ref, w3_ref, w2_ref, o_ref, h_ref, *, d, k_shift, nf):
    f = pl.program_id(1)

    @pl.when(f == 0)
    def _():
        shift = mod_ref[:, k_shift * d:(k_shift + 1) * d]
        scale = mod_ref[:, (k_shift + 1) * d:(k_shift + 2) * d]
        h_ref[...] = _modulate(x_ref[...], g_ref[...], shift, scale).astype(BF16)
        o_ref[...] = jnp.zeros_like(o_ref)

    h = h_ref[...]
    a = jnp.dot(h, w1_ref[...], preferred_element_type=F32)
    b = jnp.dot(h, w3_ref[...], preferred_element_type=F32)
    gated = (_silu(a) * b).astype(BF16)
    o_ref[...] += jnp.dot(gated, w2_ref[...], preferred_element_type=F32)

    @pl.when(f == nf - 1)
    def _():
        gate = mod_ref[:, (k_shift + 2) * d:(k_shift + 3) * d]
        o_ref[...] = x_ref[...] + 0.5 * gate * o_ref[...]


def _mod_spec(mod, tm, rows_per_mod):
    if mod.ndim == 3:
        return pl.BlockSpec((None, 1, mod.shape[-1]), lambda m, *_: ((m * tm) // rows_per_mod, 0, 0))
    return pl.BlockSpec((tm, mod.shape[-1]), lambda m, *_: (m, 0))


def _ffn(x, mod, rows_per_mod, g, w1, w3, w2, k_shift, tm, tf):
    m_rows, d = x.shape
    dff = w1.shape[1]
    nf = dff // tf
    return pl.pallas_call(
        functools.partial(_ffn_kernel, d=d, k_shift=k_shift, nf=nf),
        grid=(m_rows // tm, nf),
        in_specs=[pl.BlockSpec((tm, d), lambda m, f: (m, 0)),
                  _mod_spec(mod, tm, rows_per_mod),
                  pl.BlockSpec((1, d), lambda m, f: (0, 0)),
                  pl.BlockSpec((d, tf), lambda m, f: (0, f)),
                  pl.BlockSpec((d, tf), lambda m, f: (0, f)),
                  pl.BlockSpec((tf, d), lambda m, f: (f, 0))],
        out_specs=pl.BlockSpec((tm, d), lambda m, f: (m, 0)),
        out_shape=jax.ShapeDtypeStruct((m_rows, d), F32),
        scratch_shapes=[pltpu.VMEM((tm, d), BF16)],
        compiler_params=_params("parallel", "arbitrary"),
        name="ffn",
    )(x, mod, g.reshape(1, d), w1, w3, w2)


def _store_head_pairs(ref, chunk, c):
    lo = lax.broadcasted_iota(jnp.int32, chunk.shape, 1) < HEAD_DIM
    swapped = pltpu.roll(chunk, HEAD_DIM, 1)
    zero = jnp.zeros_like(chunk)
    cols = (jnp.where(lo, chunk, zero), jnp.where(lo, zero, swapped),
            jnp.where(lo, swapped, zero), jnp.where(lo, zero, chunk))
    for i, col in enumerate(cols):
        ref[:, (4 * c + i) * LANES:(4 * c + i + 1) * LANES] = col.astype(ref.dtype)


def _inproj_kernel(x_ref, mod_ref, g_ref, w_ref, cs_ref, sn_ref, gqk_ref, bd_ref,
                   q_ref, kx_ref, vx_ref, kf_ref, z_ref, h_ref, *, d, k_shift, nn):
    n = pl.program_id(1)

    @pl.when(n == 0)
    def _():
        shift = mod_ref[:, k_shift * d:(k_shift + 1) * d]
        scale = mod_ref[:, (k_shift + 1) * d:(k_shift + 2) * d]
        h_ref[...] = _modulate(x_ref[...], g_ref[...], shift, scale).astype(BF16)

    acc = jnp.dot(h_ref[...], w_ref[...], preferred_element_type=F32)

    @pl.when(n == 0)
    def _():
        ms = jnp.dot((acc * acc).astype(BF16), bd_ref[...], preferred_element_type=F32)
        y = acc * lax.rsqrt(ms + EPS) * gqk_ref[...]
        cs = cs_ref[...]
        sn = sn_ref[...]
        lane = lax.broadcasted_iota(jnp.int32, cs.shape, 1)
        first_half = (lane & (HEAD_DIM - 1)) < HEAD_DIM // 2
        for c in range(D_QK // LANES):
            yc = y[:, c * LANES:(c + 1) * LANES]
            partner = jnp.where(first_half,
                                pltpu.roll(yc, LANES - HEAD_DIM // 2, 1),
                                pltpu.roll(yc, HEAD_DIM // 2, 1))
            rot = yc * cs + partner * sn
            if c < D_ATTN // LANES:
                q_ref[:, c * LANES:(c + 1) * LANES] = (rot * (HEAD_DIM ** -0.5)).astype(BF16)
            else:
                kf_ref[:, c * LANES - D_ATTN:(c + 1) * LANES - D_ATTN] = rot
                _store_head_pairs(kx_ref, rot, c - D_ATTN // LANES)

    @pl.when(n > 0)
    def _():
        z_ref[...] = acc

    @pl.when(n == nn - 1)
    def _():
        for c in range(D_KV // LANES):
            lo = acc.shape[1] - D_KV + c * LANES
            _store_head_pairs(vx_ref, acc[:, lo:lo + LANES], c)


def _inproj(x, mod, rows_per_mod, g, w_in, cs, sn, gqk, bd, k_shift, tm, table_blocks):
    m_rows, d = x.shape
    n_all = w_in.shape[1]
    tn = D_QK
    nn = n_all // tn
    dx = 2 * N_KV_HEADS * LANES
    return pl.pallas_call(
        functools.partial(_inproj_kernel, d=d, k_shift=k_shift, nn=nn),
        grid=(m_rows // tm, nn),
        in_specs=[pl.BlockSpec((tm, d), lambda m, n: (m, 0)),
                  _mod_spec(mod, tm, rows_per_mod),
                  pl.BlockSpec((1, d), lambda m, n: (0, 0)),
                  pl.BlockSpec((d, tn), lambda m, n: (0, n)),
                  pl.BlockSpec((tm, LANES), lambda m, n: (m % table_blocks, 0)),
                  pl.BlockSpec((tm, LANES), lambda m, n: (m % table_blocks, 0)),
                  pl.BlockSpec((1, tn), lambda m, n: (0, 0)),
                  pl.BlockSpec((tn, tn), lambda m, n: (0, 0))],
        out_specs=[pl.BlockSpec((tm, D_ATTN), lambda m, n: (m, 0)),
                   pl.BlockSpec((tm, dx), lambda m, n: (m, 0)),
                   pl.BlockSpec((tm, dx), lambda m, n: (m, 0)),
                   pl.BlockSpec((tm, D_KV), lambda m, n: (m, 0)),
                   pl.BlockSpec((tm, tn), lambda m, n: (m, jnp.maximum(n - 1, 0)))],
        out_shape=[jax.ShapeDtypeStruct((m_rows, D_ATTN), BF16),
                   jax.ShapeDtypeStruct((m_rows, dx), BF16),
                   jax.ShapeDtypeStruct((m_rows, dx), BF16),
                   jax.ShapeDtypeStruct((m_rows, D_KV), F32),
                   jax.ShapeDtypeStruct((m_rows, n_all - tn), F32)],
        scratch_shapes=[pltpu.VMEM((tm, d), BF16)],
        compiler_params=_params("parallel", "arbitrary"),
        name="inproj",
    )(x, mod, g.reshape(1, d), w_in, cs, sn, gqk, bd)


def _attn_kernel(sink_ref, q_ref, kp_ref, kc_ref, vp_ref, vc_ref, o_ref, bias_ref, *,
                 nsub, tq, first_block_has_no_past):
    nk = 2 * WINDOW
    rows = 2 * tq
    ri = lax.broadcasted_iota(jnp.int32, (rows, 2 * nk), 0) & (tq - 1)
    kj = lax.broadcasted_iota(jnp.int32, (rows, 2 * nk), 1) & (nk - 1)
    past_off = jnp.where(pl.program_id(1) > 0, 0, WINDOW) if first_block_has_no_past else 0
    visible = ((kj < WINDOW) & (kj > ri + past_off)) | ((kj >= WINDOW) & (kj - WINDOW <= ri))
    bias_ref[...] = jnp.where(visible, 0.0, -jnp.inf)

    first_rows = lax.broadcasted_iota(jnp.int32, (rows, 1), 0) < tq
    lo_lanes = lax.broadcasted_iota(jnp.int32, (rows, LANES), 1) < HEAD_DIM
    ones_lo = (lax.broadcasted_iota(jnp.int32, (nk, LANES), 1) < HEAD_DIM).astype(BF16)
    ones_hi = (1 - ones_lo.astype(F32)).astype(BF16)
    nt = (((1,), (1,)), ((), ()))

    def keys(p_ref, c_ref, s, col):
        cols = slice(col * LANES, (col + 1) * LANES)
        parts = [p_ref[s * WINDOW:(s + 1) * WINDOW, cols], c_ref[s * tq:(s + 1) * tq, cols]]
        if tq < WINDOW:
            parts.append(jnp.zeros((WINDOW - tq, LANES), BF16))
        return parts

    for s in range(nsub):
        for g in range(N_KV_HEADS):
            c0, c1 = 2 * g, 2 * g + 1
            q4 = jnp.concatenate([q_ref[s * tq:(s + 1) * tq, c0 * LANES:(c0 + 1) * LANES],
                                  q_ref[s * tq:(s + 1) * tq, c1 * LANES:(c1 + 1) * LANES]], axis=0)
            kk = jnp.concatenate(keys(kp_ref, kc_ref, s, c0) + keys(kp_ref, kc_ref, s, c1), axis=0)
            sc = lax.dot_general(q4, kk, nt, preferred_element_type=F32) + bias_ref[...]
            sk_lo = jnp.where(first_rows, sink_ref[4 * g], sink_ref[4 * g + 2])
            sk_hi = jnp.where(first_rows, sink_ref[4 * g + 1], sink_ref[4 * g + 3])
            m_lo = jnp.maximum(jnp.max(sc[:, :nk], axis=-1, keepdims=True), sk_lo)
            m_hi = jnp.maximum(jnp.max(sc[:, nk:], axis=-1, keepdims=True), sk_hi)
            p = jnp.concatenate([jnp.exp(sc[:, :nk] - m_lo), jnp.exp(sc[:, nk:] - m_hi)], axis=1).astype(BF16)
            vv = jnp.concatenate(
                [jnp.concatenate([jnp.concatenate(keys(vp_ref, vc_ref, s, c0), axis=0), ones_lo], axis=1),
                 jnp.concatenate([jnp.concatenate(keys(vp_ref, vc_ref, s, c1), axis=0), ones_hi], axis=1)],
                axis=0)
            o = jnp.dot(p, vv, preferred_element_type=F32)
            denom = o[:, LANES:] + jnp.where(lo_lanes, jnp.exp(sk_lo - m_lo), jnp.exp(sk_hi - m_hi))
            out = (o[:, :LANES] / denom).astype(BF16)
            o_ref[s * tq:(s + 1) * tq, c0 * LANES:(c0 + 1) * LANES] = out[:tq]
            o_ref[s * tq:(s + 1) * tq, c1 * LANES:(c1 + 1) * LANES] = out[tq:]


def _attn(sinks, q_arr, kp_arr, kc_arr, vp_arr, vc_arr, grid, nsub, tq, maps, first_block_has_no_past):
    qmap, pmap = maps
    dx = kp_arr.shape[1]
    q_spec = pl.BlockSpec((nsub * tq, D_ATTN), qmap)
    cur_spec = pl.BlockSpec((nsub * tq, dx), qmap)
    past_spec = pl.BlockSpec((nsub * WINDOW, dx), pmap)
    return pl.pallas_call(
        functools.partial(_attn_kernel, nsub=nsub, tq=tq, first_block_has_no_past=first_block_has_no_past),
        grid=grid,
        in_specs=[pl.BlockSpec(memory_space=pltpu.SMEM), q_spec, past_spec, cur_spec, past_spec, cur_spec],
        out_specs=pl.BlockSpec((nsub * tq, D_ATTN), qmap),
        out_shape=jax.ShapeDtypeStruct(q_arr.shape, BF16),
        scratch_shapes=[pltpu.VMEM((2 * tq, 4 * WINDOW), F32)],
        compiler_params=_params("parallel", "arbitrary"),
        name="attn",
    )(sinks, q_arr, kp_arr, kc_arr, vp_arr, vc_arr)


def _softplus(x):
    return jnp.maximum(x, 0.0) + jnp.log1p(jnp.exp(-jnp.abs(x)))


def _lru_coeffs(xc, wrg_ref, wig_ref, brg, big, lam):
    xb = xc.astype(BF16)
    ngroups = D_RNN // MXU_DIM
    rs, igs = [], []
    for c in range(ngroups):
        xg = xb[:, c * MXU_DIM:(c + 1) * MXU_DIM]
        rs.append(jnp.dot(xg, wrg_ref[c], preferred_element_type=F32))
        igs.append(jnp.dot(xg, wig_ref[c], preferred_element_type=F32))
    r = jax.nn.sigmoid(jnp.concatenate(rs, axis=1) + brg)
    ig = jax.nn.sigmoid(jnp.concatenate(igs, axis=1) + big)
    log_a = -LRU_C * r * _softplus(-lam)
    a = jnp.exp(log_a)
    one_minus_a2 = -jnp.tanh(log_a) * (1.0 + a * a)
    u = jnp.sqrt(one_minus_a2) * (ig * xc)
    return a, u


def _rnn_prompt_kernel(rx_ref, rg_ref, cw_ref, cb_ref, wrg_ref, wig_ref, brg_ref, big_ref, lam_ref,
                       o_ref, hlast_ref, conv_ref, xs_ref, a_ref, h_ref, carry_ref, *, tc):
    t = pl.program_id(1)
    pad = SUBLANES

    @pl.when(t == 0)
    def _():
        xs_ref[0:pad, :] = jnp.zeros((pad, D_RNN), F32)
        carry_ref[...] = jnp.zeros_like(carry_ref)

    x = rx_ref[...]
    xs_ref[pad:pad + tc, :] = x
    xc = cb_ref[...] + x * cw_ref[CONV_WIDTH - 1:CONV_WIDTH, :]
    for k in range(1, CONV_WIDTH):
        xc = xc + xs_ref[pad - k:pad - k + tc, :] * cw_ref[CONV_WIDTH - 1 - k:CONV_WIDTH - k, :]
    tail = xs_ref[tc:tc + pad, :]
    xs_ref[0:pad, :] = tail
    conv_ref[...] = tail[pad - (CONV_WIDTH - 1):, :]

    a, u = _lru_coeffs(xc, wrg_ref, wig_ref, brg_ref[...], big_ref[...], lam_ref[...])
    a_ref[...] = a
    h_ref[...] = u

    row = lax.broadcasted_iota(jnp.int32, (SUBLANES, D_RNN), 0)

    def body(r, carry):
        off = pl.multiple_of(r * SUBLANES, SUBLANES)
        av = a_ref[pl.ds(off, SUBLANES), :]
        hv = h_ref[pl.ds(off, SUBLANES), :]
        for sft in (1, 2, 4):
            keep = row >= sft
            a_sh = jnp.where(keep, pltpu.roll(av, sft, 0), 1.0)
            h_sh = jnp.where(keep, pltpu.roll(hv, sft, 0), 0.0)
            hv = av * h_sh + hv
            av = av * a_sh
        hv = hv + av * carry
        h_ref[pl.ds(off, SUBLANES), :] = hv
        return jnp.broadcast_to(hv[SUBLANES - 1:SUBLANES, :], (SUBLANES, D_RNN))

    carry = lax.fori_loop(0, tc // SUBLANES, body, carry_ref[...])
    carry_ref[...] = carry
    hlast_ref[...] = carry[0:1, :]
    o_ref[...] = (h_ref[...] * jax.nn.gelu(rg_ref[...])).astype(BF16)


def _rnn_prompt(z, nbatch, seq, conv_w, conv_b, wrg, wig, b_rg, b_ig, lam, tc):
    nchunk = seq // tc
    rx_blk = Z_RX // D_RNN
    rg_blk = Z_RG // D_RNN
    vec = pl.BlockSpec((1, D_RNN), lambda b, t: (0, 0))
    wspec = pl.BlockSpec(wrg.shape, lambda b, t: (0, 0, 0))
    return pl.pallas_call(
        functools.partial(_rnn_prompt_kernel, tc=tc),
        grid=(nbatch, nchunk),
        in_specs=[pl.BlockSpec((tc, D_RNN), lambda b, t: (b * nchunk + t, rx_blk)),
                  pl.BlockSpec((tc, D_RNN), lambda b, t: (b * nchunk + t, rg_blk)),
                  pl.BlockSpec((CONV_WIDTH, D_RNN), lambda b, t: (0, 0)),
                  vec, wspec, wspec, vec, vec, vec],
        out_specs=[pl.BlockSpec((tc, D_RNN), lambda b, t: (b * nchunk + t, 0)),
                   pl.BlockSpec((None, 1, D_RNN), lambda b, t: (b, 0, 0)),
                   pl.BlockSpec((None, CONV_WIDTH - 1, D_RNN), lambda b, t: (b, 0, 0))],
        out_shape=[jax.ShapeDtypeStruct((nbatch * seq, D_RNN), BF16),
                   jax.ShapeDtypeStruct((nbatch, 1, D_RNN), F32),
                   jax.ShapeDtypeStruct((nbatch, CONV_WIDTH - 1, D_RNN), F32)],
        scratch_shapes=[pltpu.VMEM((tc + SUBLANES, D_RNN), F32),
                        pltpu.VMEM((tc, D_RNN), F32),
                        pltpu.VMEM((tc, D_RNN), F32),
                        pltpu.VMEM((SUBLANES, D_RNN), F32)],
        compiler_params=_params("parallel", "arbitrary"),
        name="rnn_prompt",
    )(z, z, conv_w, conv_b.reshape(1, D_RNN), wrg, wig, b_rg.reshape(1, D_RNN),
      b_ig.reshape(1, D_RNN), lam.reshape(1, D_RNN))


def _rnn_step_kernel(rx_ref, rg_ref, c0_ref, c1_ref, c2_ref, h0_ref, cw_ref, cb_ref, wrg_ref, wig_ref,
                     brg_ref, big_ref, lam_ref, o_ref, h_ref):
    x = rx_ref[...]
    xc = (cb_ref[...] + c0_ref[...] * cw_ref[0:1, :] + c1_ref[...] * cw_ref[1:2, :]
          + c2_ref[...] * cw_ref[2:3, :] + x * cw_ref[3:4, :])
    a, u = _lru_coeffs(xc, wrg_ref, wig_ref, brg_ref[...], big_ref[...], lam_ref[...])
    h = a * h0_ref[...] + u
    h_ref[...] = h
    o_ref[...] = (h * jax.nn.gelu(rg_ref[...])).astype(BF16)


def _rnn_step(z, conv_state, h0, conv_w, conv_b, wrg, wig, b_rg, b_ig, lam):
    rows = z.shape[0]
    full = lambda shape: pl.BlockSpec(shape, lambda i: (0,) * len(shape))
    act = full((rows, D_RNN))
    vec = full((1, D_RNN))
    return pl.pallas_call(
        _rnn_step_kernel,
        grid=(1,),
        in_specs=[pl.BlockSpec((rows, D_RNN), lambda i: (0, Z_RX // D_RNN)),
                  pl.BlockSpec((rows, D_RNN), lambda i: (0, Z_RG // D_RNN)),
                  act, act, act, act, full((CONV_WIDTH, D_RNN)), vec,
                  full(wrg.shape), full(wig.shape), vec, vec, vec],
        out_specs=[act, act],
        out_shape=[jax.ShapeDtypeStruct((rows, D_RNN), BF16),
                   jax.ShapeDtypeStruct((rows, D_RNN), F32)],
        compiler_params=_params("arbitrary"),
        name="rnn_step",
    )(z, z, conv_state[:, 0], conv_state[:, 1], conv_state[:, 2], h0, conv_w,
      conv_b.reshape(1, D_RNN), wrg, wig, b_rg.reshape(1, D_RNN), b_ig.reshape(1, D_RNN),
      lam.reshape(1, D_RNN))


def _mix_kernel(oa_ref, or_ref, ga_ref, gr_ref, x_ref, mod_ref, wpa_ref, wpr_ref, wout_ref, o_ref, *, d, k_gate):
    pa = jnp.dot(oa_ref[...], wpa_ref[...], preferred_element_type=F32)
    pr = jnp.dot(or_ref[...], wpr_ref[...], preferred_element_type=F32)
    mix = jax.nn.sigmoid(ga_ref[...]) * pa + jax.nn.sigmoid(gr_ref[...]) * pr
    gate = mod_ref[:, k_gate * d:(k_gate + 1) * d]
    o_ref[...] = x_ref[...] + gate * jnp.dot(mix.astype(BF16), wout_ref[...], preferred_element_type=F32)


def _mix(o_att, o_rnn, z, x, mod, rows_per_mod, w_pa, w_pr, w_out, k_gate, tm):
    m_rows, d = x.shape
    const = lambda shape: pl.BlockSpec(shape, lambda m: (0, 0))
    return pl.pallas_call(
        functools.partial(_mix_kernel, d=d, k_gate=k_gate),
        grid=(m_rows // tm,),
        in_specs=[pl.BlockSpec((tm, D_ATTN), lambda m: (m, 0)),
                  pl.BlockSpec((tm, D_RNN), lambda m: (m, 0)),
                  pl.BlockSpec((tm, d), lambda m: (m, Z_GATT // d)),
                  pl.BlockSpec((tm, d), lambda m: (m, Z_GRNN // d)),
                  pl.BlockSpec((tm, d), lambda m: (m, 0)),
                  _mod_spec(mod, tm, rows_per_mod),
                  const(w_pa.shape), const(w_pr.shape), const(w_out.shape)],
        out_specs=pl.BlockSpec((tm, d), lambda m: (m, 0)),
        out_shape=jax.ShapeDtypeStruct((m_rows, d), F32),
        compiler_params=_params("parallel"),
        name="mix",
    )(o_att, o_rnn, z, z, x, mod, w_pa, w_pr, w_out)


def _block_diag(w, group):
    n, r, _ = w.shape
    eye = jnp.eye(group, dtype=w.dtype)
    wg = w.reshape(n // group, group, r, r)
    return jnp.einsum("ngij,gh->ngihj", wg, eye).reshape(n // group, group * r, group * r)


def kernel(x_prompt, x_sample, c_prompt, c_sample, cache_k, cache_v, state_h, state_conv, w_ada, b_ada,
           g_norm_ffn1, g_norm_mix, g_norm_ffn2, ffn1_w1, ffn1_w3, ffn1_w2, ffn2_w1, ffn2_w3, ffn2_w2,
           w_in, g_q, g_k, sinks, conv_w, conv_b, w_rg, b_rg, w_ig, b_ig, lru_lambda, w_pa, w_pr, w_out):
    nb, seq, d = x_prompt.shape
    ns = x_sample.shape[0]
    assert d == D_MODEL and w_ada.shape[0] == 1 and x_sample.shape[1] == 1 and cache_k.shape[2] == WINDOW
    assert seq % WINDOW == 0

    bf = lambda w: w[0].astype(BF16)
    wi = w_in[0]
    w_in_p = jnp.concatenate([wi[:, :D_QK], wi[:, D_QK + D_KV:], wi[:, D_QK:D_QK + D_KV]], axis=1).astype(BF16)
    group = MXU_DIM // RNN_BLOCK
    wrg = _block_diag(w_rg[0], group).astype(BF16)
    wig = _block_diag(w_ig[0], group).astype(BF16)
    gqk = jnp.concatenate([jnp.tile(g_q[0], N_Q_HEADS), jnp.tile(g_k[0], N_KV_HEADS)]).reshape(1, D_QK)
    head_mean = jnp.kron(jnp.eye(D_QK // HEAD_DIM, dtype=F32),
                         jnp.full((HEAD_DIM, HEAD_DIM), 1.0 / HEAD_DIM, F32)).astype(BF16)
    inv = ROPE_THETA ** (-jnp.arange(HEAD_DIM // 2, dtype=F32) * 2.0 / HEAD_DIM)
    inv_row = jnp.tile(inv, LANES // (HEAD_DIM // 2)).reshape(1, LANES)

    pad_rows = (-(nb + ns)) % SUBLANES
    c_all = jnp.concatenate([c_prompt, c_sample, jnp.zeros((pad_rows, d), F32)], axis=0)
    mod_all = _ada(c_all, w_ada[0], b_ada[0])
    mod_p = mod_all[:nb].reshape(nb, 1, N_MOD * d)
    mod_s = mod_all[nb:nb + ns]

    tm_in = min(512, seq)
    cs_p, sn_p = _rope_tables(inv_row, 0, seq, tm_in)
    cs_1, sn_1 = _rope_tables(inv_row, PAST_LEN, SUBLANES, SUBLANES)
    cs_s = jnp.broadcast_to(cs_1[0:1], (ns, LANES))
    sn_s = jnp.broadcast_to(sn_1[0:1], (ns, LANES))

    sink_vec = sinks[0]
    lam = lru_lambda[0]

    m_rows = nb * seq
    xp = x_prompt.reshape(m_rows, d)
    tm_ffn = min(512, seq)
    tf = 512 if ffn1_w1.shape[2] % 512 == 0 else ffn1_w1.shape[2]
    w11, w13, w12 = bf(ffn1_w1), bf(ffn1_w3), bf(ffn1_w2)
    w21, w23, w22 = bf(ffn2_w1), bf(ffn2_w3), bf(ffn2_w2)
    wpa, wpr, wout = bf(w_pa), bf(w_pr), bf(w_out)

    x1 = _ffn(xp, mod_p, seq, g_norm_ffn1[0], w11, w13, w12, 0, tm_ffn, tf)
    q, kx, vx, kf, z = _inproj(x1, mod_p, seq, g_norm_mix[0], w_in_p, cs_p, sn_p, gqk, head_mean, 3,
                               tm_in, seq // tm_in)
    nblk = seq // WINDOW
    maps = (lambda b, n: (b * nblk + n, 0), lambda b, n: (b * nblk + jnp.maximum(n - 1, 0), 0))
    o_att = _attn(sink_vec, q, kx, kx, vx, vx, (nb, nblk), 1, WINDOW, maps, True)
    o_rnn, h_p, conv_p = _rnn_prompt(z, nb, seq, conv_w[0], conv_b[0], wrg, wig, b_rg[0], b_ig[0], lam,
                                     min(512, seq))
    x2 = _mix(o_att, o_rnn, z, x1, mod_p, seq, wpa, wpr, wout, 5, min(256, seq))
    y_p = _ffn(x2, mod_p, seq, g_norm_ffn2[0], w21, w23, w22, 6, tm_ffn, tf)

    k_prompt = kf.reshape(nb, seq, N_KV_HEADS, HEAD_DIM)[:, seq - WINDOW:][None]
    v_prompt = z[:, Z_V:].reshape(nb, seq, N_KV_HEADS, HEAD_DIM)[:, seq - WINDOW:][None]

    xs = x_sample.reshape(ns, d)
    x1s = _ffn(xs, mod_s, 1, g_norm_ffn1[0], w11, w13, w12, 0, ns, tf)
    qs, kxs, vxs, kfs, zs = _inproj(x1s, mod_s, 1, g_norm_mix[0], w_in_p, cs_s, sn_s, gqk, head_mean, 3, ns, 1)
    tq = 2 * SUBLANES
    padq = lambda a: jnp.pad(a[:, None, :], ((0, 0), (0, tq - 1), (0, 0))).reshape(ns * tq, a.shape[-1])
    vs_new = zs[:, Z_V:]

    def cache_pairs(c):
        zero = jnp.zeros_like(c)
        lo = jnp.concatenate([c, zero], axis=-1)
        hi = jnp.concatenate([zero, c], axis=-1)
        return jnp.stack([lo, hi], axis=3).reshape(ns * WINDOW, 2 * N_KV_HEADS * LANES).astype(BF16)

    nsub = 8 if ns % 8 == 0 else 1
    maps_s = (lambda b, n: (b, 0), lambda b, n: (b, 0))
    o_att_s = _attn(sink_vec, padq(qs), cache_pairs(cache_k[0]), padq(kxs), cache_pairs(cache_v[0]), padq(vxs),
                    (ns // nsub, 1), nsub, tq, maps_s, False)
    o_att_s = o_att_s.reshape(ns, tq, D_ATTN)[:, 0]
    conv_s_in = state_conv[0]
    o_rnn_s, h_s = _rnn_step(zs, conv_s_in, state_h[0], conv_w[0], conv_b[0], wrg, wig, b_rg[0], b_ig[0], lam)
    x2s = _mix(o_att_s, o_rnn_s, zs, x1s, mod_s, 1, wpa, wpr, wout, 5, ns)
    y_s = _ffn(x2s, mod_s, 1, g_norm_ffn2[0], w21, w23, w22, 6, ns, tf)

    k_sample = kfs.reshape(1, ns, 1, N_KV_HEADS, HEAD_DIM)
    v_sample = vs_new.reshape(1, ns, 1, N_KV_HEADS, HEAD_DIM)
    conv_sample = jnp.concatenate([conv_s_in[:, 1:], zs[:, None, Z_RX:Z_RX + D_RNN]], axis=1)[None]

    return (y_p.reshape(nb, seq, d), y_s.reshape(ns, 1, d), k_prompt, v_prompt, k_sample, v_sample,
            h_p.reshape(1, nb, D_RNN), h_s[None], conv_p[None], conv_sample)
```

```python
import functools

import jax
import jax.numpy as jnp
from jax import lax
from jax.experimental import pallas as pl
from jax.experimental.pallas import tpu as pltpu

F32 = jnp.float32
BF16 = jnp.bfloat16

D_MODEL = 2048
HEAD_DIM = 64
N_Q_HEADS = 16
N_KV_HEADS = 4
Q_PER_KV = N_Q_HEADS // N_KV_HEADS
D_ATTN = N_Q_HEADS * HEAD_DIM
D_KV = N_KV_HEADS * HEAD_DIM
D_QK = D_ATTN + D_KV
D_QKV = D_QK + D_KV
WINDOW = 128
ROPE_THETA = 10000.0
D_RNN = 1024
N_RNN_BLOCKS = 16
RNN_BLOCK = D_RNN // N_RNN_BLOCKS
CONV_WIDTH = 4
LRU_C = 8.0
N_MOD = 9
EPS = 1e-6
PAST_LEN = 16384

LANES = 128
SUBLANES = 8
MXU_DIM = 256
VMEM_LIMIT = 56 * 1024 * 1024

Z_RX = 0
Z_RG = Z_RX + D_RNN
Z_GATT = Z_RG + D_RNN
Z_GRNN = Z_GATT + D_MODEL
D_Z = Z_GRNN + D_MODEL
DX = 2 * N_KV_HEADS * LANES


def _params(*sem):
    return pltpu.CompilerParams(dimension_semantics=sem, vmem_limit_bytes=VMEM_LIMIT)


def _silu(x):
    return x * jax.nn.sigmoid(x)


def _modulate(x, g, shift, scale):
    ms = jnp.mean(x * x, axis=-1, keepdims=True)
    return (x * lax.rsqrt(ms + EPS) * g) * (1.0 + scale) + shift


def _mod_spec(mod, tm, rows_per_mod, width, col, ngrid):
    colf = col if callable(col) else (lambda *idx: col)
    if mod.ndim == 3:
        return pl.BlockSpec((None, 1, width), lambda *idx: ((idx[ngrid - 1] * tm) // rows_per_mod, 0, colf(*idx)))
    return pl.BlockSpec((tm, width), lambda *idx: (idx[ngrid - 1], colf(*idx)))


def _ada_kernel(c_ref, w_ref, b_ref, o_ref):
    s = _silu(c_ref[...]).astype(BF16)
    o_ref[...] = jnp.dot(s, w_ref[...].astype(BF16), preferred_element_type=F32) + b_ref[...]


def _ada(c_all, w_ada, b_ada, tn):
    rows, d = c_all.shape
    n = w_ada.shape[1]
    return pl.pallas_call(
        _ada_kernel,
        grid=(n // tn,),
        in_specs=[pl.BlockSpec((rows, d), lambda j: (0, 0)),
                  pl.BlockSpec((d, tn), lambda j: (0, j)),
                  pl.BlockSpec((1, tn), lambda j: (0, j))],
        out_specs=pl.BlockSpec((rows, tn), lambda j: (0, j)),
        out_shape=jax.ShapeDtypeStruct((rows, n), F32),
        compiler_params=_params("arbitrary"),
        name="ada",
    )(c_all, w_ada, b_ada.reshape(1, n))


def _rope_kernel(inv_ref, cs_ref, sn_ref, *, base, blk):
    r = pl.program_id(0)
    pos = (base + r * blk + lax.broadcasted_iota(jnp.int32, (blk, LANES), 0)).astype(F32)
    ang = pos * inv_ref[...]
    lane = lax.broadcasted_iota(jnp.int32, (blk, LANES), 1)
    first_half = (lane & (HEAD_DIM - 1)) < HEAD_DIM // 2
    sn = jnp.sin(ang)
    cs_ref[...] = jnp.cos(ang)
    sn_ref[...] = jnp.where(first_half, -sn, sn)


def _rope_tables(inv_row, base, rows, blk):
    out = jax.ShapeDtypeStruct((rows, LANES), F32)
    return pl.pallas_call(
        functools.partial(_rope_kernel, base=base, blk=blk),
        grid=(rows // blk,),
        in_specs=[pl.BlockSpec((1, LANES), lambda r: (0, 0))],
        out_specs=[pl.BlockSpec((blk, LANES), lambda r: (r, 0))] * 2,
        out_shape=[out, out],
        compiler_params=_params("arbitrary"),
        name="rope_tables",
    )(inv_row)


def _modulate_kernel(x_ref, shift_ref, scale_ref, g_ref, h_ref):
    h_ref[...] = _modulate(x_ref[...], g_ref[...], shift_ref[...], scale_ref[...]).astype(BF16)


def _modulated_norm(x, mod, rows_per_mod, g, k_shift, tm):
    m_rows, d = x.shape
    return pl.pallas_call(
        _modulate_kernel,
        grid=(m_rows // tm,),
        in_specs=[pl.BlockSpec((tm, d), lambda m: (m, 0)),
                  _mod_spec(mod, tm, rows_per_mod, d, k_shift, 1),
                  _mod_spec(mod, tm, rows_per_mod, d, k_shift + 1, 1),
                  pl.BlockSpec((1, d), lambda m: (0, 0))],
        out_specs=pl.BlockSpec((tm, d), lambda m: (m, 0)),
        out_shape=jax.ShapeDtypeStruct((m_rows, d), BF16),
        compiler_params=_params("arbitrary"),
        name="modnorm",
    )(x, mod, mod, g.reshape(1, d))


def _ffn_up_kernel(h_ref, w1_ref, w3_ref, g_ref, w1b_ref, w3b_ref):
    @pl.when(pl.program_id(1) == 0)
    def _():
        w1b_ref[...] = w1_ref[...].astype(BF16)
        w3b_ref[...] = w3_ref[...].astype(BF16)

    h = h_ref[...]
    a = jnp.dot(h, w1b_ref[...], preferred_element_type=F32)
    b = jnp.dot(h, w3b_ref[...], preferred_element_type=F32)
    g_ref[...] = (_silu(a) * b).astype(BF16)


def _ffn_up(h, w1, w3, tm, tf):
    m_rows, d = h.shape
    dff = w1.shape[2]
    wspec = pl.BlockSpec((None, d, tf), lambda f, m: (0, 0, f))
    return pl.pallas_call(
        _ffn_up_kernel,
        grid=(dff // tf, m_rows // tm),
        in_specs=[pl.BlockSpec((tm, d), lambda f, m: (m, 0)), wspec, wspec],
        out_specs=pl.BlockSpec((tm, tf), lambda f, m: (m, f)),
        out_shape=jax.ShapeDtypeStruct((m_rows, dff), BF16),
        scratch_shapes=[pltpu.VMEM((d, tf), BF16), pltpu.VMEM((d, tf), BF16)],
        compiler_params=_params("arbitrary", "arbitrary"),
        name="ffn_up",
    )(h, w1, w3)


def _ffn_down_kernel(g_ref, w2_ref, x_ref, gate_ref, o_ref, w2b_ref):
    @pl.when(pl.program_id(1) == 0)
    def _():
        w2b_ref[...] = w2_ref[...].astype(BF16)

    acc = jnp.dot(g_ref[...], w2b_ref[...], preferred_element_type=F32)
    o_ref[...] = x_ref[...] + 0.5 * gate_ref[...] * acc


def _ffn_down(g, w2, x, mod, rows_per_mod, k_gate, tm, tn):
    m_rows, d = x.shape
    dff = g.shape[1]
    gate_col = lambda n, m: k_gate * (d // tn) + n
    return pl.pallas_call(
        _ffn_down_kernel,
        grid=(d // tn, m_rows // tm),
        in_specs=[pl.BlockSpec((tm, dff), lambda n, m: (m, 0)),
                  pl.BlockSpec((None, dff, tn), lambda n, m: (0, 0, n)),
                  pl.BlockSpec((tm, tn), lambda n, m: (m, n)),
                  _mod_spec(mod, tm, rows_per_mod, tn, gate_col, 2)],
        out_specs=pl.BlockSpec((tm, tn), lambda n, m: (m, n)),
        out_shape=jax.ShapeDtypeStruct((m_rows, d), F32),
        scratch_shapes=[pltpu.VMEM((dff, tn), BF16)],
        compiler_params=_params("arbitrary", "arbitrary"),
        name="ffn_down",
    )(g, w2, x, mod)


def _inproj_z_kernel(h_ref, w_ref, z_ref, wb_ref):
    @pl.when(pl.program_id(1) == 0)
    def _():
        wb_ref[...] = w_ref[...].astype(BF16)

    z_ref[...] = jnp.dot(h_ref[...], wb_ref[...], preferred_element_type=F32)


def _inproj_z(h, w_in, tm, tn):
    m_rows, d = h.shape
    first = D_QKV // tn
    return pl.pallas_call(
        _inproj_z_kernel,
        grid=(D_Z // tn, m_rows // tm),
        in_specs=[pl.BlockSpec((tm, d), lambda n, m: (m, 0)),
                  pl.BlockSpec((None, d, tn), lambda n, m: (0, 0, first + n))],
        out_specs=pl.BlockSpec((tm, tn), lambda n, m: (m, n)),
        out_shape=jax.ShapeDtypeStruct((m_rows, D_Z), F32),
        scratch_shapes=[pltpu.VMEM((d, tn), BF16)],
        compiler_params=_params("arbitrary", "arbitrary"),
        name="inproj_z",
    )(h, w_in)


def _store_head_pairs(ref, chunk, c):
    lo = lax.broadcasted_iota(jnp.int32, chunk.shape, 1) < HEAD_DIM
    swapped = pltpu.roll(chunk, HEAD_DIM, 1)
    zero = jnp.zeros_like(chunk)
    cols = (jnp.where(lo, chunk, zero), jnp.where(lo, zero, swapped),
            jnp.where(lo, swapped, zero), jnp.where(lo, zero, chunk))
    for i, col in enumerate(cols):
        ref[:, (4 * c + i) * LANES:(4 * c + i + 1) * LANES] = col.astype(ref.dtype)


def _inproj_qkv_kernel(h_ref, w_ref, cs_ref, sn_ref, gqk_ref, bd_ref,
                       q_ref, kx_ref, vx_ref, kf_ref, vf_ref, wb_ref):
    @pl.when(pl.program_id(0) == 0)
    def _():
        wb_ref[...] = w_ref[...].astype(BF16)

    acc = jnp.dot(h_ref[...], wb_ref[...], preferred_element_type=F32)
    qk = acc[:, :D_QK]
    ms = jnp.dot((qk * qk).astype(BF16), bd_ref[...], preferred_element_type=F32)
    y = qk * lax.rsqrt(ms + EPS) * gqk_ref[...]
    cs = cs_ref[...]
    sn = sn_ref[...]
    lane = lax.broadcasted_iota(jnp.int32, cs.shape, 1)
    first_half = (lane & (HEAD_DIM - 1)) < HEAD_DIM // 2
    for c in range(D_QK // LANES):
        yc = y[:, c * LANES:(c + 1) * LANES]
        partner = jnp.where(first_half,
                            pltpu.roll(yc, LANES - HEAD_DIM // 2, 1),
                            pltpu.roll(yc, HEAD_DIM // 2, 1))
        rot = yc * cs + partner * sn
        if c < D_ATTN // LANES:
            q_ref[:, c * LANES:(c + 1) * LANES] = (rot * (HEAD_DIM ** -0.5)).astype(BF16)
        else:
            kf_ref[:, c * LANES - D_ATTN:(c + 1) * LANES - D_ATTN] = rot
            _store_head_pairs(kx_ref, rot, c - D_ATTN // LANES)
    v = acc[:, D_QK:]
    vf_ref[...] = v
    for c in range(D_KV // LANES):
        _store_head_pairs(vx_ref, v[:, c * LANES:(c + 1) * LANES], c)


def _inproj_qkv(h, w_in, cs, sn, gqk, bd, tm, table_blocks):
    m_rows, d = h.shape
    row = lambda width: pl.BlockSpec((tm, width), lambda m: (m, 0))
    table = pl.BlockSpec((tm, LANES), lambda m: (m % table_blocks, 0))
    return pl.pallas_call(
        _inproj_qkv_kernel,
        grid=(m_rows // tm,),
        in_specs=[row(d),
                  pl.BlockSpec((None, d, D_QKV), lambda m: (0, 0, 0), pipeline_mode=pl.Buffered(1)),
                  table, table,
                  pl.BlockSpec((1, D_QK), lambda m: (0, 0)),
                  pl.BlockSpec((D_QK, D_QK), lambda m: (0, 0), pipeline_mode=pl.Buffered(1))],
        out_specs=[row(D_ATTN), row(DX), row(DX), row(D_KV), row(D_KV)],
        out_shape=[jax.ShapeDtypeStruct((m_rows, D_ATTN), BF16),
                   jax.ShapeDtypeStruct((m_rows, DX), BF16),
                   jax.ShapeDtypeStruct((m_rows, DX), BF16),
                   jax.ShapeDtypeStruct((m_rows, D_KV), F32),
                   jax.ShapeDtypeStruct((m_rows, D_KV), F32)],
        scratch_shapes=[pltpu.VMEM((d, D_QKV), BF16)],
        compiler_params=_params("arbitrary"),
        name="inproj_qkv",
    )(h, w_in, cs, sn, gqk, bd)


def _attn_kernel(sink_ref, q_ref, kp_ref, kc_ref, vp_ref, vc_ref, o_ref, bias_ref, *,
                 nsub, tq, first_block_has_no_past):
    nk = 2 * WINDOW
    rows = 2 * tq
    ri = lax.broadcasted_iota(jnp.int32, (rows, 2 * nk), 0) & (tq - 1)
    kj = lax.broadcasted_iota(jnp.int32, (rows, 2 * nk), 1) & (nk - 1)
    past_off = jnp.where(pl.program_id(1) > 0, 0, WINDOW) if first_block_has_no_past else 0
    visible = ((kj < WINDOW) & (kj > ri + past_off)) | ((kj >= WINDOW) & (kj - WINDOW <= ri))
    bias_ref[...] = jnp.where(visible, 0.0, -jnp.inf)

    first_rows = lax.broadcasted_iota(jnp.int32, (rows, 1), 0) < tq
    lo_lanes = lax.broadcasted_iota(jnp.int32, (rows, LANES), 1) < HEAD_DIM
    ones_lo = (lax.broadcasted_iota(jnp.int32, (nk, LANES), 1) < HEAD_DIM).astype(BF16)
    ones_hi = (1 - ones_lo.astype(F32)).astype(BF16)
    nt = (((1,), (1,)), ((), ()))

    def keys(p_ref, c_ref, s, col):
        cols = slice(col * LANES, (col + 1) * LANES)
        parts = [p_ref[s * WINDOW:(s + 1) * WINDOW, cols], c_ref[s * tq:(s + 1) * tq, cols]]
        if tq < WINDOW:
            parts.append(jnp.zeros((WINDOW - tq, LANES), BF16))
        return parts

    for s in range(nsub):
        for g in range(N_KV_HEADS):
            c0, c1 = 2 * g, 2 * g + 1
            q4 = jnp.concatenate([q_ref[s * tq:(s + 1) * tq, c0 * LANES:(c0 + 1) * LANES],
                                  q_ref[s * tq:(s + 1) * tq, c1 * LANES:(c1 + 1) * LANES]], axis=0)
            kk = jnp.concatenate(keys(kp_ref, kc_ref, s, c0) + keys(kp_ref, kc_ref, s, c1), axis=0)
            sc = lax.dot_general(q4, kk, nt, preferred_element_type=F32) + bias_ref[...]
            sk_lo = jnp.where(first_rows, sink_ref[4 * g], sink_ref[4 * g + 2])
            sk_hi = jnp.where(first_rows, sink_ref[4 * g + 1], sink_ref[4 * g + 3])
            m_lo = jnp.maximum(jnp.max(sc[:, :nk], axis=-1, keepdims=True), sk_lo)
            m_hi = jnp.maximum(jnp.max(sc[:, nk:], axis=-1, keepdims=True), sk_hi)
            p = jnp.concatenate([jnp.exp(sc[:, :nk] - m_lo), jnp.exp(sc[:, nk:] - m_hi)], axis=1).astype(BF16)
            vv = jnp.concatenate(
                [jnp.concatenate([jnp.concatenate(keys(vp_ref, vc_ref, s, c0), axis=0), ones_lo], axis=1),
                 jnp.concatenate([jnp.concatenate(keys(vp_ref, vc_ref, s, c1), axis=0), ones_hi], axis=1)],
                axis=0)
            o = jnp.dot(p, vv, preferred_element_type=F32)
            denom = o[:, LANES:] + jnp.where(lo_lanes, jnp.exp(sk_lo - m_lo), jnp.exp(sk_hi - m_hi))
            out = (o[:, :LANES] / denom).astype(BF16)
            o_ref[s * tq:(s + 1) * tq, c0 * LANES:(c0 + 1) * LANES] = out[:tq]
            o_ref[s * tq:(s + 1) * tq, c1 * LANES:(c1 + 1) * LANES] = out[tq:]


def _attn(sinks, q_arr, kp_arr, kc_arr, vp_arr, vc_arr, grid, nsub, tq, maps, first_block_has_no_past):
    qmap, pmap = maps
    q_spec = pl.BlockSpec((nsub * tq, D_ATTN), qmap)
    cur_spec = pl.BlockSpec((nsub * tq, DX), qmap)
    past_spec = pl.BlockSpec((nsub * WINDOW, DX), pmap)
    return pl.pallas_call(
        functools.partial(_attn_kernel, nsub=nsub, tq=tq, first_block_has_no_past=first_block_has_no_past),
        grid=grid,
        in_specs=[pl.BlockSpec(memory_space=pltpu.SMEM), q_spec, past_spec, cur_spec, past_spec, cur_spec],
        out_specs=pl.BlockSpec((nsub * tq, D_ATTN), qmap),
        out_shape=jax.ShapeDtypeStruct(q_arr.shape, BF16),
        scratch_shapes=[pltpu.VMEM((2 * tq, 4 * WINDOW), F32)],
        compiler_params=_params("arbitrary", "arbitrary"),
        name="attn",
    )(sinks, q_arr, kp_arr, kc_arr, vp_arr, vc_arr)


def _softplus(x):
    return jnp.maximum(x, 0.0) + jnp.log1p(jnp.exp(-jnp.abs(x)))


def _lru_coeffs(xc, wrg_ref, wig_ref, brg, big, lam):
    xb = xc.astype(BF16)
    ngroups = D_RNN // MXU_DIM
    rs, igs = [], []
    for c in range(ngroups):
        xg = xb[:, c * MXU_DIM:(c + 1) * MXU_DIM]
        rs.append(jnp.dot(xg, wrg_ref[c], preferred_element_type=F32))
        igs.append(jnp.dot(xg, wig_ref[c], preferred_element_type=F32))
    r = jax.nn.sigmoid(jnp.concatenate(rs, axis=1) + brg)
    ig = jax.nn.sigmoid(jnp.concatenate(igs, axis=1) + big)
    log_a = -LRU_C * r * _softplus(-lam)
    a = jnp.exp(log_a)
    one_minus_a2 = -jnp.tanh(log_a) * (1.0 + a * a)
    u = jnp.sqrt(one_minus_a2) * (ig * xc)
    return a, u


def _rnn_prompt_kernel(rx_ref, rg_ref, cw_ref, cb_ref, wrg_ref, wig_ref, brg_ref, big_ref, lam_ref,
                       o_ref, hlast_ref, conv_ref, xs_ref, a_ref, h_ref, carry_ref, *, tc):
    t = pl.program_id(1)
    pad = SUBLANES

    @pl.when(t == 0)
    def _():
        xs_ref[0:pad, :] = jnp.zeros((pad, D_RNN), F32)
        carry_ref[...] = jnp.zeros_like(carry_ref)

    x = rx_ref[...]
    xs_ref[pad:pad + tc, :] = x
    xc = cb_ref[...] + x * cw_ref[CONV_WIDTH - 1:CONV_WIDTH, :]
    for k in range(1, CONV_WIDTH):
        xc = xc + xs_ref[pad - k:pad - k + tc, :] * cw_ref[CONV_WIDTH - 1 - k:CONV_WIDTH - k, :]
    tail = xs_ref[tc:tc + pad, :]
    xs_ref[0:pad, :] = tail
    conv_ref[...] = tail[pad - (CONV_WIDTH - 1):, :]

    a, u = _lru_coeffs(xc, wrg_ref, wig_ref, brg_ref[...], big_ref[...], lam_ref[...])
    a_ref[...] = a
    h_ref[...] = u

    row = lax.broadcasted_iota(jnp.int32, (SUBLANES, D_RNN), 0)

    def body(r, carry):
        off = pl.multiple_of(r * SUBLANES, SUBLANES)
        av = a_ref[pl.ds(off, SUBLANES), :]
        hv = h_ref[pl.ds(off, SUBLANES), :]
        for sft in (1, 2, 4):
            keep = row >= sft
            a_sh = jnp.where(keep, pltpu.roll(av, sft, 0), 1.0)
            h_sh = jnp.where(keep, pltpu.roll(hv, sft, 0), 0.0)
            hv = av * h_sh + hv
            av = av * a_sh
        hv = hv + av * carry
        h_ref[pl.ds(off, SUBLANES), :] = hv
        return jnp.broadcast_to(hv[SUBLANES - 1:SUBLANES, :], (SUBLANES, D_RNN))

    carry = lax.fori_loop(0, tc // SUBLANES, body, carry_ref[...])
    carry_ref[...] = carry
    hlast_ref[...] = carry[0:1, :]
    o_ref[...] = (h_ref[...] * jax.nn.gelu(rg_ref[...])).astype(BF16)


def _rnn_prompt(z, nbatch, seq, conv_w, conv_b, wrg, wig, b_rg, b_ig, lam, tc):
    nchunk = seq // tc
    rx_blk = Z_RX // D_RNN
    rg_blk = Z_RG // D_RNN
    vec = pl.BlockSpec((1, D_RNN), lambda b, t: (0, 0))
    wspec = pl.BlockSpec(wrg.shape, lambda b, t: (0, 0, 0))
    return pl.pallas_call(
        functools.partial(_rnn_prompt_kernel, tc=tc),
        grid=(nbatch, nchunk),
        in_specs=[pl.BlockSpec((tc, D_RNN), lambda b, t: (b * nchunk + t, rx_blk)),
                  pl.BlockSpec((tc, D_RNN), lambda b, t: (b * nchunk + t, rg_blk)),
                  pl.BlockSpec((CONV_WIDTH, D_RNN), lambda b, t: (0, 0)),
                  vec, wspec, wspec, vec, vec, vec],
        out_specs=[pl.BlockSpec((tc, D_RNN), lambda b, t: (b * nchunk + t, 0)),
                   pl.BlockSpec((None, 1, D_RNN), lambda b, t: (b, 0, 0)),
                   pl.BlockSpec((None, CONV_WIDTH - 1, D_RNN), lambda b, t: (b, 0, 0))],
        out_shape=[jax.ShapeDtypeStruct((nbatch * seq, D_RNN), BF16),
                   jax.ShapeDtypeStruct((nbatch, 1, D_RNN), F32),
                   jax.ShapeDtypeStruct((nbatch, CONV_WIDTH - 1, D_RNN), F32)],
        scratch_shapes=[pltpu.VMEM((tc + SUBLANES, D_RNN), F32),
                        pltpu.VMEM((tc, D_RNN), F32),
                        pltpu.VMEM((tc, D_RNN), F32),
                        pltpu.VMEM((SUBLANES, D_RNN), F32)],
        compiler_params=_params("arbitrary", "arbitrary"),
        name="rnn_prompt",
    )(z, z, conv_w, conv_b.reshape(1, D_RNN), wrg, wig, b_rg.reshape(1, D_RNN),
      b_ig.reshape(1, D_RNN), lam.reshape(1, D_RNN))


def _rnn_step_kernel(rx_ref, rg_ref, c0_ref, c1_ref, c2_ref, h0_ref, cw_ref, cb_ref, wrg_ref, wig_ref,
                     brg_ref, big_ref, lam_ref, o_ref, h_ref):
    x = rx_ref[...]
    xc = (cb_ref[...] + c0_ref[...] * cw_ref[0:1, :] + c1_ref[...] * cw_ref[1:2, :]
          + c2_ref[...] * cw_ref[2:3, :] + x * cw_ref[3:4, :])
    a, u = _lru_coeffs(xc, wrg_ref, wig_ref, brg_ref[...], big_ref[...], lam_ref[...])
    h = a * h0_ref[...] + u
    h_ref[...] = h
    o_ref[...] = (h * jax.nn.gelu(rg_ref[...])).astype(BF16)


def _rnn_step(z, conv_state, h0, conv_w, conv_b, wrg, wig, b_rg, b_ig, lam):
    rows = z.shape[0]
    full = lambda shape: pl.BlockSpec(shape, lambda i: (0,) * len(shape))
    act = full((rows, D_RNN))
    vec = full((1, D_RNN))
    return pl.pallas_call(
        _rnn_step_kernel,
        grid=(1,),
        in_specs=[pl.BlockSpec((rows, D_RNN), lambda i: (0, Z_RX // D_RNN)),
                  pl.BlockSpec((rows, D_RNN), lambda i: (0, Z_RG // D_RNN)),
                  act, act, act, act, full((CONV_WIDTH, D_RNN)), vec,
                  full(wrg.shape), full(wig.shape), vec, vec, vec],
        out_specs=[act, act],
        out_shape=[jax.ShapeDtypeStruct((rows, D_RNN), BF16),
                   jax.ShapeDtypeStruct((rows, D_RNN), F32)],
        compiler_params=_params("arbitrary"),
        name="rnn_step",
    )(z, z, conv_state[:, 0], conv_state[:, 1], conv_state[:, 2], h0, conv_w,
      conv_b.reshape(1, D_RNN), wrg, wig, b_rg.reshape(1, D_RNN), b_ig.reshape(1, D_RNN),
      lam.reshape(1, D_RNN))


def _mix_kernel(oa_ref, or_ref, ga_ref, gr_ref, x_ref, gate_ref, shift_ref, scale_ref, g_ref,
                wpa_ref, wpr_ref, wout_ref, o_ref, h_ref):
    pa = jnp.dot(oa_ref[...], wpa_ref[...], preferred_element_type=F32)
    pr = jnp.dot(or_ref[...], wpr_ref[...], preferred_element_type=F32)
    mix = jax.nn.sigmoid(ga_ref[...]) * pa + jax.nn.sigmoid(gr_ref[...]) * pr
    x = x_ref[...] + gate_ref[...] * jnp.dot(mix.astype(BF16), wout_ref[...], preferred_element_type=F32)
    o_ref[...] = x
    h_ref[...] = _modulate(x, g_ref[...], shift_ref[...], scale_ref[...]).astype(BF16)


def _mix(o_att, o_rnn, z, x, mod, rows_per_mod, g_next, w_pa, w_pr, w_out, k_gate, k_shift_next, tm):
    m_rows, d = x.shape
    const = lambda shape: pl.BlockSpec(shape, lambda m: (0, 0), pipeline_mode=pl.Buffered(1))
    row = lambda width, col: pl.BlockSpec((tm, width), lambda m: (m, col))
    mspec = lambda k: _mod_spec(mod, tm, rows_per_mod, d, k, 1)
    return pl.pallas_call(
        _mix_kernel,
        grid=(m_rows // tm,),
        in_specs=[row(D_ATTN, 0), row(D_RNN, 0), row(d, Z_GATT // d), row(d, Z_GRNN // d), row(d, 0),
                  mspec(k_gate), mspec(k_shift_next), mspec(k_shift_next + 1), const((1, d)),
                  const(w_pa.shape), const(w_pr.shape), const(w_out.shape)],
        out_specs=[row(d, 0), row(d, 0)],
        out_shape=[jax.ShapeDtypeStruct((m_rows, d), F32), jax.ShapeDtypeStruct((m_rows, d), BF16)],
        compiler_params=_params("arbitrary"),
        name="mix",
    )(o_att, o_rnn, z, z, x, mod, mod, mod, g_next.reshape(1, d), w_pa, w_pr, w_out)


def _block_diag(w, group):
    n, r, _ = w.shape
    eye = jnp.eye(group, dtype=w.dtype)
    wg = w.reshape(n // group, group, r, r)
    return jnp.einsum("ngij,gh->ngihj", wg, eye).reshape(n // group, group * r, group * r)


def _tiles(rows, dff):
    pick = lambda n, prefs: next((t for t in prefs if n % t == 0), n)
    return dict(
        tm_norm=pick(rows, (512,)),
        tm_up=pick(rows, (1024, 512)), tf=pick(dff, (512,)),
        tm_down=pick(rows, (512,)), tn_down=512,
        tm_z=pick(rows, (512,)), tn_z=D_QKV,
        tm_qkv=pick(rows, (512,)),
        tm_mix=pick(rows, (256,)),
    )


def kernel(x_prompt, x_sample, c_prompt, c_sample, cache_k, cache_v, state_h, state_conv, w_ada, b_ada,
           g_norm_ffn1, g_norm_mix, g_norm_ffn2, ffn1_w1, ffn1_w3, ffn1_w2, ffn2_w1, ffn2_w3, ffn2_w2,
           w_in, g_q, g_k, sinks, conv_w, conv_b, w_rg, b_rg, w_ig, b_ig, lru_lambda, w_pa, w_pr, w_out):
    nb, seq, d = x_prompt.shape
    ns = x_sample.shape[0]
    dff = ffn1_w1.shape[2]
    assert d == D_MODEL and w_ada.shape[0] == 1 and x_sample.shape[1] == 1 and cache_k.shape[2] == WINDOW
    assert seq % WINDOW == 0 and w_in.shape[2] == D_QKV + D_Z

    group = MXU_DIM // RNN_BLOCK
    wrg = _block_diag(w_rg[0], group).astype(BF16)
    wig = _block_diag(w_ig[0], group).astype(BF16)
    wpa, wpr, wout = w_pa[0].astype(BF16), w_pr[0].astype(BF16), w_out[0].astype(BF16)
    gqk = jnp.concatenate([jnp.tile(g_q[0], N_Q_HEADS), jnp.tile(g_k[0], N_KV_HEADS)]).reshape(1, D_QK)
    head_mean = jnp.kron(jnp.eye(D_QK // HEAD_DIM, dtype=F32),
                         jnp.full((HEAD_DIM, HEAD_DIM), 1.0 / HEAD_DIM, F32)).astype(BF16)
    inv = ROPE_THETA ** (-jnp.arange(HEAD_DIM // 2, dtype=F32) * 2.0 / HEAD_DIM)
    inv_row = jnp.tile(inv, LANES // (HEAD_DIM // 2)).reshape(1, LANES)
    sink_vec = sinks[0]
    lam = lru_lambda[0]

    pad_rows = (-(nb + ns)) % SUBLANES
    c_all = jnp.concatenate([c_prompt, c_sample, jnp.zeros((pad_rows, d), F32)], axis=0)
    mod_all = _ada(c_all, w_ada[0], b_ada[0], 1024)
    mod_p = mod_all[:nb].reshape(nb, 1, N_MOD * d)
    mod_s = mod_all[nb:nb + ns]

    def trunk(x, mod, rows_per_mod, cs, sn, table_blocks, t):
        h1 = _modulated_norm(x, mod, rows_per_mod, g_norm_ffn1[0], 0, t["tm_norm"])
        g1 = _ffn_up(h1, ffn1_w1, ffn1_w3, t["tm_up"], t["tf"])
        x1 = _ffn_down(g1, ffn1_w2, x, mod, rows_per_mod, 2, t["tm_down"], t["tn_down"])
        hm = _modulated_norm(x1, mod, rows_per_mod, g_norm_mix[0], 3, t["tm_norm"])
        z = _inproj_z(hm, w_in, t["tm_z"], t["tn_z"])
        q, kx, vx, kf, vf = _inproj_qkv(hm, w_in, cs, sn, gqk, head_mean, t["tm_qkv"], table_blocks)
        return x1, z, q, kx, vx, kf, vf

    def tail(o_att, o_rnn, z, x1, mod, rows_per_mod, t):
        x2, h2 = _mix(o_att, o_rnn, z, x1, mod, rows_per_mod, g_norm_ffn2[0], wpa, wpr, wout, 5, 6, t["tm_mix"])
        g2 = _ffn_up(h2, ffn2_w1, ffn2_w3, t["tm_up"], t["tf"])
        return _ffn_down(g2, ffn2_w2, x2, mod, rows_per_mod, 8, t["tm_down"], t["tn_down"])

    tp = _tiles(nb * seq, dff)
    tp = {k: (min(v, seq) if k.startswith("tm") else v) for k, v in tp.items()}
    cs_p, sn_p = _rope_tables(inv_row, 0, seq, tp["tm_qkv"])
    x1, z, q, kx, vx, kf, vf = trunk(x_prompt.reshape(nb * seq, d), mod_p, seq, cs_p, sn_p,
                                     seq // tp["tm_qkv"], tp)
    nblk = seq // WINDOW
    maps = (lambda b, n: (b * nblk + n, 0), lambda b, n: (b * nblk + jnp.maximum(n - 1, 0), 0))
    o_att = _attn(sink_vec, q, kx, kx, vx, vx, (nb, nblk), 1, WINDOW, maps, True)
    o_rnn, h_p, conv_p = _rnn_prompt(z, nb, seq, conv_w[0], conv_b[0], wrg, wig, b_rg[0], b_ig[0], lam,
                                     min(512, seq))
    y_p = tail(o_att, o_rnn, z, x1, mod_p, seq, tp)

    last = lambda a: a.reshape(nb, seq, D_KV)[:, seq - WINDOW:].reshape(1, nb, WINDOW, N_KV_HEADS, HEAD_DIM)
    k_prompt, v_prompt = last(kf), last(vf)

    ts = _tiles(ns, dff)
    cs_1, sn_1 = _rope_tables(inv_row, PAST_LEN, SUBLANES, SUBLANES)
    cs_s = jnp.broadcast_to(cs_1[0:1], (ns, LANES))
    sn_s = jnp.broadcast_to(sn_1[0:1], (ns, LANES))
    x1s, zs, qs, kxs, vxs, kfs, vfs = trunk(x_sample.reshape(ns, d), mod_s, 1, cs_s, sn_s, 1, ts)
    tq = 2 * SUBLANES
    padq = lambda a: jnp.pad(a[:, None, :], ((0, 0), (0, tq - 1), (0, 0))).reshape(ns * tq, a.shape[-1])

    def cache_pairs(c):
        zero = jnp.zeros_like(c)
        lo = jnp.concatenate([c, zero], axis=-1)
        hi = jnp.concatenate([zero, c], axis=-1)
        return jnp.stack([lo, hi], axis=3).reshape(ns * WINDOW, DX).astype(BF16)

    nsub = 8 if ns % 8 == 0 else 1
    maps_s = (lambda b, n: (b, 0), lambda b, n: (b, 0))
    o_att_s = _attn(sink_vec, padq(qs), cache_pairs(cache_k[0]), padq(kxs), cache_pairs(cache_v[0]), padq(vxs),
                    (ns // nsub, 1), nsub, tq, maps_s, False)
    o_att_s = o_att_s.reshape(ns, tq, D_ATTN)[:, 0]
    conv_s_in = state_conv[0]
    o_rnn_s, h_s = _rnn_step(zs, conv_s_in, state_h[0], conv_w[0], conv_b[0], wrg, wig, b_rg[0], b_ig[0], lam)
    y_s = tail(o_att_s, o_rnn_s, zs, x1s, mod_s, 1, ts)

    k_sample = kfs.reshape(1, ns, 1, N_KV_HEADS, HEAD_DIM)
    v_sample = vfs.reshape(1, ns, 1, N_KV_HEADS, HEAD_DIM)
    conv_sample = jnp.concatenate([conv_s_in[:, 1:], zs[:, None, Z_RX:Z_RX + D_RNN]], axis=1)[None]

    return (y_p.reshape(nb, seq, d), y_s.reshape(ns, 1, d), k_prompt, v_prompt, k_sample, v_sample,
            h_p.reshape(1, nb, D_RNN), h_s[None], conv_p[None], conv_sample)
```

```python
import functools

import jax
import jax.numpy as jnp
from jax import lax
from jax.experimental import pallas as pl
from jax.experimental.pallas import tpu as pltpu

F32 = jnp.float32
BF16 = jnp.bfloat16

D_MODEL = 2048
HEAD_DIM = 64
N_Q_HEADS = 16
N_KV_HEADS = 4
Q_PER_KV = N_Q_HEADS // N_KV_HEADS
D_ATTN = N_Q_HEADS * HEAD_DIM
D_KV = N_KV_HEADS * HEAD_DIM
D_QK = D_ATTN + D_KV
D_QKV = D_QK + D_KV
WINDOW = 128
ROPE_THETA = 10000.0
D_RNN = 1024
N_RNN_BLOCKS = 16
RNN_BLOCK = D_RNN // N_RNN_BLOCKS
CONV_WIDTH = 4
LRU_C = 8.0
N_MOD = 9
EPS = 1e-6
PAST_LEN = 16384

LANES = 128
SUBLANES = 8
MXU_DIM = 256
VMEM_LIMIT = 56 * 1024 * 1024

Z_RX = 0
Z_RG = Z_RX + D_RNN
Z_GATT = Z_RG + D_RNN
Z_GRNN = Z_GATT + D_MODEL
D_Z = Z_GRNN + D_MODEL
DX = 2 * N_KV_HEADS * LANES


def _params(*sem):
    return pltpu.CompilerParams(dimension_semantics=sem, vmem_limit_bytes=VMEM_LIMIT)


def _silu(x):
    return x * jax.nn.sigmoid(x)


def _modulate(x, g, shift, scale):
    ms = jnp.mean(x * x, axis=-1, keepdims=True)
    return (x * lax.rsqrt(ms + EPS) * g) * (1.0 + scale) + shift


def _mod_spec(mod, tm, rows_per_mod, width, col, ngrid):
    colf = col if callable(col) else (lambda *idx: col)
    if mod.ndim == 3:
        return pl.BlockSpec((None, 1, width), lambda *idx: ((idx[ngrid - 1] * tm) // rows_per_mod, 0, colf(*idx)))
    return pl.BlockSpec((tm, width), lambda *idx: (idx[ngrid - 1], colf(*idx)))


def _ada_kernel(c_ref, w_ref, b_ref, o_ref):
    s = _silu(c_ref[...]).astype(BF16)
    o_ref[...] = jnp.dot(s, w_ref[...].astype(BF16), preferred_element_type=F32) + b_ref[...]


def _ada(c_all, w_ada, b_ada, tn):
    rows, d = c_all.shape
    n = w_ada.shape[1]
    return pl.pallas_call(
        _ada_kernel,
        grid=(n // tn,),
        in_specs=[pl.BlockSpec((rows, d), lambda j: (0, 0)),
                  pl.BlockSpec((d, tn), lambda j: (0, j)),
                  pl.BlockSpec((1, tn), lambda j: (0, j))],
        out_specs=pl.BlockSpec((rows, tn), lambda j: (0, j)),
        out_shape=jax.ShapeDtypeStruct((rows, n), F32),
        compiler_params=_params("arbitrary"),
        name="ada",
    )(c_all, w_ada, b_ada.reshape(1, n))


def _rope_kernel(inv_ref, cs_ref, sn_ref, *, base, blk):
    r = pl.program_id(0)
    pos = (base + r * blk + lax.broadcasted_iota(jnp.int32, (blk, LANES), 0)).astype(F32)
    ang = pos * inv_ref[...]
    lane = lax.broadcasted_iota(jnp.int32, (blk, LANES), 1)
    first_half = (lane & (HEAD_DIM - 1)) < HEAD_DIM // 2
    sn = jnp.sin(ang)
    cs_ref[...] = jnp.cos(ang)
    sn_ref[...] = jnp.where(first_half, -sn, sn)


def _rope_tables(inv_row, base, rows, blk):
    out = jax.ShapeDtypeStruct((rows, LANES), F32)
    return pl.pallas_call(
        functools.partial(_rope_kernel, base=base, blk=blk),
        grid=(rows // blk,),
        in_specs=[pl.BlockSpec((1, LANES), lambda r: (0, 0))],
        out_specs=[pl.BlockSpec((blk, LANES), lambda r: (r, 0))] * 2,
        out_shape=[out, out],
        compiler_params=_params("arbitrary"),
        name="rope_tables",
    )(inv_row)


def _modulate_kernel(x_ref, shift_ref, scale_ref, g_ref, h_ref):
    h_ref[...] = _modulate(x_ref[...], g_ref[...], shift_ref[...], scale_ref[...]).astype(BF16)


def _modulated_norm(x, mod, rows_per_mod, g, k_shift, tm):
    m_rows, d = x.shape
    return pl.pallas_call(
        _modulate_kernel,
        grid=(m_rows // tm,),
        in_specs=[pl.BlockSpec((tm, d), lambda m: (m, 0)),
                  _mod_spec(mod, tm, rows_per_mod, d, k_shift, 1),
                  _mod_spec(mod, tm, rows_per_mod, d, k_shift + 1, 1),
                  pl.BlockSpec((1, d), lambda m: (0, 0))],
        out_specs=pl.BlockSpec((tm, d), lambda m: (m, 0)),
        out_shape=jax.ShapeDtypeStruct((m_rows, d), BF16),
        compiler_params=_params("arbitrary"),
        name="modnorm",
    )(x, mod, mod, g.reshape(1, d))


def _ffn_up_kernel(h_ref, w1_ref, w3_ref, g_ref, w1b_ref, w3b_ref):
    @pl.when(pl.program_id(1) == 0)
    def _():
        w1b_ref[...] = w1_ref[...].astype(BF16)
        w3b_ref[...] = w3_ref[...].astype(BF16)

    h = h_ref[...]
    a = jnp.dot(h, w1b_ref[...], preferred_element_type=F32)
    b = jnp.dot(h, w3b_ref[...], preferred_element_type=F32)
    g_ref[...] = (_silu(a) * b).astype(BF16)


def _ffn_up(h, w1, w3, tm, tf):
    m_rows, d = h.shape
    dff = w1.shape[2]
    wspec = pl.BlockSpec((None, d, tf), lambda f, m: (0, 0, f))
    return pl.pallas_call(
        _ffn_up_kernel,
        grid=(dff // tf, m_rows // tm),
        in_specs=[pl.BlockSpec((tm, d), lambda f, m: (m, 0)), wspec, wspec],
        out_specs=pl.BlockSpec((tm, tf), lambda f, m: (m, f)),
        out_shape=jax.ShapeDtypeStruct((m_rows, dff), BF16),
        scratch_shapes=[pltpu.VMEM((d, tf), BF16), pltpu.VMEM((d, tf), BF16)],
        compiler_params=_params("arbitrary", "arbitrary"),
        name="ffn_up",
    )(h, w1, w3)


def _ffn_down_kernel(g_ref, w2_ref, x_ref, gate_ref, o_ref, w2b_ref):
    @pl.when(pl.program_id(1) == 0)
    def _():
        w2b_ref[...] = w2_ref[...].astype(BF16)

    acc = jnp.dot(g_ref[...], w2b_ref[...], preferred_element_type=F32)
    o_ref[...] = x_ref[...] + 0.5 * gate_ref[...] * acc


def _ffn_down(g, w2, x, mod, rows_per_mod, k_gate, tm, tn):
    m_rows, d = x.shape
    dff = g.shape[1]
    gate_col = lambda n, m: k_gate * (d // tn) + n
    return pl.pallas_call(
        _ffn_down_kernel,
        grid=(d // tn, m_rows // tm),
        in_specs=[pl.BlockSpec((tm, dff), lambda n, m: (m, 0)),
                  pl.BlockSpec((None, dff, tn), lambda n, m: (0, 0, n)),
                  pl.BlockSpec((tm, tn), lambda n, m: (m, n)),
                  _mod_spec(mod, tm, rows_per_mod, tn, gate_col, 2)],
        out_specs=pl.BlockSpec((tm, tn), lambda n, m: (m, n)),
        out_shape=jax.ShapeDtypeStruct((m_rows, d), F32),
        scratch_shapes=[pltpu.VMEM((dff, tn), BF16)],
        compiler_params=_params("arbitrary", "arbitrary"),
        name="ffn_down",
    )(g, w2, x, mod)


def _inproj_z_kernel(h_ref, w_ref, z_ref, wb_ref):
    @pl.when(pl.program_id(1) == 0)
    def _():
        wb_ref[...] = w_ref[...].astype(BF16)

    z_ref[...] = jnp.dot(h_ref[...], wb_ref[...], preferred_element_type=F32)


def _inproj_z(h, w_in, tm, tn):
    m_rows, d = h.shape
    first = D_QKV // tn
    return pl.pallas_call(
        _inproj_z_kernel,
        grid=(D_Z // tn, m_rows // tm),
        in_specs=[pl.BlockSpec((tm, d), lambda n, m: (m, 0)),
                  pl.BlockSpec((None, d, tn), lambda n, m: (0, 0, first + n))],
        out_specs=pl.BlockSpec((tm, tn), lambda n, m: (m, n)),
        out_shape=jax.ShapeDtypeStruct((m_rows, D_Z), F32),
        scratch_shapes=[pltpu.VMEM((d, tn), BF16)],
        compiler_params=_params("arbitrary", "arbitrary"),
        name="inproj_z",
    )(h, w_in)


def _store_head_pairs(ref, rows, chunk, c):
    lo = lax.broadcasted_iota(jnp.int32, chunk.shape, 1) < HEAD_DIM
    swapped = pltpu.roll(chunk, HEAD_DIM, 1)
    zero = jnp.zeros_like(chunk)
    cols = (jnp.where(lo, chunk, zero), jnp.where(lo, zero, swapped),
            jnp.where(lo, swapped, zero), jnp.where(lo, zero, chunk))
    for i, col in enumerate(cols):
        ref[rows, (4 * c + i) * LANES:(4 * c + i + 1) * LANES] = col.astype(ref.dtype)


def _inproj_qkv_kernel(h_ref, w_ref, cs_ref, sn_ref, gqk_ref, red_ref, exp_ref,
                       q_ref, kx_ref, vx_ref, kf_ref, vf_ref, wb_ref, *, nsplit):
    @pl.when(pl.program_id(0) == 0)
    def _():
        wb_ref[...] = w_ref[...].astype(BF16)

    step = h_ref.shape[0] // nsplit
    for r in range(nsplit):
        rows = slice(r * step, (r + 1) * step)
        acc = jnp.dot(h_ref[rows, :], wb_ref[...], preferred_element_type=F32)
        qk = acc[:, :D_QK]
        hm = jnp.dot((qk * qk).astype(BF16), red_ref[...], preferred_element_type=F32)
        hi = hm.astype(BF16)
        lo = (hm - hi.astype(F32)).astype(BF16)
        ms = jnp.dot(jnp.concatenate([hi, lo], axis=1), exp_ref[...], preferred_element_type=F32)
        y = qk * lax.rsqrt(ms + EPS) * gqk_ref[...]
        cs = cs_ref[rows, :]
        sn = sn_ref[rows, :]
        lane = lax.broadcasted_iota(jnp.int32, cs.shape, 1)
        first_half = (lane & (HEAD_DIM - 1)) < HEAD_DIM // 2
        for c in range(D_QK // LANES):
            yc = y[:, c * LANES:(c + 1) * LANES]
            partner = jnp.where(first_half,
                                pltpu.roll(yc, LANES - HEAD_DIM // 2, 1),
                                pltpu.roll(yc, HEAD_DIM // 2, 1))
            rot = yc * cs + partner * sn
            if c < D_ATTN // LANES:
                q_ref[rows, c * LANES:(c + 1) * LANES] = (rot * (HEAD_DIM ** -0.5)).astype(BF16)
            else:
                kf_ref[rows, c * LANES - D_ATTN:(c + 1) * LANES - D_ATTN] = rot
                _store_head_pairs(kx_ref, rows, rot, c - D_ATTN // LANES)
        v = acc[:, D_QK:]
        vf_ref[rows, :] = v
        for c in range(D_KV // LANES):
            _store_head_pairs(vx_ref, rows, v[:, c * LANES:(c + 1) * LANES], c)


def _inproj_qkv(h, w_in, cs, sn, gqk, head_reduce, head_expand, tm, table_blocks):
    m_rows, d = h.shape
    row = lambda width: pl.BlockSpec((tm, width), lambda m: (m, 0))
    table = pl.BlockSpec((tm, LANES), lambda m: (m % table_blocks, 0))
    nsplit = 2 if tm % (2 * MXU_DIM) == 0 else 1
    return pl.pallas_call(
        functools.partial(_inproj_qkv_kernel, nsplit=nsplit),
        grid=(m_rows // tm,),
        in_specs=[row(d),
                  pl.BlockSpec((None, d, D_QKV), lambda m: (0, 0, 0), pipeline_mode=pl.Buffered(1)),
                  table, table,
                  pl.BlockSpec((1, D_QK), lambda m: (0, 0)),
                  pl.BlockSpec(head_reduce.shape, lambda m: (0, 0)),
                  pl.BlockSpec(head_expand.shape, lambda m: (0, 0))],
        out_specs=[row(D_ATTN), row(DX), row(DX), row(D_KV), row(D_KV)],
        out_shape=[jax.ShapeDtypeStruct((m_rows, D_ATTN), BF16),
                   jax.ShapeDtypeStruct((m_rows, DX), BF16),
                   jax.ShapeDtypeStruct((m_rows, DX), BF16),
                   jax.ShapeDtypeStruct((m_rows, D_KV), F32),
                   jax.ShapeDtypeStruct((m_rows, D_KV), F32)],
        scratch_shapes=[pltpu.VMEM((d, D_QKV), BF16)],
        compiler_params=_params("arbitrary"),
        name="inproj_qkv",
    )(h, w_in, cs, sn, gqk, head_reduce, head_expand)


def _attn_bias(bias_ref, tq, past_off):
    nk = 2 * WINDOW
    ri = lax.broadcasted_iota(jnp.int32, (2 * tq, 2 * nk), 0) & (tq - 1)
    kj = lax.broadcasted_iota(jnp.int32, (2 * tq, 2 * nk), 1) & (nk - 1)
    visible = ((kj < WINDOW) & (kj > ri + past_off)) | ((kj >= WINDOW) & (kj - WINDOW <= ri))
    bias_ref[...] = jnp.where(visible, 0.0, -jnp.inf)


def _attn_blocks(sink_ref, bias_ref, q_ref, kp_ref, kc_ref, vp_ref, vc_ref, nsub, tq, store):
    nk = 2 * WINDOW
    rows = 2 * tq
    first_rows = lax.broadcasted_iota(jnp.int32, (rows, 1), 0) < tq
    lo_lanes = lax.broadcasted_iota(jnp.int32, (rows, LANES), 1) < HEAD_DIM
    ones_lo = (lax.broadcasted_iota(jnp.int32, (nk, LANES), 1) < HEAD_DIM).astype(BF16)
    ones_hi = (1 - ones_lo.astype(F32)).astype(BF16)
    nt = (((1,), (1,)), ((), ()))

    def keys(p_ref, c_ref, s, col):
        cols = slice(col * LANES, (col + 1) * LANES)
        parts = [p_ref[s * WINDOW:(s + 1) * WINDOW, cols], c_ref[s * tq:(s + 1) * tq, cols]]
        if tq < WINDOW:
            parts.append(jnp.zeros((WINDOW - tq, LANES), BF16))
        return parts

    for s in range(nsub):
        for g in range(N_KV_HEADS):
            c0, c1 = 2 * g, 2 * g + 1
            q4 = jnp.concatenate([q_ref[s * tq:(s + 1) * tq, c0 * LANES:(c0 + 1) * LANES],
                                  q_ref[s * tq:(s + 1) * tq, c1 * LANES:(c1 + 1) * LANES]], axis=0)
            kk = jnp.concatenate(keys(kp_ref, kc_ref, s, c0) + keys(kp_ref, kc_ref, s, c1), axis=0)
            sc = lax.dot_general(q4, kk, nt, preferred_element_type=F32) + bias_ref[...]
            sk_lo = jnp.where(first_rows, sink_ref[4 * g], sink_ref[4 * g + 2])
            sk_hi = jnp.where(first_rows, sink_ref[4 * g + 1], sink_ref[4 * g + 3])
            m_lo = jnp.maximum(jnp.max(sc[:, :nk], axis=-1, keepdims=True), sk_lo)
            m_hi = jnp.maximum(jnp.max(sc[:, nk:], axis=-1, keepdims=True), sk_hi)
            p = jnp.concatenate([jnp.exp(sc[:, :nk] - m_lo), jnp.exp(sc[:, nk:] - m_hi)], axis=1).astype(BF16)
            vv = jnp.concatenate(
                [jnp.concatenate([jnp.concatenate(keys(vp_ref, vc_ref, s, c0), axis=0), ones_lo], axis=1),
                 jnp.concatenate([jnp.concatenate(keys(vp_ref, vc_ref, s, c1), axis=0), ones_hi], axis=1)],
                axis=0)
            o = jnp.dot(p, vv, preferred_element_type=F32)
            denom = o[:, LANES:] + jnp.where(lo_lanes, jnp.exp(sk_lo - m_lo), jnp.exp(sk_hi - m_hi))
            store(s, c0, c1, o[:, :LANES] / denom)


def _attn_kernel(sink_ref, q_ref, kp_ref, kc_ref, vp_ref, vc_ref, o_ref, bias_ref):
    tq = WINDOW
    _attn_bias(bias_ref, tq, jnp.where(pl.program_id(1) > 0, 0, WINDOW))

    def store(s, c0, c1, out):
        o_ref[:, c0 * LANES:(c0 + 1) * LANES] = out[:tq].astype(BF16)
        o_ref[:, c1 * LANES:(c1 + 1) * LANES] = out[tq:].astype(BF16)

    _attn_blocks(sink_ref, bias_ref, q_ref, kp_ref, kc_ref, vp_ref, vc_ref, 1, tq, store)


def _attn(sinks, q, kx, vx, nbatch, nblk):
    cur = lambda b, n: (b * nblk + n, 0)
    past = lambda b, n: (b * nblk + jnp.maximum(n - 1, 0), 0)
    kv_cur = pl.BlockSpec((WINDOW, DX), cur)
    kv_past = pl.BlockSpec((WINDOW, DX), past)
    return pl.pallas_call(
        _attn_kernel,
        grid=(nbatch, nblk),
        in_specs=[pl.BlockSpec(memory_space=pltpu.SMEM), pl.BlockSpec((WINDOW, D_ATTN), cur),
                  kv_past, kv_cur, kv_past, kv_cur],
        out_specs=pl.BlockSpec((WINDOW, D_ATTN), cur),
        out_shape=jax.ShapeDtypeStruct(q.shape, BF16),
        scratch_shapes=[pltpu.VMEM((2 * WINDOW, 4 * WINDOW), F32)],
        compiler_params=_params("arbitrary", "arbitrary"),
        name="attn",
    )(sinks, q, kx, kx, vx, vx)


def _attn_step_kernel(sink_ref, q_ref, kn_ref, vn_ref, ck_ref, cv_ref, o_ref,
                      bias_ref, qx_ref, kcx_ref, vcx_ref, kpx_ref, vpx_ref, *, nsub, tq):
    _attn_bias(bias_ref, tq, 0)
    for c in range(D_KV // LANES):
        _store_head_pairs(kpx_ref, slice(None), ck_ref[:, c * LANES:(c + 1) * LANES], c)
        _store_head_pairs(vpx_ref, slice(None), cv_ref[:, c * LANES:(c + 1) * LANES], c)
    for src, dst in ((q_ref, qx_ref), (kn_ref, kcx_ref), (vn_ref, vcx_ref)):
        rows = src[...].astype(F32)
        for s in range(nsub):
            dst[s * tq:(s + 1) * tq, :] = jnp.broadcast_to(rows[s:s + 1, :], (tq, rows.shape[1])).astype(BF16)

    def store(s, c0, c1, out):
        o_ref[s:s + 1, c0 * LANES:(c0 + 1) * LANES] = out[0:1]
        o_ref[s:s + 1, c1 * LANES:(c1 + 1) * LANES] = out[tq:tq + 1]

    _attn_blocks(sink_ref, bias_ref, qx_ref, kpx_ref, kcx_ref, vpx_ref, vcx_ref, nsub, tq, store)


def _attn_step(sinks, q, kx_new, vx_new, cache_k, cache_v, nsub):
    ns = q.shape[0]
    tq = 2 * SUBLANES
    row = lambda width: pl.BlockSpec((nsub, width), lambda b: (b, 0))
    cache = pl.BlockSpec((nsub * WINDOW, D_KV), lambda b: (b, 0))
    return pl.pallas_call(
        functools.partial(_attn_step_kernel, nsub=nsub, tq=tq),
        grid=(ns // nsub,),
        in_specs=[pl.BlockSpec(memory_space=pltpu.SMEM), row(D_ATTN), row(DX), row(DX), cache, cache],
        out_specs=row(D_ATTN),
        out_shape=jax.ShapeDtypeStruct((ns, D_ATTN), F32),
        scratch_shapes=[pltpu.VMEM((2 * tq, 4 * WINDOW), F32),
                        pltpu.VMEM((nsub * tq, D_ATTN), BF16),
                        pltpu.VMEM((nsub * tq, DX), BF16),
                        pltpu.VMEM((nsub * tq, DX), BF16),
                        pltpu.VMEM((nsub * WINDOW, DX), BF16),
                        pltpu.VMEM((nsub * WINDOW, DX), BF16)],
        compiler_params=_params("arbitrary"),
        name="attn_step",
    )(sinks, q, kx_new, vx_new, cache_k, cache_v)


def _softplus(x):
    return jnp.maximum(x, 0.0) + jnp.log1p(jnp.exp(-jnp.abs(x)))


def _lru_coeffs(xc, wrg_ref, wig_ref, brg, big, lam):
    xb = xc.astype(BF16)
    ngroups = D_RNN // MXU_DIM
    rs, igs = [], []
    for c in range(ngroups):
        xg = xb[:, c * MXU_DIM:(c + 1) * MXU_DIM]
        rs.append(jnp.dot(xg, wrg_ref[c], preferred_element_type=F32))
        igs.append(jnp.dot(xg, wig_ref[c], preferred_element_type=F32))
    r = jax.nn.sigmoid(jnp.concatenate(rs, axis=1) + brg)
    ig = jax.nn.sigmoid(jnp.concatenate(igs, axis=1) + big)
    log_a = -LRU_C * r * _softplus(-lam)
    a = jnp.exp(log_a)
    one_minus_a2 = -jnp.tanh(log_a) * (1.0 + a * a)
    u = jnp.sqrt(one_minus_a2) * (ig * xc)
    return a, u


def _rnn_prompt_kernel(rx_ref, rg_ref, cw_ref, cb_ref, wrg_ref, wig_ref, brg_ref, big_ref, lam_ref,
                       o_ref, hlast_ref, conv_ref, xs_ref, a_ref, h_ref, carry_ref, *, tc):
    t = pl.program_id(1)
    pad = SUBLANES

    @pl.when(t == 0)
    def _():
        xs_ref[0:pad, :] = jnp.zeros((pad, D_RNN), F32)
        carry_ref[...] = jnp.zeros_like(carry_ref)

    x = rx_ref[...]
    xs_ref[pad:pad + tc, :] = x
    xc = cb_ref[...] + x * cw_ref[CONV_WIDTH - 1:CONV_WIDTH, :]
    for k in range(1, CONV_WIDTH):
        xc = xc + xs_ref[pad - k:pad - k + tc, :] * cw_ref[CONV_WIDTH - 1 - k:CONV_WIDTH - k, :]
    tail = xs_ref[tc:tc + pad, :]
    xs_ref[0:pad, :] = tail
    conv_ref[...] = tail[pad - (CONV_WIDTH - 1):, :]

    a, u = _lru_coeffs(xc, wrg_ref, wig_ref, brg_ref[...], big_ref[...], lam_ref[...])
    a_ref[...] = a
    h_ref[...] = u

    row = lax.broadcasted_iota(jnp.int32, (SUBLANES, D_RNN), 0)

    def body(r, carry):
        off = pl.multiple_of(r * SUBLANES, SUBLANES)
        av = a_ref[pl.ds(off, SUBLANES), :]
        hv = h_ref[pl.ds(off, SUBLANES), :]
        for sft in (1, 2, 4):
            keep = row >= sft
            a_sh = jnp.where(keep, pltpu.roll(av, sft, 0), 1.0)
            h_sh = jnp.where(keep, pltpu.roll(hv, sft, 0), 0.0)
            hv = av * h_sh + hv
            av = av * a_sh
        hv = hv + av * carry
        h_ref[pl.ds(off, SUBLANES), :] = hv
        return jnp.broadcast_to(hv[SUBLANES - 1:SUBLANES, :], (SUBLANES, D_RNN))

    carry = lax.fori_loop(0, tc // SUBLANES, body, carry_ref[...])
    carry_ref[...] = carry
    hlast_ref[...] = carry[0:1, :]
    o_ref[...] = (h_ref[...] * jax.nn.gelu(rg_ref[...])).astype(BF16)


def _rnn_prompt(z, nbatch, seq, conv_w, conv_b, wrg, wig, b_rg, b_ig, lam, tc):
    nchunk = seq // tc
    rx_blk = Z_RX // D_RNN
    rg_blk = Z_RG // D_RNN
    vec = pl.BlockSpec((1, D_RNN), lambda b, t: (0, 0))
    wspec = pl.BlockSpec(wrg.shape, lambda b, t: (0, 0, 0))
    return pl.pallas_call(
        functools.partial(_rnn_prompt_kernel, tc=tc),
        grid=(nbatch, nchunk),
        in_specs=[pl.BlockSpec((tc, D_RNN), lambda b, t: (b * nchunk + t, rx_blk)),
                  pl.BlockSpec((tc, D_RNN), lambda b, t: (b * nchunk + t, rg_blk)),
                  pl.BlockSpec((CONV_WIDTH, D_RNN), lambda b, t: (0, 0)),
                  vec, wspec, wspec, vec, vec, vec],
        out_specs=[pl.BlockSpec((tc, D_RNN), lambda b, t: (b * nchunk + t, 0)),
                   pl.BlockSpec((None, 1, D_RNN), lambda b, t: (b, 0, 0)),
                   pl.BlockSpec((None, CONV_WIDTH - 1, D_RNN), lambda b, t: (b, 0, 0))],
        out_shape=[jax.ShapeDtypeStruct((nbatch * seq, D_RNN), BF16),
                   jax.ShapeDtypeStruct((nbatch, 1, D_RNN), F32),
                   jax.ShapeDtypeStruct((nbatch, CONV_WIDTH - 1, D_RNN), F32)],
        scratch_shapes=[pltpu.VMEM((tc + SUBLANES, D_RNN), F32),
                        pltpu.VMEM((tc, D_RNN), F32),
                        pltpu.VMEM((tc, D_RNN), F32),
                        pltpu.VMEM((SUBLANES, D_RNN), F32)],
        compiler_params=_params("arbitrary", "arbitrary"),
        name="rnn_prompt",
    )(z, z, conv_w, conv_b.reshape(1, D_RNN), wrg, wig, b_rg.reshape(1, D_RNN),
      b_ig.reshape(1, D_RNN), lam.reshape(1, D_RNN))


def _rnn_step_kernel(rx_ref, rg_ref, c0_ref, c1_ref, c2_ref, h0_ref, cw_ref, cb_ref, wrg_ref, wig_ref,
                     brg_ref, big_ref, lam_ref, o_ref, h_ref):
    x = rx_ref[...]
    xc = (cb_ref[...] + c0_ref[...] * cw_ref[0:1, :] + c1_ref[...] * cw_ref[1:2, :]
          + c2_ref[...] * cw_ref[2:3, :] + x * cw_ref[3:4, :])
    a, u = _lru_coeffs(xc, wrg_ref, wig_ref, brg_ref[...], big_ref[...], lam_ref[...])
    h = a * h0_ref[...] + u
    h_ref[...] = h
    o_ref[...] = (h * jax.nn.gelu(rg_ref[...])).astype(BF16)


def _rnn_step(z, conv_state, h0, conv_w, conv_b, wrg, wig, b_rg, b_ig, lam):
    rows = z.shape[0]
    full = lambda shape: pl.BlockSpec(shape, lambda i: (0,) * len(shape))
    act = full((rows, D_RNN))
    vec = full((1, D_RNN))
    return pl.pallas_call(
        _rnn_step_kernel,
        grid=(1,),
        in_specs=[pl.BlockSpec((rows, D_RNN), lambda i: (0, Z_RX // D_RNN)),
                  pl.BlockSpec((rows, D_RNN), lambda i: (0, Z_RG // D_RNN)),
                  act, act, act, act, full((CONV_WIDTH, D_RNN)), vec,
                  full(wrg.shape), full(wig.shape), vec, vec, vec],
        out_specs=[act, act],
        out_shape=[jax.ShapeDtypeStruct((rows, D_RNN), BF16),
                   jax.ShapeDtypeStruct((rows, D_RNN), F32)],
        compiler_params=_params("arbitrary"),
        name="rnn_step",
    )(z, z, conv_state[:, 0], conv_state[:, 1], conv_state[:, 2], h0, conv_w,
      conv_b.reshape(1, D_RNN), wrg, wig, b_rg.reshape(1, D_RNN), b_ig.reshape(1, D_RNN),
      lam.reshape(1, D_RNN))


def _mix_kernel(oa_ref, or_ref, ga_ref, gr_ref, x_ref, gate_ref, shift_ref, scale_ref, g_ref,
                wpa_ref, wpr_ref, wout_ref, o_ref, h_ref):
    pa = jnp.dot(oa_ref[...].astype(BF16), wpa_ref[...], preferred_element_type=F32)
    pr = jnp.dot(or_ref[...], wpr_ref[...], preferred_element_type=F32)
    mix = jax.nn.sigmoid(ga_ref[...]) * pa + jax.nn.sigmoid(gr_ref[...]) * pr
    x = x_ref[...] + gate_ref[...] * jnp.dot(mix.astype(BF16), wout_ref[...], preferred_element_type=F32)
    o_ref[...] = x
    h_ref[...] = _modulate(x, g_ref[...], shift_ref[...], scale_ref[...]).astype(BF16)


def _mix(o_att, o_rnn, z, x, mod, rows_per_mod, g_next, w_pa, w_pr, w_out, k_gate, k_shift_next, tm):
    m_rows, d = x.shape
    const = lambda shape: pl.BlockSpec(shape, lambda m: (0, 0), pipeline_mode=pl.Buffered(1))
    row = lambda width, col: pl.BlockSpec((tm, width), lambda m: (m, col))
    mspec = lambda k: _mod_spec(mod, tm, rows_per_mod, d, k, 1)
    return pl.pallas_call(
        _mix_kernel,
        grid=(m_rows // tm,),
        in_specs=[row(D_ATTN, 0), row(D_RNN, 0), row(d, Z_GATT // d), row(d, Z_GRNN // d), row(d, 0),
                  mspec(k_gate), mspec(k_shift_next), mspec(k_shift_next + 1), const((1, d)),
                  const(w_pa.shape), const(w_pr.shape), const(w_out.shape)],
        out_specs=[row(d, 0), row(d, 0)],
        out_shape=[jax.ShapeDtypeStruct((m_rows, d), F32), jax.ShapeDtypeStruct((m_rows, d), BF16)],
        compiler_params=_params("arbitrary"),
        name="mix",
    )(o_att, o_rnn, z, z, x, mod, mod, mod, g_next.reshape(1, d), w_pa, w_pr, w_out)


def _block_diag(w, group):
    n, r, _ = w.shape
    eye = jnp.eye(group, dtype=w.dtype)
    wg = w.reshape(n // group, group, r, r)
    return jnp.einsum("ngij,gh->ngihj", wg, eye).reshape(n // group, group * r, group * r)


def _tiles(rows, dff):
    pick = lambda n, prefs: next((t for t in prefs if n % t == 0), n)
    return dict(
        tm_norm=pick(rows, (512,)),
        tm_up=pick(rows, (1024, 512)), tf=pick(dff, (512,)),
        tm_down=pick(rows, (512,)), tn_down=512,
        tm_z=pick(rows, (512,)), tn_z=D_QKV,
        tm_qkv=pick(rows, (512,)),
        tm_mix=pick(rows, (256,)),
    )


def kernel(x_prompt, x_sample, c_prompt, c_sample, cache_k, cache_v, state_h, state_conv, w_ada, b_ada,
           g_norm_ffn1, g_norm_mix, g_norm_ffn2, ffn1_w1, ffn1_w3, ffn1_w2, ffn2_w1, ffn2_w3, ffn2_w2,
           w_in, g_q, g_k, sinks, conv_w, conv_b, w_rg, b_rg, w_ig, b_ig, lru_lambda, w_pa, w_pr, w_out):
    nb, seq, d = x_prompt.shape
    ns = x_sample.shape[0]
    dff = ffn1_w1.shape[2]
    assert d == D_MODEL and w_ada.shape[0] == 1 and x_sample.shape[1] == 1 and cache_k.shape[2] == WINDOW
    assert seq % WINDOW == 0 and w_in.shape[2] == D_QKV + D_Z

    group = MXU_DIM // RNN_BLOCK
    wrg = _block_diag(w_rg[0], group).astype(BF16)
    wig = _block_diag(w_ig[0], group).astype(BF16)
    wpa, wpr, wout = w_pa[0].astype(BF16), w_pr[0].astype(BF16), w_out[0].astype(BF16)
    gqk = jnp.concatenate([jnp.tile(g_q[0], N_Q_HEADS), jnp.tile(g_k[0], N_KV_HEADS)]).reshape(1, D_QK)
    head_of = jnp.arange(D_QK) // HEAD_DIM
    onehot = (head_of[:, None] == jnp.arange(LANES)[None, :]).astype(F32)
    head_reduce = (onehot / HEAD_DIM).astype(BF16)
    head_expand = jnp.concatenate([onehot.T, onehot.T], axis=0).astype(BF16)
    inv = ROPE_THETA ** (-jnp.arange(HEAD_DIM // 2, dtype=F32) * 2.0 / HEAD_DIM)
    inv_row = jnp.tile(inv, LANES // (HEAD_DIM // 2)).reshape(1, LANES)
    sink_vec = sinks[0]
    lam = lru_lambda[0]

    pad_rows = (-(nb + ns)) % SUBLANES
    c_all = jnp.concatenate([c_prompt, c_sample, jnp.zeros((pad_rows, d), F32)], axis=0)
    mod_all = _ada(c_all, w_ada[0], b_ada[0], 1024)
    mod_p = mod_all[:nb].reshape(nb, 1, N_MOD * d)
    mod_s = mod_all[nb:nb + ns]

    def trunk(x, mod, rows_per_mod, cs, sn, table_blocks, t):
        h1 = _modulated_norm(x, mod, rows_per_mod, g_norm_ffn1[0], 0, t["tm_norm"])
        g1 = _ffn_up(h1, ffn1_w1, ffn1_w3, t["tm_up"], t["tf"])
        x1 = _ffn_down(g1, ffn1_w2, x, mod, rows_per_mod, 2, t["tm_down"], t["tn_down"])
        hm = _modulated_norm(x1, mod, rows_per_mod, g_norm_mix[0], 3, t["tm_norm"])
        z = _inproj_z(hm, w_in, t["tm_z"], t["tn_z"])
        q, kx, vx, kf, vf = _inproj_qkv(hm, w_in, cs, sn, gqk, head_reduce, head_expand, t["tm_qkv"],
                                        table_blocks)
        return x1, z, q, kx, vx, kf, vf

    def tail(o_att, o_rnn, z, x1, mod, rows_per_mod, t):
        x2, h2 = _mix(o_att, o_rnn, z, x1, mod, rows_per_mod, g_norm_ffn2[0], wpa, wpr, wout, 5, 6, t["tm_mix"])
        g2 = _ffn_up(h2, ffn2_w1, ffn2_w3, t["tm_up"], t["tf"])
        return _ffn_down(g2, ffn2_w2, x2, mod, rows_per_mod, 8, t["tm_down"], t["tn_down"])

    tp = _tiles(nb * seq, dff)
    tp = {k: (min(v, seq) if k.startswith("tm") else v) for k, v in tp.items()}
    cs_p, sn_p = _rope_tables(inv_row, 0, seq, tp["tm_qkv"])
    x1, z, q, kx, vx, kf, vf = trunk(x_prompt.reshape(nb * seq, d), mod_p, seq, cs_p, sn_p,
                                     seq // tp["tm_qkv"], tp)
    o_att = _attn(sink_vec, q, kx, vx, nb, seq // WINDOW)
    o_rnn, h_p, conv_p = _rnn_prompt(z, nb, seq, conv_w[0], conv_b[0], wrg, wig, b_rg[0], b_ig[0], lam,
                                     min(512, seq))
    y_p = tail(o_att, o_rnn, z, x1, mod_p, seq, tp)

    last = lambda a: a.reshape(nb, seq, D_KV)[:, seq - WINDOW:].reshape(1, nb, WINDOW, N_KV_HEADS, HEAD_DIM)
    k_prompt, v_prompt = last(kf), last(vf)

    ts = _tiles(ns, dff)
    cs_1, sn_1 = _rope_tables(inv_row, PAST_LEN, SUBLANES, SUBLANES)
    cs_s = jnp.broadcast_to(cs_1[0:1], (ns, LANES))
    sn_s = jnp.broadcast_to(sn_1[0:1], (ns, LANES))
    x1s, zs, qs, kxs, vxs, kfs, vfs = trunk(x_sample.reshape(ns, d), mod_s, 1, cs_s, sn_s, 1, ts)
    nsub = 2 * SUBLANES
    assert ns % nsub == 0
    o_att_s = _attn_step(sink_vec, qs, kxs, vxs, cache_k[0].reshape(ns * WINDOW, D_KV),
                         cache_v[0].reshape(ns * WINDOW, D_KV), nsub)
    conv_s_in = state_conv[0]
    o_rnn_s, h_s = _rnn_step(zs, conv_s_in, state_h[0], conv_w[0], conv_b[0], wrg, wig, b_rg[0], b_ig[0], lam)
    y_s = tail(o_att_s, o_rnn_s, zs, x1s, mod_s, 1, ts)

    k_sample = kfs.reshape(1, ns, 1, N_KV_HEADS, HEAD_DIM)
    v_sample = vfs.reshape(1, ns, 1, N_KV_HEADS, HEAD_DIM)
    conv_sample = jnp.concatenate([conv_s_in[:, 1:], zs[:, None, Z_RX:Z_RX + D_RNN]], axis=1)[None]

    return (y_p.reshape(nb, seq, d), y_s.reshape(ns, 1, d), k_prompt, v_prompt, k_sample, v_sample,
            h_p.reshape(1, nb, D_RNN), h_s[None], conv_p[None], conv_sample)
```

```python
import functools

import jax
import jax.numpy as jnp
from jax import lax
from jax.experimental import pallas as pl
from jax.experimental.pallas import tpu as pltpu

F32 = jnp.float32
BF16 = jnp.bfloat16

D_MODEL = 2048
HEAD_DIM = 64
N_Q_HEADS = 16
N_KV_HEADS = 4
Q_PER_KV = N_Q_HEADS // N_KV_HEADS
D_ATTN = N_Q_HEADS * HEAD_DIM
D_KV = N_KV_HEADS * HEAD_DIM
D_QK = D_ATTN + D_KV
D_QKV = D_QK + D_KV
WINDOW = 128
ROPE_THETA = 10000.0
D_RNN = 1024
N_RNN_BLOCKS = 16
RNN_BLOCK = D_RNN // N_RNN_BLOCKS
CONV_WIDTH = 4
LRU_C = 8.0
N_MOD = 9
EPS = 1e-6
PAST_LEN = 16384

LANES = 128
SUBLANES = 8
MXU_DIM = 256
VMEM_LIMIT = 56 * 1024 * 1024

Z_RX = 0
Z_RG = Z_RX + D_RNN
Z_GATT = Z_RG + D_RNN
Z_GRNN = Z_GATT + D_MODEL
D_Z = Z_GRNN + D_MODEL
DX = 2 * N_KV_HEADS * LANES


def _params(*sem):
    return pltpu.CompilerParams(dimension_semantics=sem, vmem_limit_bytes=VMEM_LIMIT)


def _silu(x):
    return x * jax.nn.sigmoid(x)


def _modulate(x, g, shift, scale):
    ms = jnp.mean(x * x, axis=-1, keepdims=True)
    return (x * lax.rsqrt(ms + EPS) * g) * (1.0 + scale) + shift


def _mod_spec(mod, tm, rows_per_mod, width, col, ngrid):
    colf = col if callable(col) else (lambda *idx: col)
    if mod.ndim == 3:
        return pl.BlockSpec((None, 1, width), lambda *idx: ((idx[ngrid - 1] * tm) // rows_per_mod, 0, colf(*idx)))
    return pl.BlockSpec((tm, width), lambda *idx: (idx[ngrid - 1], colf(*idx)))


def _ada_kernel(c_ref, w_ref, b_ref, o_ref):
    s = _silu(c_ref[...]).astype(BF16)
    o_ref[...] = jnp.dot(s, w_ref[...].astype(BF16), preferred_element_type=F32) + b_ref[...]


def _ada(c_all, w_ada, b_ada, tn):
    rows, d = c_all.shape
    n = w_ada.shape[1]
    return pl.pallas_call(
        _ada_kernel,
        grid=(n // tn,),
        in_specs=[pl.BlockSpec((rows, d), lambda j: (0, 0)),
                  pl.BlockSpec((d, tn), lambda j: (0, j)),
                  pl.BlockSpec((1, tn), lambda j: (0, j))],
        out_specs=pl.BlockSpec((rows, tn), lambda j: (0, j)),
        out_shape=jax.ShapeDtypeStruct((rows, n), F32),
        compiler_params=_params("arbitrary"),
        name="ada",
    )(c_all, w_ada, b_ada.reshape(1, n))


def _rope_kernel(inv_ref, cs_ref, sn_ref, *, base, blk):
    r = pl.program_id(0)
    pos = (base + r * blk + lax.broadcasted_iota(jnp.int32, (blk, LANES), 0)).astype(F32)
    ang = pos * inv_ref[...]
    lane = lax.broadcasted_iota(jnp.int32, (blk, LANES), 1)
    first_half = (lane & (HEAD_DIM - 1)) < HEAD_DIM // 2
    sn = jnp.sin(ang)
    cs_ref[...] = jnp.cos(ang)
    sn_ref[...] = jnp.where(first_half, -sn, sn)


def _rope_tables(inv_row, base, rows, blk):
    out = jax.ShapeDtypeStruct((rows, LANES), F32)
    return pl.pallas_call(
        functools.partial(_rope_kernel, base=base, blk=blk),
        grid=(rows // blk,),
        in_specs=[pl.BlockSpec((1, LANES), lambda r: (0, 0))],
        out_specs=[pl.BlockSpec((blk, LANES), lambda r: (r, 0))] * 2,
        out_shape=[out, out],
        compiler_params=_params("arbitrary"),
        name="rope_tables",
    )(inv_row)


def _modulate_kernel(x_ref, shift_ref, scale_ref, g_ref, h_ref):
    h_ref[...] = _modulate(x_ref[...], g_ref[...], shift_ref[...], scale_ref[...]).astype(BF16)


def _modulated_norm(x, mod, rows_per_mod, g, k_shift, tm):
    m_rows, d = x.shape
    return pl.pallas_call(
        _modulate_kernel,
        grid=(m_rows // tm,),
        in_specs=[pl.BlockSpec((tm, d), lambda m: (m, 0)),
                  _mod_spec(mod, tm, rows_per_mod, d, k_shift, 1),
                  _mod_spec(mod, tm, rows_per_mod, d, k_shift + 1, 1),
                  pl.BlockSpec((1, d), lambda m: (0, 0))],
        out_specs=pl.BlockSpec((tm, d), lambda m: (m, 0)),
        out_shape=jax.ShapeDtypeStruct((m_rows, d), BF16),
        compiler_params=_params("arbitrary"),
        name="modnorm",
    )(x, mod, mod, g.reshape(1, d))


def _is_param(w):
    return w.dtype == F32


def _weight_specs(w, rows, cols, col_map, first=0):
    if _is_param(w):
        return (pl.BlockSpec((None, rows, cols), lambda *idx: (0, 0, first + col_map(*idx))),
                pl.BlockSpec((rows, cols), lambda *idx: (0, col_map(*idx))),
                jax.ShapeDtypeStruct((rows, w.shape[2] - first * cols), BF16))
    return pl.BlockSpec((rows, cols), lambda *idx: (0, col_map(*idx))), None, None


def _round_weights(pairs, row_axis):
    @pl.when(pl.program_id(row_axis) == 0)
    def _():
        for w_ref, wb_ref in pairs:
            wb_ref[...] = w_ref[...].astype(BF16)


def _ffn_up_kernel(*refs, cast):
    if cast:
        h_ref, w1_ref, w3_ref, g_ref, w1b_ref, w3b_ref = refs
        _round_weights(((w1_ref, w1b_ref), (w3_ref, w3b_ref)), 1)
    else:
        h_ref, w1b_ref, w3b_ref, g_ref = refs
    h = h_ref[...]
    a = jnp.dot(h, w1b_ref[...], preferred_element_type=F32)
    b = jnp.dot(h, w3b_ref[...], preferred_element_type=F32)
    g_ref[...] = (_silu(a) * b).astype(BF16)


def _ffn_up(h, w1, w3, tm, tf):
    m_rows, d = h.shape
    cast = _is_param(w1)
    dff = w1.shape[-1]
    w_in_spec, w_out_spec, w_out_shape = _weight_specs(w1, d, tf, lambda f, m: f)
    outs = pl.pallas_call(
        functools.partial(_ffn_up_kernel, cast=cast),
        grid=(dff // tf, m_rows // tm),
        in_specs=[pl.BlockSpec((tm, d), lambda f, m: (m, 0)), w_in_spec, w_in_spec],
        out_specs=[pl.BlockSpec((tm, tf), lambda f, m: (m, f))] + ([w_out_spec] * 2 if cast else []),
        out_shape=[jax.ShapeDtypeStruct((m_rows, dff), BF16)] + ([w_out_shape] * 2 if cast else []),
        compiler_params=_params("arbitrary", "arbitrary"),
        name="ffn_up",
    )(h, w1, w3)
    return tuple(outs) if cast else (outs[0], w1, w3)


def _ffn_down_kernel(*refs, cast):
    if cast:
        g_ref, w2_ref, x_ref, gate_ref, o_ref, w2b_ref = refs
        _round_weights(((w2_ref, w2b_ref),), 1)
    else:
        g_ref, w2b_ref, x_ref, gate_ref, o_ref = refs
    acc = jnp.dot(g_ref[...], w2b_ref[...], preferred_element_type=F32)
    o_ref[...] = x_ref[...] + 0.5 * gate_ref[...] * acc


def _ffn_down(g, w2, x, mod, rows_per_mod, k_gate, tm, tn):
    m_rows, d = x.shape
    dff = g.shape[1]
    cast = _is_param(w2)
    gate_col = lambda n, m: k_gate * (d // tn) + n
    w_in_spec, w_out_spec, w_out_shape = _weight_specs(w2, dff, tn, lambda n, m: n)
    outs = pl.pallas_call(
        functools.partial(_ffn_down_kernel, cast=cast),
        grid=(d // tn, m_rows // tm),
        in_specs=[pl.BlockSpec((tm, dff), lambda n, m: (m, 0)),
                  w_in_spec,
                  pl.BlockSpec((tm, tn), lambda n, m: (m, n)),
                  _mod_spec(mod, tm, rows_per_mod, tn, gate_col, 2)],
        out_specs=[pl.BlockSpec((tm, tn), lambda n, m: (m, n))] + ([w_out_spec] if cast else []),
        out_shape=[jax.ShapeDtypeStruct((m_rows, d), F32)] + ([w_out_shape] if cast else []),
        compiler_params=_params("arbitrary", "arbitrary"),
        name="ffn_down",
    )(g, w2, x, mod)
    return tuple(outs) if cast else (outs[0], w2)


def _inproj_z_kernel(*refs, cast):
    if cast:
        h_ref, w_ref, z_ref, wb_ref = refs
        _round_weights(((w_ref, wb_ref),), 1)
    else:
        h_ref, wb_ref, z_ref = refs
    z_ref[...] = jnp.dot(h_ref[...], wb_ref[...], preferred_element_type=F32)


def _inproj_z(h, w, tm, tn):
    m_rows, d = h.shape
    cast = _is_param(w)
    w_in_spec, w_out_spec, w_out_shape = _weight_specs(w, d, tn, lambda n, m: n, D_QKV // tn if cast else 0)
    outs = pl.pallas_call(
        functools.partial(_inproj_z_kernel, cast=cast),
        grid=(D_Z // tn, m_rows // tm),
        in_specs=[pl.BlockSpec((tm, d), lambda n, m: (m, 0)), w_in_spec],
        out_specs=[pl.BlockSpec((tm, tn), lambda n, m: (m, n))] + ([w_out_spec] if cast else []),
        out_shape=[jax.ShapeDtypeStruct((m_rows, D_Z), F32)] + ([w_out_shape] if cast else []),
        compiler_params=_params("arbitrary", "arbitrary"),
        name="inproj_z",
    )(h, w)
    return tuple(outs) if cast else (outs[0], w)


def _store_head_pairs(ref, rows, chunk, c):
    lo = lax.broadcasted_iota(jnp.int32, chunk.shape, 1) < HEAD_DIM
    swapped = pltpu.roll(chunk, HEAD_DIM, 1)
    zero = jnp.zeros_like(chunk)
    cols = (jnp.where(lo, chunk, zero), jnp.where(lo, zero, swapped),
            jnp.where(lo, swapped, zero), jnp.where(lo, zero, chunk))
    for i, col in enumerate(cols):
        ref[rows, (4 * c + i) * LANES:(4 * c + i + 1) * LANES] = col.astype(ref.dtype)


def _inproj_qkv_kernel(*refs, nsplit, cast):
    if cast:
        h_ref, w_ref, cs_ref, sn_ref, gqk_ref, red_ref, exp_ref, q_ref, kx_ref, vx_ref, kf_ref, vf_ref, wb_ref = refs
        _round_weights(((w_ref, wb_ref),), 0)
    else:
        h_ref, wb_ref, cs_ref, sn_ref, gqk_ref, red_ref, exp_ref, q_ref, kx_ref, vx_ref, kf_ref, vf_ref = refs

    step = h_ref.shape[0] // nsplit
    for r in range(nsplit):
        rows = slice(r * step, (r + 1) * step)
        acc = jnp.dot(h_ref[rows, :], wb_ref[...], preferred_element_type=F32)
        qk = acc[:, :D_QK]
        hm = jnp.dot((qk * qk).astype(BF16), red_ref[...], preferred_element_type=F32)
        hi = hm.astype(BF16)
        lo = (hm - hi.astype(F32)).astype(BF16)
        ms = jnp.dot(jnp.concatenate([hi, lo], axis=1), exp_ref[...], preferred_element_type=F32)
        y = qk * lax.rsqrt(ms + EPS) * gqk_ref[...]
        cs = cs_ref[rows, :]
        sn = sn_ref[rows, :]
        lane = lax.broadcasted_iota(jnp.int32, cs.shape, 1)
        first_half = (lane & (HEAD_DIM - 1)) < HEAD_DIM // 2
        for c in range(D_QK // LANES):
            yc = y[:, c * LANES:(c + 1) * LANES]
            partner = jnp.where(first_half,
                                pltpu.roll(yc, LANES - HEAD_DIM // 2, 1),
                                pltpu.roll(yc, HEAD_DIM // 2, 1))
            rot = yc * cs + partner * sn
            if c < D_ATTN // LANES:
                q_ref[rows, c * LANES:(c + 1) * LANES] = (rot * (HEAD_DIM ** -0.5)).astype(BF16)
            else:
                kf_ref[rows, c * LANES - D_ATTN:(c + 1) * LANES - D_ATTN] = rot
                _store_head_pairs(kx_ref, rows, rot, c - D_ATTN // LANES)
        v = acc[:, D_QK:]
        vf_ref[rows, :] = v
        for c in range(D_KV // LANES):
            _store_head_pairs(vx_ref, rows, v[:, c * LANES:(c + 1) * LANES], c)


def _inproj_qkv(h, w, cs, sn, gqk, head_reduce, head_expand, tm, table_blocks):
    m_rows, d = h.shape
    cast = _is_param(w)
    row = lambda width: pl.BlockSpec((tm, width), lambda m: (m, 0))
    table = pl.BlockSpec((tm, LANES), lambda m: (m % table_blocks, 0))
    nsplit = 2 if tm % (2 * MXU_DIM) == 0 else 1
    if cast:
        w_spec = pl.BlockSpec((None, d, D_QKV), lambda m: (0, 0, 0), pipeline_mode=pl.Buffered(1))
    else:
        w_spec = pl.BlockSpec((d, D_QKV), lambda m: (0, 0), pipeline_mode=pl.Buffered(1))
    w_out_spec = [pl.BlockSpec((d, D_QKV), lambda m: (0, 0))] if cast else []
    w_out_shape = [jax.ShapeDtypeStruct((d, D_QKV), BF16)] if cast else []
    outs = pl.pallas_call(
        functools.partial(_inproj_qkv_kernel, nsplit=nsplit, cast=cast),
        grid=(m_rows // tm,),
        in_specs=[row(d), w_spec, table, table,
                  pl.BlockSpec((1, D_QK), lambda m: (0, 0)),
                  pl.BlockSpec(head_reduce.shape, lambda m: (0, 0)),
                  pl.BlockSpec(head_expand.shape, lambda m: (0, 0))],
        out_specs=[row(D_ATTN), row(DX), row(DX), row(D_KV), row(D_KV)] + w_out_spec,
        out_shape=[jax.ShapeDtypeStruct((m_rows, D_ATTN), BF16),
                   jax.ShapeDtypeStruct((m_rows, DX), BF16),
                   jax.ShapeDtypeStruct((m_rows, DX), BF16),
                   jax.ShapeDtypeStruct((m_rows, D_KV), F32),
                   jax.ShapeDtypeStruct((m_rows, D_KV), F32)] + w_out_shape,
        compiler_params=_params("arbitrary"),
        name="inproj_qkv",
    )(h, w, cs, sn, gqk, head_reduce, head_expand)
    return tuple(outs) if cast else tuple(outs) + (w,)


def _attn_bias(bias_ref, tq, past_off):
    nk = 2 * WINDOW
    ri = lax.broadcasted_iota(jnp.int32, (2 * tq, 2 * nk), 0) & (tq - 1)
    kj = lax.broadcasted_iota(jnp.int32, (2 * tq, 2 * nk), 1) & (nk - 1)
    visible = ((kj < WINDOW) & (kj > ri + past_off)) | ((kj >= WINDOW) & (kj - WINDOW <= ri))
    bias_ref[...] = jnp.where(visible, 0.0, -jnp.inf)


def _attn_blocks(sink_ref, bias_of, q_ref, past_of, kc_ref, vc_ref, nsub, tq, store):
    nk = 2 * WINDOW
    rows = 2 * tq
    first_rows = lax.broadcasted_iota(jnp.int32, (rows, 1), 0) < tq
    lo_lanes = lax.broadcasted_iota(jnp.int32, (rows, LANES), 1) < HEAD_DIM
    ones_lo = (lax.broadcasted_iota(jnp.int32, (nk, LANES), 1) < HEAD_DIM).astype(BF16)
    ones_hi = (1 - ones_lo.astype(F32)).astype(BF16)
    nt = (((1,), (1,)), ((), ()))

    def keys(which, s, col):
        cols = slice(col * LANES, (col + 1) * LANES)
        past = past_of(s)
        parts = [past[which][past[2]:past[2] + WINDOW, cols], (kc_ref, vc_ref)[which][s * tq:(s + 1) * tq, cols]]
        if tq < WINDOW:
            parts.append(jnp.zeros((WINDOW - tq, LANES), BF16))
        return parts

    for s in range(nsub):
        for g in range(N_KV_HEADS):
            c0, c1 = 2 * g, 2 * g + 1
            q4 = jnp.concatenate([q_ref[s * tq:(s + 1) * tq, c0 * LANES:(c0 + 1) * LANES],
                                  q_ref[s * tq:(s + 1) * tq, c1 * LANES:(c1 + 1) * LANES]], axis=0)
            kk = jnp.concatenate(keys(0, s, c0) + keys(0, s, c1), axis=0)
            sc = lax.dot_general(q4, kk, nt, preferred_element_type=F32) + bias_of(s)[...]
            sk_lo = jnp.where(first_rows, sink_ref[4 * g], sink_ref[4 * g + 2])
            sk_hi = jnp.where(first_rows, sink_ref[4 * g + 1], sink_ref[4 * g + 3])
            m_lo = jnp.maximum(jnp.max(sc[:, :nk], axis=-1, keepdims=True), sk_lo)
            m_hi = jnp.maximum(jnp.max(sc[:, nk:], axis=-1, keepdims=True), sk_hi)
            p = jnp.concatenate([jnp.exp(sc[:, :nk] - m_lo), jnp.exp(sc[:, nk:] - m_hi)], axis=1).astype(BF16)
            vv = jnp.concatenate(
                [jnp.concatenate([jnp.concatenate(keys(1, s, c0), axis=0), ones_lo], axis=1),
                 jnp.concatenate([jnp.concatenate(keys(1, s, c1), axis=0), ones_hi], axis=1)],
                axis=0)
            o = jnp.dot(p, vv, preferred_element_type=F32)
            denom = o[:, LANES:] + jnp.where(lo_lanes, jnp.exp(sk_lo - m_lo), jnp.exp(sk_hi - m_hi))
            store(s, c0, c1, o[:, :LANES] / denom)


def _attn_kernel(sink_ref, q_ref, kp_ref, kc_ref, vp_ref, vc_ref, o_ref, bias0_ref, bias_ref, *, nq):
    tq = WINDOW
    _attn_bias(bias0_ref, tq, jnp.where(pl.program_id(1) > 0, 0, WINDOW))
    if nq > 1:
        _attn_bias(bias_ref, tq, 0)

    def store(s, c0, c1, out):
        o_ref[s * tq:(s + 1) * tq, c0 * LANES:(c0 + 1) * LANES] = out[:tq].astype(BF16)
        o_ref[s * tq:(s + 1) * tq, c1 * LANES:(c1 + 1) * LANES] = out[tq:].astype(BF16)

    past_of = lambda s: (kp_ref, vp_ref, 0) if s == 0 else (kc_ref, vc_ref, (s - 1) * WINDOW)
    bias_of = lambda s: bias0_ref if s == 0 else bias_ref
    _attn_blocks(sink_ref, bias_of, q_ref, past_of, kc_ref, vc_ref, nq, tq, store)


def _attn(sinks, q, kx, vx, nbatch, nblk, nq):
    steps = nblk // nq
    cur = lambda b, n: (b * steps + n, 0)
    past = lambda b, n: (b * nblk + jnp.maximum(n * nq - 1, 0), 0)
    kv_cur = pl.BlockSpec((nq * WINDOW, DX), cur)
    kv_past = pl.BlockSpec((WINDOW, DX), past)
    bias = pltpu.VMEM((2 * WINDOW, 4 * WINDOW), F32)
    return pl.pallas_call(
        functools.partial(_attn_kernel, nq=nq),
        grid=(nbatch, steps),
        in_specs=[pl.BlockSpec(memory_space=pltpu.SMEM), pl.BlockSpec((nq * WINDOW, D_ATTN), cur),
                  kv_past, kv_cur, kv_past, kv_cur],
        out_specs=pl.BlockSpec((nq * WINDOW, D_ATTN), cur),
        out_shape=jax.ShapeDtypeStruct(q.shape, BF16),
        scratch_shapes=[bias, bias],
        compiler_params=_params("arbitrary", "arbitrary"),
        name="attn",
    )(sinks, q, kx, kx, vx, vx)


def _attn_step_kernel(sink_ref, q_ref, kn_ref, vn_ref, ck_ref, cv_ref, o_ref,
                      bias_ref, qx_ref, kcx_ref, vcx_ref, kpx_ref, vpx_ref, *, nsub, tq):
    _attn_bias(bias_ref, tq, 0)
    for c in range(D_KV // LANES):
        _store_head_pairs(kpx_ref, slice(None), ck_ref[:, c * LANES:(c + 1) * LANES], c)
        _store_head_pairs(vpx_ref, slice(None), cv_ref[:, c * LANES:(c + 1) * LANES], c)
    for src, dst in ((q_ref, qx_ref), (kn_ref, kcx_ref), (vn_ref, vcx_ref)):
        rows = src[...].astype(F32)
        for s in range(nsub):
            dst[s * tq:(s + 1) * tq, :] = jnp.broadcast_to(rows[s:s + 1, :], (tq, rows.shape[1])).astype(BF16)

    def store(s, c0, c1, out):
        o_ref[s:s + 1, c0 * LANES:(c0 + 1) * LANES] = out[0:1]
        o_ref[s:s + 1, c1 * LANES:(c1 + 1) * LANES] = out[tq:tq + 1]

    past_of = lambda s: (kpx_ref, vpx_ref, s * WINDOW)
    _attn_blocks(sink_ref, lambda s: bias_ref, qx_ref, past_of, kcx_ref, vcx_ref, nsub, tq, store)


def _attn_step(sinks, q, kx_new, vx_new, cache_k, cache_v, nsub):
    ns = q.shape[0]
    tq = 2 * SUBLANES
    row = lambda width: pl.BlockSpec((nsub, width), lambda b: (b, 0))
    cache = pl.BlockSpec((nsub * WINDOW, D_KV), lambda b: (b, 0))
    return pl.pallas_call(
        functools.partial(_attn_step_kernel, nsub=nsub, tq=tq),
        grid=(ns // nsub,),
        in_specs=[pl.BlockSpec(memory_space=pltpu.SMEM), row(D_ATTN), row(DX), row(DX), cache, cache],
        out_specs=row(D_ATTN),
        out_shape=jax.ShapeDtypeStruct((ns, D_ATTN), F32),
        scratch_shapes=[pltpu.VMEM((2 * tq, 4 * WINDOW), F32),
                        pltpu.VMEM((nsub * tq, D_ATTN), BF16),
                        pltpu.VMEM((nsub * tq, DX), BF16),
                        pltpu.VMEM((nsub * tq, DX), BF16),
                        pltpu.VMEM((nsub * WINDOW, DX), BF16),
                        pltpu.VMEM((nsub * WINDOW, DX), BF16)],
        compiler_params=_params("arbitrary"),
        name="attn_step",
    )(sinks, q, kx_new, vx_new, cache_k, cache_v)


def _softplus(x):
    return jnp.maximum(x, 0.0) + jnp.log1p(jnp.exp(-jnp.abs(x)))


def _lru_coeffs(xc, wrg_ref, wig_ref, brg, big, lam):
    xb = xc.astype(BF16)
    ngroups = D_RNN // MXU_DIM
    rs, igs = [], []
    for c in range(ngroups):
        xg = xb[:, c * MXU_DIM:(c + 1) * MXU_DIM]
        rs.append(jnp.dot(xg, wrg_ref[c], preferred_element_type=F32))
        igs.append(jnp.dot(xg, wig_ref[c], preferred_element_type=F32))
    r = jax.nn.sigmoid(jnp.concatenate(rs, axis=1) + brg)
    ig = jax.nn.sigmoid(jnp.concatenate(igs, axis=1) + big)
    log_a = -LRU_C * r * _softplus(-lam)
    a = jnp.exp(log_a)
    one_minus_a2 = -jnp.tanh(log_a) * (1.0 + a * a)
    u = jnp.sqrt(one_minus_a2) * (ig * xc)
    return a, u


def _rnn_prompt_kernel(rx_ref, rg_ref, cw_ref, cb_ref, wrg_ref, wig_ref, brg_ref, big_ref, lam_ref,
                       o_ref, hlast_ref, conv_ref, xs_ref, a_ref, h_ref, carry_ref, *, tc):
    t = pl.program_id(1)
    pad = SUBLANES

    @pl.when(t == 0)
    def _():
        xs_ref[0:pad, :] = jnp.zeros((pad, D_RNN), F32)
        carry_ref[...] = jnp.zeros_like(carry_ref)

    x = rx_ref[...]
    xs_ref[pad:pad + tc, :] = x
    xc = cb_ref[...] + x * cw_ref[CONV_WIDTH - 1:CONV_WIDTH, :]
    for k in range(1, CONV_WIDTH):
        xc = xc + xs_ref[pad - k:pad - k + tc, :] * cw_ref[CONV_WIDTH - 1 - k:CONV_WIDTH - k, :]
    tail = xs_ref[tc:tc + pad, :]
    xs_ref[0:pad, :] = tail
    conv_ref[...] = tail[pad - (CONV_WIDTH - 1):, :]

    a, u = _lru_coeffs(xc, wrg_ref, wig_ref, brg_ref[...], big_ref[...], lam_ref[...])
    a_ref[...] = a
    h_ref[...] = u

    row = lax.broadcasted_iota(jnp.int32, (SUBLANES, D_RNN), 0)

    def body(r, carry):
        off = pl.multiple_of(r * SUBLANES, SUBLANES)
        av = a_ref[pl.ds(off, SUBLANES), :]
        hv = h_ref[pl.ds(off, SUBLANES), :]
        for sft in (1, 2, 4):
            keep = row >= sft
            a_sh = jnp.where(keep, pltpu.roll(av, sft, 0), 1.0)
            h_sh = jnp.where(keep, pltpu.roll(hv, sft, 0), 0.0)
            hv = av * h_sh + hv
            av = av * a_sh
        hv = hv + av * carry
        h_ref[pl.ds(off, SUBLANES), :] = hv
        return jnp.broadcast_to(hv[SUBLANES - 1:SUBLANES, :], (SUBLANES, D_RNN))

    carry = lax.fori_loop(0, tc // SUBLANES, body, carry_ref[...])
    carry_ref[...] = carry
    hlast_ref[...] = carry[0:1, :]
    o_ref[...] = (h_ref[...] * jax.nn.gelu(rg_ref[...])).astype(BF16)


def _rnn_prompt(z, nbatch, seq, conv_w, conv_b, wrg, wig, b_rg, b_ig, lam, tc):
    nchunk = seq // tc
    rx_blk = Z_RX // D_RNN
    rg_blk = Z_RG // D_RNN
    vec = pl.BlockSpec((1, D_RNN), lambda b, t: (0, 0))
    wspec = pl.BlockSpec(wrg.shape, lambda b, t: (0, 0, 0))
    return pl.pallas_call(
        functools.partial(_rnn_prompt_kernel, tc=tc),
        grid=(nbatch, nchunk),
        in_specs=[pl.BlockSpec((tc, D_RNN), lambda b, t: (b * nchunk + t, rx_blk)),
                  pl.BlockSpec((tc, D_RNN), lambda b, t: (b * nchunk + t, rg_blk)),
                  pl.BlockSpec((CONV_WIDTH, D_RNN), lambda b, t: (0, 0)),
                  vec, wspec, wspec, vec, vec, vec],
        out_specs=[pl.BlockSpec((tc, D_RNN), lambda b, t: (b * nchunk + t, 0)),
                   pl.BlockSpec((None, 1, D_RNN), lambda b, t: (b, 0, 0)),
                   pl.BlockSpec((None, CONV_WIDTH - 1, D_RNN), lambda b, t: (b, 0, 0))],
        out_shape=[jax.ShapeDtypeStruct((nbatch * seq, D_RNN), BF16),
                   jax.ShapeDtypeStruct((nbatch, 1, D_RNN), F32),
                   jax.ShapeDtypeStruct((nbatch, CONV_WIDTH - 1, D_RNN), F32)],
        scratch_shapes=[pltpu.VMEM((tc + SUBLANES, D_RNN), F32),
                        pltpu.VMEM((tc, D_RNN), F32),
                        pltpu.VMEM((tc, D_RNN), F32),
                        pltpu.VMEM((SUBLANES, D_RNN), F32)],
        compiler_params=_params("arbitrary", "arbitrary"),
        name="rnn_prompt",
    )(z, z, conv_w, conv_b.reshape(1, D_RNN), wrg, wig, b_rg.reshape(1, D_RNN),
      b_ig.reshape(1, D_RNN), lam.reshape(1, D_RNN))


def _rnn_step_kernel(rx_ref, rg_ref, c0_ref, c1_ref, c2_ref, h0_ref, cw_ref, cb_ref, wrg_ref, wig_ref,
                     brg_ref, big_ref, lam_ref, o_ref, h_ref):
    x = rx_ref[...]
    xc = (cb_ref[...] + c0_ref[...] * cw_ref[0:1, :] + c1_ref[...] * cw_ref[1:2, :]
          + c2_ref[...] * cw_ref[2:3, :] + x * cw_ref[3:4, :])
    a, u = _lru_coeffs(xc, wrg_ref, wig_ref, brg_ref[...], big_ref[...], lam_ref[...])
    h = a * h0_ref[...] + u
    h_ref[...] = h
    o_ref[...] = (h * jax.nn.gelu(rg_ref[...])).astype(BF16)


def _rnn_step(z, conv_state, h0, conv_w, conv_b, wrg, wig, b_rg, b_ig, lam):
    rows = z.shape[0]
    full = lambda shape: pl.BlockSpec(shape, lambda i: (0,) * len(shape))
    act = full((rows, D_RNN))
    vec = full((1, D_RNN))
    return pl.pallas_call(
        _rnn_step_kernel,
        grid=(1,),
        in_specs=[pl.BlockSpec((rows, D_RNN), lambda i: (0, Z_RX // D_RNN)),
                  pl.BlockSpec((rows, D_RNN), lambda i: (0, Z_RG // D_RNN)),
                  act, act, act, act, full((CONV_WIDTH, D_RNN)), vec,
                  full(wrg.shape), full(wig.shape), vec, vec, vec],
        out_specs=[act, act],
        out_shape=[jax.ShapeDtypeStruct((rows, D_RNN), BF16),
                   jax.ShapeDtypeStruct((rows, D_RNN), F32)],
        compiler_params=_params("arbitrary"),
        name="rnn_step",
    )(z, z, conv_state[:, 0], conv_state[:, 1], conv_state[:, 2], h0, conv_w,
      conv_b.reshape(1, D_RNN), wrg, wig, b_rg.reshape(1, D_RNN), b_ig.reshape(1, D_RNN),
      lam.reshape(1, D_RNN))


def _mix_kernel(oa_ref, or_ref, ga_ref, gr_ref, x_ref, gate_ref, shift_ref, scale_ref, g_ref,
                wpa_ref, wpr_ref, wout_ref, o_ref, h_ref):
    pa = jnp.dot(oa_ref[...].astype(BF16), wpa_ref[...], preferred_element_type=F32)
    pr = jnp.dot(or_ref[...], wpr_ref[...], preferred_element_type=F32)
    mix = jax.nn.sigmoid(ga_ref[...]) * pa + jax.nn.sigmoid(gr_ref[...]) * pr
    x = x_ref[...] + gate_ref[...] * jnp.dot(mix.astype(BF16), wout_ref[...], preferred_element_type=F32)
    o_ref[...] = x
    h_ref[...] = _modulate(x, g_ref[...], shift_ref[...], scale_ref[...]).astype(BF16)


def _mix(o_att, o_rnn, z, x, mod, rows_per_mod, g_next, w_pa, w_pr, w_out, k_gate, k_shift_next, tm):
    m_rows, d = x.shape
    const = lambda shape: pl.BlockSpec(shape, lambda m: (0, 0), pipeline_mode=pl.Buffered(1))
    row = lambda width, col: pl.BlockSpec((tm, width), lambda m: (m, col))
    mspec = lambda k: _mod_spec(mod, tm, rows_per_mod, d, k, 1)
    return pl.pallas_call(
        _mix_kernel,
        grid=(m_rows // tm,),
        in_specs=[row(D_ATTN, 0), row(D_RNN, 0), row(d, Z_GATT // d), row(d, Z_GRNN // d), row(d, 0),
                  mspec(k_gate), mspec(k_shift_next), mspec(k_shift_next + 1), const((1, d)),
                  const(w_pa.shape), const(w_pr.shape), const(w_out.shape)],
        out_specs=[row(d, 0), row(d, 0)],
        out_shape=[jax.ShapeDtypeStruct((m_rows, d), F32), jax.ShapeDtypeStruct((m_rows, d), BF16)],
        compiler_params=_params("arbitrary"),
        name="mix",
    )(o_att, o_rnn, z, z, x, mod, mod, mod, g_next.reshape(1, d), w_pa, w_pr, w_out)


def _block_diag(w, group):
    n, r, _ = w.shape
    eye = jnp.eye(group, dtype=w.dtype)
    wg = w.reshape(n // group, group, r, r)
    return jnp.einsum("ngij,gh->ngihj", wg, eye).reshape(n // group, group * r, group * r)


def _tiles(rows, dff):
    pick = lambda n, prefs: next((t for t in prefs if n % t == 0), n)
    return dict(
        tm_norm=pick(rows, (512,)),
        tm_up=pick(rows, (1024, 512)), tf=pick(dff, (512,)),
        tm_down=pick(rows, (512,)), tn_down=512,
        tm_z=pick(rows, (512,)), tn_z=D_QKV,
        tm_qkv=pick(rows, (512,)),
        tm_mix=pick(rows, (256,)),
    )


def kernel(x_prompt, x_sample, c_prompt, c_sample, cache_k, cache_v, state_h, state_conv, w_ada, b_ada,
           g_norm_ffn1, g_norm_mix, g_norm_ffn2, ffn1_w1, ffn1_w3, ffn1_w2, ffn2_w1, ffn2_w3, ffn2_w2,
           w_in, g_q, g_k, sinks, conv_w, conv_b, w_rg, b_rg, w_ig, b_ig, lru_lambda, w_pa, w_pr, w_out):
    nb, seq, d = x_prompt.shape
    ns = x_sample.shape[0]
    dff = ffn1_w1.shape[2]
    assert d == D_MODEL and w_ada.shape[0] == 1 and x_sample.shape[1] == 1 and cache_k.shape[2] == WINDOW
    assert seq % WINDOW == 0 and w_in.shape[2] == D_QKV + D_Z

    group = MXU_DIM // RNN_BLOCK
    wrg = _block_diag(w_rg[0], group).astype(BF16)
    wig = _block_diag(w_ig[0], group).astype(BF16)
    wpa, wpr, wout = w_pa[0].astype(BF16), w_pr[0].astype(BF16), w_out[0].astype(BF16)
    gqk = jnp.concatenate([jnp.tile(g_q[0], N_Q_HEADS), jnp.tile(g_k[0], N_KV_HEADS)]).reshape(1, D_QK)
    head_of = jnp.arange(D_QK) // HEAD_DIM
    onehot = (head_of[:, None] == jnp.arange(LANES)[None, :]).astype(F32)
    head_reduce = (onehot / HEAD_DIM).astype(BF16)
    head_expand = jnp.concatenate([onehot.T, onehot.T], axis=0).astype(BF16)
    inv = ROPE_THETA ** (-jnp.arange(HEAD_DIM // 2, dtype=F32) * 2.0 / HEAD_DIM)
    inv_row = jnp.tile(inv, LANES // (HEAD_DIM // 2)).reshape(1, LANES)
    sink_vec = sinks[0]
    lam = lru_lambda[0]

    pad_rows = (-(nb + ns)) % SUBLANES
    c_all = jnp.concatenate([c_prompt, c_sample, jnp.zeros((pad_rows, d), F32)], axis=0)
    mod_all = _ada(c_all, w_ada[0], b_ada[0], 1024)
    mod_p = mod_all[:nb].reshape(nb, 1, N_MOD * d)
    mod_s = mod_all[nb:nb + ns]

    wts = dict(f1w1=ffn1_w1, f1w3=ffn1_w3, f1w2=ffn1_w2, f2w1=ffn2_w1, f2w3=ffn2_w3, f2w2=ffn2_w2,
               wz=w_in, wqkv=w_in)

    def trunk(x, mod, rows_per_mod, cs, sn, table_blocks, t):
        h1 = _modulated_norm(x, mod, rows_per_mod, g_norm_ffn1[0], 0, t["tm_norm"])
        g1, wts["f1w1"], wts["f1w3"] = _ffn_up(h1, wts["f1w1"], wts["f1w3"], t["tm_up"], t["tf"])
        x1, wts["f1w2"] = _ffn_down(g1, wts["f1w2"], x, mod, rows_per_mod, 2, t["tm_down"], t["tn_down"])
        hm = _modulated_norm(x1, mod, rows_per_mod, g_norm_mix[0], 3, t["tm_norm"])
        z, wts["wz"] = _inproj_z(hm, wts["wz"], t["tm_z"], t["tn_z"])
        q, kx, vx, kf, vf, wts["wqkv"] = _inproj_qkv(hm, wts["wqkv"], cs, sn, gqk, head_reduce, head_expand,
                                                     t["tm_qkv"], table_blocks)
        return x1, z, q, kx, vx, kf, vf

    def tail(o_att, o_rnn, z, x1, mod, rows_per_mod, t):
        x2, h2 = _mix(o_att, o_rnn, z, x1, mod, rows_per_mod, g_norm_ffn2[0], wpa, wpr, wout, 5, 6, t["tm_mix"])
        g2, wts["f2w1"], wts["f2w3"] = _ffn_up(h2, wts["f2w1"], wts["f2w3"], t["tm_up"], t["tf"])
        y, wts["f2w2"] = _ffn_down(g2, wts["f2w2"], x2, mod, rows_per_mod, 8, t["tm_down"], t["tn_down"])
        return y

    tp = _tiles(nb * seq, dff)
    tp = {k: (min(v, seq) if k.startswith("tm") else v) for k, v in tp.items()}
    cs_p, sn_p = _rope_tables(inv_row, 0, seq, tp["tm_qkv"])
    x1, z, q, kx, vx, kf, vf = trunk(x_prompt.reshape(nb * seq, d), mod_p, seq, cs_p, sn_p,
                                     seq // tp["tm_qkv"], tp)
    nblk = seq // WINDOW
    o_att = _attn(sink_vec, q, kx, vx, nb, nblk, 2 if nblk % 2 == 0 else 1)
    o_rnn, h_p, conv_p = _rnn_prompt(z, nb, seq, conv_w[0], conv_b[0], wrg, wig, b_rg[0], b_ig[0], lam,
                                     min(512, seq))
    y_p = tail(o_att, o_rnn, z, x1, mod_p, seq, tp)

    last = lambda a: a.reshape(nb, seq, D_KV)[:, seq - WINDOW:].reshape(1, nb, WINDOW, N_KV_HEADS, HEAD_DIM)
    k_prompt, v_prompt = last(kf), last(vf)

    ts = _tiles(ns, dff)
    cs_1, sn_1 = _rope_tables(inv_row, PAST_LEN, SUBLANES, SUBLANES)
    cs_s = jnp.broadcast_to(cs_1[0:1], (ns, LANES))
    sn_s = jnp.broadcast_to(sn_1[0:1], (ns, LANES))
    x1s, zs, qs, kxs, vxs, kfs, vfs = trunk(x_sample.reshape(ns, d), mod_s, 1, cs_s, sn_s, 1, ts)
    nsub = 2 * SUBLANES
    assert ns % nsub == 0
    o_att_s = _attn_step(sink_vec, qs, kxs, vxs, cache_k[0].reshape(ns * WINDOW, D_KV),
                         cache_v[0].reshape(ns * WINDOW, D_KV), nsub)
    conv_s_in = state_conv[0]
    o_rnn_s, h_s = _rnn_step(zs, conv_s_in, state_h[0], conv_w[0], conv_b[0], wrg, wig, b_rg[0], b_ig[0], lam)
    y_s = tail(o_att_s, o_rnn_s, zs, x1s, mod_s, 1, ts)

    k_sample = kfs.reshape(1, ns, 1, N_KV_HEADS, HEAD_DIM)
    v_sample = vfs.reshape(1, ns, 1, N_KV_HEADS, HEAD_DIM)
    conv_sample = jnp.concatenate([conv_s_in[:, 1:], zs[:, None, Z_RX:Z_RX + D_RNN]], axis=1)[None]

    return (y_p.reshape(nb, seq, d), y_s.reshape(ns, 1, d), k_prompt, v_prompt, k_sample, v_sample,
            h_p.reshape(1, nb, D_RNN), h_s[None], conv_p[None], conv_sample)
```

```python
import functools

import jax
import jax.numpy as jnp
from jax import lax
from jax.experimental import pallas as pl
from jax.experimental.pallas import tpu as pltpu

F32 = jnp.float32
BF16 = jnp.bfloat16

D_MODEL = 2048
HEAD_DIM = 64
N_Q_HEADS = 16
N_KV_HEADS = 4
Q_PER_KV = N_Q_HEADS // N_KV_HEADS
D_ATTN = N_Q_HEADS * HEAD_DIM
D_KV = N_KV_HEADS * HEAD_DIM
D_QK = D_ATTN + D_KV
D_QKV = D_QK + D_KV
WINDOW = 128
ROPE_THETA = 10000.0
D_RNN = 1024
N_RNN_BLOCKS = 16
RNN_BLOCK = D_RNN // N_RNN_BLOCKS
CONV_WIDTH = 4
LRU_C = 8.0
N_MOD = 9
EPS = 1e-6
PAST_LEN = 16384

LANES = 128
SUBLANES = 8
MXU_DIM = 256
VMEM_LIMIT = 56 * 1024 * 1024

Z_RX = 0
Z_RG = Z_RX + D_RNN
Z_GATT = Z_RG + D_RNN
Z_GRNN = Z_GATT + D_MODEL
D_Z = Z_GRNN + D_MODEL
DX = 2 * N_KV_HEADS * LANES


def _params(*sem):
    return pltpu.CompilerParams(dimension_semantics=sem, vmem_limit_bytes=VMEM_LIMIT)


def _silu(x):
    return x * jax.nn.sigmoid(x)


def _modulate(x, g, shift, scale):
    ms = jnp.mean(x * x, axis=-1, keepdims=True)
    return (x * lax.rsqrt(ms + EPS) * g) * (1.0 + scale) + shift


def _mod_spec(mod, tm, rows_per_mod, width, col, ngrid):
    colf = col if callable(col) else (lambda *idx: col)
    if mod.ndim == 3:
        return pl.BlockSpec((None, 1, width), lambda *idx: ((idx[ngrid - 1] * tm) // rows_per_mod, 0, colf(*idx)))
    return pl.BlockSpec((tm, width), lambda *idx: (idx[ngrid - 1], colf(*idx)))


def _ada_kernel(c_ref, w_ref, b_ref, o_ref):
    s = _silu(c_ref[...]).astype(BF16)
    o_ref[...] = jnp.dot(s, w_ref[...].astype(BF16), preferred_element_type=F32) + b_ref[...]


def _ada(c_all, w_ada, b_ada, tn):
    rows, d = c_all.shape
    n = w_ada.shape[1]
    return pl.pallas_call(
        _ada_kernel,
        grid=(n // tn,),
        in_specs=[pl.BlockSpec((rows, d), lambda j: (0, 0)),
                  pl.BlockSpec((d, tn), lambda j: (0, j)),
                  pl.BlockSpec((1, tn), lambda j: (0, j))],
        out_specs=pl.BlockSpec((rows, tn), lambda j: (0, j)),
        out_shape=jax.ShapeDtypeStruct((rows, n), F32),
        compiler_params=_params("arbitrary"),
        name="ada",
    )(c_all, w_ada, b_ada.reshape(1, n))


def _rope_kernel(inv_ref, cs_ref, sn_ref, *, base, blk):
    r = pl.program_id(0)
    pos = (base + r * blk + lax.broadcasted_iota(jnp.int32, (blk, LANES), 0)).astype(F32)
    ang = pos * inv_ref[...]
    lane = lax.broadcasted_iota(jnp.int32, (blk, LANES), 1)
    first_half = (lane & (HEAD_DIM - 1)) < HEAD_DIM // 2
    sn = jnp.sin(ang)
    cs_ref[...] = jnp.cos(ang)
    sn_ref[...] = jnp.where(first_half, -sn, sn)


def _rope_tables(inv_row, base, rows, blk):
    out = jax.ShapeDtypeStruct((rows, LANES), F32)
    return pl.pallas_call(
        functools.partial(_rope_kernel, base=base, blk=blk),
        grid=(rows // blk,),
        in_specs=[pl.BlockSpec((1, LANES), lambda r: (0, 0))],
        out_specs=[pl.BlockSpec((blk, LANES), lambda r: (r, 0))] * 2,
        out_shape=[out, out],
        compiler_params=_params("arbitrary"),
        name="rope_tables",
    )(inv_row)


def _modulate_kernel(x_ref, shift_ref, scale_ref, g_ref, h_ref):
    h_ref[...] = _modulate(x_ref[...], g_ref[...], shift_ref[...], scale_ref[...]).astype(BF16)


def _modulated_norm(x, mod, rows_per_mod, g, k_shift, tm):
    m_rows, d = x.shape
    return pl.pallas_call(
        _modulate_kernel,
        grid=(m_rows // tm,),
        in_specs=[pl.BlockSpec((tm, d), lambda m: (m, 0)),
                  _mod_spec(mod, tm, rows_per_mod, d, k_shift, 1),
                  _mod_spec(mod, tm, rows_per_mod, d, k_shift + 1, 1),
                  pl.BlockSpec((1, d), lambda m: (0, 0))],
        out_specs=pl.BlockSpec((tm, d), lambda m: (m, 0)),
        out_shape=jax.ShapeDtypeStruct((m_rows, d), BF16),
        compiler_params=_params("arbitrary"),
        name="modnorm",
    )(x, mod, mod, g.reshape(1, d))


def _ffn_up_kernel(h_ref, w1_ref, w3_ref, g_ref, w1b_ref, w3b_ref):
    @pl.when(pl.program_id(1) == 0)
    def _():
        w1b_ref[...] = w1_ref[...].astype(BF16)
        w3b_ref[...] = w3_ref[...].astype(BF16)

    h = h_ref[...]
    a = jnp.dot(h, w1b_ref[...], preferred_element_type=F32)
    b = jnp.dot(h, w3b_ref[...], preferred_element_type=F32)
    g_ref[...] = (_silu(a) * b).astype(BF16)


def _ffn_up(h, w1, w3, tm, tf):
    m_rows, d = h.shape
    dff = w1.shape[2]
    wspec = pl.BlockSpec((None, d, tf), lambda f, m: (0, 0, f))
    return pl.pallas_call(
        _ffn_up_kernel,
        grid=(dff // tf, m_rows // tm),
        in_specs=[pl.BlockSpec((tm, d), lambda f, m: (m, 0)), wspec, wspec],
        out_specs=pl.BlockSpec((tm, tf), lambda f, m: (m, f)),
        out_shape=jax.ShapeDtypeStruct((m_rows, dff), BF16),
        scratch_shapes=[pltpu.VMEM((d, tf), BF16), pltpu.VMEM((d, tf), BF16)],
        compiler_params=_params("arbitrary", "arbitrary"),
        name="ffn_up",
    )(h, w1, w3)


def _ffn_down_kernel(g_ref, w2_ref, x_ref, gate_ref, o_ref, w2b_ref):
    @pl.when(pl.program_id(1) == 0)
    def _():
        w2b_ref[...] = w2_ref[...].astype(BF16)

    acc = jnp.dot(g_ref[...], w2b_ref[...], preferred_element_type=F32)
    o_ref[...] = x_ref[...] + 0.5 * gate_ref[...] * acc


def _ffn_down(g, w2, x, mod, rows_per_mod, k_gate, tm, tn):
    m_rows, d = x.shape
    dff = g.shape[1]
    gate_col = lambda n, m: k_gate * (d // tn) + n
    return pl.pallas_call(
        _ffn_down_kernel,
        grid=(d // tn, m_rows // tm),
        in_specs=[pl.BlockSpec((tm, dff), lambda n, m: (m, 0)),
                  pl.BlockSpec((None, dff, tn), lambda n, m: (0, 0, n)),
                  pl.BlockSpec((tm, tn), lambda n, m: (m, n)),
                  _mod_spec(mod, tm, rows_per_mod, tn, gate_col, 2)],
        out_specs=pl.BlockSpec((tm, tn), lambda n, m: (m, n)),
        out_shape=jax.ShapeDtypeStruct((m_rows, d), F32),
        scratch_shapes=[pltpu.VMEM((dff, tn), BF16)],
        compiler_params=_params("arbitrary", "arbitrary"),
        name="ffn_down",
    )(g, w2, x, mod)


def _inproj_z_kernel(h_ref, w_ref, z_ref, wb_ref):
    @pl.when(pl.program_id(1) == 0)
    def _():
        wb_ref[...] = w_ref[...].astype(BF16)

    z_ref[...] = jnp.dot(h_ref[...], wb_ref[...], preferred_element_type=F32)


def _inproj_z(h, w_in, tm, tn):
    m_rows, d = h.shape
    first = D_QKV // tn
    return pl.pallas_call(
        _inproj_z_kernel,
        grid=(D_Z // tn, m_rows // tm),
        in_specs=[pl.BlockSpec((tm, d), lambda n, m: (m, 0)),
                  pl.BlockSpec((None, d, tn), lambda n, m: (0, 0, first + n))],
        out_specs=pl.BlockSpec((tm, tn), lambda n, m: (m, n)),
        out_shape=jax.ShapeDtypeStruct((m_rows, D_Z), F32),
        scratch_shapes=[pltpu.VMEM((d, tn), BF16)],
        compiler_params=_params("arbitrary", "arbitrary"),
        name="inproj_z",
    )(h, w_in)


def _store_head_pairs(ref, rows, chunk, c):
    lo = lax.broadcasted_iota(jnp.int32, chunk.shape, 1) < HEAD_DIM
    swapped = pltpu.roll(chunk, HEAD_DIM, 1)
    zero = jnp.zeros_like(chunk)
    cols = (jnp.where(lo, chunk, zero), jnp.where(lo, zero, swapped),
            jnp.where(lo, swapped, zero), jnp.where(lo, zero, chunk))
    for i, col in enumerate(cols):
        ref[rows, (4 * c + i) * LANES:(4 * c + i + 1) * LANES] = col.astype(ref.dtype)


def _inproj_qkv_kernel(x_ref, shift_ref, scale_ref, g_ref, w_ref, cs_ref, sn_ref, gqk_ref, red_ref, exp_ref,
                       h_ref, q_ref, kx_ref, vx_ref, kf_ref, vf_ref, wb_ref, *, nsplit):
    @pl.when(pl.program_id(0) == 0)
    def _():
        wb_ref[...] = w_ref[...].astype(BF16)

    step = x_ref.shape[0] // nsplit
    for r in range(nsplit):
        rows = slice(r * step, (r + 1) * step)
        shift = shift_ref[...] if shift_ref.shape[0] == 1 else shift_ref[rows, :]
        scale = scale_ref[...] if scale_ref.shape[0] == 1 else scale_ref[rows, :]
        h = _modulate(x_ref[rows, :], g_ref[...], shift, scale).astype(BF16)
        h_ref[rows, :] = h
        acc = jnp.dot(h, wb_ref[...], preferred_element_type=F32)
        qk = acc[:, :D_QK]
        hm = jnp.dot((qk * qk).astype(BF16), red_ref[...], preferred_element_type=F32)
        hi = hm.astype(BF16)
        lo = (hm - hi.astype(F32)).astype(BF16)
        ms = jnp.dot(jnp.concatenate([hi, lo], axis=1), exp_ref[...], preferred_element_type=F32)
        y = qk * lax.rsqrt(ms + EPS) * gqk_ref[...]
        cs = cs_ref[rows, :]
        sn = sn_ref[rows, :]
        lane = lax.broadcasted_iota(jnp.int32, cs.shape, 1)
        first_half = (lane & (HEAD_DIM - 1)) < HEAD_DIM // 2
        for c in range(D_QK // LANES):
            yc = y[:, c * LANES:(c + 1) * LANES]
            partner = jnp.where(first_half,
                                pltpu.roll(yc, LANES - HEAD_DIM // 2, 1),
                                pltpu.roll(yc, HEAD_DIM // 2, 1))
            rot = yc * cs + partner * sn
            if c < D_ATTN // LANES:
                q_ref[rows, c * LANES:(c + 1) * LANES] = (rot * (HEAD_DIM ** -0.5)).astype(BF16)
            else:
                kf_ref[rows, c * LANES - D_ATTN:(c + 1) * LANES - D_ATTN] = rot
                _store_head_pairs(kx_ref, rows, rot, c - D_ATTN // LANES)
        v = acc[:, D_QK:]
        vf_ref[rows, :] = v
        for c in range(D_KV // LANES):
            _store_head_pairs(vx_ref, rows, v[:, c * LANES:(c + 1) * LANES], c)


def _inproj_qkv(x, mod, rows_per_mod, g, k_shift, w_in, cs, sn, gqk, head_reduce, head_expand, tm, table_blocks):
    m_rows, d = x.shape
    row = lambda width: pl.BlockSpec((tm, width), lambda m: (m, 0))
    table = pl.BlockSpec((tm, LANES), lambda m: (m % table_blocks, 0))
    nsplit = 2 if tm % (2 * MXU_DIM) == 0 else 1
    return pl.pallas_call(
        functools.partial(_inproj_qkv_kernel, nsplit=nsplit),
        grid=(m_rows // tm,),
        in_specs=[row(d),
                  _mod_spec(mod, tm, rows_per_mod, d, k_shift, 1),
                  _mod_spec(mod, tm, rows_per_mod, d, k_shift + 1, 1),
                  pl.BlockSpec((1, d), lambda m: (0, 0)),
                  pl.BlockSpec((None, d, D_QKV), lambda m: (0, 0, 0), pipeline_mode=pl.Buffered(1)),
                  table, table,
                  pl.BlockSpec((1, D_QK), lambda m: (0, 0)),
                  pl.BlockSpec(head_reduce.shape, lambda m: (0, 0)),
                  pl.BlockSpec(head_expand.shape, lambda m: (0, 0))],
        out_specs=[row(d), row(D_ATTN), row(DX), row(DX), row(D_KV), row(D_KV)],
        out_shape=[jax.ShapeDtypeStruct((m_rows, d), BF16),
                   jax.ShapeDtypeStruct((m_rows, D_ATTN), BF16),
                   jax.ShapeDtypeStruct((m_rows, DX), BF16),
                   jax.ShapeDtypeStruct((m_rows, DX), BF16),
                   jax.ShapeDtypeStruct((m_rows, D_KV), F32),
                   jax.ShapeDtypeStruct((m_rows, D_KV), F32)],
        scratch_shapes=[pltpu.VMEM((d, D_QKV), BF16)],
        compiler_params=_params("arbitrary"),
        name="inproj_qkv",
    )(x, mod, mod, g.reshape(1, d), w_in, cs, sn, gqk, head_reduce, head_expand)


def _attn_bias(bias_ref, tq, past_off):
    nk = 2 * WINDOW
    ri = lax.broadcasted_iota(jnp.int32, (2 * tq, 2 * nk), 0) & (tq - 1)
    kj = lax.broadcasted_iota(jnp.int32, (2 * tq, 2 * nk), 1) & (nk - 1)
    visible = ((kj < WINDOW) & (kj > ri + past_off)) | ((kj >= WINDOW) & (kj - WINDOW <= ri))
    bias_ref[...] = jnp.where(visible, 0.0, -jnp.inf)


def _attn_blocks(sink_ref, bias_of, q_ref, past_of, kc_ref, vc_ref, nsub, tq, store):
    nk = 2 * WINDOW
    rows = 2 * tq
    first_rows = lax.broadcasted_iota(jnp.int32, (rows, 1), 0) < tq
    lo_lanes = lax.broadcasted_iota(jnp.int32, (rows, LANES), 1) < HEAD_DIM
    ones_lo = (lax.broadcasted_iota(jnp.int32, (nk, LANES), 1) < HEAD_DIM).astype(BF16)
    ones_hi = (1 - ones_lo.astype(F32)).astype(BF16)
    nt = (((1,), (1,)), ((), ()))

    def keys(which, s, col):
        cols = slice(col * LANES, (col + 1) * LANES)
        past = past_of(s)
        parts = [past[which][past[2]:past[2] + WINDOW, cols], (kc_ref, vc_ref)[which][s * tq:(s + 1) * tq, cols]]
        if tq < WINDOW:
            parts.append(jnp.zeros((WINDOW - tq, LANES), BF16))
        return parts

    for s in range(nsub):
        for g in range(N_KV_HEADS):
            c0, c1 = 2 * g, 2 * g + 1
            q4 = jnp.concatenate([q_ref[s * tq:(s + 1) * tq, c0 * LANES:(c0 + 1) * LANES],
                                  q_ref[s * tq:(s + 1) * tq, c1 * LANES:(c1 + 1) * LANES]], axis=0)
            kk = jnp.concatenate(keys(0, s, c0) + keys(0, s, c1), axis=0)
            sc = lax.dot_general(q4, kk, nt, preferred_element_type=F32) + bias_of(s)[...]
            sk_lo = jnp.where(first_rows, sink_ref[4 * g], sink_ref[4 * g + 2])
            sk_hi = jnp.where(first_rows, sink_ref[4 * g + 1], sink_ref[4 * g + 3])
            m_lo = jnp.maximum(jnp.max(sc[:, :nk], axis=-1, keepdims=True), sk_lo)
            m_hi = jnp.maximum(jnp.max(sc[:, nk:], axis=-1, keepdims=True), sk_hi)
            p = jnp.concatenate([jnp.exp(sc[:, :nk] - m_lo), jnp.exp(sc[:, nk:] - m_hi)], axis=1).astype(BF16)
            vv = jnp.concatenate(
                [jnp.concatenate([jnp.concatenate(keys(1, s, c0), axis=0), ones_lo], axis=1),
                 jnp.concatenate([jnp.concatenate(keys(1, s, c1), axis=0), ones_hi], axis=1)],
                axis=0)
            o = jnp.dot(p, vv, preferred_element_type=F32)
            denom = o[:, LANES:] + jnp.where(lo_lanes, jnp.exp(sk_lo - m_lo), jnp.exp(sk_hi - m_hi))
            store(s, c0, c1, o[:, :LANES] / denom)


def _attn_kernel(sink_ref, q_ref, kp_ref, kc_ref, vp_ref, vc_ref, o_ref, bias0_ref, bias_ref, *, nq):
    tq = WINDOW
    _attn_bias(bias0_ref, tq, jnp.where(pl.program_id(1) > 0, 0, WINDOW))
    if nq > 1:
        _attn_bias(bias_ref, tq, 0)

    def store(s, c0, c1, out):
        o_ref[s * tq:(s + 1) * tq, c0 * LANES:(c0 + 1) * LANES] = out[:tq].astype(BF16)
        o_ref[s * tq:(s + 1) * tq, c1 * LANES:(c1 + 1) * LANES] = out[tq:].astype(BF16)

    past_of = lambda s: (kp_ref, vp_ref, 0) if s == 0 else (kc_ref, vc_ref, (s - 1) * WINDOW)
    bias_of = lambda s: bias0_ref if s == 0 else bias_ref
    _attn_blocks(sink_ref, bias_of, q_ref, past_of, kc_ref, vc_ref, nq, tq, store)


def _attn(sinks, q, kx, vx, nbatch, nblk, nq):
    steps = nblk // nq
    cur = lambda b, n: (b * steps + n, 0)
    past = lambda b, n: (b * nblk + jnp.maximum(n * nq - 1, 0), 0)
    kv_cur = pl.BlockSpec((nq * WINDOW, DX), cur)
    kv_past = pl.BlockSpec((WINDOW, DX), past)
    bias = pltpu.VMEM((2 * WINDOW, 4 * WINDOW), F32)
    return pl.pallas_call(
        functools.partial(_attn_kernel, nq=nq),
        grid=(nbatch, steps),
        in_specs=[pl.BlockSpec(memory_space=pltpu.SMEM), pl.BlockSpec((nq * WINDOW, D_ATTN), cur),
                  kv_past, kv_cur, kv_past, kv_cur],
        out_specs=pl.BlockSpec((nq * WINDOW, D_ATTN), cur),
        out_shape=jax.ShapeDtypeStruct(q.shape, BF16),
        scratch_shapes=[bias, bias],
        compiler_params=_params("arbitrary", "arbitrary"),
        name="attn",
    )(sinks, q, kx, kx, vx, vx)


def _attn_step_kernel(sink_ref, q_ref, kn_ref, vn_ref, ck_ref, cv_ref, o_ref,
                      bias_ref, qx_ref, kcx_ref, vcx_ref, kpx_ref, vpx_ref, *, nsub, tq):
    _attn_bias(bias_ref, tq, 0)
    for c in range(D_KV // LANES):
        _store_head_pairs(kpx_ref, slice(None), ck_ref[:, c * LANES:(c + 1) * LANES], c)
        _store_head_pairs(vpx_ref, slice(None), cv_ref[:, c * LANES:(c + 1) * LANES], c)
    for src, dst in ((q_ref, qx_ref), (kn_ref, kcx_ref), (vn_ref, vcx_ref)):
        rows = src[...].astype(F32)
        for s in range(nsub):
            dst[s * tq:(s + 1) * tq, :] = jnp.broadcast_to(rows[s:s + 1, :], (tq, rows.shape[1])).astype(BF16)

    def store(s, c0, c1, out):
        o_ref[s:s + 1, c0 * LANES:(c0 + 1) * LANES] = out[0:1]
        o_ref[s:s + 1, c1 * LANES:(c1 + 1) * LANES] = out[tq:tq + 1]

    past_of = lambda s: (kpx_ref, vpx_ref, s * WINDOW)
    _attn_blocks(sink_ref, lambda s: bias_ref, qx_ref, past_of, kcx_ref, vcx_ref, nsub, tq, store)


def _attn_step(sinks, q, kx_new, vx_new, cache_k, cache_v, nsub):
    ns = q.shape[0]
    tq = 2 * SUBLANES
    row = lambda width: pl.BlockSpec((nsub, width), lambda b: (b, 0))
    cache = pl.BlockSpec((nsub * WINDOW, D_KV), lambda b: (b, 0))
    return pl.pallas_call(
        functools.partial(_attn_step_kernel, nsub=nsub, tq=tq),
        grid=(ns // nsub,),
        in_specs=[pl.BlockSpec(memory_space=pltpu.SMEM), row(D_ATTN), row(DX), row(DX), cache, cache],
        out_specs=row(D_ATTN),
        out_shape=jax.ShapeDtypeStruct((ns, D_ATTN), F32),
        scratch_shapes=[pltpu.VMEM((2 * tq, 4 * WINDOW), F32),
                        pltpu.VMEM((nsub * tq, D_ATTN), BF16),
                        pltpu.VMEM((nsub * tq, DX), BF16),
                        pltpu.VMEM((nsub * tq, DX), BF16),
                        pltpu.VMEM((nsub * WINDOW, DX), BF16),
                        pltpu.VMEM((nsub * WINDOW, DX), BF16)],
        compiler_params=_params("arbitrary"),
        name="attn_step",
    )(sinks, q, kx_new, vx_new, cache_k, cache_v)


def _softplus(x):
    return jnp.maximum(x, 0.0) + jnp.log1p(jnp.exp(-jnp.abs(x)))


def _lru_coeffs(xc, wrg_ref, wig_ref, brg, big, lam):
    xb = xc.astype(BF16)
    ngroups = D_RNN // MXU_DIM
    rs, igs = [], []
    for c in range(ngroups):
        xg = xb[:, c * MXU_DIM:(c + 1) * MXU_DIM]
        rs.append(jnp.dot(xg, wrg_ref[c], preferred_element_type=F32))
        igs.append(jnp.dot(xg, wig_ref[c], preferred_element_type=F32))
    r = jax.nn.sigmoid(jnp.concatenate(rs, axis=1) + brg)
    ig = jax.nn.sigmoid(jnp.concatenate(igs, axis=1) + big)
    log_a = -LRU_C * r * _softplus(-lam)
    a = jnp.exp(log_a)
    one_minus_a2 = -jnp.tanh(log_a) * (1.0 + a * a)
    u = jnp.sqrt(one_minus_a2) * (ig * xc)
    return a, u


def _rnn_prompt_kernel(rx_ref, rg_ref, cw_ref, cb_ref, wrg_ref, wig_ref, brg_ref, big_ref, lam_ref,
                       o_ref, hlast_ref, conv_ref, xs_ref, a_ref, h_ref, carry_ref, *, tc):
    t = pl.program_id(1)
    pad = SUBLANES

    @pl.when(t == 0)
    def _():
        xs_ref[0:pad, :] = jnp.zeros((pad, D_RNN), F32)
        carry_ref[...] = jnp.zeros_like(carry_ref)

    x = rx_ref[...]
    xs_ref[pad:pad + tc, :] = x
    xc = cb_ref[...] + x * cw_ref[CONV_WIDTH - 1:CONV_WIDTH, :]
    for k in range(1, CONV_WIDTH):
        xc = xc + xs_ref[pad - k:pad - k + tc, :] * cw_ref[CONV_WIDTH - 1 - k:CONV_WIDTH - k, :]
    tail = xs_ref[tc:tc + pad, :]
    xs_ref[0:pad, :] = tail
    conv_ref[...] = tail[pad - (CONV_WIDTH - 1):, :]

    a, u = _lru_coeffs(xc, wrg_ref, wig_ref, brg_ref[...], big_ref[...], lam_ref[...])
    a_ref[...] = a
    h_ref[...] = u

    row = lax.broadcasted_iota(jnp.int32, (SUBLANES, D_RNN), 0)

    def body(r, carry):
        off = pl.multiple_of(r * SUBLANES, SUBLANES)
        av = a_ref[pl.ds(off, SUBLANES), :]
        hv = h_ref[pl.ds(off, SUBLANES), :]
        for sft in (1, 2, 4):
            keep = row >= sft
            a_sh = jnp.where(keep, pltpu.roll(av, sft, 0), 1.0)
            h_sh = jnp.where(keep, pltpu.roll(hv, sft, 0), 0.0)
            hv = av * h_sh + hv
            av = av * a_sh
        hv = hv + av * carry
        h_ref[pl.ds(off, SUBLANES), :] = hv
        return jnp.broadcast_to(hv[SUBLANES - 1:SUBLANES, :], (SUBLANES, D_RNN))

    carry = lax.fori_loop(0, tc // SUBLANES, body, carry_ref[...])
    carry_ref[...] = carry
    hlast_ref[...] = carry[0:1, :]
    o_ref[...] = (h_ref[...] * jax.nn.gelu(rg_ref[...])).astype(BF16)


def _rnn_prompt(z, nbatch, seq, conv_w, conv_b, wrg, wig, b_rg, b_ig, lam, tc):
    nchunk = seq // tc
    rx_blk = Z_RX // D_RNN
    rg_blk = Z_RG // D_RNN
    vec = pl.BlockSpec((1, D_RNN), lambda b, t: (0, 0))
    wspec = pl.BlockSpec(wrg.shape, lambda b, t: (0, 0, 0))
    return pl.pallas_call(
        functools.partial(_rnn_prompt_kernel, tc=tc),
        grid=(nbatch, nchunk),
        in_specs=[pl.BlockSpec((tc, D_RNN), lambda b, t: (b * nchunk + t, rx_blk)),
                  pl.BlockSpec((tc, D_RNN), lambda b, t: (b * nchunk + t, rg_blk)),
                  pl.BlockSpec((CONV_WIDTH, D_RNN), lambda b, t: (0, 0)),
                  vec, wspec, wspec, vec, vec, vec],
        out_specs=[pl.BlockSpec((tc, D_RNN), lambda b, t: (b * nchunk + t, 0)),
                   pl.BlockSpec((None, 1, D_RNN), lambda b, t: (b, 0, 0)),
                   pl.BlockSpec((None, CONV_WIDTH - 1, D_RNN), lambda b, t: (b, 0, 0))],
        out_shape=[jax.ShapeDtypeStruct((nbatch * seq, D_RNN), BF16),
                   jax.ShapeDtypeStruct((nbatch, 1, D_RNN), F32),
                   jax.ShapeDtypeStruct((nbatch, CONV_WIDTH - 1, D_RNN), F32)],
        scratch_shapes=[pltpu.VMEM((tc + SUBLANES, D_RNN), F32),
                        pltpu.VMEM((tc, D_RNN), F32),
                        pltpu.VMEM((tc, D_RNN), F32),
                        pltpu.VMEM((SUBLANES, D_RNN), F32)],
        compiler_params=_params("arbitrary", "arbitrary"),
        name="rnn_prompt",
    )(z, z, conv_w, conv_b.reshape(1, D_RNN), wrg, wig, b_rg.reshape(1, D_RNN),
      b_ig.reshape(1, D_RNN), lam.reshape(1, D_RNN))


def _rnn_step_kernel(rx_ref, rg_ref, c0_ref, c1_ref, c2_ref, h0_ref, cw_ref, cb_ref, wrg_ref, wig_ref,
                     brg_ref, big_ref, lam_ref, o_ref, h_ref):
    x = rx_ref[...]
    xc = (cb_ref[...] + c0_ref[...] * cw_ref[0:1, :] + c1_ref[...] * cw_ref[1:2, :]
          + c2_ref[...] * cw_ref[2:3, :] + x * cw_ref[3:4, :])
    a, u = _lru_coeffs(xc, wrg_ref, wig_ref, brg_ref[...], big_ref[...], lam_ref[...])
    h = a * h0_ref[...] + u
    h_ref[...] = h
    o_ref[...] = (h * jax.nn.gelu(rg_ref[...])).astype(BF16)


def _rnn_step(z, conv_state, h0, conv_w, conv_b, wrg, wig, b_rg, b_ig, lam):
    rows = z.shape[0]
    full = lambda shape: pl.BlockSpec(shape, lambda i: (0,) * len(shape))
    act = full((rows, D_RNN))
    vec = full((1, D_RNN))
    return pl.pallas_call(
        _rnn_step_kernel,
        grid=(1,),
        in_specs=[pl.BlockSpec((rows, D_RNN), lambda i: (0, Z_RX // D_RNN)),
                  pl.BlockSpec((rows, D_RNN), lambda i: (0, Z_RG // D_RNN)),
                  act, act, act, act, full((CONV_WIDTH, D_RNN)), vec,
                  full(wrg.shape), full(wig.shape), vec, vec, vec],
        out_specs=[act, act],
        out_shape=[jax.ShapeDtypeStruct((rows, D_RNN), BF16),
                   jax.ShapeDtypeStruct((rows, D_RNN), F32)],
        compiler_params=_params("arbitrary"),
        name="rnn_step",
    )(z, z, conv_state[:, 0], conv_state[:, 1], conv_state[:, 2], h0, conv_w,
      conv_b.reshape(1, D_RNN), wrg, wig, b_rg.reshape(1, D_RNN), b_ig.reshape(1, D_RNN),
      lam.reshape(1, D_RNN))


def _mix_kernel(oa_ref, or_ref, ga_ref, gr_ref, x_ref, gate_ref, shift_ref, scale_ref, g_ref,
                wpa_ref, wpr_ref, wout_ref, o_ref, h_ref):
    pa = jnp.dot(oa_ref[...].astype(BF16), wpa_ref[...], preferred_element_type=F32)
    pr = jnp.dot(or_ref[...], wpr_ref[...], preferred_element_type=F32)
    mix = jax.nn.sigmoid(ga_ref[...]) * pa + jax.nn.sigmoid(gr_ref[...]) * pr
    x = x_ref[...] + gate_ref[...] * jnp.dot(mix.astype(BF16), wout_ref[...], preferred_element_type=F32)
    o_ref[...] = x
    h_ref[...] = _modulate(x, g_ref[...], shift_ref[...], scale_ref[...]).astype(BF16)


def _mix(o_att, o_rnn, z, x, mod, rows_per_mod, g_next, w_pa, w_pr, w_out, k_gate, k_shift_next, tm):
    m_rows, d = x.shape
    const = lambda shape: pl.BlockSpec(shape, lambda m: (0, 0), pipeline_mode=pl.Buffered(1))
    row = lambda width, col: pl.BlockSpec((tm, width), lambda m: (m, col))
    mspec = lambda k: _mod_spec(mod, tm, rows_per_mod, d, k, 1)
    return pl.pallas_call(
        _mix_kernel,
        grid=(m_rows // tm,),
        in_specs=[row(D_ATTN, 0), row(D_RNN, 0), row(d, Z_GATT // d), row(d, Z_GRNN // d), row(d, 0),
                  mspec(k_gate), mspec(k_shift_next), mspec(k_shift_next + 1), const((1, d)),
                  const(w_pa.shape), const(w_pr.shape), const(w_out.shape)],
        out_specs=[row(d, 0), row(d, 0)],
        out_shape=[jax.ShapeDtypeStruct((m_rows, d), F32), jax.ShapeDtypeStruct((m_rows, d), BF16)],
        compiler_params=_params("arbitrary"),
        name="mix",
    )(o_att, o_rnn, z, z, x, mod, mod, mod, g_next.reshape(1, d), w_pa, w_pr, w_out)


def _block_diag(w, group):
    n, r, _ = w.shape
    eye = jnp.eye(group, dtype=w.dtype)
    wg = w.reshape(n // group, group, r, r)
    return jnp.einsum("ngij,gh->ngihj", wg, eye).reshape(n // group, group * r, group * r)


def _tiles(rows, dff):
    pick = lambda n, prefs: next((t for t in prefs if n % t == 0), n)
    return dict(
        tm_norm=pick(rows, (512,)),
        tm_up=pick(rows, (1024, 512)), tf=pick(dff, (512,)),
        tm_down=pick(rows, (512,)), tn_down=512,
        tm_z=pick(rows, (512,)), tn_z=D_QKV,
        tm_qkv=pick(rows, (512,)),
        tm_mix=pick(rows, (256,)),
    )


def kernel(x_prompt, x_sample, c_prompt, c_sample, cache_k, cache_v, state_h, state_conv, w_ada, b_ada,
           g_norm_ffn1, g_norm_mix, g_norm_ffn2, ffn1_w1, ffn1_w3, ffn1_w2, ffn2_w1, ffn2_w3, ffn2_w2,
           w_in, g_q, g_k, sinks, conv_w, conv_b, w_rg, b_rg, w_ig, b_ig, lru_lambda, w_pa, w_pr, w_out):
    nb, seq, d = x_prompt.shape
    ns = x_sample.shape[0]
    dff = ffn1_w1.shape[2]
    assert d == D_MODEL and w_ada.shape[0] == 1 and x_sample.shape[1] == 1 and cache_k.shape[2] == WINDOW
    assert seq % WINDOW == 0 and w_in.shape[2] == D_QKV + D_Z

    group = MXU_DIM // RNN_BLOCK
    wrg = _block_diag(w_rg[0], group).astype(BF16)
    wig = _block_diag(w_ig[0], group).astype(BF16)
    wpa, wpr, wout = w_pa[0].astype(BF16), w_pr[0].astype(BF16), w_out[0].astype(BF16)
    gqk = jnp.concatenate([jnp.tile(g_q[0], N_Q_HEADS), jnp.tile(g_k[0], N_KV_HEADS)]).reshape(1, D_QK)
    head_of = jnp.arange(D_QK) // HEAD_DIM
    onehot = (head_of[:, None] == jnp.arange(LANES)[None, :]).astype(F32)
    head_reduce = (onehot / HEAD_DIM).astype(BF16)
    head_expand = jnp.concatenate([onehot.T, onehot.T], axis=0).astype(BF16)
    inv = ROPE_THETA ** (-jnp.arange(HEAD_DIM // 2, dtype=F32) * 2.0 / HEAD_DIM)
    inv_row = jnp.tile(inv, LANES // (HEAD_DIM // 2)).reshape(1, LANES)
    sink_vec = sinks[0]
    lam = lru_lambda[0]

    pad_rows = (-(nb + ns)) % SUBLANES
    c_all = jnp.concatenate([c_prompt, c_sample, jnp.zeros((pad_rows, d), F32)], axis=0)
    mod_all = _ada(c_all, w_ada[0], b_ada[0], 1024)
    mod_p = mod_all[:nb].reshape(nb, 1, N_MOD * d)
    mod_s = mod_all[nb:nb + ns]

    def trunk(x, mod, rows_per_mod, cs, sn, table_blocks, t):
        h1 = _modulated_norm(x, mod, rows_per_mod, g_norm_ffn1[0], 0, t["tm_norm"])
        g1 = _ffn_up(h1, ffn1_w1, ffn1_w3, t["tm_up"], t["tf"])
        x1 = _ffn_down(g1, ffn1_w2, x, mod, rows_per_mod, 2, t["tm_down"], t["tn_down"])
        hm, q, kx, vx, kf, vf = _inproj_qkv(x1, mod, rows_per_mod, g_norm_mix[0], 3, w_in, cs, sn, gqk,
                                            head_reduce, head_expand, t["tm_qkv"], table_blocks)
        z = _inproj_z(hm, w_in, t["tm_z"], t["tn_z"])
        return x1, z, q, kx, vx, kf, vf

    def tail(o_att, o_rnn, z, x1, mod, rows_per_mod, t):
        x2, h2 = _mix(o_att, o_rnn, z, x1, mod, rows_per_mod, g_norm_ffn2[0], wpa, wpr, wout, 5, 6, t["tm_mix"])
        g2 = _ffn_up(h2, ffn2_w1, ffn2_w3, t["tm_up"], t["tf"])
        return _ffn_down(g2, ffn2_w2, x2, mod, rows_per_mod, 8, t["tm_down"], t["tn_down"])

    tp = _tiles(nb * seq, dff)
    tp = {k: (min(v, seq) if k.startswith("tm") else v) for k, v in tp.items()}
    cs_p, sn_p = _rope_tables(inv_row, 0, seq, tp["tm_qkv"])
    x1, z, q, kx, vx, kf, vf = trunk(x_prompt.reshape(nb * seq, d), mod_p, seq, cs_p, sn_p,
                                     seq // tp["tm_qkv"], tp)
    nblk = seq // WINDOW
    o_att = _attn(sink_vec, q, kx, vx, nb, nblk, next(n for n in (4, 2, 1) if nblk % n == 0))
    o_rnn, h_p, conv_p = _rnn_prompt(z, nb, seq, conv_w[0], conv_b[0], wrg, wig, b_rg[0], b_ig[0], lam,
                                     min(512, seq))
    y_p = tail(o_att, o_rnn, z, x1, mod_p, seq, tp)

    last = lambda a: a.reshape(nb, seq, D_KV)[:, seq - WINDOW:].reshape(1, nb, WINDOW, N_KV_HEADS, HEAD_DIM)
    k_prompt, v_prompt = last(kf), last(vf)

    ts = _tiles(ns, dff)
    cs_1, sn_1 = _rope_tables(inv_row, PAST_LEN, SUBLANES, SUBLANES)
    cs_s = jnp.broadcast_to(cs_1[0:1], (ns, LANES))
    sn_s = jnp.broadcast_to(sn_1[0:1], (ns, LANES))
    x1s, zs, qs, kxs, vxs, kfs, vfs = trunk(x_sample.reshape(ns, d), mod_s, 1, cs_s, sn_s, 1, ts)
    nsub = 2 * SUBLANES
    assert ns % nsub == 0
    o_att_s = _attn_step(sink_vec, qs, kxs, vxs, cache_k[0].reshape(ns * WINDOW, D_KV),
                         cache_v[0].reshape(ns * WINDOW, D_KV), nsub)
    conv_s_in = state_conv[0]
    o_rnn_s, h_s = _rnn_step(zs, conv_s_in, state_h[0], conv_w[0], conv_b[0], wrg, wig, b_rg[0], b_ig[0], lam)
    y_s = tail(o_att_s, o_rnn_s, zs, x1s, mod_s, 1, ts)

    k_sample = kfs.reshape(1, ns, 1, N_KV_HEADS, HEAD_DIM)
    v_sample = vfs.reshape(1, ns, 1, N_KV_HEADS, HEAD_DIM)
    conv_sample = jnp.concatenate([conv_s_in[:, 1:], zs[:, None, Z_RX:Z_RX + D_RNN]], axis=1)[None]

    return (y_p.reshape(nb, seq, d), y_s.reshape(ns, 1, d), k_prompt, v_prompt, k_sample, v_sample,
            h_p.reshape(1, nb, D_RNN), h_s[None], conv_p[None], conv_sample)
```

```python
import functools

import jax
import jax.numpy as jnp
from jax import lax
from jax.experimental import pallas as pl
from jax.experimental.pallas import tpu as pltpu

F32 = jnp.float32
BF16 = jnp.bfloat16

D_MODEL = 2048
HEAD_DIM = 64
N_Q_HEADS = 16
N_KV_HEADS = 4
Q_PER_KV = N_Q_HEADS // N_KV_HEADS
D_ATTN = N_Q_HEADS * HEAD_DIM
D_KV = N_KV_HEADS * HEAD_DIM
D_QK = D_ATTN + D_KV
D_QKV = D_QK + D_KV
WINDOW = 128
ROPE_THETA = 10000.0
D_RNN = 1024
N_RNN_BLOCKS = 16
RNN_BLOCK = D_RNN // N_RNN_BLOCKS
CONV_WIDTH = 4
LRU_C = 8.0
N_MOD = 9
EPS = 1e-6
PAST_LEN = 16384

LANES = 128
SUBLANES = 8
MXU_DIM = 256
VMEM_LIMIT = 56 * 1024 * 1024

Z_RX = 0
Z_RG = Z_RX + D_RNN
Z_GATT = Z_RG + D_RNN
Z_GRNN = Z_GATT + D_MODEL
D_Z = Z_GRNN + D_MODEL
DX = 2 * N_KV_HEADS * LANES


def _params(*sem):
    return pltpu.CompilerParams(dimension_semantics=sem, vmem_limit_bytes=VMEM_LIMIT)


def _silu(x):
    return x * jax.nn.sigmoid(x)


def _modulate(x, g, shift, scale):
    ms = jnp.mean(x * x, axis=-1, keepdims=True)
    return (x * lax.rsqrt(ms + EPS) * g) * (1.0 + scale) + shift


def _mod_spec(mod, tm, rows_per_mod, width, col, ngrid):
    colf = col if callable(col) else (lambda *idx: col)
    if mod.ndim == 3:
        return pl.BlockSpec((None, 1, width), lambda *idx: ((idx[ngrid - 1] * tm) // rows_per_mod, 0, colf(*idx)))
    return pl.BlockSpec((tm, width), lambda *idx: (idx[ngrid - 1], colf(*idx)))


def _ada_kernel(c_ref, w_ref, b_ref, o_ref):
    s = _silu(c_ref[...]).astype(BF16)
    o_ref[...] = jnp.dot(s, w_ref[...].astype(BF16), preferred_element_type=F32) + b_ref[...]


def _ada(c_all, w_ada, b_ada, tn):
    rows, d = c_all.shape
    n = w_ada.shape[1]
    return pl.pallas_call(
        _ada_kernel,
        grid=(n // tn,),
        in_specs=[pl.BlockSpec((rows, d), lambda j: (0, 0)),
                  pl.BlockSpec((d, tn), lambda j: (0, j)),
                  pl.BlockSpec((1, tn), lambda j: (0, j))],
        out_specs=pl.BlockSpec((rows, tn), lambda j: (0, j)),
        out_shape=jax.ShapeDtypeStruct((rows, n), F32),
        compiler_params=_params("arbitrary"),
        name="ada",
    )(c_all, w_ada, b_ada.reshape(1, n))


def _rope_kernel(inv_ref, cs_ref, sn_ref, *, base, blk):
    r = pl.program_id(0)
    pos = (base + r * blk + lax.broadcasted_iota(jnp.int32, (blk, LANES), 0)).astype(F32)
    ang = pos * inv_ref[...]
    lane = lax.broadcasted_iota(jnp.int32, (blk, LANES), 1)
    first_half = (lane & (HEAD_DIM - 1)) < HEAD_DIM // 2
    sn = jnp.sin(ang)
    cs_ref[...] = jnp.cos(ang)
    sn_ref[...] = jnp.where(first_half, -sn, sn)


def _rope_tables(inv_row, base, rows, blk):
    out = jax.ShapeDtypeStruct((rows, LANES), F32)
    return pl.pallas_call(
        functools.partial(_rope_kernel, base=base, blk=blk),
        grid=(rows // blk,),
        in_specs=[pl.BlockSpec((1, LANES), lambda r: (0, 0))],
        out_specs=[pl.BlockSpec((blk, LANES), lambda r: (r, 0))] * 2,
        out_shape=[out, out],
        compiler_params=_params("arbitrary"),
        name="rope_tables",
    )(inv_row)


def _modulate_kernel(x_ref, shift_ref, scale_ref, g_ref, h_ref):
    h_ref[...] = _modulate(x_ref[...], g_ref[...], shift_ref[...], scale_ref[...]).astype(BF16)


def _modulated_norm(x, mod, rows_per_mod, g, k_shift, tm):
    m_rows, d = x.shape
    return pl.pallas_call(
        _modulate_kernel,
        grid=(m_rows // tm,),
        in_specs=[pl.BlockSpec((tm, d), lambda m: (m, 0)),
                  _mod_spec(mod, tm, rows_per_mod, d, k_shift, 1),
                  _mod_spec(mod, tm, rows_per_mod, d, k_shift + 1, 1),
                  pl.BlockSpec((1, d), lambda m: (0, 0))],
        out_specs=pl.BlockSpec((tm, d), lambda m: (m, 0)),
        out_shape=jax.ShapeDtypeStruct((m_rows, d), BF16),
        compiler_params=_params("arbitrary"),
        name="modnorm",
    )(x, mod, mod, g.reshape(1, d))


def _fetch_weight_tiles(weights, tn, first):
    t = pl.program_id(0)
    nt = pl.num_programs(0)

    def copy(w_hbm, stage_ref, sem, tile):
        slot = tile % stage_ref.shape[0]
        cols = pl.ds(pl.multiple_of((first + tile) * tn, LANES), tn)
        return pltpu.make_async_copy(w_hbm.at[0, :, cols], stage_ref.at[slot], sem.at[slot])

    @pl.when(pl.program_id(1) == 0)
    def _():
        for w_hbm, stage_ref, wb_ref, sem in weights:
            slots = stage_ref.shape[0]

            @pl.when(t == 0)
            def _():
                copy(w_hbm, stage_ref, sem, 0).start()

            def start_next():
                @pl.when(t + 1 < nt)
                def _():
                    copy(w_hbm, stage_ref, sem, t + 1).start()

            if slots > 1:
                start_next()
            copy(w_hbm, stage_ref, sem, t).wait()
            wb_ref[...] = stage_ref[t % slots].astype(BF16)
            if slots == 1:
                start_next()


def _weight_scratch(rows, tn, slots):
    return [pltpu.VMEM((slots, rows, tn), F32), pltpu.VMEM((rows, tn), BF16), pltpu.SemaphoreType.DMA((slots,))]


def _row_groups(tm):
    n = max(1, tm // 1024)
    return [slice(r * (tm // n), (r + 1) * (tm // n)) for r in range(n)]


def _ffn_up_kernel(h_ref, w1_hbm, w3_hbm, g_ref, s1_ref, w1b_ref, sem1, s3_ref, w3b_ref, sem3, *, tf):
    _fetch_weight_tiles(((w1_hbm, s1_ref, w1b_ref, sem1), (w3_hbm, s3_ref, w3b_ref, sem3)), tf, 0)
    for rows in _row_groups(h_ref.shape[0]):
        h = h_ref[rows, :]
        a = jnp.dot(h, w1b_ref[...], preferred_element_type=F32)
        b = jnp.dot(h, w3b_ref[...], preferred_element_type=F32)
        g_ref[rows, :] = (_silu(a) * b).astype(BF16)


def _ffn_up(h, w1, w3, tm, tf, slots):
    m_rows, d = h.shape
    dff = w1.shape[2]
    hbm = pl.BlockSpec(memory_space=pl.ANY)
    return pl.pallas_call(
        functools.partial(_ffn_up_kernel, tf=tf),
        grid=(dff // tf, m_rows // tm),
        in_specs=[pl.BlockSpec((tm, d), lambda f, m: (m, 0)), hbm, hbm],
        out_specs=pl.BlockSpec((tm, tf), lambda f, m: (m, f)),
        out_shape=jax.ShapeDtypeStruct((m_rows, dff), BF16),
        scratch_shapes=_weight_scratch(d, tf, slots) + _weight_scratch(d, tf, slots),
        compiler_params=_params("arbitrary", "arbitrary"),
        name="ffn_up",
    )(h, w1, w3)


def _ffn_down_kernel(g_ref, w2_hbm, x_ref, gate_ref, o_ref, s2_ref, w2b_ref, sem2, *, tn):
    _fetch_weight_tiles(((w2_hbm, s2_ref, w2b_ref, sem2),), tn, 0)
    for rows in _row_groups(g_ref.shape[0]):
        acc = jnp.dot(g_ref[rows, :], w2b_ref[...], preferred_element_type=F32)
        gate = gate_ref[...] if gate_ref.shape[0] == 1 else gate_ref[rows, :]
        o_ref[rows, :] = x_ref[rows, :] + 0.5 * gate * acc


def _ffn_down(g, w2, x, mod, rows_per_mod, k_gate, tm, tn, slots):
    m_rows, d = x.shape
    dff = g.shape[1]
    gate_col = lambda n, m: k_gate * (d // tn) + n
    return pl.pallas_call(
        functools.partial(_ffn_down_kernel, tn=tn),
        grid=(d // tn, m_rows // tm),
        in_specs=[pl.BlockSpec((tm, dff), lambda n, m: (m, 0)),
                  pl.BlockSpec(memory_space=pl.ANY),
                  pl.BlockSpec((tm, tn), lambda n, m: (m, n)),
                  _mod_spec(mod, tm, rows_per_mod, tn, gate_col, 2)],
        out_specs=pl.BlockSpec((tm, tn), lambda n, m: (m, n)),
        out_shape=jax.ShapeDtypeStruct((m_rows, d), F32),
        scratch_shapes=_weight_scratch(dff, tn, slots),
        compiler_params=_params("arbitrary", "arbitrary"),
        name="ffn_down",
    )(g, w2, x, mod)


def _inproj_z_kernel(h_ref, w_hbm, z_ref, s_ref, wb_ref, sem, *, tn, first):
    _fetch_weight_tiles(((w_hbm, s_ref, wb_ref, sem),), tn, first)
    for rows in _row_groups(h_ref.shape[0]):
        z_ref[rows, :] = jnp.dot(h_ref[rows, :], wb_ref[...], preferred_element_type=F32)


def _inproj_z(h, w_in, tm, tn, slots):
    m_rows, d = h.shape
    return pl.pallas_call(
        functools.partial(_inproj_z_kernel, tn=tn, first=D_QKV // tn),
        grid=(D_Z // tn, m_rows // tm),
        in_specs=[pl.BlockSpec((tm, d), lambda n, m: (m, 0)), pl.BlockSpec(memory_space=pl.ANY)],
        out_specs=pl.BlockSpec((tm, tn), lambda n, m: (m, n)),
        out_shape=jax.ShapeDtypeStruct((m_rows, D_Z), F32),
        scratch_shapes=_weight_scratch(d, tn, slots),
        compiler_params=_params("arbitrary", "arbitrary"),
        name="inproj_z",
    )(h, w_in)


def _store_head_pairs(ref, rows, chunk, c):
    lo = lax.broadcasted_iota(jnp.int32, chunk.shape, 1) < HEAD_DIM
    swapped = pltpu.roll(chunk, HEAD_DIM, 1)
    zero = jnp.zeros_like(chunk)
    cols = (jnp.where(lo, chunk, zero), jnp.where(lo, zero, swapped),
            jnp.where(lo, swapped, zero), jnp.where(lo, zero, chunk))
    for i, col in enumerate(cols):
        ref[rows, (4 * c + i) * LANES:(4 * c + i + 1) * LANES] = col.astype(ref.dtype)


def _inproj_qkv_kernel(x_ref, shift_ref, scale_ref, g_ref, w_ref, cs_ref, sn_ref, gqk_ref, red_ref, exp_ref,
                       h_ref, q_ref, kx_ref, vx_ref, kf_ref, vf_ref, wb_ref, *, nsplit):
    @pl.when(pl.program_id(0) == 0)
    def _():
        wb_ref[...] = w_ref[...].astype(BF16)

    step = x_ref.shape[0] // nsplit
    for r in range(nsplit):
        rows = slice(r * step, (r + 1) * step)
        shift = shift_ref[...] if shift_ref.shape[0] == 1 else shift_ref[rows, :]
        scale = scale_ref[...] if scale_ref.shape[0] == 1 else scale_ref[rows, :]
        h = _modulate(x_ref[rows, :], g_ref[...], shift, scale).astype(BF16)
        h_ref[rows, :] = h
        acc = jnp.dot(h, wb_ref[...], preferred_element_type=F32)
        qk = acc[:, :D_QK]
        hm = jnp.dot((qk * qk).astype(BF16), red_ref[...], preferred_element_type=F32)
        hi = hm.astype(BF16)
        lo = (hm - hi.astype(F32)).astype(BF16)
        ms = jnp.dot(jnp.concatenate([hi, lo], axis=1), exp_ref[...], preferred_element_type=F32)
        y = qk * lax.rsqrt(ms + EPS) * gqk_ref[...]
        cs = cs_ref[rows, :]
        sn = sn_ref[rows, :]
        lane = lax.broadcasted_iota(jnp.int32, cs.shape, 1)
        first_half = (lane & (HEAD_DIM - 1)) < HEAD_DIM // 2
        for c in range(D_QK // LANES):
            yc = y[:, c * LANES:(c + 1) * LANES]
            partner = jnp.where(first_half,
                                pltpu.roll(yc, LANES - HEAD_DIM // 2, 1),
                                pltpu.roll(yc, HEAD_DIM // 2, 1))
            rot = yc * cs + partner * sn
            if c < D_ATTN // LANES:
                q_ref[rows, c * LANES:(c + 1) * LANES] = (rot * (HEAD_DIM ** -0.5)).astype(BF16)
            else:
                kf_ref[rows, c * LANES - D_ATTN:(c + 1) * LANES - D_ATTN] = rot
                _store_head_pairs(kx_ref, rows, rot, c - D_ATTN // LANES)
        v = acc[:, D_QK:]
        vf_ref[rows, :] = v
        for c in range(D_KV // LANES):
            _store_head_pairs(vx_ref, rows, v[:, c * LANES:(c + 1) * LANES], c)


def _inproj_qkv(x, mod, rows_per_mod, g, k_shift, w_in, cs, sn, gqk, head_reduce, head_expand, tm, table_blocks):
    m_rows, d = x.shape
    row = lambda width: pl.BlockSpec((tm, width), lambda m: (m, 0))
    table = pl.BlockSpec((tm, LANES), lambda m: (m % table_blocks, 0))
    nsplit = 2 if tm % (2 * MXU_DIM) == 0 else 1
    return pl.pallas_call(
        functools.partial(_inproj_qkv_kernel, nsplit=nsplit),
        grid=(m_rows // tm,),
        in_specs=[row(d),
                  _mod_spec(mod, tm, rows_per_mod, d, k_shift, 1),
                  _mod_spec(mod, tm, rows_per_mod, d, k_shift + 1, 1),
                  pl.BlockSpec((1, d), lambda m: (0, 0)),
                  pl.BlockSpec((None, d, D_QKV), lambda m: (0, 0, 0), pipeline_mode=pl.Buffered(1)),
                  table, table,
                  pl.BlockSpec((1, D_QK), lambda m: (0, 0)),
                  pl.BlockSpec(head_reduce.shape, lambda m: (0, 0)),
                  pl.BlockSpec(head_expand.shape, lambda m: (0, 0))],
        out_specs=[row(d), row(D_ATTN), row(DX), row(DX), row(D_KV), row(D_KV)],
        out_shape=[jax.ShapeDtypeStruct((m_rows, d), BF16),
                   jax.ShapeDtypeStruct((m_rows, D_ATTN), BF16),
                   jax.ShapeDtypeStruct((m_rows, DX), BF16),
                   jax.ShapeDtypeStruct((m_rows, DX), BF16),
                   jax.ShapeDtypeStruct((m_rows, D_KV), F32),
                   jax.ShapeDtypeStruct((m_rows, D_KV), F32)],
        scratch_shapes=[pltpu.VMEM((d, D_QKV), BF16)],
        compiler_params=_params("arbitrary"),
        name="inproj_qkv",
    )(x, mod, mod, g.reshape(1, d), w_in, cs, sn, gqk, head_reduce, head_expand)


def _attn_bias(bias_ref, tq, past_off):
    nk = 2 * WINDOW
    ri = lax.broadcasted_iota(jnp.int32, (2 * tq, 2 * nk), 0) & (tq - 1)
    kj = lax.broadcasted_iota(jnp.int32, (2 * tq, 2 * nk), 1) & (nk - 1)
    visible = ((kj < WINDOW) & (kj > ri + past_off)) | ((kj >= WINDOW) & (kj - WINDOW <= ri))
    bias_ref[...] = jnp.where(visible, 0.0, -jnp.inf)


def _attn_blocks(sink_ref, bias_of, q_ref, past_of, kc_ref, vc_ref, nsub, tq, store):
    nk = 2 * WINDOW
    rows = 2 * tq
    first_rows = lax.broadcasted_iota(jnp.int32, (rows, 1), 0) < tq
    lo_lanes = lax.broadcasted_iota(jnp.int32, (rows, LANES), 1) < HEAD_DIM
    ones_lo = (lax.broadcasted_iota(jnp.int32, (nk, LANES), 1) < HEAD_DIM).astype(BF16)
    ones_hi = (1 - ones_lo.astype(F32)).astype(BF16)
    nt = (((1,), (1,)), ((), ()))

    def keys(which, s, col):
        cols = slice(col * LANES, (col + 1) * LANES)
        past = past_of(s)
        parts = [past[which][past[2]:past[2] + WINDOW, cols], (kc_ref, vc_ref)[which][s * tq:(s + 1) * tq, cols]]
        if tq < WINDOW:
            parts.append(jnp.zeros((WINDOW - tq, LANES), BF16))
        return parts

    for s in range(nsub):
        for g in range(N_KV_HEADS):
            c0, c1 = 2 * g, 2 * g + 1
            q4 = jnp.concatenate([q_ref[s * tq:(s + 1) * tq, c0 * LANES:(c0 + 1) * LANES],
                                  q_ref[s * tq:(s + 1) * tq, c1 * LANES:(c1 + 1) * LANES]], axis=0)
            kk = jnp.concatenate(keys(0, s, c0) + keys(0, s, c1), axis=0)
            sc = lax.dot_general(q4, kk, nt, preferred_element_type=F32) + bias_of(s)[...]
            sk_lo = jnp.where(first_rows, sink_ref[4 * g], sink_ref[4 * g + 2])
            sk_hi = jnp.where(first_rows, sink_ref[4 * g + 1], sink_ref[4 * g + 3])
            m_lo = jnp.maximum(jnp.max(sc[:, :nk], axis=-1, keepdims=True), sk_lo)
            m_hi = jnp.maximum(jnp.max(sc[:, nk:], axis=-1, keepdims=True), sk_hi)
            p = jnp.concatenate([jnp.exp(sc[:, :nk] - m_lo), jnp.exp(sc[:, nk:] - m_hi)], axis=1).astype(BF16)
            vv = jnp.concatenate(
                [jnp.concatenate([jnp.concatenate(keys(1, s, c0), axis=0), ones_lo], axis=1),
                 jnp.concatenate([jnp.concatenate(keys(1, s, c1), axis=0), ones_hi], axis=1)],
                axis=0)
            o = jnp.dot(p, vv, preferred_element_type=F32)
            denom = o[:, LANES:] + jnp.where(lo_lanes, jnp.exp(sk_lo - m_lo), jnp.exp(sk_hi - m_hi))
            store(s, c0, c1, o[:, :LANES] / denom)


def _attn_kernel(sink_ref, q_ref, kp_ref, kc_ref, vp_ref, vc_ref, o_ref, bias0_ref, bias_ref, *, nq):
    tq = WINDOW
    _attn_bias(bias0_ref, tq, jnp.where(pl.program_id(1) > 0, 0, WINDOW))
    if nq > 1:
        _attn_bias(bias_ref, tq, 0)

    def store(s, c0, c1, out):
        o_ref[s * tq:(s + 1) * tq, c0 * LANES:(c0 + 1) * LANES] = out[:tq].astype(BF16)
        o_ref[s * tq:(s + 1) * tq, c1 * LANES:(c1 + 1) * LANES] = out[tq:].astype(BF16)

    past_of = lambda s: (kp_ref, vp_ref, 0) if s == 0 else (kc_ref, vc_ref, (s - 1) * WINDOW)
    bias_of = lambda s: bias0_ref if s == 0 else bias_ref
    _attn_blocks(sink_ref, bias_of, q_ref, past_of, kc_ref, vc_ref, nq, tq, store)


def _attn(sinks, q, kx, vx, nbatch, nblk, nq):
    steps = nblk // nq
    cur = lambda b, n: (b * steps + n, 0)
    past = lambda b, n: (b * nblk + jnp.maximum(n * nq - 1, 0), 0)
    kv_cur = pl.BlockSpec((nq * WINDOW, DX), cur)
    kv_past = pl.BlockSpec((WINDOW, DX), past)
    bias = pltpu.VMEM((2 * WINDOW, 4 * WINDOW), F32)
    return pl.pallas_call(
        functools.partial(_attn_kernel, nq=nq),
        grid=(nbatch, steps),
        in_specs=[pl.BlockSpec(memory_space=pltpu.SMEM), pl.BlockSpec((nq * WINDOW, D_ATTN), cur),
                  kv_past, kv_cur, kv_past, kv_cur],
        out_specs=pl.BlockSpec((nq * WINDOW, D_ATTN), cur),
        out_shape=jax.ShapeDtypeStruct(q.shape, BF16),
        scratch_shapes=[bias, bias],
        compiler_params=_params("arbitrary", "arbitrary"),
        name="attn",
    )(sinks, q, kx, kx, vx, vx)


def _attn_step_kernel(sink_ref, q_ref, kn_ref, vn_ref, ck_ref, cv_ref, o_ref,
                      bias_ref, qx_ref, kcx_ref, vcx_ref, kpx_ref, vpx_ref, *, nsub, tq):
    _attn_bias(bias_ref, tq, 0)
    for c in range(D_KV // LANES):
        _store_head_pairs(kpx_ref, slice(None), ck_ref[:, c * LANES:(c + 1) * LANES], c)
        _store_head_pairs(vpx_ref, slice(None), cv_ref[:, c * LANES:(c + 1) * LANES], c)
    for src, dst in ((q_ref, qx_ref), (kn_ref, kcx_ref), (vn_ref, vcx_ref)):
        rows = src[...].astype(F32)
        for s in range(nsub):
            dst[s * tq:(s + 1) * tq, :] = jnp.broadcast_to(rows[s:s + 1, :], (tq, rows.shape[1])).astype(BF16)

    def store(s, c0, c1, out):
        o_ref[s:s + 1, c0 * LANES:(c0 + 1) * LANES] = out[0:1]
        o_ref[s:s + 1, c1 * LANES:(c1 + 1) * LANES] = out[tq:tq + 1]

    past_of = lambda s: (kpx_ref, vpx_ref, s * WINDOW)
    _attn_blocks(sink_ref, lambda s: bias_ref, qx_ref, past_of, kcx_ref, vcx_ref, nsub, tq, store)


def _attn_step(sinks, q, kx_new, vx_new, cache_k, cache_v, nsub):
    ns = q.shape[0]
    tq = 2 * SUBLANES
    row = lambda width: pl.BlockSpec((nsub, width), lambda b: (b, 0))
    cache = pl.BlockSpec((nsub * WINDOW, D_KV), lambda b: (b, 0))
    return pl.pallas_call(
        functools.partial(_attn_step_kernel, nsub=nsub, tq=tq),
        grid=(ns // nsub,),
        in_specs=[pl.BlockSpec(memory_space=pltpu.SMEM), row(D_ATTN), row(DX), row(DX), cache, cache],
        out_specs=row(D_ATTN),
        out_shape=jax.ShapeDtypeStruct((ns, D_ATTN), F32),
        scratch_shapes=[pltpu.VMEM((2 * tq, 4 * WINDOW), F32),
                        pltpu.VMEM((nsub * tq, D_ATTN), BF16),
                        pltpu.VMEM((nsub * tq, DX), BF16),
                        pltpu.VMEM((nsub * tq, DX), BF16),
                        pltpu.VMEM((nsub * WINDOW, DX), BF16),
                        pltpu.VMEM((nsub * WINDOW, DX), BF16)],
        compiler_params=_params("arbitrary"),
        name="attn_step",
    )(sinks, q, kx_new, vx_new, cache_k, cache_v)


def _softplus(x):
    return jnp.maximum(x, 0.0) + jnp.log1p(jnp.exp(-jnp.abs(x)))


def _lru_coeffs(xc, wrg_ref, wig_ref, brg, big, lam):
    xb = xc.astype(BF16)
    ngroups = D_RNN // MXU_DIM
    rs, igs = [], []
    for c in range(ngroups):
        xg = xb[:, c * MXU_DIM:(c + 1) * MXU_DIM]
        rs.append(jnp.dot(xg, wrg_ref[c], preferred_element_type=F32))
        igs.append(jnp.dot(xg, wig_ref[c], preferred_element_type=F32))
    r = jax.nn.sigmoid(jnp.concatenate(rs, axis=1) + brg)
    ig = jax.nn.sigmoid(jnp.concatenate(igs, axis=1) + big)
    log_a = -LRU_C * r * _softplus(-lam)
    a = jnp.exp(log_a)
    one_minus_a2 = -jnp.tanh(log_a) * (1.0 + a * a)
    u = jnp.sqrt(one_minus_a2) * (ig * xc)
    return a, u


def _rnn_prompt_kernel(rx_ref, rg_ref, cw_ref, cb_ref, wrg_ref, wig_ref, brg_ref, big_ref, lam_ref,
                       o_ref, hlast_ref, conv_ref, xs_ref, a_ref, h_ref, carry_ref, *, tc):
    t = pl.program_id(1)
    pad = SUBLANES

    @pl.when(t == 0)
    def _():
        xs_ref[0:pad, :] = jnp.zeros((pad, D_RNN), F32)
        carry_ref[...] = jnp.zeros_like(carry_ref)

    x = rx_ref[...]
    xs_ref[pad:pad + tc, :] = x
    xc = cb_ref[...] + x * cw_ref[CONV_WIDTH - 1:CONV_WIDTH, :]
    for k in range(1, CONV_WIDTH):
        xc = xc + xs_ref[pad - k:pad - k + tc, :] * cw_ref[CONV_WIDTH - 1 - k:CONV_WIDTH - k, :]
    tail = xs_ref[tc:tc + pad, :]
    xs_ref[0:pad, :] = tail
    conv_ref[...] = tail[pad - (CONV_WIDTH - 1):, :]

    a, u = _lru_coeffs(xc, wrg_ref, wig_ref, brg_ref[...], big_ref[...], lam_ref[...])
    a_ref[...] = a
    h_ref[...] = u

    row = lax.broadcasted_iota(jnp.int32, (SUBLANES, D_RNN), 0)

    def body(r, carry):
        off = pl.multiple_of(r * SUBLANES, SUBLANES)
        av = a_ref[pl.ds(off, SUBLANES), :]
        hv = h_ref[pl.ds(off, SUBLANES), :]
        for sft in (1, 2, 4):
            keep = row >= sft
            a_sh = jnp.where(keep, pltpu.roll(av, sft, 0), 1.0)
            h_sh = jnp.where(keep, pltpu.roll(hv, sft, 0), 0.0)
            hv = av * h_sh + hv
            av = av * a_sh
        hv = hv + av * carry
        h_ref[pl.ds(off, SUBLANES), :] = hv
        return jnp.broadcast_to(hv[SUBLANES - 1:SUBLANES, :], (SUBLANES, D_RNN))

    carry = lax.fori_loop(0, tc // SUBLANES, body, carry_ref[...])
    carry_ref[...] = carry
    hlast_ref[...] = carry[0:1, :]
    o_ref[...] = (h_ref[...] * jax.nn.gelu(rg_ref[...])).astype(BF16)


def _rnn_prompt(z, nbatch, seq, conv_w, conv_b, wrg, wig, b_rg, b_ig, lam, tc):
    nchunk = seq // tc
    rx_blk = Z_RX // D_RNN
    rg_blk = Z_RG // D_RNN
    vec = pl.BlockSpec((1, D_RNN), lambda b, t: (0, 0))
    wspec = pl.BlockSpec(wrg.shape, lambda b, t: (0, 0, 0))
    return pl.pallas_call(
        functools.partial(_rnn_prompt_kernel, tc=tc),
        grid=(nbatch, nchunk),
        in_specs=[pl.BlockSpec((tc, D_RNN), lambda b, t: (b * nchunk + t, rx_blk)),
                  pl.BlockSpec((tc, D_RNN), lambda b, t: (b * nchunk + t, rg_blk)),
                  pl.BlockSpec((CONV_WIDTH, D_RNN), lambda b, t: (0, 0)),
                  vec, wspec, wspec, vec, vec, vec],
        out_specs=[pl.BlockSpec((tc, D_RNN), lambda b, t: (b * nchunk + t, 0)),
                   pl.BlockSpec((None, 1, D_RNN), lambda b, t: (b, 0, 0)),
                   pl.BlockSpec((None, CONV_WIDTH - 1, D_RNN), lambda b, t: (b, 0, 0))],
        out_shape=[jax.ShapeDtypeStruct((nbatch * seq, D_RNN), BF16),
                   jax.ShapeDtypeStruct((nbatch, 1, D_RNN), F32),
                   jax.ShapeDtypeStruct((nbatch, CONV_WIDTH - 1, D_RNN), F32)],
        scratch_shapes=[pltpu.VMEM((tc + SUBLANES, D_RNN), F32),
                        pltpu.VMEM((tc, D_RNN), F32),
                        pltpu.VMEM((tc, D_RNN), F32),
                        pltpu.VMEM((SUBLANES, D_RNN), F32)],
        compiler_params=_params("arbitrary", "arbitrary"),
        name="rnn_prompt",
    )(z, z, conv_w, conv_b.reshape(1, D_RNN), wrg, wig, b_rg.reshape(1, D_RNN),
      b_ig.reshape(1, D_RNN), lam.reshape(1, D_RNN))


def _rnn_step_kernel(rx_ref, rg_ref, c0_ref, c1_ref, c2_ref, h0_ref, cw_ref, cb_ref, wrg_ref, wig_ref,
                     brg_ref, big_ref, lam_ref, o_ref, h_ref):
    x = rx_ref[...]
    xc = (cb_ref[...] + c0_ref[...] * cw_ref[0:1, :] + c1_ref[...] * cw_ref[1:2, :]
          + c2_ref[...] * cw_ref[2:3, :] + x * cw_ref[3:4, :])
    a, u = _lru_coeffs(xc, wrg_ref, wig_ref, brg_ref[...], big_ref[...], lam_ref[...])
    h = a * h0_ref[...] + u
    h_ref[...] = h
    o_ref[...] = (h * jax.nn.gelu(rg_ref[...])).astype(BF16)


def _rnn_step(z, conv_state, h0, conv_w, conv_b, wrg, wig, b_rg, b_ig, lam):
    rows = z.shape[0]
    full = lambda shape: pl.BlockSpec(shape, lambda i: (0,) * len(shape))
    act = full((rows, D_RNN))
    vec = full((1, D_RNN))
    return pl.pallas_call(
        _rnn_step_kernel,
        grid=(1,),
        in_specs=[pl.BlockSpec((rows, D_RNN), lambda i: (0, Z_RX // D_RNN)),
                  pl.BlockSpec((rows, D_RNN), lambda i: (0, Z_RG // D_RNN)),
                  act, act, act, act, full((CONV_WIDTH, D_RNN)), vec,
                  full(wrg.shape), full(wig.shape), vec, vec, vec],
        out_specs=[act, act],
        out_shape=[jax.ShapeDtypeStruct((rows, D_RNN), BF16),
                   jax.ShapeDtypeStruct((rows, D_RNN), F32)],
        compiler_params=_params("arbitrary"),
        name="rnn_step",
    )(z, z, conv_state[:, 0], conv_state[:, 1], conv_state[:, 2], h0, conv_w,
      conv_b.reshape(1, D_RNN), wrg, wig, b_rg.reshape(1, D_RNN), b_ig.reshape(1, D_RNN),
      lam.reshape(1, D_RNN))


def _mix_kernel(oa_ref, or_ref, ga_ref, gr_ref, x_ref, gate_ref, shift_ref, scale_ref, g_ref,
                wpa_ref, wpr_ref, wout_ref, o_ref, h_ref):
    pa = jnp.dot(oa_ref[...].astype(BF16), wpa_ref[...], preferred_element_type=F32)
    pr = jnp.dot(or_ref[...], wpr_ref[...], preferred_element_type=F32)
    mix = jax.nn.sigmoid(ga_ref[...]) * pa + jax.nn.sigmoid(gr_ref[...]) * pr
    x = x_ref[...] + gate_ref[...] * jnp.dot(mix.astype(BF16), wout_ref[...], preferred_element_type=F32)
    o_ref[...] = x
    h_ref[...] = _modulate(x, g_ref[...], shift_ref[...], scale_ref[...]).astype(BF16)


def _mix(o_att, o_rnn, z, x, mod, rows_per_mod, g_next, w_pa, w_pr, w_out, k_gate, k_shift_next, tm):
    m_rows, d = x.shape
    const = lambda shape: pl.BlockSpec(shape, lambda m: (0, 0), pipeline_mode=pl.Buffered(1))
    row = lambda width, col: pl.BlockSpec((tm, width), lambda m: (m, col))
    mspec = lambda k: _mod_spec(mod, tm, rows_per_mod, d, k, 1)
    return pl.pallas_call(
        _mix_kernel,
        grid=(m_rows // tm,),
        in_specs=[row(D_ATTN, 0), row(D_RNN, 0), row(d, Z_GATT // d), row(d, Z_GRNN // d), row(d, 0),
                  mspec(k_gate), mspec(k_shift_next), mspec(k_shift_next + 1), const((1, d)),
                  const(w_pa.shape), const(w_pr.shape), const(w_out.shape)],
        out_specs=[row(d, 0), row(d, 0)],
        out_shape=[jax.ShapeDtypeStruct((m_rows, d), F32), jax.ShapeDtypeStruct((m_rows, d), BF16)],
        compiler_params=_params("arbitrary"),
        name="mix",
    )(o_att, o_rnn, z, z, x, mod, mod, mod, g_next.reshape(1, d), w_pa, w_pr, w_out)


def _block_diag(w, group):
    n, r, _ = w.shape
    eye = jnp.eye(group, dtype=w.dtype)
    wg = w.reshape(n // group, group, r, r)
    return jnp.einsum("ngij,gh->ngihj", wg, eye).reshape(n // group, group * r, group * r)


def _tiles(rows, dff):
    pick = lambda n, prefs: next((t for t in prefs if n % t == 0), n)
    return dict(
        tm_norm=pick(rows, (512,)),
        tm_up=pick(rows, (2048, 1024, 512)), tf=pick(dff, (512,)),
        tm_down=pick(rows, (1024, 512)), tn_down=512,
        tm_z=pick(rows, (1024, 512)), tn_z=D_QKV,
        tm_qkv=pick(rows, (512,)),
        tm_mix=pick(rows, (256,)),
        slots=1 if rows > 1024 else 2,
    )


def kernel(x_prompt, x_sample, c_prompt, c_sample, cache_k, cache_v, state_h, state_conv, w_ada, b_ada,
           g_norm_ffn1, g_norm_mix, g_norm_ffn2, ffn1_w1, ffn1_w3, ffn1_w2, ffn2_w1, ffn2_w3, ffn2_w2,
           w_in, g_q, g_k, sinks, conv_w, conv_b, w_rg, b_rg, w_ig, b_ig, lru_lambda, w_pa, w_pr, w_out):
    nb, seq, d = x_prompt.shape
    ns = x_sample.shape[0]
    dff = ffn1_w1.shape[2]
    assert d == D_MODEL and w_ada.shape[0] == 1 and x_sample.shape[1] == 1 and cache_k.shape[2] == WINDOW
    assert seq % WINDOW == 0 and w_in.shape[2] == D_QKV + D_Z

    group = MXU_DIM // RNN_BLOCK
    wrg = _block_diag(w_rg[0], group).astype(BF16)
    wig = _block_diag(w_ig[0], group).astype(BF16)
    wpa, wpr, wout = w_pa[0].astype(BF16), w_pr[0].astype(BF16), w_out[0].astype(BF16)
    gqk = jnp.concatenate([jnp.tile(g_q[0], N_Q_HEADS), jnp.tile(g_k[0], N_KV_HEADS)]).reshape(1, D_QK)
    head_of = jnp.arange(D_QK) // HEAD_DIM
    onehot = (head_of[:, None] == jnp.arange(LANES)[None, :]).astype(F32)
    head_reduce = (onehot / HEAD_DIM).astype(BF16)
    head_expand = jnp.concatenate([onehot.T, onehot.T], axis=0).astype(BF16)
    inv = ROPE_THETA ** (-jnp.arange(HEAD_DIM // 2, dtype=F32) * 2.0 / HEAD_DIM)
    inv_row = jnp.tile(inv, LANES // (HEAD_DIM // 2)).reshape(1, LANES)
    sink_vec = sinks[0]
    lam = lru_lambda[0]

    pad_rows = (-(nb + ns)) % SUBLANES
    c_all = jnp.concatenate([c_prompt, c_sample, jnp.zeros((pad_rows, d), F32)], axis=0)
    mod_all = _ada(c_all, w_ada[0], b_ada[0], 1024)
    mod_p = mod_all[:nb].reshape(nb, 1, N_MOD * d)
    mod_s = mod_all[nb:nb + ns]

    def trunk(x, mod, rows_per_mod, cs, sn, table_blocks, t):
        h1 = _modulated_norm(x, mod, rows_per_mod, g_norm_ffn1[0], 0, t["tm_norm"])
        g1 = _ffn_up(h1, ffn1_w1, ffn1_w3, t["tm_up"], t["tf"], t["slots"])
        x1 = _ffn_down(g1, ffn1_w2, x, mod, rows_per_mod, 2, t["tm_down"], t["tn_down"], t["slots"])
        hm, q, kx, vx, kf, vf = _inproj_qkv(x1, mod, rows_per_mod, g_norm_mix[0], 3, w_in, cs, sn, gqk,
                                            head_reduce, head_expand, t["tm_qkv"], table_blocks)
        z = _inproj_z(hm, w_in, t["tm_z"], t["tn_z"], t["slots"])
        return x1, z, q, kx, vx, kf, vf

    def tail(o_att, o_rnn, z, x1, mod, rows_per_mod, t):
        x2, h2 = _mix(o_att, o_rnn, z, x1, mod, rows_per_mod, g_norm_ffn2[0], wpa, wpr, wout, 5, 6, t["tm_mix"])
        g2 = _ffn_up(h2, ffn2_w1, ffn2_w3, t["tm_up"], t["tf"], t["slots"])
        return _ffn_down(g2, ffn2_w2, x2, mod, rows_per_mod, 8, t["tm_down"], t["tn_down"], t["slots"])

    tp = _tiles(nb * seq, dff)
    tp = {k: (min(v, seq) if k.startswith("tm") else v) for k, v in tp.items()}
    cs_p, sn_p = _rope_tables(inv_row, 0, seq, tp["tm_qkv"])
    x1, z, q, kx, vx, kf, vf = trunk(x_prompt.reshape(nb * seq, d), mod_p, seq, cs_p, sn_p,
                                     seq // tp["tm_qkv"], tp)
    nblk = seq // WINDOW
    o_att = _attn(sink_vec, q, kx, vx, nb, nblk, next(n for n in (4, 2, 1) if nblk % n == 0))
    o_rnn, h_p, conv_p = _rnn_prompt(z, nb, seq, conv_w[0], conv_b[0], wrg, wig, b_rg[0], b_ig[0], lam,
                                     min(512, seq))
    y_p = tail(o_att, o_rnn, z, x1, mod_p, seq, tp)

    last = lambda a: a.reshape(nb, seq, D_KV)[:, seq - WINDOW:].reshape(1, nb, WINDOW, N_KV_HEADS, HEAD_DIM)
    k_prompt, v_prompt = last(kf), last(vf)

    ts = _tiles(ns, dff)
    cs_1, sn_1 = _rope_tables(inv_row, PAST_LEN, SUBLANES, SUBLANES)
    cs_s = jnp.broadcast_to(cs_1[0:1], (ns, LANES))
    sn_s = jnp.broadcast_to(sn_1[0:1], (ns, LANES))
    x1s, zs, qs, kxs, vxs, kfs, vfs = trunk(x_sample.reshape(ns, d), mod_s, 1, cs_s, sn_s, 1, ts)
    nsub = 2 * SUBLANES
    assert ns % nsub == 0
    o_att_s = _attn_step(sink_vec, qs, kxs, vxs, cache_k[0].reshape(ns * WINDOW, D_KV),
                         cache_v[0].reshape(ns * WINDOW, D_KV), nsub)
    conv_s_in = state_conv[0]
    o_rnn_s, h_s = _rnn_step(zs, conv_s_in, state_h[0], conv_w[0], conv_b[0], wrg, wig, b_rg[0], b_ig[0], lam)
    y_s = tail(o_att_s, o_rnn_s, zs, x1s, mod_s, 1, ts)

    k_sample = kfs.reshape(1, ns, 1, N_KV_HEADS, HEAD_DIM)
    v_sample = vfs.reshape(1, ns, 1, N_KV_HEADS, HEAD_DIM)
    conv_sample = jnp.concatenate([conv_s_in[:, 1:], zs[:, None, Z_RX:Z_RX + D_RNN]], axis=1)[None]

    return (y_p.reshape(nb, seq, d), y_s.reshape(ns, 1, d), k_prompt, v_prompt, k_sample, v_sample,
            h_p.reshape(1, nb, D_RNN), h_s[None], conv_p[None], conv_sample)
```

```python
import functools

import jax
import jax.numpy as jnp
from jax import lax
from jax.experimental import pallas as pl
from jax.experimental.pallas import tpu as pltpu

F32 = jnp.float32
BF16 = jnp.bfloat16

D_MODEL = 2048
HEAD_DIM = 64
N_Q_HEADS = 16
N_KV_HEADS = 4
Q_PER_KV = N_Q_HEADS // N_KV_HEADS
D_ATTN = N_Q_HEADS * HEAD_DIM
D_KV = N_KV_HEADS * HEAD_DIM
D_QK = D_ATTN + D_KV
D_QKV = D_QK + D_KV
WINDOW = 128
ROPE_THETA = 10000.0
D_RNN = 1024
N_RNN_BLOCKS = 16
RNN_BLOCK = D_RNN // N_RNN_BLOCKS
CONV_WIDTH = 4
LRU_C = 8.0
N_MOD = 9
EPS = 1e-6
PAST_LEN = 16384

LANES = 128
SUBLANES = 8
MXU_DIM = 256
VMEM_LIMIT = 56 * 1024 * 1024

Z_RX = 0
Z_RG = Z_RX + D_RNN
Z_GATT = Z_RG + D_RNN
Z_GRNN = Z_GATT + D_MODEL
D_Z = Z_GRNN + D_MODEL
DX = 2 * N_KV_HEADS * LANES


def _params(*sem):
    return pltpu.CompilerParams(dimension_semantics=sem, vmem_limit_bytes=VMEM_LIMIT)


def _silu(x):
    return x * jax.nn.sigmoid(x)


def _modulate(x, g, shift, scale):
    ms = jnp.mean(x * x, axis=-1, keepdims=True)
    return (x * lax.rsqrt(ms + EPS) * g) * (1.0 + scale) + shift


def _mod_spec(mod, tm, rows_per_mod, width, col, ngrid):
    colf = col if callable(col) else (lambda *idx: col)
    if mod.ndim == 3:
        return pl.BlockSpec((None, 1, width), lambda *idx: ((idx[ngrid - 1] * tm) // rows_per_mod, 0, colf(*idx)))
    return pl.BlockSpec((tm, width), lambda *idx: (idx[ngrid - 1], colf(*idx)))


def _ada_kernel(c_ref, w_ref, b_ref, o_ref):
    s = _silu(c_ref[...]).astype(BF16)
    o_ref[...] = jnp.dot(s, w_ref[...].astype(BF16), preferred_element_type=F32) + b_ref[...]


def _ada(c_all, w_ada, b_ada, tn):
    rows, d = c_all.shape
    n = w_ada.shape[1]
    return pl.pallas_call(
        _ada_kernel,
        grid=(n // tn,),
        in_specs=[pl.BlockSpec((rows, d), lambda j: (0, 0)),
                  pl.BlockSpec((d, tn), lambda j: (0, j)),
                  pl.BlockSpec((1, tn), lambda j: (0, j))],
        out_specs=pl.BlockSpec((rows, tn), lambda j: (0, j)),
        out_shape=jax.ShapeDtypeStruct((rows, n), F32),
        compiler_params=_params("arbitrary"),
        name="ada",
    )(c_all, w_ada, b_ada.reshape(1, n))


def _rope_kernel(inv_ref, cs_ref, sn_ref, *, base, blk):
    r = pl.program_id(0)
    pos = (base + r * blk + lax.broadcasted_iota(jnp.int32, (blk, LANES), 0)).astype(F32)
    ang = pos * inv_ref[...]
    lane = lax.broadcasted_iota(jnp.int32, (blk, LANES), 1)
    first_half = (lane & (HEAD_DIM - 1)) < HEAD_DIM // 2
    sn = jnp.sin(ang)
    cs_ref[...] = jnp.cos(ang)
    sn_ref[...] = jnp.where(first_half, -sn, sn)


def _rope_tables(inv_row, base, rows, blk):
    out = jax.ShapeDtypeStruct((rows, LANES), F32)
    return pl.pallas_call(
        functools.partial(_rope_kernel, base=base, blk=blk),
        grid=(rows // blk,),
        in_specs=[pl.BlockSpec((1, LANES), lambda r: (0, 0))],
        out_specs=[pl.BlockSpec((blk, LANES), lambda r: (r, 0))] * 2,
        out_shape=[out, out],
        compiler_params=_params("arbitrary"),
        name="rope_tables",
    )(inv_row)


def _modulate_kernel(x_ref, shift_ref, scale_ref, g_ref, h_ref):
    h_ref[...] = _modulate(x_ref[...], g_ref[...], shift_ref[...], scale_ref[...]).astype(BF16)


def _modulated_norm(x, mod, rows_per_mod, g, k_shift, tm):
    m_rows, d = x.shape
    return pl.pallas_call(
        _modulate_kernel,
        grid=(m_rows // tm,),
        in_specs=[pl.BlockSpec((tm, d), lambda m: (m, 0)),
                  _mod_spec(mod, tm, rows_per_mod, d, k_shift, 1),
                  _mod_spec(mod, tm, rows_per_mod, d, k_shift + 1, 1),
                  pl.BlockSpec((1, d), lambda m: (0, 0))],
        out_specs=pl.BlockSpec((tm, d), lambda m: (m, 0)),
        out_shape=jax.ShapeDtypeStruct((m_rows, d), BF16),
        compiler_params=_params("arbitrary"),
        name="modnorm",
    )(x, mod, mod, g.reshape(1, d))


HBM = pl.BlockSpec(memory_space=pl.ANY)


def _is_param(w):
    return w.dtype == F32


def _stream_weight_tiles(weights, tn, first):
    t, nt = pl.program_id(0), pl.num_programs(0)
    m, nm = pl.program_id(1), pl.num_programs(1)
    cols = lambda tile: pl.ds(pl.multiple_of(tile * tn, LANES), tn)
    fetch = lambda w_hbm, stage_ref, sem, tile: pltpu.make_async_copy(
        w_hbm.at[0, :, cols(first + tile)], stage_ref, sem.at[0])
    put = lambda wb_ref, wout_hbm, sem, tile: pltpu.make_async_copy(wb_ref, wout_hbm.at[:, cols(tile)], sem.at[1])

    @pl.when(m == 0)
    def _():
        for w_hbm, stage_ref, wb_ref, wout_hbm, sem in weights:
            @pl.when(t == 0)
            def _():
                fetch(w_hbm, stage_ref, sem, 0).start()

            fetch(w_hbm, stage_ref, sem, t).wait()

            @pl.when(t > 0)
            def _():
                put(wb_ref, wout_hbm, sem, t - 1).wait()

            wb_ref[...] = stage_ref[...].astype(BF16)
            put(wb_ref, wout_hbm, sem, t).start()

            @pl.when(t + 1 < nt)
            def _():
                fetch(w_hbm, stage_ref, sem, t + 1).start()

    @pl.when((t == nt - 1) & (m == nm - 1))
    def _():
        for w_hbm, stage_ref, wb_ref, wout_hbm, sem in weights:
            put(wb_ref, wout_hbm, sem, t).wait()


def _weight_scratch(rows, tn):
    return [pltpu.VMEM((rows, tn), F32), pltpu.VMEM((rows, tn), BF16), pltpu.SemaphoreType.DMA((2,))]


def _row_groups(tm):
    n = max(1, tm // 1024)
    return [slice(r * (tm // n), (r + 1) * (tm // n)) for r in range(n)]


def _ffn_up_rows(h_ref, w1b_ref, w3b_ref, g_ref):
    for rows in _row_groups(h_ref.shape[0]):
        h = h_ref[rows, :]
        a = jnp.dot(h, w1b_ref[...], preferred_element_type=F32)
        b = jnp.dot(h, w3b_ref[...], preferred_element_type=F32)
        g_ref[rows, :] = (_silu(a) * b).astype(BF16)


def _ffn_up_param_kernel(h_ref, w1_hbm, w3_hbm, g_ref, w1o_hbm, w3o_hbm,
                         s1_ref, w1b_ref, sem1, s3_ref, w3b_ref, sem3, *, tf):
    _stream_weight_tiles(((w1_hbm, s1_ref, w1b_ref, w1o_hbm, sem1), (w3_hbm, s3_ref, w3b_ref, w3o_hbm, sem3)),
                         tf, 0)
    _ffn_up_rows(h_ref, w1b_ref, w3b_ref, g_ref)


def _ffn_up(h, w1, w3, tm, tf):
    m_rows, d = h.shape
    dff = w1.shape[-1]
    grid = (dff // tf, m_rows // tm)
    h_spec = pl.BlockSpec((tm, d), lambda f, m: (m, 0))
    g_spec = pl.BlockSpec((tm, tf), lambda f, m: (m, f))
    g_shape = jax.ShapeDtypeStruct((m_rows, dff), BF16)
    if not _is_param(w1):
        w_spec = pl.BlockSpec((d, tf), lambda f, m: (0, f))
        g = pl.pallas_call(
            _ffn_up_rows, grid=grid, in_specs=[h_spec, w_spec, w_spec], out_specs=g_spec, out_shape=g_shape,
            compiler_params=_params("arbitrary", "arbitrary"), name="ffn_up_rows",
        )(h, w1, w3)
        return g, w1, w3
    w_shape = jax.ShapeDtypeStruct((d, dff), BF16)
    return pl.pallas_call(
        functools.partial(_ffn_up_param_kernel, tf=tf),
        grid=grid,
        in_specs=[h_spec, HBM, HBM],
        out_specs=[g_spec, HBM, HBM],
        out_shape=[g_shape, w_shape, w_shape],
        scratch_shapes=_weight_scratch(d, tf) + _weight_scratch(d, tf),
        compiler_params=_params("arbitrary", "arbitrary"),
        name="ffn_up",
    )(h, w1, w3)


def _ffn_down_rows(g_ref, w2b_ref, x_ref, gate_ref, o_ref):
    for rows in _row_groups(g_ref.shape[0]):
        acc = jnp.dot(g_ref[rows, :], w2b_ref[...], preferred_element_type=F32)
        gate = gate_ref[...] if gate_ref.shape[0] == 1 else gate_ref[rows, :]
        o_ref[rows, :] = x_ref[rows, :] + 0.5 * gate * acc


def _ffn_down_param_kernel(g_ref, w2_hbm, x_ref, gate_ref, o_ref, w2o_hbm, s2_ref, w2b_ref, sem2, *, tn):
    _stream_weight_tiles(((w2_hbm, s2_ref, w2b_ref, w2o_hbm, sem2),), tn, 0)
    _ffn_down_rows(g_ref, w2b_ref, x_ref, gate_ref, o_ref)


def _ffn_down(g, w2, x, mod, rows_per_mod, k_gate, tm, tn):
    m_rows, d = x.shape
    dff = g.shape[1]
    grid = (d // tn, m_rows // tm)
    gate_col = lambda n, m: k_gate * (d // tn) + n
    g_spec = pl.BlockSpec((tm, dff), lambda n, m: (m, 0))
    x_spec = pl.BlockSpec((tm, tn), lambda n, m: (m, n))
    gate_spec = _mod_spec(mod, tm, rows_per_mod, tn, gate_col, 2)
    o_shape = jax.ShapeDtypeStruct((m_rows, d), F32)
    if not _is_param(w2):
        out = pl.pallas_call(
            _ffn_down_rows, grid=grid,
            in_specs=[g_spec, pl.BlockSpec((dff, tn), lambda n, m: (0, n)), x_spec, gate_spec],
            out_specs=x_spec, out_shape=o_shape,
            compiler_params=_params("arbitrary", "arbitrary"), name="ffn_down_rows",
        )(g, w2, x, mod)
        return out, w2
    return pl.pallas_call(
        functools.partial(_ffn_down_param_kernel, tn=tn),
        grid=grid,
        in_specs=[g_spec, HBM, x_spec, gate_spec],
        out_specs=[x_spec, HBM],
        out_shape=[o_shape, jax.ShapeDtypeStruct((dff, d), BF16)],
        scratch_shapes=_weight_scratch(dff, tn),
        compiler_params=_params("arbitrary", "arbitrary"),
        name="ffn_down",
    )(g, w2, x, mod)


def _inproj_z_rows(h_ref, wb_ref, z_ref):
    for rows in _row_groups(h_ref.shape[0]):
        z_ref[rows, :] = jnp.dot(h_ref[rows, :], wb_ref[...], preferred_element_type=F32)


def _inproj_z_param_kernel(h_ref, w_hbm, z_ref, wo_hbm, s_ref, wb_ref, sem, *, tn, first):
    _stream_weight_tiles(((w_hbm, s_ref, wb_ref, wo_hbm, sem),), tn, first)
    _inproj_z_rows(h_ref, wb_ref, z_ref)


def _inproj_z(h, w, tm, tn):
    m_rows, d = h.shape
    grid = (D_Z // tn, m_rows // tm)
    h_spec = pl.BlockSpec((tm, d), lambda n, m: (m, 0))
    z_spec = pl.BlockSpec((tm, tn), lambda n, m: (m, n))
    z_shape = jax.ShapeDtypeStruct((m_rows, D_Z), F32)
    if not _is_param(w):
        z = pl.pallas_call(
            _inproj_z_rows, grid=grid, in_specs=[h_spec, pl.BlockSpec((d, tn), lambda n, m: (0, n))],
            out_specs=z_spec, out_shape=z_shape,
            compiler_params=_params("arbitrary", "arbitrary"), name="inproj_z_rows",
        )(h, w)
        return z, w
    return pl.pallas_call(
        functools.partial(_inproj_z_param_kernel, tn=tn, first=D_QKV // tn),
        grid=grid,
        in_specs=[h_spec, HBM],
        out_specs=[z_spec, HBM],
        out_shape=[z_shape, jax.ShapeDtypeStruct((d, D_Z), BF16)],
        scratch_shapes=_weight_scratch(d, tn),
        compiler_params=_params("arbitrary", "arbitrary"),
        name="inproj_z",
    )(h, w)


def _store_head_pairs(ref, rows, chunk, c):
    lo = lax.broadcasted_iota(jnp.int32, chunk.shape, 1) < HEAD_DIM
    swapped = pltpu.roll(chunk, HEAD_DIM, 1)
    zero = jnp.zeros_like(chunk)
    cols = (jnp.where(lo, chunk, zero), jnp.where(lo, zero, swapped),
            jnp.where(lo, swapped, zero), jnp.where(lo, zero, chunk))
    for i, col in enumerate(cols):
        ref[rows, (4 * c + i) * LANES:(4 * c + i + 1) * LANES] = col.astype(ref.dtype)


def _inproj_qkv_kernel(x_ref, shift_ref, scale_ref, g_ref, w_ref, cs_ref, sn_ref, gqk_ref, red_ref, exp_ref,
                       h_ref, q_ref, kx_ref, vx_ref, kf_ref, vf_ref, wb_ref, *, nsplit):
    @pl.when(pl.program_id(0) == 0)
    def _():
        wb_ref[...] = w_ref[...].astype(BF16)

    step = x_ref.shape[0] // nsplit
    for r in range(nsplit):
        rows = slice(r * step, (r + 1) * step)
        shift = shift_ref[...] if shift_ref.shape[0] == 1 else shift_ref[rows, :]
        scale = scale_ref[...] if scale_ref.shape[0] == 1 else scale_ref[rows, :]
        h = _modulate(x_ref[rows, :], g_ref[...], shift, scale).astype(BF16)
        h_ref[rows, :] = h
        acc = jnp.dot(h, wb_ref[...], preferred_element_type=F32)
        qk = acc[:, :D_QK]
        hm = jnp.dot((qk * qk).astype(BF16), red_ref[...], preferred_element_type=F32)
        hi = hm.astype(BF16)
        lo = (hm - hi.astype(F32)).astype(BF16)
        ms = jnp.dot(jnp.concatenate([hi, lo], axis=1), exp_ref[...], preferred_element_type=F32)
        y = qk * lax.rsqrt(ms + EPS) * gqk_ref[...]
        cs = cs_ref[rows, :]
        sn = sn_ref[rows, :]
        lane = lax.broadcasted_iota(jnp.int32, cs.shape, 1)
        first_half = (lane & (HEAD_DIM - 1)) < HEAD_DIM // 2
        for c in range(D_QK // LANES):
            yc = y[:, c * LANES:(c + 1) * LANES]
            partner = jnp.where(first_half,
                                pltpu.roll(yc, LANES - HEAD_DIM // 2, 1),
                                pltpu.roll(yc, HEAD_DIM // 2, 1))
            rot = yc * cs + partner * sn
            if c < D_ATTN // LANES:
                q_ref[rows, c * LANES:(c + 1) * LANES] = (rot * (HEAD_DIM ** -0.5)).astype(BF16)
            else:
                kf_ref[rows, c * LANES - D_ATTN:(c + 1) * LANES - D_ATTN] = rot
                _store_head_pairs(kx_ref, rows, rot, c - D_ATTN // LANES)
        v = acc[:, D_QK:]
        vf_ref[rows, :] = v
        for c in range(D_KV // LANES):
            _store_head_pairs(vx_ref, rows, v[:, c * LANES:(c + 1) * LANES], c)


def _inproj_qkv(x, mod, rows_per_mod, g, k_shift, w_in, cs, sn, gqk, head_reduce, head_expand, tm, table_blocks):
    m_rows, d = x.shape
    row = lambda width: pl.BlockSpec((tm, width), lambda m: (m, 0))
    table = pl.BlockSpec((tm, LANES), lambda m: (m % table_blocks, 0))
    nsplit = 2 if tm % (2 * MXU_DIM) == 0 else 1
    return pl.pallas_call(
        functools.partial(_inproj_qkv_kernel, nsplit=nsplit),
        grid=(m_rows // tm,),
        in_specs=[row(d),
                  _mod_spec(mod, tm, rows_per_mod, d, k_shift, 1),
                  _mod_spec(mod, tm, rows_per_mod, d, k_shift + 1, 1),
                  pl.BlockSpec((1, d), lambda m: (0, 0)),
                  pl.BlockSpec((None, d, D_QKV), lambda m: (0, 0, 0), pipeline_mode=pl.Buffered(1)),
                  table, table,
                  pl.BlockSpec((1, D_QK), lambda m: (0, 0)),
                  pl.BlockSpec(head_reduce.shape, lambda m: (0, 0)),
                  pl.BlockSpec(head_expand.shape, lambda m: (0, 0))],
        out_specs=[row(d), row(D_ATTN), row(DX), row(DX), row(D_KV), row(D_KV)],
        out_shape=[jax.ShapeDtypeStruct((m_rows, d), BF16),
                   jax.ShapeDtypeStruct((m_rows, D_ATTN), BF16),
                   jax.ShapeDtypeStruct((m_rows, DX), BF16),
                   jax.ShapeDtypeStruct((m_rows, DX), BF16),
                   jax.ShapeDtypeStruct((m_rows, D_KV), F32),
                   jax.ShapeDtypeStruct((m_rows, D_KV), F32)],
        scratch_shapes=[pltpu.VMEM((d, D_QKV), BF16)],
        compiler_params=_params("arbitrary"),
        name="inproj_qkv",
    )(x, mod, mod, g.reshape(1, d), w_in, cs, sn, gqk, head_reduce, head_expand)


def _attn_bias(bias_ref, tq, past_off):
    nk = 2 * WINDOW
    ri = lax.broadcasted_iota(jnp.int32, (2 * tq, 2 * nk), 0) & (tq - 1)
    kj = lax.broadcasted_iota(jnp.int32, (2 * tq, 2 * nk), 1) & (nk - 1)
    visible = ((kj < WINDOW) & (kj > ri + past_off)) | ((kj >= WINDOW) & (kj - WINDOW <= ri))
    bias_ref[...] = jnp.where(visible, 0.0, -jnp.inf)


def _attn_blocks(sink_ref, bias_of, q_ref, past_of, kc_ref, vc_ref, nsub, tq, store):
    nk = 2 * WINDOW
    rows = 2 * tq
    first_rows = lax.broadcasted_iota(jnp.int32, (rows, 1), 0) < tq
    lo_lanes = lax.broadcasted_iota(jnp.int32, (rows, LANES), 1) < HEAD_DIM
    ones_lo = (lax.broadcasted_iota(jnp.int32, (nk, LANES), 1) < HEAD_DIM).astype(BF16)
    ones_hi = (1 - ones_lo.astype(F32)).astype(BF16)
    nt = (((1,), (1,)), ((), ()))

    def keys(which, s, col):
        cols = slice(col * LANES, (col + 1) * LANES)
        past = past_of(s)
        parts = [past[which][past[2]:past[2] + WINDOW, cols], (kc_ref, vc_ref)[which][s * tq:(s + 1) * tq, cols]]
        if tq < WINDOW:
            parts.append(jnp.zeros((WINDOW - tq, LANES), BF16))
        return parts

    for s in range(nsub):
        for g in range(N_KV_HEADS):
            c0, c1 = 2 * g, 2 * g + 1
            q4 = jnp.concatenate([q_ref[s * tq:(s + 1) * tq, c0 * LANES:(c0 + 1) * LANES],
                                  q_ref[s * tq:(s + 1) * tq, c1 * LANES:(c1 + 1) * LANES]], axis=0)
            kk = jnp.concatenate(keys(0, s, c0) + keys(0, s, c1), axis=0)
            sc = lax.dot_general(q4, kk, nt, preferred_element_type=F32) + bias_of(s)[...]
            sk_lo = jnp.where(first_rows, sink_ref[4 * g], sink_ref[4 * g + 2])
            sk_hi = jnp.where(first_rows, sink_ref[4 * g + 1], sink_ref[4 * g + 3])
            m_lo = jnp.maximum(jnp.max(sc[:, :nk], axis=-1, keepdims=True), sk_lo)
            m_hi = jnp.maximum(jnp.max(sc[:, nk:], axis=-1, keepdims=True), sk_hi)
            p = jnp.concatenate([jnp.exp(sc[:, :nk] - m_lo), jnp.exp(sc[:, nk:] - m_hi)], axis=1).astype(BF16)
            vv = jnp.concatenate(
                [jnp.concatenate([jnp.concatenate(keys(1, s, c0), axis=0), ones_lo], axis=1),
                 jnp.concatenate([jnp.concatenate(keys(1, s, c1), axis=0), ones_hi], axis=1)],
                axis=0)
            o = jnp.dot(p, vv, preferred_element_type=F32)
            denom = o[:, LANES:] + jnp.where(lo_lanes, jnp.exp(sk_lo - m_lo), jnp.exp(sk_hi - m_hi))
            store(s, c0, c1, o[:, :LANES] / denom)


def _attn_kernel(sink_ref, q_ref, kp_ref, kc_ref, vp_ref, vc_ref, o_ref, bias0_ref, bias_ref, *, nq):
    tq = WINDOW
    _attn_bias(bias0_ref, tq, jnp.where(pl.program_id(1) > 0, 0, WINDOW))
    if nq > 1:
        _attn_bias(bias_ref, tq, 0)

    def store(s, c0, c1, out):
        o_ref[s * tq:(s + 1) * tq, c0 * LANES:(c0 + 1) * LANES] = out[:tq].astype(BF16)
        o_ref[s * tq:(s + 1) * tq, c1 * LANES:(c1 + 1) * LANES] = out[tq:].astype(BF16)

    past_of = lambda s: (kp_ref, vp_ref, 0) if s == 0 else (kc_ref, vc_ref, (s - 1) * WINDOW)
    bias_of = lambda s: bias0_ref if s == 0 else bias_ref
    _attn_blocks(sink_ref, bias_of, q_ref, past_of, kc_ref, vc_ref, nq, tq, store)


def _attn(sinks, q, kx, vx, nbatch, nblk, nq):
    steps = nblk // nq
    cur = lambda b, n: (b * steps + n, 0)
    past = lambda b, n: (b * nblk + jnp.maximum(n * nq - 1, 0), 0)
    kv_cur = pl.BlockSpec((nq * WINDOW, DX), cur)
    kv_past = pl.BlockSpec((WINDOW, DX), past)
    bias = pltpu.VMEM((2 * WINDOW, 4 * WINDOW), F32)
    return pl.pallas_call(
        functools.partial(_attn_kernel, nq=nq),
        grid=(nbatch, steps),
        in_specs=[pl.BlockSpec(memory_space=pltpu.SMEM), pl.BlockSpec((nq * WINDOW, D_ATTN), cur),
                  kv_past, kv_cur, kv_past, kv_cur],
        out_specs=pl.BlockSpec((nq * WINDOW, D_ATTN), cur),
        out_shape=jax.ShapeDtypeStruct(q.shape, BF16),
        scratch_shapes=[bias, bias],
        compiler_params=_params("arbitrary", "arbitrary"),
        name="attn",
    )(sinks, q, kx, kx, vx, vx)


def _attn_step_kernel(sink_ref, q_ref, kn_ref, vn_ref, ck_ref, cv_ref, o_ref,
                      bias_ref, qx_ref, kcx_ref, vcx_ref, kpx_ref, vpx_ref, *, nsub, tq):
    _attn_bias(bias_ref, tq, 0)
    for c in range(D_KV // LANES):
        _store_head_pairs(kpx_ref, slice(None), ck_ref[:, c * LANES:(c + 1) * LANES], c)
        _store_head_pairs(vpx_ref, slice(None), cv_ref[:, c * LANES:(c + 1) * LANES], c)
    for src, dst in ((q_ref, qx_ref), (kn_ref, kcx_ref), (vn_ref, vcx_ref)):
        rows = src[...].astype(F32)
        for s in range(nsub):
            dst[s * tq:(s + 1) * tq, :] = jnp.broadcast_to(rows[s:s + 1, :], (tq, rows.shape[1])).astype(BF16)

    def store(s, c0, c1, out):
        o_ref[s:s + 1, c0 * LANES:(c0 + 1) * LANES] = out[0:1]
        o_ref[s:s + 1, c1 * LANES:(c1 + 1) * LANES] = out[tq:tq + 1]

    past_of = lambda s: (kpx_ref, vpx_ref, s * WINDOW)
    _attn_blocks(sink_ref, lambda s: bias_ref, qx_ref, past_of, kcx_ref, vcx_ref, nsub, tq, store)


def _attn_step(sinks, q, kx_new, vx_new, cache_k, cache_v, nsub):
    ns = q.shape[0]
    tq = 2 * SUBLANES
    row = lambda width: pl.BlockSpec((nsub, width), lambda b: (b, 0))
    cache = pl.BlockSpec((nsub * WINDOW, D_KV), lambda b: (b, 0))
    return pl.pallas_call(
        functools.partial(_attn_step_kernel, nsub=nsub, tq=tq),
        grid=(ns // nsub,),
        in_specs=[pl.BlockSpec(memory_space=pltpu.SMEM), row(D_ATTN), row(DX), row(DX), cache, cache],
        out_specs=row(D_ATTN),
        out_shape=jax.ShapeDtypeStruct((ns, D_ATTN), F32),
        scratch_shapes=[pltpu.VMEM((2 * tq, 4 * WINDOW), F32),
                        pltpu.VMEM((nsub * tq, D_ATTN), BF16),
                        pltpu.VMEM((nsub * tq, DX), BF16),
                        pltpu.VMEM((nsub * tq, DX), BF16),
                        pltpu.VMEM((nsub * WINDOW, DX), BF16),
                        pltpu.VMEM((nsub * WINDOW, DX), BF16)],
        compiler_params=_params("arbitrary"),
        name="attn_step",
    )(sinks, q, kx_new, vx_new, cache_k, cache_v)


def _softplus(x):
    return jnp.maximum(x, 0.0) + jnp.log1p(jnp.exp(-jnp.abs(x)))


def _lru_coeffs(xc, wrg_ref, wig_ref, brg, big, lam):
    xb = xc.astype(BF16)
    ngroups = D_RNN // MXU_DIM
    rs, igs = [], []
    for c in range(ngroups):
        xg = xb[:, c * MXU_DIM:(c + 1) * MXU_DIM]
        rs.append(jnp.dot(xg, wrg_ref[c], preferred_element_type=F32))
        igs.append(jnp.dot(xg, wig_ref[c], preferred_element_type=F32))
    r = jax.nn.sigmoid(jnp.concatenate(rs, axis=1) + brg)
    ig = jax.nn.sigmoid(jnp.concatenate(igs, axis=1) + big)
    log_a = -LRU_C * r * _softplus(-lam)
    a = jnp.exp(log_a)
    one_minus_a2 = -jnp.tanh(log_a) * (1.0 + a * a)
    u = jnp.sqrt(one_minus_a2) * (ig * xc)
    return a, u


def _rnn_prompt_kernel(rx_ref, rg_ref, cw_ref, cb_ref, wrg_ref, wig_ref, brg_ref, big_ref, lam_ref,
                       o_ref, hlast_ref, conv_ref, xs_ref, a_ref, h_ref, carry_ref, *, tc):
    t = pl.program_id(1)
    pad = SUBLANES

    @pl.when(t == 0)
    def _():
        xs_ref[0:pad, :] = jnp.zeros((pad, D_RNN), F32)
        carry_ref[...] = jnp.zeros_like(carry_ref)

    x = rx_ref[...]
    xs_ref[pad:pad + tc, :] = x
    xc = cb_ref[...] + x * cw_ref[CONV_WIDTH - 1:CONV_WIDTH, :]
    for k in range(1, CONV_WIDTH):
        xc = xc + xs_ref[pad - k:pad - k + tc, :] * cw_ref[CONV_WIDTH - 1 - k:CONV_WIDTH - k, :]
    tail = xs_ref[tc:tc + pad, :]
    xs_ref[0:pad, :] = tail
    conv_ref[...] = tail[pad - (CONV_WIDTH - 1):, :]

    a, u = _lru_coeffs(xc, wrg_ref, wig_ref, brg_ref[...], big_ref[...], lam_ref[...])
    a_ref[...] = a
    h_ref[...] = u

    row = lax.broadcasted_iota(jnp.int32, (SUBLANES, D_RNN), 0)

    def body(r, carry):
        off = pl.multiple_of(r * SUBLANES, SUBLANES)
        av = a_ref[pl.ds(off, SUBLANES), :]
        hv = h_ref[pl.ds(off, SUBLANES), :]
        for sft in (1, 2, 4):
            keep = row >= sft
            a_sh = jnp.where(keep, pltpu.roll(av, sft, 0), 1.0)
            h_sh = jnp.where(keep, pltpu.roll(hv, sft, 0), 0.0)
            hv = av * h_sh + hv
            av = av * a_sh
        hv = hv + av * carry
        h_ref[pl.ds(off, SUBLANES), :] = hv
        return jnp.broadcast_to(hv[SUBLANES - 1:SUBLANES, :], (SUBLANES, D_RNN))

    carry = lax.fori_loop(0, tc // SUBLANES, body, carry_ref[...])
    carry_ref[...] = carry
    hlast_ref[...] = carry[0:1, :]
    o_ref[...] = (h_ref[...] * jax.nn.gelu(rg_ref[...])).astype(BF16)


def _rnn_prompt(z, nbatch, seq, conv_w, conv_b, wrg, wig, b_rg, b_ig, lam, tc):
    nchunk = seq // tc
    rx_blk = Z_RX // D_RNN
    rg_blk = Z_RG // D_RNN
    vec = pl.BlockSpec((1, D_RNN), lambda b, t: (0, 0))
    wspec = pl.BlockSpec(wrg.shape, lambda b, t: (0, 0, 0))
    return pl.pallas_call(
        functools.partial(_rnn_prompt_kernel, tc=tc),
        grid=(nbatch, nchunk),
        in_specs=[pl.BlockSpec((tc, D_RNN), lambda b, t: (b * nchunk + t, rx_blk)),
                  pl.BlockSpec((tc, D_RNN), lambda b, t: (b * nchunk + t, rg_blk)),
                  pl.BlockSpec((CONV_WIDTH, D_RNN), lambda b, t: (0, 0)),
                  vec, wspec, wspec, vec, vec, vec],
        out_specs=[pl.BlockSpec((tc, D_RNN), lambda b, t: (b * nchunk + t, 0)),
                   pl.BlockSpec((None, 1, D_RNN), lambda b, t: (b, 0, 0)),
                   pl.BlockSpec((None, CONV_WIDTH - 1, D_RNN), lambda b, t: (b, 0, 0))],
        out_shape=[jax.ShapeDtypeStruct((nbatch * seq, D_RNN), BF16),
                   jax.ShapeDtypeStruct((nbatch, 1, D_RNN), F32),
                   jax.ShapeDtypeStruct((nbatch, CONV_WIDTH - 1, D_RNN), F32)],
        scratch_shapes=[pltpu.VMEM((tc + SUBLANES, D_RNN), F32),
                        pltpu.VMEM((tc, D_RNN), F32),
                        pltpu.VMEM((tc, D_RNN), F32),
                        pltpu.VMEM((SUBLANES, D_RNN), F32)],
        compiler_params=_params("arbitrary", "arbitrary"),
        name="rnn_prompt",
    )(z, z, conv_w, conv_b.reshape(1, D_RNN), wrg, wig, b_rg.reshape(1, D_RNN),
      b_ig.reshape(1, D_RNN), lam.reshape(1, D_RNN))


def _rnn_step_kernel(rx_ref, rg_ref, c0_ref, c1_ref, c2_ref, h0_ref, cw_ref, cb_ref, wrg_ref, wig_ref,
                     brg_ref, big_ref, lam_ref, o_ref, h_ref):
    x = rx_ref[...]
    xc = (cb_ref[...] + c0_ref[...] * cw_ref[0:1, :] + c1_ref[...] * cw_ref[1:2, :]
          + c2_ref[...] * cw_ref[2:3, :] + x * cw_ref[3:4, :])
    a, u = _lru_coeffs(xc, wrg_ref, wig_ref, brg_ref[...], big_ref[...], lam_ref[...])
    h = a * h0_ref[...] + u
    h_ref[...] = h
    o_ref[...] = (h * jax.nn.gelu(rg_ref[...])).astype(BF16)


def _rnn_step(z, conv_state, h0, conv_w, conv_b, wrg, wig, b_rg, b_ig, lam):
    rows = z.shape[0]
    full = lambda shape: pl.BlockSpec(shape, lambda i: (0,) * len(shape))
    act = full((rows, D_RNN))
    vec = full((1, D_RNN))
    return pl.pallas_call(
        _rnn_step_kernel,
        grid=(1,),
        in_specs=[pl.BlockSpec((rows, D_RNN), lambda i: (0, Z_RX // D_RNN)),
                  pl.BlockSpec((rows, D_RNN), lambda i: (0, Z_RG // D_RNN)),
                  act, act, act, act, full((CONV_WIDTH, D_RNN)), vec,
                  full(wrg.shape), full(wig.shape), vec, vec, vec],
        out_specs=[act, act],
        out_shape=[jax.ShapeDtypeStruct((rows, D_RNN), BF16),
                   jax.ShapeDtypeStruct((rows, D_RNN), F32)],
        compiler_params=_params("arbitrary"),
        name="rnn_step",
    )(z, z, conv_state[:, 0], conv_state[:, 1], conv_state[:, 2], h0, conv_w,
      conv_b.reshape(1, D_RNN), wrg, wig, b_rg.reshape(1, D_RNN), b_ig.reshape(1, D_RNN),
      lam.reshape(1, D_RNN))


def _mix_kernel(oa_ref, or_ref, ga_ref, gr_ref, x_ref, gate_ref, shift_ref, scale_ref, g_ref,
                wpa_ref, wpr_ref, wout_ref, o_ref, h_ref):
    pa = jnp.dot(oa_ref[...].astype(BF16), wpa_ref[...], preferred_element_type=F32)
    pr = jnp.dot(or_ref[...], wpr_ref[...], preferred_element_type=F32)
    mix = jax.nn.sigmoid(ga_ref[...]) * pa + jax.nn.sigmoid(gr_ref[...]) * pr
    x = x_ref[...] + gate_ref[...] * jnp.dot(mix.astype(BF16), wout_ref[...], preferred_element_type=F32)
    o_ref[...] = x
    h_ref[...] = _modulate(x, g_ref[...], shift_ref[...], scale_ref[...]).astype(BF16)


def _mix(o_att, o_rnn, z, x, mod, rows_per_mod, g_next, w_pa, w_pr, w_out, k_gate, k_shift_next, tm):
    m_rows, d = x.shape
    const = lambda shape: pl.BlockSpec(shape, lambda m: (0, 0), pipeline_mode=pl.Buffered(1))
    row = lambda width, col: pl.BlockSpec((tm, width), lambda m: (m, col))
    mspec = lambda k: _mod_spec(mod, tm, rows_per_mod, d, k, 1)
    return pl.pallas_call(
        _mix_kernel,
        grid=(m_rows // tm,),
        in_specs=[row(D_ATTN, 0), row(D_RNN, 0), row(d, Z_GATT // d), row(d, Z_GRNN // d), row(d, 0),
                  mspec(k_gate), mspec(k_shift_next), mspec(k_shift_next + 1), const((1, d)),
                  const(w_pa.shape), const(w_pr.shape), const(w_out.shape)],
        out_specs=[row(d, 0), row(d, 0)],
        out_shape=[jax.ShapeDtypeStruct((m_rows, d), F32), jax.ShapeDtypeStruct((m_rows, d), BF16)],
        compiler_params=_params("arbitrary"),
        name="mix",
    )(o_att, o_rnn, z, z, x, mod, mod, mod, g_next.reshape(1, d), w_pa, w_pr, w_out)


def _block_diag(w, group):
    n, r, _ = w.shape
    eye = jnp.eye(group, dtype=w.dtype)
    wg = w.reshape(n // group, group, r, r)
    return jnp.einsum("ngij,gh->ngihj", wg, eye).reshape(n // group, group * r, group * r)


def _tiles(rows, dff):
    pick = lambda n, prefs: next((t for t in prefs if n % t == 0), n)
    return dict(
        tm_norm=pick(rows, (512,)),
        tm_up=pick(rows, (2048, 1024, 512)), tf=pick(dff, (512,)),
        tm_down=pick(rows, (1024, 512)), tn_down=512,
        tm_z=pick(rows, (1024, 512)), tn_z=D_QKV,
        tm_qkv=pick(rows, (512,)),
        tm_mix=pick(rows, (256,)),
    )


def kernel(x_prompt, x_sample, c_prompt, c_sample, cache_k, cache_v, state_h, state_conv, w_ada, b_ada,
           g_norm_ffn1, g_norm_mix, g_norm_ffn2, ffn1_w1, ffn1_w3, ffn1_w2, ffn2_w1, ffn2_w3, ffn2_w2,
           w_in, g_q, g_k, sinks, conv_w, conv_b, w_rg, b_rg, w_ig, b_ig, lru_lambda, w_pa, w_pr, w_out):
    nb, seq, d = x_prompt.shape
    ns = x_sample.shape[0]
    dff = ffn1_w1.shape[2]
    assert d == D_MODEL and w_ada.shape[0] == 1 and x_sample.shape[1] == 1 and cache_k.shape[2] == WINDOW
    assert seq % WINDOW == 0 and w_in.shape[2] == D_QKV + D_Z

    group = MXU_DIM // RNN_BLOCK
    wrg = _block_diag(w_rg[0], group).astype(BF16)
    wig = _block_diag(w_ig[0], group).astype(BF16)
    wpa, wpr, wout = w_pa[0].astype(BF16), w_pr[0].astype(BF16), w_out[0].astype(BF16)
    gqk = jnp.concatenate([jnp.tile(g_q[0], N_Q_HEADS), jnp.tile(g_k[0], N_KV_HEADS)]).reshape(1, D_QK)
    head_of = jnp.arange(D_QK) // HEAD_DIM
    onehot = (head_of[:, None] == jnp.arange(LANES)[None, :]).astype(F32)
    head_reduce = (onehot / HEAD_DIM).astype(BF16)
    head_expand = jnp.concatenate([onehot.T, onehot.T], axis=0).astype(BF16)
    inv = ROPE_THETA ** (-jnp.arange(HEAD_DIM // 2, dtype=F32) * 2.0 / HEAD_DIM)
    inv_row = jnp.tile(inv, LANES // (HEAD_DIM // 2)).reshape(1, LANES)
    sink_vec = sinks[0]
    lam = lru_lambda[0]

    pad_rows = (-(nb + ns)) % SUBLANES
    c_all = jnp.concatenate([c_prompt, c_sample, jnp.zeros((pad_rows, d), F32)], axis=0)
    mod_all = _ada(c_all, w_ada[0], b_ada[0], 1024)
    mod_p = mod_all[:nb].reshape(nb, 1, N_MOD * d)
    mod_s = mod_all[nb:nb + ns]

    wts = dict(f1w1=ffn1_w1, f1w3=ffn1_w3, f1w2=ffn1_w2, f2w1=ffn2_w1, f2w3=ffn2_w3, f2w2=ffn2_w2, wz=w_in)

    def trunk(x, mod, rows_per_mod, cs, sn, table_blocks, t):
        h1 = _modulated_norm(x, mod, rows_per_mod, g_norm_ffn1[0], 0, t["tm_norm"])
        g1, wts["f1w1"], wts["f1w3"] = _ffn_up(h1, wts["f1w1"], wts["f1w3"], t["tm_up"], t["tf"])
        x1, wts["f1w2"] = _ffn_down(g1, wts["f1w2"], x, mod, rows_per_mod, 2, t["tm_down"], t["tn_down"])
        hm, q, kx, vx, kf, vf = _inproj_qkv(x1, mod, rows_per_mod, g_norm_mix[0], 3, w_in, cs, sn, gqk,
                                            head_reduce, head_expand, t["tm_qkv"], table_blocks)
        z, wts["wz"] = _inproj_z(hm, wts["wz"], t["tm_z"], t["tn_z"])
        return x1, z, q, kx, vx, kf, vf

    def tail(o_att, o_rnn, z, x1, mod, rows_per_mod, t):
        x2, h2 = _mix(o_att, o_rnn, z, x1, mod, rows_per_mod, g_norm_ffn2[0], wpa, wpr, wout, 5, 6, t["tm_mix"])
        g2, wts["f2w1"], wts["f2w3"] = _ffn_up(h2, wts["f2w1"], wts["f2w3"], t["tm_up"], t["tf"])
        y, wts["f2w2"] = _ffn_down(g2, wts["f2w2"], x2, mod, rows_per_mod, 8, t["tm_down"], t["tn_down"])
        return y

    tp = _tiles(nb * seq, dff)
    tp = {k: (min(v, seq) if k.startswith("tm") else v) for k, v in tp.items()}
    cs_p, sn_p = _rope_tables(inv_row, 0, seq, tp["tm_qkv"])
    x1, z, q, kx, vx, kf, vf = trunk(x_prompt.reshape(nb * seq, d), mod_p, seq, cs_p, sn_p,
                                     seq // tp["tm_qkv"], tp)
    nblk = seq // WINDOW
    o_att = _attn(sink_vec, q, kx, vx, nb, nblk, next(n for n in (4, 2, 1) if nblk % n == 0))
    o_rnn, h_p, conv_p = _rnn_prompt(z, nb, seq, conv_w[0], conv_b[0], wrg, wig, b_rg[0], b_ig[0], lam,
                                     min(512, seq))
    y_p = tail(o_att, o_rnn, z, x1, mod_p, seq, tp)

    last = lambda a: a.reshape(nb, seq, D_KV)[:, seq - WINDOW:].reshape(1, nb, WINDOW, N_KV_HEADS, HEAD_DIM)
    k_prompt, v_prompt = last(kf), last(vf)

    ts = _tiles(ns, dff)
    cs_1, sn_1 = _rope_tables(inv_row, PAST_LEN, SUBLANES, SUBLANES)
    cs_s = jnp.broadcast_to(cs_1[0:1], (ns, LANES))
    sn_s = jnp.broadcast_to(sn_1[0:1], (ns, LANES))
    x1s, zs, qs, kxs, vxs, kfs, vfs = trunk(x_sample.reshape(ns, d), mod_s, 1, cs_s, sn_s, 1, ts)
    nsub = 2 * SUBLANES
    assert ns % nsub == 0
    o_att_s = _attn_step(sink_vec, qs, kxs, vxs, cache_k[0].reshape(ns * WINDOW, D_KV),
                         cache_v[0].reshape(ns * WINDOW, D_KV), nsub)
    conv_s_in = state_conv[0]
    o_rnn_s, h_s = _rnn_step(zs, conv_s_in, state_h[0], conv_w[0], conv_b[0], wrg, wig, b_rg[0], b_ig[0], lam)
    y_s = tail(o_att_s, o_rnn_s, zs, x1s, mod_s, 1, ts)

    k_sample = kfs.reshape(1, ns, 1, N_KV_HEADS, HEAD_DIM)
    v_sample = vfs.reshape(1, ns, 1, N_KV_HEADS, HEAD_DIM)
    conv_sample = jnp.concatenate([conv_s_in[:, 1:], zs[:, None, Z_RX:Z_RX + D_RNN]], axis=1)[None]

    return (y_p.reshape(nb, seq, d), y_s.reshape(ns, 1, d), k_prompt, v_prompt, k_sample, v_sample,
            h_p.reshape(1, nb, D_RNN), h_s[None], conv_p[None], conv_sample)
```

```python
import functools

import jax
import jax.numpy as jnp
from jax import lax
from jax.experimental import pallas as pl
from jax.experimental.pallas import tpu as pltpu

F32 = jnp.float32
BF16 = jnp.bfloat16

D_MODEL = 2048
HEAD_DIM = 64
N_Q_HEADS = 16
N_KV_HEADS = 4
Q_PER_KV = N_Q_HEADS // N_KV_HEADS
D_ATTN = N_Q_HEADS * HEAD_DIM
D_KV = N_KV_HEADS * HEAD_DIM
D_QK = D_ATTN + D_KV
D_QKV = D_QK + D_KV
WINDOW = 128
ROPE_THETA = 10000.0
D_RNN = 1024
N_RNN_BLOCKS = 16
RNN_BLOCK = D_RNN // N_RNN_BLOCKS
CONV_WIDTH = 4
LRU_C = 8.0
N_MOD = 9
EPS = 1e-6
PAST_LEN = 16384

LANES = 128
SUBLANES = 8
MXU_DIM = 256
VMEM_LIMIT = 56 * 1024 * 1024

Z_RX = 0
Z_RG = Z_RX + D_RNN
Z_GATT = Z_RG + D_RNN
Z_GRNN = Z_GATT + D_MODEL
D_Z = Z_GRNN + D_MODEL
DX = 2 * N_KV_HEADS * LANES


def _params(*sem):
    return pltpu.CompilerParams(dimension_semantics=sem, vmem_limit_bytes=VMEM_LIMIT)


def _silu(x):
    return x * jax.nn.sigmoid(x)


def _modulate(x, g, shift, scale):
    ms = jnp.mean(x * x, axis=-1, keepdims=True)
    return (x * lax.rsqrt(ms + EPS) * g) * (1.0 + scale) + shift


def _mod_spec(mod, tm, rows_per_mod, width, col, ngrid):
    colf = col if callable(col) else (lambda *idx: col)
    if mod.ndim == 3:
        return pl.BlockSpec((None, 1, width), lambda *idx: ((idx[ngrid - 1] * tm) // rows_per_mod, 0, colf(*idx)))
    return pl.BlockSpec((tm, width), lambda *idx: (idx[ngrid - 1], colf(*idx)))


def _ada_kernel(c_ref, w_ref, b_ref, o_ref):
    s = _silu(c_ref[...]).astype(BF16)
    o_ref[...] = jnp.dot(s, w_ref[...].astype(BF16), preferred_element_type=F32) + b_ref[...]


def _ada(c_all, w_ada, b_ada, tn):
    rows, d = c_all.shape
    n = w_ada.shape[1]
    return pl.pallas_call(
        _ada_kernel,
        grid=(n // tn,),
        in_specs=[pl.BlockSpec((rows, d), lambda j: (0, 0)),
                  pl.BlockSpec((d, tn), lambda j: (0, j)),
                  pl.BlockSpec((1, tn), lambda j: (0, j))],
        out_specs=pl.BlockSpec((rows, tn), lambda j: (0, j)),
        out_shape=jax.ShapeDtypeStruct((rows, n), F32),
        compiler_params=_params("arbitrary"),
        name="ada",
    )(c_all, w_ada, b_ada.reshape(1, n))


def _rope_kernel(inv_ref, cs_ref, sn_ref, *, base, blk):
    r = pl.program_id(0)
    pos = (base + r * blk + lax.broadcasted_iota(jnp.int32, (blk, LANES), 0)).astype(F32)
    ang = pos * inv_ref[...]
    lane = lax.broadcasted_iota(jnp.int32, (blk, LANES), 1)
    first_half = (lane & (HEAD_DIM - 1)) < HEAD_DIM // 2
    sn = jnp.sin(ang)
    cs_ref[...] = jnp.cos(ang)
    sn_ref[...] = jnp.where(first_half, -sn, sn)


def _rope_tables(inv_row, base, rows, blk):
    out = jax.ShapeDtypeStruct((rows, LANES), F32)
    return pl.pallas_call(
        functools.partial(_rope_kernel, base=base, blk=blk),
        grid=(rows // blk,),
        in_specs=[pl.BlockSpec((1, LANES), lambda r: (0, 0))],
        out_specs=[pl.BlockSpec((blk, LANES), lambda r: (r, 0))] * 2,
        out_shape=[out, out],
        compiler_params=_params("arbitrary"),
        name="rope_tables",
    )(inv_row)


def _modulate_kernel(x_ref, shift_ref, scale_ref, g_ref, h_ref):
    group = 2 * SUBLANES
    per_row = shift_ref.shape[0] > 1

    def body(r, carry):
        rows = pl.ds(pl.multiple_of(r * group, group), group)
        shift = shift_ref[rows, :] if per_row else shift_ref[...]
        scale = scale_ref[rows, :] if per_row else scale_ref[...]
        h_ref[rows, :] = _modulate(x_ref[rows, :], g_ref[...], shift, scale).astype(BF16)
        return carry

    lax.fori_loop(0, x_ref.shape[0] // group, body, 0, unroll=8)


def _modulated_norm(x, mod, rows_per_mod, g, k_shift, tm):
    m_rows, d = x.shape
    return pl.pallas_call(
        _modulate_kernel,
        grid=(m_rows // tm,),
        in_specs=[pl.BlockSpec((tm, d), lambda m: (m, 0)),
                  _mod_spec(mod, tm, rows_per_mod, d, k_shift, 1),
                  _mod_spec(mod, tm, rows_per_mod, d, k_shift + 1, 1),
                  pl.BlockSpec((1, d), lambda m: (0, 0))],
        out_specs=pl.BlockSpec((tm, d), lambda m: (m, 0)),
        out_shape=jax.ShapeDtypeStruct((m_rows, d), BF16),
        compiler_params=_params("arbitrary"),
        name="modnorm",
    )(x, mod, mod, g.reshape(1, d))


HBM = pl.BlockSpec(memory_space=pl.ANY)


def _is_param(w):
    return w.dtype == F32


def _stream_weight_tiles(weights, tn, first):
    t, nt = pl.program_id(0), pl.num_programs(0)
    m, nm = pl.program_id(1), pl.num_programs(1)
    cols = lambda tile: pl.ds(pl.multiple_of(tile * tn, LANES), tn)
    fetch = lambda w_hbm, stage_ref, sem, tile: pltpu.make_async_copy(
        w_hbm.at[0, :, cols(first + tile)], stage_ref, sem.at[0])
    put = lambda wb_ref, wout_hbm, sem, tile: pltpu.make_async_copy(wb_ref, wout_hbm.at[:, cols(tile)], sem.at[1])

    @pl.when(m == 0)
    def _():
        for w_hbm, stage_ref, wb_ref, wout_hbm, sem in weights:
            @pl.when(t == 0)
            def _():
                fetch(w_hbm, stage_ref, sem, 0).start()

            fetch(w_hbm, stage_ref, sem, t).wait()

            @pl.when(t > 0)
            def _():
                put(wb_ref, wout_hbm, sem, t - 1).wait()

            wb_ref[...] = stage_ref[...].astype(BF16)
            put(wb_ref, wout_hbm, sem, t).start()

            @pl.when(t + 1 < nt)
            def _():
                fetch(w_hbm, stage_ref, sem, t + 1).start()

    @pl.when((t == nt - 1) & (m == nm - 1))
    def _():
        for w_hbm, stage_ref, wb_ref, wout_hbm, sem in weights:
            put(wb_ref, wout_hbm, sem, t).wait()


def _weight_scratch(rows, tn):
    return [pltpu.VMEM((rows, tn), F32), pltpu.VMEM((rows, tn), BF16), pltpu.SemaphoreType.DMA((2,))]


def _row_groups(tm):
    n = max(1, tm // 1024)
    return [slice(r * (tm // n), (r + 1) * (tm // n)) for r in range(n)]


def _ffn_up_rows(h_ref, w1b_ref, w3b_ref, g_ref):
    for rows in _row_groups(h_ref.shape[0]):
        h = h_ref[rows, :]
        a = jnp.dot(h, w1b_ref[...], preferred_element_type=F32)
        b = jnp.dot(h, w3b_ref[...], preferred_element_type=F32)
        g_ref[rows, :] = (_silu(a) * b).astype(BF16)


def _ffn_up_param_kernel(h_ref, w1_hbm, w3_hbm, g_ref, w1o_hbm, w3o_hbm,
                         s1_ref, w1b_ref, sem1, s3_ref, w3b_ref, sem3, *, tf):
    _stream_weight_tiles(((w1_hbm, s1_ref, w1b_ref, w1o_hbm, sem1), (w3_hbm, s3_ref, w3b_ref, w3o_hbm, sem3)),
                         tf, 0)
    _ffn_up_rows(h_ref, w1b_ref, w3b_ref, g_ref)


def _ffn_up(h, w1, w3, tm, tf):
    m_rows, d = h.shape
    dff = w1.shape[-1]
    grid = (dff // tf, m_rows // tm)
    h_spec = pl.BlockSpec((tm, d), lambda f, m: (m, 0))
    g_spec = pl.BlockSpec((tm, tf), lambda f, m: (m, f))
    g_shape = jax.ShapeDtypeStruct((m_rows, dff), BF16)
    if not _is_param(w1):
        w_spec = pl.BlockSpec((d, tf), lambda f, m: (0, f))
        g = pl.pallas_call(
            _ffn_up_rows, grid=grid, in_specs=[h_spec, w_spec, w_spec], out_specs=g_spec, out_shape=g_shape,
            compiler_params=_params("arbitrary", "arbitrary"), name="ffn_up_rows",
        )(h, w1, w3)
        return g, w1, w3
    w_shape = jax.ShapeDtypeStruct((d, dff), BF16)
    return pl.pallas_call(
        functools.partial(_ffn_up_param_kernel, tf=tf),
        grid=grid,
        in_specs=[h_spec, HBM, HBM],
        out_specs=[g_spec, HBM, HBM],
        out_shape=[g_shape, w_shape, w_shape],
        scratch_shapes=_weight_scratch(d, tf) + _weight_scratch(d, tf),
        compiler_params=_params("arbitrary", "arbitrary"),
        name="ffn_up",
    )(h, w1, w3)


def _ffn_down_rows(g_ref, w2b_ref, x_ref, gate_ref, o_ref):
    for rows in _row_groups(g_ref.shape[0]):
        acc = jnp.dot(g_ref[rows, :], w2b_ref[...], preferred_element_type=F32)
        gate = gate_ref[...] if gate_ref.shape[0] == 1 else gate_ref[rows, :]
        o_ref[rows, :] = x_ref[rows, :] + 0.5 * gate * acc


def _ffn_down_param_kernel(g_ref, w2_hbm, x_ref, gate_ref, o_ref, w2o_hbm, s2_ref, w2b_ref, sem2, *, tn):
    _stream_weight_tiles(((w2_hbm, s2_ref, w2b_ref, w2o_hbm, sem2),), tn, 0)
    _ffn_down_rows(g_ref, w2b_ref, x_ref, gate_ref, o_ref)


def _ffn_down(g, w2, x, mod, rows_per_mod, k_gate, tm, tn):
    m_rows, d = x.shape
    dff = g.shape[1]
    grid = (d // tn, m_rows // tm)
    gate_col = lambda n, m: k_gate * (d // tn) + n
    g_spec = pl.BlockSpec((tm, dff), lambda n, m: (m, 0))
    x_spec = pl.BlockSpec((tm, tn), lambda n, m: (m, n))
    gate_spec = _mod_spec(mod, tm, rows_per_mod, tn, gate_col, 2)
    o_shape = jax.ShapeDtypeStruct((m_rows, d), F32)
    if not _is_param(w2):
        out = pl.pallas_call(
            _ffn_down_rows, grid=grid,
            in_specs=[g_spec, pl.BlockSpec((dff, tn), lambda n, m: (0, n)), x_spec, gate_spec],
            out_specs=x_spec, out_shape=o_shape,
            compiler_params=_params("arbitrary", "arbitrary"), name="ffn_down_rows",
        )(g, w2, x, mod)
        return out, w2
    return pl.pallas_call(
        functools.partial(_ffn_down_param_kernel, tn=tn),
        grid=grid,
        in_specs=[g_spec, HBM, x_spec, gate_spec],
        out_specs=[x_spec, HBM],
        out_shape=[o_shape, jax.ShapeDtypeStruct((dff, d), BF16)],
        scratch_shapes=_weight_scratch(dff, tn),
        compiler_params=_params("arbitrary", "arbitrary"),
        name="ffn_down",
    )(g, w2, x, mod)


def _inproj_z_rows(h_ref, wb_ref, z_ref):
    for rows in _row_groups(h_ref.shape[0]):
        z_ref[rows, :] = jnp.dot(h_ref[rows, :], wb_ref[...], preferred_element_type=F32)


def _inproj_z_param_kernel(h_ref, w_hbm, z_ref, wo_hbm, s_ref, wb_ref, sem, *, tn, first):
    _stream_weight_tiles(((w_hbm, s_ref, wb_ref, wo_hbm, sem),), tn, first)
    _inproj_z_rows(h_ref, wb_ref, z_ref)


def _inproj_z(h, w, tm, tn):
    m_rows, d = h.shape
    grid = (D_Z // tn, m_rows // tm)
    h_spec = pl.BlockSpec((tm, d), lambda n, m: (m, 0))
    z_spec = pl.BlockSpec((tm, tn), lambda n, m: (m, n))
    z_shape = jax.ShapeDtypeStruct((m_rows, D_Z), F32)
    if not _is_param(w):
        z = pl.pallas_call(
            _inproj_z_rows, grid=grid, in_specs=[h_spec, pl.BlockSpec((d, tn), lambda n, m: (0, n))],
            out_specs=z_spec, out_shape=z_shape,
            compiler_params=_params("arbitrary", "arbitrary"), name="inproj_z_rows",
        )(h, w)
        return z, w
    return pl.pallas_call(
        functools.partial(_inproj_z_param_kernel, tn=tn, first=D_QKV // tn),
        grid=grid,
        in_specs=[h_spec, HBM],
        out_specs=[z_spec, HBM],
        out_shape=[z_shape, jax.ShapeDtypeStruct((d, D_Z), BF16)],
        scratch_shapes=_weight_scratch(d, tn),
        compiler_params=_params("arbitrary", "arbitrary"),
        name="inproj_z",
    )(h, w)


def _store_head_pairs(ref, rows, chunk, c):
    lo = lax.broadcasted_iota(jnp.int32, chunk.shape, 1) < HEAD_DIM
    swapped = pltpu.roll(chunk, HEAD_DIM, 1)
    zero = jnp.zeros_like(chunk)
    cols = (jnp.where(lo, chunk, zero), jnp.where(lo, zero, swapped),
            jnp.where(lo, swapped, zero), jnp.where(lo, zero, chunk))
    for i, col in enumerate(cols):
        ref[rows, (4 * c + i) * LANES:(4 * c + i + 1) * LANES] = col.astype(ref.dtype)


def _inproj_qkv_kernel(x_ref, shift_ref, scale_ref, g_ref, w_ref, cs_ref, sn_ref, gqk_ref, red_ref, exp_ref,
                       h_ref, q_ref, kx_ref, vx_ref, kf_ref, vf_ref, wb_ref, *, nsplit):
    @pl.when(pl.program_id(0) == 0)
    def _():
        wb_ref[...] = w_ref[...].astype(BF16)

    step = x_ref.shape[0] // nsplit
    for r in range(nsplit):
        rows = slice(r * step, (r + 1) * step)
        shift = shift_ref[...] if shift_ref.shape[0] == 1 else shift_ref[rows, :]
        scale = scale_ref[...] if scale_ref.shape[0] == 1 else scale_ref[rows, :]
        h = _modulate(x_ref[rows, :], g_ref[...], shift, scale).astype(BF16)
        h_ref[rows, :] = h
        acc = jnp.dot(h, wb_ref[...], preferred_element_type=F32)
        qk = acc[:, :D_QK]
        hm = jnp.dot((qk * qk).astype(BF16), red_ref[...], preferred_element_type=F32)
        hi = hm.astype(BF16)
        lo = (hm - hi.astype(F32)).astype(BF16)
        ms = jnp.dot(jnp.concatenate([hi, lo], axis=1), exp_ref[...], preferred_element_type=F32)
        y = qk * lax.rsqrt(ms + EPS) * gqk_ref[...]
        cs = cs_ref[rows, :]
        sn = sn_ref[rows, :]
        lane = lax.broadcasted_iota(jnp.int32, cs.shape, 1)
        first_half = (lane & (HEAD_DIM - 1)) < HEAD_DIM // 2
        for c in range(D_QK // LANES):
            yc = y[:, c * LANES:(c + 1) * LANES]
            partner = jnp.where(first_half,
                                pltpu.roll(yc, LANES - HEAD_DIM // 2, 1),
                                pltpu.roll(yc, HEAD_DIM // 2, 1))
            rot = yc * cs + partner * sn
            if c < D_ATTN // LANES:
                q_ref[rows, c * LANES:(c + 1) * LANES] = (rot * (HEAD_DIM ** -0.5)).astype(BF16)
            else:
                kf_ref[rows, c * LANES - D_ATTN:(c + 1) * LANES - D_ATTN] = rot
                _store_head_pairs(kx_ref, rows, rot, c - D_ATTN // LANES)
        v = acc[:, D_QK:]
        vf_ref[rows, :] = v
        for c in range(D_KV // LANES):
            _store_head_pairs(vx_ref, rows, v[:, c * LANES:(c + 1) * LANES], c)


def _inproj_qkv(x, mod, rows_per_mod, g, k_shift, w_in, cs, sn, gqk, head_reduce, head_expand, tm, table_blocks):
    m_rows, d = x.shape
    row = lambda width: pl.BlockSpec((tm, width), lambda m: (m, 0))
    table = pl.BlockSpec((tm, LANES), lambda m: (m % table_blocks, 0))
    nsplit = 2 if tm % (2 * MXU_DIM) == 0 else 1
    return pl.pallas_call(
        functools.partial(_inproj_qkv_kernel, nsplit=nsplit),
        grid=(m_rows // tm,),
        in_specs=[row(d),
                  _mod_spec(mod, tm, rows_per_mod, d, k_shift, 1),
                  _mod_spec(mod, tm, rows_per_mod, d, k_shift + 1, 1),
                  pl.BlockSpec((1, d), lambda m: (0, 0)),
                  pl.BlockSpec((None, d, D_QKV), lambda m: (0, 0, 0), pipeline_mode=pl.Buffered(1)),
                  table, table,
                  pl.BlockSpec((1, D_QK), lambda m: (0, 0)),
                  pl.BlockSpec(head_reduce.shape, lambda m: (0, 0)),
                  pl.BlockSpec(head_expand.shape, lambda m: (0, 0))],
        out_specs=[row(d), row(D_ATTN), row(DX), row(DX), row(D_KV), row(D_KV)],
        out_shape=[jax.ShapeDtypeStruct((m_rows, d), BF16),
                   jax.ShapeDtypeStruct((m_rows, D_ATTN), BF16),
                   jax.ShapeDtypeStruct((m_rows, DX), BF16),
                   jax.ShapeDtypeStruct((m_rows, DX), BF16),
                   jax.ShapeDtypeStruct((m_rows, D_KV), F32),
                   jax.ShapeDtypeStruct((m_rows, D_KV), F32)],
        scratch_shapes=[pltpu.VMEM((d, D_QKV), BF16)],
        compiler_params=_params("arbitrary"),
        name="inproj_qkv",
    )(x, mod, mod, g.reshape(1, d), w_in, cs, sn, gqk, head_reduce, head_expand)


def _attn_bias(bias_ref, tq, past_off):
    nk = 2 * WINDOW
    ri = lax.broadcasted_iota(jnp.int32, (2 * tq, 2 * nk), 0) & (tq - 1)
    kj = lax.broadcasted_iota(jnp.int32, (2 * tq, 2 * nk), 1) & (nk - 1)
    visible = ((kj < WINDOW) & (kj > ri + past_off)) | ((kj >= WINDOW) & (kj - WINDOW <= ri))
    bias_ref[...] = jnp.where(visible, 0.0, -jnp.inf)


def _attn_blocks(sink_ref, bias_of, q_ref, past_of, kc_ref, vc_ref, nsub, tq, store):
    nk = 2 * WINDOW
    rows = 2 * tq
    first_rows = lax.broadcasted_iota(jnp.int32, (rows, 1), 0) < tq
    lo_lanes = lax.broadcasted_iota(jnp.int32, (rows, LANES), 1) < HEAD_DIM
    ones_lo = (lax.broadcasted_iota(jnp.int32, (nk, LANES), 1) < HEAD_DIM).astype(BF16)
    ones_hi = (1 - ones_lo.astype(F32)).astype(BF16)
    nt = (((1,), (1,)), ((), ()))

    def keys(which, s, col):
        cols = slice(col * LANES, (col + 1) * LANES)
        past = past_of(s)
        parts = [past[which][past[2]:past[2] + WINDOW, cols], (kc_ref, vc_ref)[which][s * tq:(s + 1) * tq, cols]]
        if tq < WINDOW:
            parts.append(jnp.zeros((WINDOW - tq, LANES), BF16))
        return parts

    for s in range(nsub):
        for g in range(N_KV_HEADS):
            c0, c1 = 2 * g, 2 * g + 1
            q4 = jnp.concatenate([q_ref[s * tq:(s + 1) * tq, c0 * LANES:(c0 + 1) * LANES],
                                  q_ref[s * tq:(s + 1) * tq, c1 * LANES:(c1 + 1) * LANES]], axis=0)
            kk = jnp.concatenate(keys(0, s, c0) + keys(0, s, c1), axis=0)
            sc = lax.dot_general(q4, kk, nt, preferred_element_type=F32) + bias_of(s)[...]
            sk_lo = jnp.where(first_rows, sink_ref[4 * g], sink_ref[4 * g + 2])
            sk_hi = jnp.where(first_rows, sink_ref[4 * g + 1], sink_ref[4 * g + 3])
            m_lo = jnp.maximum(jnp.max(sc[:, :nk], axis=-1, keepdims=True), sk_lo)
            m_hi = jnp.maximum(jnp.max(sc[:, nk:], axis=-1, keepdims=True), sk_hi)
            p = jnp.concatenate([jnp.exp(sc[:, :nk] - m_lo), jnp.exp(sc[:, nk:] - m_hi)], axis=1).astype(BF16)
            vv = jnp.concatenate(
                [jnp.concatenate([jnp.concatenate(keys(1, s, c0), axis=0), ones_lo], axis=1),
                 jnp.concatenate([jnp.concatenate(keys(1, s, c1), axis=0), ones_hi], axis=1)],
                axis=0)
            o = jnp.dot(p, vv, preferred_element_type=F32)
            denom = o[:, LANES:] + jnp.where(lo_lanes, jnp.exp(sk_lo - m_lo), jnp.exp(sk_hi - m_hi))
            store(s, c0, c1, o[:, :LANES] / denom)


def _attn_kernel(sink_ref, q_ref, kp_ref, kc_ref, vp_ref, vc_ref, o_ref, bias0_ref, bias_ref, *, nq):
    tq = WINDOW
    _attn_bias(bias0_ref, tq, jnp.where(pl.program_id(1) > 0, 0, WINDOW))
    if nq > 1:
        _attn_bias(bias_ref, tq, 0)

    def store(s, c0, c1, out):
        o_ref[s * tq:(s + 1) * tq, c0 * LANES:(c0 + 1) * LANES] = out[:tq].astype(BF16)
        o_ref[s * tq:(s + 1) * tq, c1 * LANES:(c1 + 1) * LANES] = out[tq:].astype(BF16)

    past_of = lambda s: (kp_ref, vp_ref, 0) if s == 0 else (kc_ref, vc_ref, (s - 1) * WINDOW)
    bias_of = lambda s: bias0_ref if s == 0 else bias_ref
    _attn_blocks(sink_ref, bias_of, q_ref, past_of, kc_ref, vc_ref, nq, tq, store)


def _attn(sinks, q, kx, vx, nbatch, nblk, nq):
    steps = nblk // nq
    cur = lambda b, n: (b * steps + n, 0)
    past = lambda b, n: (b * nblk + jnp.maximum(n * nq - 1, 0), 0)
    kv_cur = pl.BlockSpec((nq * WINDOW, DX), cur)
    kv_past = pl.BlockSpec((WINDOW, DX), past)
    bias = pltpu.VMEM((2 * WINDOW, 4 * WINDOW), F32)
    return pl.pallas_call(
        functools.partial(_attn_kernel, nq=nq),
        grid=(nbatch, steps),
        in_specs=[pl.BlockSpec(memory_space=pltpu.SMEM), pl.BlockSpec((nq * WINDOW, D_ATTN), cur),
                  kv_past, kv_cur, kv_past, kv_cur],
        out_specs=pl.BlockSpec((nq * WINDOW, D_ATTN), cur),
        out_shape=jax.ShapeDtypeStruct(q.shape, BF16),
        scratch_shapes=[bias, bias],
        compiler_params=_params("arbitrary", "arbitrary"),
        name="attn",
    )(sinks, q, kx, kx, vx, vx)


def _attn_step_kernel(sink_ref, q_ref, kn_ref, vn_ref, ck_ref, cv_ref, o_ref,
                      bias_ref, qx_ref, kcx_ref, vcx_ref, kpx_ref, vpx_ref, *, nsub, tq):
    _attn_bias(bias_ref, tq, 0)
    for c in range(D_KV // LANES):
        _store_head_pairs(kpx_ref, slice(None), ck_ref[:, c * LANES:(c + 1) * LANES], c)
        _store_head_pairs(vpx_ref, slice(None), cv_ref[:, c * LANES:(c + 1) * LANES], c)
    for src, dst in ((q_ref, qx_ref), (kn_ref, kcx_ref), (vn_ref, vcx_ref)):
        rows = src[...].astype(F32)
        for s in range(nsub):
            dst[s * tq:(s + 1) * tq, :] = jnp.broadcast_to(rows[s:s + 1, :], (tq, rows.shape[1])).astype(BF16)

    def store(s, c0, c1, out):
        o_ref[s:s + 1, c0 * LANES:(c0 + 1) * LANES] = out[0:1]
        o_ref[s:s + 1, c1 * LANES:(c1 + 1) * LANES] = out[tq:tq + 1]

    past_of = lambda s: (kpx_ref, vpx_ref, s * WINDOW)
    _attn_blocks(sink_ref, lambda s: bias_ref, qx_ref, past_of, kcx_ref, vcx_ref, nsub, tq, store)


def _attn_step(sinks, q, kx_new, vx_new, cache_k, cache_v, nsub):
    ns = q.shape[0]
    tq = 2 * SUBLANES
    row = lambda width: pl.BlockSpec((nsub, width), lambda b: (b, 0))
    cache = pl.BlockSpec((nsub * WINDOW, D_KV), lambda b: (b, 0))
    return pl.pallas_call(
        functools.partial(_attn_step_kernel, nsub=nsub, tq=tq),
        grid=(ns // nsub,),
        in_specs=[pl.BlockSpec(memory_space=pltpu.SMEM), row(D_ATTN), row(DX), row(DX), cache, cache],
        out_specs=row(D_ATTN),
        out_shape=jax.ShapeDtypeStruct((ns, D_ATTN), F32),
        scratch_shapes=[pltpu.VMEM((2 * tq, 4 * WINDOW), F32),
                        pltpu.VMEM((nsub * tq, D_ATTN), BF16),
                        pltpu.VMEM((nsub * tq, DX), BF16),
                        pltpu.VMEM((nsub * tq, DX), BF16),
                        pltpu.VMEM((nsub * WINDOW, DX), BF16),
                        pltpu.VMEM((nsub * WINDOW, DX), BF16)],
        compiler_params=_params("arbitrary"),
        name="attn_step",
    )(sinks, q, kx_new, vx_new, cache_k, cache_v)


def _softplus(x):
    return jnp.maximum(x, 0.0) + jnp.log1p(jnp.exp(-jnp.abs(x)))


def _lru_coeffs(xc, wrg_ref, wig_ref, brg, big, lam):
    xb = xc.astype(BF16)
    ngroups = D_RNN // MXU_DIM
    rs, igs = [], []
    for c in range(ngroups):
        xg = xb[:, c * MXU_DIM:(c + 1) * MXU_DIM]
        rs.append(jnp.dot(xg, wrg_ref[c], preferred_element_type=F32))
        igs.append(jnp.dot(xg, wig_ref[c], preferred_element_type=F32))
    r = jax.nn.sigmoid(jnp.concatenate(rs, axis=1) + brg)
    ig = jax.nn.sigmoid(jnp.concatenate(igs, axis=1) + big)
    log_a = -LRU_C * r * _softplus(-lam)
    a = jnp.exp(log_a)
    one_minus_a2 = -jnp.tanh(log_a) * (1.0 + a * a)
    u = jnp.sqrt(one_minus_a2) * (ig * xc)
    return a, u


def _rnn_prompt_kernel(rx_ref, rg_ref, cw_ref, cb_ref, wrg_ref, wig_ref, brg_ref, big_ref, lam_ref,
                       o_ref, hlast_ref, conv_ref, xs_ref, a_ref, h_ref, carry_ref, *, tc):
    t = pl.program_id(1)
    pad = SUBLANES

    @pl.when(t == 0)
    def _():
        xs_ref[0:pad, :] = jnp.zeros((pad, D_RNN), F32)
        carry_ref[...] = jnp.zeros_like(carry_ref)

    x = rx_ref[...]
    xs_ref[pad:pad + tc, :] = x
    xc = cb_ref[...] + x * cw_ref[CONV_WIDTH - 1:CONV_WIDTH, :]
    for k in range(1, CONV_WIDTH):
        xc = xc + xs_ref[pad - k:pad - k + tc, :] * cw_ref[CONV_WIDTH - 1 - k:CONV_WIDTH - k, :]
    tail = xs_ref[tc:tc + pad, :]
    xs_ref[0:pad, :] = tail
    conv_ref[...] = tail[pad - (CONV_WIDTH - 1):, :]

    a, u = _lru_coeffs(xc, wrg_ref, wig_ref, brg_ref[...], big_ref[...], lam_ref[...])
    a_ref[...] = a
    h_ref[...] = u

    row = lax.broadcasted_iota(jnp.int32, (SUBLANES, D_RNN), 0)

    def body(r, carry):
        off = pl.multiple_of(r * SUBLANES, SUBLANES)
        av = a_ref[pl.ds(off, SUBLANES), :]
        hv = h_ref[pl.ds(off, SUBLANES), :]
        for sft in (1, 2, 4):
            keep = row >= sft
            a_sh = jnp.where(keep, pltpu.roll(av, sft, 0), 1.0)
            h_sh = jnp.where(keep, pltpu.roll(hv, sft, 0), 0.0)
            hv = av * h_sh + hv
            av = av * a_sh
        hv = hv + av * carry
        h_ref[pl.ds(off, SUBLANES), :] = hv
        return jnp.broadcast_to(hv[SUBLANES - 1:SUBLANES, :], (SUBLANES, D_RNN))

    carry = lax.fori_loop(0, tc // SUBLANES, body, carry_ref[...])
    carry_ref[...] = carry
    hlast_ref[...] = carry[0:1, :]
    o_ref[...] = (h_ref[...] * jax.nn.gelu(rg_ref[...])).astype(BF16)


def _rnn_prompt(z, nbatch, seq, conv_w, conv_b, wrg, wig, b_rg, b_ig, lam, tc):
    nchunk = seq // tc
    rx_blk = Z_RX // D_RNN
    rg_blk = Z_RG // D_RNN
    vec = pl.BlockSpec((1, D_RNN), lambda b, t: (0, 0))
    wspec = pl.BlockSpec(wrg.shape, lambda b, t: (0, 0, 0))
    return pl.pallas_call(
        functools.partial(_rnn_prompt_kernel, tc=tc),
        grid=(nbatch, nchunk),
        in_specs=[pl.BlockSpec((tc, D_RNN), lambda b, t: (b * nchunk + t, rx_blk)),
                  pl.BlockSpec((tc, D_RNN), lambda b, t: (b * nchunk + t, rg_blk)),
                  pl.BlockSpec((CONV_WIDTH, D_RNN), lambda b, t: (0, 0)),
                  vec, wspec, wspec, vec, vec, vec],
        out_specs=[pl.BlockSpec((tc, D_RNN), lambda b, t: (b * nchunk + t, 0)),
                   pl.BlockSpec((None, 1, D_RNN), lambda b, t: (b, 0, 0)),
                   pl.BlockSpec((None, CONV_WIDTH - 1, D_RNN), lambda b, t: (b, 0, 0))],
        out_shape=[jax.ShapeDtypeStruct((nbatch * seq, D_RNN), BF16),
                   jax.ShapeDtypeStruct((nbatch, 1, D_RNN), F32),
                   jax.ShapeDtypeStruct((nbatch, CONV_WIDTH - 1, D_RNN), F32)],
        scratch_shapes=[pltpu.VMEM((tc + SUBLANES, D_RNN), F32),
                        pltpu.VMEM((tc, D_RNN), F32),
                        pltpu.VMEM((tc, D_RNN), F32),
                        pltpu.VMEM((SUBLANES, D_RNN), F32)],
        compiler_params=_params("arbitrary", "arbitrary"),
        name="rnn_prompt",
    )(z, z, conv_w, conv_b.reshape(1, D_RNN), wrg, wig, b_rg.reshape(1, D_RNN),
      b_ig.reshape(1, D_RNN), lam.reshape(1, D_RNN))


def _rnn_step_kernel(rx_ref, rg_ref, c0_ref, c1_ref, c2_ref, h0_ref, cw_ref, cb_ref, wrg_ref, wig_ref,
                     brg_ref, big_ref, lam_ref, o_ref, h_ref):
    x = rx_ref[...]
    xc = (cb_ref[...] + c0_ref[...] * cw_ref[0:1, :] + c1_ref[...] * cw_ref[1:2, :]
          + c2_ref[...] * cw_ref[2:3, :] + x * cw_ref[3:4, :])
    a, u = _lru_coeffs(xc, wrg_ref, wig_ref, brg_ref[...], big_ref[...], lam_ref[...])
    h = a * h0_ref[...] + u
    h_ref[...] = h
    o_ref[...] = (h * jax.nn.gelu(rg_ref[...])).astype(BF16)


def _rnn_step(z, conv_state, h0, conv_w, conv_b, wrg, wig, b_rg, b_ig, lam):
    rows = z.shape[0]
    full = lambda shape: pl.BlockSpec(shape, lambda i: (0,) * len(shape))
    act = full((rows, D_RNN))
    vec = full((1, D_RNN))
    return pl.pallas_call(
        _rnn_step_kernel,
        grid=(1,),
        in_specs=[pl.BlockSpec((rows, D_RNN), lambda i: (0, Z_RX // D_RNN)),
                  pl.BlockSpec((rows, D_RNN), lambda i: (0, Z_RG // D_RNN)),
                  act, act, act, act, full((CONV_WIDTH, D_RNN)), vec,
                  full(wrg.shape), full(wig.shape), vec, vec, vec],
        out_specs=[act, act],
        out_shape=[jax.ShapeDtypeStruct((rows, D_RNN), BF16),
                   jax.ShapeDtypeStruct((rows, D_RNN), F32)],
        compiler_params=_params("arbitrary"),
        name="rnn_step",
    )(z, z, conv_state[:, 0], conv_state[:, 1], conv_state[:, 2], h0, conv_w,
      conv_b.reshape(1, D_RNN), wrg, wig, b_rg.reshape(1, D_RNN), b_ig.reshape(1, D_RNN),
      lam.reshape(1, D_RNN))


def _mix_kernel(oa_ref, or_ref, ga_ref, gr_ref, x_ref, gate_ref, shift_ref, scale_ref, g_ref,
                wpa_ref, wpr_ref, wout_ref, o_ref, h_ref):
    pa = jnp.dot(oa_ref[...].astype(BF16), wpa_ref[...], preferred_element_type=F32)
    pr = jnp.dot(or_ref[...], wpr_ref[...], preferred_element_type=F32)
    mix = jax.nn.sigmoid(ga_ref[...]) * pa + jax.nn.sigmoid(gr_ref[...]) * pr
    x = x_ref[...] + gate_ref[...] * jnp.dot(mix.astype(BF16), wout_ref[...], preferred_element_type=F32)
    o_ref[...] = x
    h_ref[...] = _modulate(x, g_ref[...], shift_ref[...], scale_ref[...]).astype(BF16)


def _mix(o_att, o_rnn, z, x, mod, rows_per_mod, g_next, w_pa, w_pr, w_out, k_gate, k_shift_next, tm):
    m_rows, d = x.shape
    const = lambda shape: pl.BlockSpec(shape, lambda m: (0, 0), pipeline_mode=pl.Buffered(1))
    row = lambda width, col: pl.BlockSpec((tm, width), lambda m: (m, col))
    mspec = lambda k: _mod_spec(mod, tm, rows_per_mod, d, k, 1)
    return pl.pallas_call(
        _mix_kernel,
        grid=(m_rows // tm,),
        in_specs=[row(D_ATTN, 0), row(D_RNN, 0), row(d, Z_GATT // d), row(d, Z_GRNN // d), row(d, 0),
                  mspec(k_gate), mspec(k_shift_next), mspec(k_shift_next + 1), const((1, d)),
                  const(w_pa.shape), const(w_pr.shape), const(w_out.shape)],
        out_specs=[row(d, 0), row(d, 0)],
        out_shape=[jax.ShapeDtypeStruct((m_rows, d), F32), jax.ShapeDtypeStruct((m_rows, d), BF16)],
        compiler_params=_params("arbitrary"),
        name="mix",
    )(o_att, o_rnn, z, z, x, mod, mod, mod, g_next.reshape(1, d), w_pa, w_pr, w_out)


def _block_diag(w, group):
    n, r, _ = w.shape
    eye = jnp.eye(group, dtype=w.dtype)
    wg = w.reshape(n // group, group, r, r)
    return jnp.einsum("ngij,gh->ngihj", wg, eye).reshape(n // group, group * r, group * r)


def _tiles(rows, dff):
    pick = lambda n, prefs: next((t for t in prefs if n % t == 0), n)
    return dict(
        tm_norm=pick(rows, (1024, 512)),
        tm_up=pick(rows, (2048, 1024, 512)), tf=pick(dff, (512,)),
        tm_down=pick(rows, (1024, 512)), tn_down=512,
        tm_z=pick(rows, (1024, 512)), tn_z=D_QKV,
        tm_qkv=pick(rows, (512,)),
        tm_mix=pick(rows, (256,)),
    )


def kernel(x_prompt, x_sample, c_prompt, c_sample, cache_k, cache_v, state_h, state_conv, w_ada, b_ada,
           g_norm_ffn1, g_norm_mix, g_norm_ffn2, ffn1_w1, ffn1_w3, ffn1_w2, ffn2_w1, ffn2_w3, ffn2_w2,
           w_in, g_q, g_k, sinks, conv_w, conv_b, w_rg, b_rg, w_ig, b_ig, lru_lambda, w_pa, w_pr, w_out):
    nb, seq, d = x_prompt.shape
    ns = x_sample.shape[0]
    dff = ffn1_w1.shape[2]
    assert d == D_MODEL and w_ada.shape[0] == 1 and x_sample.shape[1] == 1 and cache_k.shape[2] == WINDOW
    assert seq % WINDOW == 0 and w_in.shape[2] == D_QKV + D_Z

    group = MXU_DIM // RNN_BLOCK
    wrg = _block_diag(w_rg[0], group).astype(BF16)
    wig = _block_diag(w_ig[0], group).astype(BF16)
    wpa, wpr, wout = w_pa[0].astype(BF16), w_pr[0].astype(BF16), w_out[0].astype(BF16)
    gqk = jnp.concatenate([jnp.tile(g_q[0], N_Q_HEADS), jnp.tile(g_k[0], N_KV_HEADS)]).reshape(1, D_QK)
    head_of = jnp.arange(D_QK) // HEAD_DIM
    onehot = (head_of[:, None] == jnp.arange(LANES)[None, :]).astype(F32)
    head_reduce = (onehot / HEAD_DIM).astype(BF16)
    head_expand = jnp.concatenate([onehot.T, onehot.T], axis=0).astype(BF16)
    inv = ROPE_THETA ** (-jnp.arange(HEAD_DIM // 2, dtype=F32) * 2.0 / HEAD_DIM)
    inv_row = jnp.tile(inv, LANES // (HEAD_DIM // 2)).reshape(1, LANES)
    sink_vec = sinks[0]
    lam = lru_lambda[0]

    pad_rows = (-(nb + ns)) % SUBLANES
    c_all = jnp.concatenate([c_prompt, c_sample, jnp.zeros((pad_rows, d), F32)], axis=0)
    mod_all = _ada(c_all, w_ada[0], b_ada[0], 2048)
    mod_p = mod_all[:nb].reshape(nb, 1, N_MOD * d)
    mod_s = mod_all[nb:nb + ns]

    wts = dict(f1w1=ffn1_w1, f1w3=ffn1_w3, f1w2=ffn1_w2, f2w1=ffn2_w1, f2w3=ffn2_w3, f2w2=ffn2_w2, wz=w_in)

    def trunk(x, mod, rows_per_mod, cs, sn, table_blocks, t):
        h1 = _modulated_norm(x, mod, rows_per_mod, g_norm_ffn1[0], 0, t["tm_norm"])
        g1, wts["f1w1"], wts["f1w3"] = _ffn_up(h1, wts["f1w1"], wts["f1w3"], t["tm_up"], t["tf"])
        x1, wts["f1w2"] = _ffn_down(g1, wts["f1w2"], x, mod, rows_per_mod, 2, t["tm_down"], t["tn_down"])
        hm, q, kx, vx, kf, vf = _inproj_qkv(x1, mod, rows_per_mod, g_norm_mix[0], 3, w_in, cs, sn, gqk,
                                            head_reduce, head_expand, t["tm_qkv"], table_blocks)
        z, wts["wz"] = _inproj_z(hm, wts["wz"], t["tm_z"], t["tn_z"])
        return x1, z, q, kx, vx, kf, vf

    def tail(o_att, o_rnn, z, x1, mod, rows_per_mod, t):
        x2, h2 = _mix(o_att, o_rnn, z, x1, mod, rows_per_mod, g_norm_ffn2[0], wpa, wpr, wout, 5, 6, t["tm_mix"])
        g2, wts["f2w1"], wts["f2w3"] = _ffn_up(h2, wts["f2w1"], wts["f2w3"], t["tm_up"], t["tf"])
        y, wts["f2w2"] = _ffn_down(g2, wts["f2w2"], x2, mod, rows_per_mod, 8, t["tm_down"], t["tn_down"])
        return y

    tp = _tiles(nb * seq, dff)
    tp = {k: (min(v, seq) if k.startswith("tm") else v) for k, v in tp.items()}
    cs_p, sn_p = _rope_tables(inv_row, 0, seq, tp["tm_qkv"])
    x1, z, q, kx, vx, kf, vf = trunk(x_prompt.reshape(nb * seq, d), mod_p, seq, cs_p, sn_p,
                                     seq // tp["tm_qkv"], tp)
    nblk = seq // WINDOW
    o_att = _attn(sink_vec, q, kx, vx, nb, nblk, next(n for n in (4, 2, 1) if nblk % n == 0))
    o_rnn, h_p, conv_p = _rnn_prompt(z, nb, seq, conv_w[0], conv_b[0], wrg, wig, b_rg[0], b_ig[0], lam,
                                     min(512, seq))
    y_p = tail(o_att, o_rnn, z, x1, mod_p, seq, tp)

    last = lambda a: a.reshape(nb, seq, D_KV)[:, seq - WINDOW:].reshape(1, nb, WINDOW, N_KV_HEADS, HEAD_DIM)
    k_prompt, v_prompt = last(kf), last(vf)

    ts = _tiles(ns, dff)
    cs_1, sn_1 = _rope_tables(inv_row, PAST_LEN, SUBLANES, SUBLANES)
    cs_s = jnp.broadcast_to(cs_1[0:1], (ns, LANES))
    sn_s = jnp.broadcast_to(sn_1[0:1], (ns, LANES))
    x1s, zs, qs, kxs, vxs, kfs, vfs = trunk(x_sample.reshape(ns, d), mod_s, 1, cs_s, sn_s, 1, ts)
    nsub = 2 * SUBLANES
    assert ns % nsub == 0
    o_att_s = _attn_step(sink_vec, qs, kxs, vxs, cache_k[0].reshape(ns * WINDOW, D_KV),
                         cache_v[0].reshape(ns * WINDOW, D_KV), nsub)
    conv_s_in = state_conv[0]
    o_rnn_s, h_s = _rnn_step(zs, conv_s_in, state_h[0], conv_w[0], conv_b[0], wrg, wig, b_rg[0], b_ig[0], lam)
    y_s = tail(o_att_s, o_rnn_s, zs, x1s, mod_s, 1, ts)

    k_sample = kfs.reshape(1, ns, 1, N_KV_HEADS, HEAD_DIM)
    v_sample = vfs.reshape(1, ns, 1, N_KV_HEADS, HEAD_DIM)
    conv_sample = jnp.concatenate([conv_s_in[:, 1:], zs[:, None, Z_RX:Z_RX + D_RNN]], axis=1)[None]

    return (y_p.reshape(nb, seq, d), y_s.reshape(ns, 1, d), k_prompt, v_prompt, k_sample, v_sample,
            h_p.reshape(1, nb, D_RNN), h_s[None], conv_p[None], conv_sample)
```

```python
import functools

import jax
import jax.numpy as jnp
from jax import lax
from jax.experimental import pallas as pl
from jax.experimental.pallas import tpu as pltpu

F32 = jnp.float32
BF16 = jnp.bfloat16

D_MODEL = 2048
HEAD_DIM = 64
N_Q_HEADS = 16
N_KV_HEADS = 4
Q_PER_KV = N_Q_HEADS // N_KV_HEADS
D_ATTN = N_Q_HEADS * HEAD_DIM
D_KV = N_KV_HEADS * HEAD_DIM
D_QK = D_ATTN + D_KV
D_QKV = D_QK + D_KV
WINDOW = 128
ROPE_THETA = 10000.0
D_RNN = 1024
N_RNN_BLOCKS = 16
RNN_BLOCK = D_RNN // N_RNN_BLOCKS
CONV_WIDTH = 4
LRU_C = 8.0
N_MOD = 9
EPS = 1e-6
PAST_LEN = 16384

LANES = 128
SUBLANES = 8
MXU_DIM = 256
VMEM_LIMIT = 56 * 1024 * 1024

Z_RX = 0
Z_RG = Z_RX + D_RNN
Z_GATT = Z_RG + D_RNN
Z_GRNN = Z_GATT + D_MODEL
D_Z = Z_GRNN + D_MODEL
DX = 2 * N_KV_HEADS * LANES


def _params(*sem):
    return pltpu.CompilerParams(dimension_semantics=sem, vmem_limit_bytes=VMEM_LIMIT)


def _silu(x):
    return x * jax.nn.sigmoid(x)


def _modulate(x, g, shift, scale):
    ms = jnp.mean(x * x, axis=-1, keepdims=True)
    return (x * lax.rsqrt(ms + EPS) * g) * (1.0 + scale) + shift


def _mod_spec(mod, tm, rows_per_mod, width, col, ngrid):
    colf = col if callable(col) else (lambda *idx: col)
    if mod.ndim == 3:
        return pl.BlockSpec((None, 1, width), lambda *idx: ((idx[ngrid - 1] * tm) // rows_per_mod, 0, colf(*idx)))
    return pl.BlockSpec((tm, width), lambda *idx: (idx[ngrid - 1], colf(*idx)))


def _ada_kernel(c_ref, w_ref, b_ref, o_ref):
    s = _silu(c_ref[...]).astype(BF16)
    o_ref[...] = jnp.dot(s, w_ref[...].astype(BF16), preferred_element_type=F32) + b_ref[...]


def _ada(c_all, w_ada, b_ada, tn):
    rows, d = c_all.shape
    n = w_ada.shape[1]
    return pl.pallas_call(
        _ada_kernel,
        grid=(n // tn,),
        in_specs=[pl.BlockSpec((rows, d), lambda j: (0, 0)),
                  pl.BlockSpec((d, tn), lambda j: (0, j)),
                  pl.BlockSpec((1, tn), lambda j: (0, j))],
        out_specs=pl.BlockSpec((rows, tn), lambda j: (0, j)),
        out_shape=jax.ShapeDtypeStruct((rows, n), F32),
        compiler_params=_params("arbitrary"),
        name="ada",
    )(c_all, w_ada, b_ada.reshape(1, n))


def _rope_kernel(inv_ref, cs_ref, sn_ref, *, base, blk):
    r = pl.program_id(0)
    pos = (base + r * blk + lax.broadcasted_iota(jnp.int32, (blk, LANES), 0)).astype(F32)
    ang = pos * inv_ref[...]
    lane = lax.broadcasted_iota(jnp.int32, (blk, LANES), 1)
    first_half = (lane & (HEAD_DIM - 1)) < HEAD_DIM // 2
    sn = jnp.sin(ang)
    cs_ref[...] = jnp.cos(ang)
    sn_ref[...] = jnp.where(first_half, -sn, sn)


def _rope_tables(inv_row, base, rows, blk):
    out = jax.ShapeDtypeStruct((rows, LANES), F32)
    return pl.pallas_call(
        functools.partial(_rope_kernel, base=base, blk=blk),
        grid=(rows // blk,),
        in_specs=[pl.BlockSpec((1, LANES), lambda r: (0, 0))],
        out_specs=[pl.BlockSpec((blk, LANES), lambda r: (r, 0))] * 2,
        out_shape=[out, out],
        compiler_params=_params("arbitrary"),
        name="rope_tables",
    )(inv_row)


def _modulate_kernel(x_ref, shift_ref, scale_ref, g_ref, h_ref):
    group = 2 * SUBLANES
    per_row = shift_ref.shape[0] > 1

    def body(r, carry):
        rows = pl.ds(pl.multiple_of(r * group, group), group)
        shift = shift_ref[rows, :] if per_row else shift_ref[...]
        scale = scale_ref[rows, :] if per_row else scale_ref[...]
        h_ref[rows, :] = _modulate(x_ref[rows, :], g_ref[...], shift, scale).astype(BF16)
        return carry

    lax.fori_loop(0, x_ref.shape[0] // group, body, 0, unroll=8)


def _modulated_norm(x, mod, rows_per_mod, g, k_shift, tm):
    m_rows, d = x.shape
    return pl.pallas_call(
        _modulate_kernel,
        grid=(m_rows // tm,),
        in_specs=[pl.BlockSpec((tm, d), lambda m: (m, 0)),
                  _mod_spec(mod, tm, rows_per_mod, d, k_shift, 1),
                  _mod_spec(mod, tm, rows_per_mod, d, k_shift + 1, 1),
                  pl.BlockSpec((1, d), lambda m: (0, 0))],
        out_specs=pl.BlockSpec((tm, d), lambda m: (m, 0)),
        out_shape=jax.ShapeDtypeStruct((m_rows, d), BF16),
        compiler_params=_params("arbitrary"),
        name="modnorm",
    )(x, mod, mod, g.reshape(1, d))


HBM = pl.BlockSpec(memory_space=pl.ANY)


def _is_param(w):
    return w.dtype == F32


def _stream_weight_tiles(weights, tn, first):
    t, nt = pl.program_id(0), pl.num_programs(0)
    m, nm = pl.program_id(1), pl.num_programs(1)
    cols = lambda tile: pl.ds(pl.multiple_of(tile * tn, LANES), tn)
    fetch = lambda w_hbm, stage_ref, sem, tile: pltpu.make_async_copy(
        w_hbm.at[0, :, cols(first + tile)], stage_ref, sem.at[0])
    put = lambda wb_ref, wout_hbm, sem, tile: pltpu.make_async_copy(wb_ref, wout_hbm.at[:, cols(tile)], sem.at[1])

    @pl.when(m == 0)
    def _():
        for w_hbm, stage_ref, wb_ref, wout_hbm, sem in weights:
            @pl.when(t == 0)
            def _():
                fetch(w_hbm, stage_ref, sem, 0).start()

            fetch(w_hbm, stage_ref, sem, t).wait()

            @pl.when(t > 0)
            def _():
                put(wb_ref, wout_hbm, sem, t - 1).wait()

            wb_ref[...] = stage_ref[...].astype(BF16)
            put(wb_ref, wout_hbm, sem, t).start()

            @pl.when(t + 1 < nt)
            def _():
                fetch(w_hbm, stage_ref, sem, t + 1).start()

    @pl.when((t == nt - 1) & (m == nm - 1))
    def _():
        for w_hbm, stage_ref, wb_ref, wout_hbm, sem in weights:
            put(wb_ref, wout_hbm, sem, t).wait()


def _weight_scratch(rows, tn):
    return [pltpu.VMEM((rows, tn), F32), pltpu.VMEM((rows, tn), BF16), pltpu.SemaphoreType.DMA((2,))]


def _row_groups(tm):
    n = max(1, tm // 1024)
    return [slice(r * (tm // n), (r + 1) * (tm // n)) for r in range(n)]


def _ffn_up_rows(h_ref, w1b_ref, w3b_ref, g_ref):
    for rows in _row_groups(h_ref.shape[0]):
        h = h_ref[rows, :]
        a = jnp.dot(h, w1b_ref[...], preferred_element_type=F32)
        b = jnp.dot(h, w3b_ref[...], preferred_element_type=F32)
        g_ref[rows, :] = (_silu(a) * b).astype(BF16)


def _ffn_up_param_kernel(h_ref, w1_hbm, w3_hbm, g_ref, w1o_hbm, w3o_hbm,
                         s1_ref, w1b_ref, sem1, s3_ref, w3b_ref, sem3, *, tf):
    _stream_weight_tiles(((w1_hbm, s1_ref, w1b_ref, w1o_hbm, sem1), (w3_hbm, s3_ref, w3b_ref, w3o_hbm, sem3)),
                         tf, 0)
    _ffn_up_rows(h_ref, w1b_ref, w3b_ref, g_ref)


def _ffn_up(h, w1, w3, tm, tf):
    m_rows, d = h.shape
    dff = w1.shape[-1]
    grid = (dff // tf, m_rows // tm)
    h_spec = pl.BlockSpec((tm, d), lambda f, m: (m, 0))
    g_spec = pl.BlockSpec((tm, tf), lambda f, m: (m, f))
    g_shape = jax.ShapeDtypeStruct((m_rows, dff), BF16)
    if not _is_param(w1):
        w_spec = pl.BlockSpec((d, tf), lambda f, m: (0, f))
        g = pl.pallas_call(
            _ffn_up_rows, grid=grid, in_specs=[h_spec, w_spec, w_spec], out_specs=g_spec, out_shape=g_shape,
            compiler_params=_params("arbitrary", "arbitrary"), name="ffn_up_rows",
        )(h, w1, w3)
        return g, w1, w3
    w_shape = jax.ShapeDtypeStruct((d, dff), BF16)
    return pl.pallas_call(
        functools.partial(_ffn_up_param_kernel, tf=tf),
        grid=grid,
        in_specs=[h_spec, HBM, HBM],
        out_specs=[g_spec, HBM, HBM],
        out_shape=[g_shape, w_shape, w_shape],
        scratch_shapes=_weight_scratch(d, tf) + _weight_scratch(d, tf),
        compiler_params=_params("arbitrary", "arbitrary"),
        name="ffn_up",
    )(h, w1, w3)


def _ffn_down_rows(g_ref, w2b_ref, x_ref, gate_ref, o_ref):
    for rows in _row_groups(g_ref.shape[0]):
        acc = jnp.dot(g_ref[rows, :], w2b_ref[...], preferred_element_type=F32)
        gate = gate_ref[...] if gate_ref.shape[0] == 1 else gate_ref[rows, :]
        o_ref[rows, :] = x_ref[rows, :] + 0.5 * gate * acc


def _ffn_down_param_kernel(g_ref, w2_hbm, x_ref, gate_ref, o_ref, w2o_hbm, s2_ref, w2b_ref, sem2, *, tn):
    _stream_weight_tiles(((w2_hbm, s2_ref, w2b_ref, w2o_hbm, sem2),), tn, 0)
    _ffn_down_rows(g_ref, w2b_ref, x_ref, gate_ref, o_ref)


def _ffn_down(g, w2, x, mod, rows_per_mod, k_gate, tm, tn):
    m_rows, d = x.shape
    dff = g.shape[1]
    grid = (d // tn, m_rows // tm)
    gate_col = lambda n, m: k_gate * (d // tn) + n
    g_spec = pl.BlockSpec((tm, dff), lambda n, m: (m, 0))
    x_spec = pl.BlockSpec((tm, tn), lambda n, m: (m, n))
    gate_spec = _mod_spec(mod, tm, rows_per_mod, tn, gate_col, 2)
    o_shape = jax.ShapeDtypeStruct((m_rows, d), F32)
    if not _is_param(w2):
        out = pl.pallas_call(
            _ffn_down_rows, grid=grid,
            in_specs=[g_spec, pl.BlockSpec((dff, tn), lambda n, m: (0, n)), x_spec, gate_spec],
            out_specs=x_spec, out_shape=o_shape,
            compiler_params=_params("arbitrary", "arbitrary"), name="ffn_down_rows",
        )(g, w2, x, mod)
        return out, w2
    return pl.pallas_call(
        functools.partial(_ffn_down_param_kernel, tn=tn),
        grid=grid,
        in_specs=[g_spec, HBM, x_spec, gate_spec],
        out_specs=[x_spec, HBM],
        out_shape=[o_shape, jax.ShapeDtypeStruct((dff, d), BF16)],
        scratch_shapes=_weight_scratch(dff, tn),
        compiler_params=_params("arbitrary", "arbitrary"),
        name="ffn_down",
    )(g, w2, x, mod)


def _inproj_z_rows(h_ref, wb_ref, z_ref):
    for rows in _row_groups(h_ref.shape[0]):
        z_ref[rows, :] = jnp.dot(h_ref[rows, :], wb_ref[...], preferred_element_type=F32)


def _inproj_z_param_kernel(h_ref, w_hbm, z_ref, wo_hbm, s_ref, wb_ref, sem, *, tn, first):
    _stream_weight_tiles(((w_hbm, s_ref, wb_ref, wo_hbm, sem),), tn, first)
    _inproj_z_rows(h_ref, wb_ref, z_ref)


def _inproj_z(h, w, tm, tn):
    m_rows, d = h.shape
    grid = (D_Z // tn, m_rows // tm)
    h_spec = pl.BlockSpec((tm, d), lambda n, m: (m, 0))
    z_spec = pl.BlockSpec((tm, tn), lambda n, m: (m, n))
    z_shape = jax.ShapeDtypeStruct((m_rows, D_Z), F32)
    if not _is_param(w):
        z = pl.pallas_call(
            _inproj_z_rows, grid=grid, in_specs=[h_spec, pl.BlockSpec((d, tn), lambda n, m: (0, n))],
            out_specs=z_spec, out_shape=z_shape,
            compiler_params=_params("arbitrary", "arbitrary"), name="inproj_z_rows",
        )(h, w)
        return z, w
    return pl.pallas_call(
        functools.partial(_inproj_z_param_kernel, tn=tn, first=D_QKV // tn),
        grid=grid,
        in_specs=[h_spec, HBM],
        out_specs=[z_spec, HBM],
        out_shape=[z_shape, jax.ShapeDtypeStruct((d, D_Z), BF16)],
        scratch_shapes=_weight_scratch(d, tn),
        compiler_params=_params("arbitrary", "arbitrary"),
        name="inproj_z",
    )(h, w)


def _store_head_pairs(ref, rows, chunk, c):
    lo = lax.broadcasted_iota(jnp.int32, chunk.shape, 1) < HEAD_DIM
    swapped = pltpu.roll(chunk, HEAD_DIM, 1)
    zero = jnp.zeros_like(chunk)
    cols = (jnp.where(lo, chunk, zero), jnp.where(lo, zero, swapped),
            jnp.where(lo, swapped, zero), jnp.where(lo, zero, chunk))
    for i, col in enumerate(cols):
        ref[rows, (4 * c + i) * LANES:(4 * c + i + 1) * LANES] = col.astype(ref.dtype)


def _inproj_qkv_kernel(x_ref, shift_ref, scale_ref, g_ref, w_ref, cs_ref, sn_ref, gqk_ref, red_ref, exp_ref,
                       h_ref, q_ref, kx_ref, vx_ref, kf_ref, vf_ref, wb_ref, *, nsplit):
    @pl.when(pl.program_id(0) == 0)
    def _():
        wb_ref[...] = w_ref[...].astype(BF16)

    step = x_ref.shape[0] // nsplit
    for r in range(nsplit):
        rows = slice(r * step, (r + 1) * step)
        shift = shift_ref[...] if shift_ref.shape[0] == 1 else shift_ref[rows, :]
        scale = scale_ref[...] if scale_ref.shape[0] == 1 else scale_ref[rows, :]
        h = _modulate(x_ref[rows, :], g_ref[...], shift, scale).astype(BF16)
        h_ref[rows, :] = h
        acc = jnp.dot(h, wb_ref[...], preferred_element_type=F32)
        qk = acc[:, :D_QK]
        hm = jnp.dot((qk * qk).astype(BF16), red_ref[...], preferred_element_type=F32)
        hi = hm.astype(BF16)
        lo = (hm - hi.astype(F32)).astype(BF16)
        ms = jnp.dot(jnp.concatenate([hi, lo], axis=1), exp_ref[...], preferred_element_type=F32)
        y = qk * lax.rsqrt(ms + EPS) * gqk_ref[...]
        cs = cs_ref[rows, :]
        sn = sn_ref[rows, :]
        lane = lax.broadcasted_iota(jnp.int32, cs.shape, 1)
        first_half = (lane & (HEAD_DIM - 1)) < HEAD_DIM // 2
        for c in range(D_QK // LANES):
            yc = y[:, c * LANES:(c + 1) * LANES]
            partner = jnp.where(first_half,
                                pltpu.roll(yc, LANES - HEAD_DIM // 2, 1),
                                pltpu.roll(yc, HEAD_DIM // 2, 1))
            rot = yc * cs + partner * sn
            if c < D_ATTN // LANES:
                q_ref[rows, c * LANES:(c + 1) * LANES] = (rot * (HEAD_DIM ** -0.5)).astype(BF16)
            else:
                kf_ref[rows, c * LANES - D_ATTN:(c + 1) * LANES - D_ATTN] = rot
                _store_head_pairs(kx_ref, rows, rot, c - D_ATTN // LANES)
        v = acc[:, D_QK:]
        vf_ref[rows, :] = v
        for c in range(D_KV // LANES):
            _store_head_pairs(vx_ref, rows, v[:, c * LANES:(c + 1) * LANES], c)


def _inproj_qkv(x, mod, rows_per_mod, g, k_shift, w_in, cs, sn, gqk, head_reduce, head_expand, tm, table_blocks):
    m_rows, d = x.shape
    row = lambda width: pl.BlockSpec((tm, width), lambda m: (m, 0))
    table = pl.BlockSpec((tm, LANES), lambda m: (m % table_blocks, 0))
    nsplit = 2 if tm % (2 * MXU_DIM) == 0 else 1
    return pl.pallas_call(
        functools.partial(_inproj_qkv_kernel, nsplit=nsplit),
        grid=(m_rows // tm,),
        in_specs=[row(d),
                  _mod_spec(mod, tm, rows_per_mod, d, k_shift, 1),
                  _mod_spec(mod, tm, rows_per_mod, d, k_shift + 1, 1),
                  pl.BlockSpec((1, d), lambda m: (0, 0)),
                  pl.BlockSpec((None, d, D_QKV), lambda m: (0, 0, 0), pipeline_mode=pl.Buffered(1)),
                  table, table,
                  pl.BlockSpec((1, D_QK), lambda m: (0, 0)),
                  pl.BlockSpec(head_reduce.shape, lambda m: (0, 0)),
                  pl.BlockSpec(head_expand.shape, lambda m: (0, 0))],
        out_specs=[row(d), row(D_ATTN), row(DX), row(DX), row(D_KV), row(D_KV)],
        out_shape=[jax.ShapeDtypeStruct((m_rows, d), BF16),
                   jax.ShapeDtypeStruct((m_rows, D_ATTN), BF16),
                   jax.ShapeDtypeStruct((m_rows, DX), BF16),
                   jax.ShapeDtypeStruct((m_rows, DX), BF16),
                   jax.ShapeDtypeStruct((m_rows, D_KV), F32),
                   jax.ShapeDtypeStruct((m_rows, D_KV), F32)],
        scratch_shapes=[pltpu.VMEM((d, D_QKV), BF16)],
        compiler_params=_params("arbitrary"),
        name="inproj_qkv",
    )(x, mod, mod, g.reshape(1, d), w_in, cs, sn, gqk, head_reduce, head_expand)


def _attn_bias(bias_ref, tq, past_off):
    nk = 2 * WINDOW
    ri = lax.broadcasted_iota(jnp.int32, (2 * tq, 2 * nk), 0) & (tq - 1)
    kj = lax.broadcasted_iota(jnp.int32, (2 * tq, 2 * nk), 1) & (nk - 1)
    visible = ((kj < WINDOW) & (kj > ri + past_off)) | ((kj >= WINDOW) & (kj - WINDOW <= ri))
    bias_ref[...] = jnp.where(visible, 0.0, -jnp.inf)


def _attn_blocks(sink_ref, bias_of, q_ref, past_of, kc_ref, vc_ref, nsub, tq, store):
    nk = 2 * WINDOW
    rows = 2 * tq
    first_rows = lax.broadcasted_iota(jnp.int32, (rows, 1), 0) < tq
    lo_lanes = lax.broadcasted_iota(jnp.int32, (rows, LANES), 1) < HEAD_DIM
    ones_lo = (lax.broadcasted_iota(jnp.int32, (nk, LANES), 1) < HEAD_DIM).astype(BF16)
    ones_hi = (1 - ones_lo.astype(F32)).astype(BF16)
    nt = (((1,), (1,)), ((), ()))

    def keys(which, s, col):
        cols = slice(col * LANES, (col + 1) * LANES)
        past = past_of(s)
        parts = [past[which][past[2]:past[2] + WINDOW, cols], (kc_ref, vc_ref)[which][s * tq:(s + 1) * tq, cols]]
        if tq < WINDOW:
            parts.append(jnp.zeros((WINDOW - tq, LANES), BF16))
        return parts

    for s in range(nsub):
        for g in range(N_KV_HEADS):
            c0, c1 = 2 * g, 2 * g + 1
            q4 = jnp.concatenate([q_ref[s * tq:(s + 1) * tq, c0 * LANES:(c0 + 1) * LANES],
                                  q_ref[s * tq:(s + 1) * tq, c1 * LANES:(c1 + 1) * LANES]], axis=0)
            kk = jnp.concatenate(keys(0, s, c0) + keys(0, s, c1), axis=0)
            sc = lax.dot_general(q4, kk, nt, preferred_element_type=F32) + bias_of(s)[...]
            sk_lo = jnp.where(first_rows, sink_ref[4 * g], sink_ref[4 * g + 2])
            sk_hi = jnp.where(first_rows, sink_ref[4 * g + 1], sink_ref[4 * g + 3])
            m_lo = jnp.maximum(jnp.max(sc[:, :nk], axis=-1, keepdims=True), sk_lo)
            m_hi = jnp.maximum(jnp.max(sc[:, nk:], axis=-1, keepdims=True), sk_hi)
            p = jnp.concatenate([jnp.exp(sc[:, :nk] - m_lo), jnp.exp(sc[:, nk:] - m_hi)], axis=1).astype(BF16)
            vv = jnp.concatenate(
                [jnp.concatenate([jnp.concatenate(keys(1, s, c0), axis=0), ones_lo], axis=1),
                 jnp.concatenate([jnp.concatenate(keys(1, s, c1), axis=0), ones_hi], axis=1)],
                axis=0)
            o = jnp.dot(p, vv, preferred_element_type=F32)
            denom = o[:, LANES:] + jnp.where(lo_lanes, jnp.exp(sk_lo - m_lo), jnp.exp(sk_hi - m_hi))
            store(s, c0, c1, o[:, :LANES] / denom)


def _attn_kernel(sink_ref, q_ref, kp_ref, kc_ref, vp_ref, vc_ref, o_ref, bias0_ref, bias_ref, *, nq):
    tq = WINDOW
    _attn_bias(bias0_ref, tq, jnp.where(pl.program_id(1) > 0, 0, WINDOW))
    if nq > 1:
        _attn_bias(bias_ref, tq, 0)

    def store(s, c0, c1, out):
        o_ref[s * tq:(s + 1) * tq, c0 * LANES:(c0 + 1) * LANES] = out[:tq].astype(BF16)
        o_ref[s * tq:(s + 1) * tq, c1 * LANES:(c1 + 1) * LANES] = out[tq:].astype(BF16)

    past_of = lambda s: (kp_ref, vp_ref, 0) if s == 0 else (kc_ref, vc_ref, (s - 1) * WINDOW)
    bias_of = lambda s: bias0_ref if s == 0 else bias_ref
    _attn_blocks(sink_ref, bias_of, q_ref, past_of, kc_ref, vc_ref, nq, tq, store)


def _attn(sinks, q, kx, vx, nbatch, nblk, nq):
    steps = nblk // nq
    cur = lambda b, n: (b * steps + n, 0)
    past = lambda b, n: (b * nblk + jnp.maximum(n * nq - 1, 0), 0)
    kv_cur = pl.BlockSpec((nq * WINDOW, DX), cur)
    kv_past = pl.BlockSpec((WINDOW, DX), past)
    bias = pltpu.VMEM((2 * WINDOW, 4 * WINDOW), F32)
    return pl.pallas_call(
        functools.partial(_attn_kernel, nq=nq),
        grid=(nbatch, steps),
        in_specs=[pl.BlockSpec(memory_space=pltpu.SMEM), pl.BlockSpec((nq * WINDOW, D_ATTN), cur),
                  kv_past, kv_cur, kv_past, kv_cur],
        out_specs=pl.BlockSpec((nq * WINDOW, D_ATTN), cur),
        out_shape=jax.ShapeDtypeStruct(q.shape, BF16),
        scratch_shapes=[bias, bias],
        compiler_params=_params("arbitrary", "arbitrary"),
        name="attn",
    )(sinks, q, kx, kx, vx, vx)


def _rows_from_lanes(row, nrows):
    out_rows = max(nrows, SUBLANES)
    rep = jnp.broadcast_to(row, (out_rows, row.shape[1]))
    ridx = lax.broadcasted_iota(jnp.int32, (out_rows, HEAD_DIM), 0)
    out = jnp.zeros((out_rows, HEAD_DIM), row.dtype)
    for i in range(nrows):
        out = jnp.where(ridx == i, rep[:, i * HEAD_DIM:(i + 1) * HEAD_DIM], out)
    return out


def _attn_step_kernel(sink_ref, q_ref, kn_ref, vn_ref, ck_ref, cv_ref, o_ref, *, nsub):
    nkey = WINDOW * N_KV_HEADS
    ncol = nkey + LANES
    hrow = lax.broadcasted_iota(jnp.int32, (N_Q_HEADS, ncol), 0) >> 2
    col = lax.broadcasted_iota(jnp.int32, (N_Q_HEADS, ncol), 1)
    visible = (((col < nkey) & ((col & (N_KV_HEADS - 1)) == hrow) & (col >= N_KV_HEADS))
               | ((col >= nkey) & ((col - nkey) == hrow)))
    bias = jnp.where(visible, 0.0, -jnp.inf)
    hcol = lax.broadcasted_iota(jnp.int32, (N_Q_HEADS, 1), 0)
    sk = jnp.zeros((N_Q_HEADS, 1), F32)
    for h in range(N_Q_HEADS):
        sk = jnp.where(hcol == h, sink_ref[h], sk)
    nt = (((1,), (1,)), ((), ()))
    qf = q_ref[...].astype(F32)
    pad = jnp.zeros((LANES - SUBLANES, HEAD_DIM), F32)
    for s in range(nsub):
        qh = _rows_from_lanes(qf[s:s + 1, :], N_Q_HEADS).astype(BF16)
        keys = jnp.concatenate([ck_ref[s * nkey:(s + 1) * nkey, :],
                                _rows_from_lanes(kn_ref[s:s + 1, :], N_KV_HEADS), pad], axis=0).astype(BF16)
        vals = jnp.concatenate([cv_ref[s * nkey:(s + 1) * nkey, :],
                                _rows_from_lanes(vn_ref[s:s + 1, :], N_KV_HEADS), pad], axis=0).astype(BF16)
        sc = lax.dot_general(qh, keys, nt, preferred_element_type=F32) + bias
        m = jnp.maximum(jnp.max(sc, axis=-1, keepdims=True), sk)
        p = jnp.exp(sc - m)
        denom = jnp.sum(p, axis=-1, keepdims=True) + jnp.exp(sk - m)
        o = jnp.dot(p.astype(BF16), vals, preferred_element_type=F32) / denom
        for c in range(N_Q_HEADS // 2):
            o_ref[s:s + 1, c * LANES:(c + 1) * LANES] = jnp.concatenate(
                [o[2 * c:2 * c + 1, :], o[2 * c + 1:2 * c + 2, :]], axis=1)


def _attn_step(sinks, q, k_new, v_new, cache_k, cache_v, nsub):
    ns = q.shape[0]
    row = lambda width: pl.BlockSpec((nsub, width), lambda b: (b, 0))
    cache = pl.BlockSpec((nsub * WINDOW * N_KV_HEADS, HEAD_DIM), lambda b: (b, 0))
    return pl.pallas_call(
        functools.partial(_attn_step_kernel, nsub=nsub),
        grid=(ns // nsub,),
        in_specs=[pl.BlockSpec(memory_space=pltpu.SMEM), row(D_ATTN), row(D_KV), row(D_KV), cache, cache],
        out_specs=row(D_ATTN),
        out_shape=jax.ShapeDtypeStruct((ns, D_ATTN), F32),
        compiler_params=_params("arbitrary"),
        name="attn_step",
    )(sinks, q, k_new, v_new, cache_k, cache_v)


def _softplus(x):
    return jnp.maximum(x, 0.0) + jnp.log1p(jnp.exp(-jnp.abs(x)))


def _lru_coeffs(xc, wrg_ref, wig_ref, brg, big, lam):
    xb = xc.astype(BF16)
    ngroups = D_RNN // MXU_DIM
    rs, igs = [], []
    for c in range(ngroups):
        xg = xb[:, c * MXU_DIM:(c + 1) * MXU_DIM]
        rs.append(jnp.dot(xg, wrg_ref[c], preferred_element_type=F32))
        igs.append(jnp.dot(xg, wig_ref[c], preferred_element_type=F32))
    r = jax.nn.sigmoid(jnp.concatenate(rs, axis=1) + brg)
    ig = jax.nn.sigmoid(jnp.concatenate(igs, axis=1) + big)
    log_a = -LRU_C * r * _softplus(-lam)
    a = jnp.exp(log_a)
    one_minus_a2 = -jnp.tanh(log_a) * (1.0 + a * a)
    u = jnp.sqrt(one_minus_a2) * (ig * xc)
    return a, u


def _rnn_prompt_kernel(rx_ref, rg_ref, cw_ref, cb_ref, wrg_ref, wig_ref, brg_ref, big_ref, lam_ref,
                       o_ref, hlast_ref, conv_ref, xs_ref, a_ref, h_ref, carry_ref, *, tc):
    t = pl.program_id(1)
    pad = SUBLANES

    @pl.when(t == 0)
    def _():
        xs_ref[0:pad, :] = jnp.zeros((pad, D_RNN), F32)
        carry_ref[...] = jnp.zeros_like(carry_ref)

    x = rx_ref[...]
    xs_ref[pad:pad + tc, :] = x
    xc = cb_ref[...] + x * cw_ref[CONV_WIDTH - 1:CONV_WIDTH, :]
    for k in range(1, CONV_WIDTH):
        xc = xc + xs_ref[pad - k:pad - k + tc, :] * cw_ref[CONV_WIDTH - 1 - k:CONV_WIDTH - k, :]
    tail = xs_ref[tc:tc + pad, :]
    xs_ref[0:pad, :] = tail
    conv_ref[...] = tail[pad - (CONV_WIDTH - 1):, :]

    a, u = _lru_coeffs(xc, wrg_ref, wig_ref, brg_ref[...], big_ref[...], lam_ref[...])
    a_ref[...] = a
    h_ref[...] = u

    row = lax.broadcasted_iota(jnp.int32, (SUBLANES, D_RNN), 0)

    def body(r, carry):
        off = pl.multiple_of(r * SUBLANES, SUBLANES)
        av = a_ref[pl.ds(off, SUBLANES), :]
        hv = h_ref[pl.ds(off, SUBLANES), :]
        for sft in (1, 2, 4):
            keep = row >= sft
            a_sh = jnp.where(keep, pltpu.roll(av, sft, 0), 1.0)
            h_sh = jnp.where(keep, pltpu.roll(hv, sft, 0), 0.0)
            hv = av * h_sh + hv
            av = av * a_sh
        hv = hv + av * carry
        h_ref[pl.ds(off, SUBLANES), :] = hv
        return jnp.broadcast_to(hv[SUBLANES - 1:SUBLANES, :], (SUBLANES, D_RNN))

    carry = lax.fori_loop(0, tc // SUBLANES, body, carry_ref[...])
    carry_ref[...] = carry
    hlast_ref[...] = carry[0:1, :]
    o_ref[...] = (h_ref[...] * jax.nn.gelu(rg_ref[...])).astype(BF16)


def _rnn_prompt(z, nbatch, seq, conv_w, conv_b, wrg, wig, b_rg, b_ig, lam, tc):
    nchunk = seq // tc
    rx_blk = Z_RX // D_RNN
    rg_blk = Z_RG // D_RNN
    vec = pl.BlockSpec((1, D_RNN), lambda b, t: (0, 0))
    wspec = pl.BlockSpec(wrg.shape, lambda b, t: (0, 0, 0))
    return pl.pallas_call(
        functools.partial(_rnn_prompt_kernel, tc=tc),
        grid=(nbatch, nchunk),
        in_specs=[pl.BlockSpec((tc, D_RNN), lambda b, t: (b * nchunk + t, rx_blk)),
                  pl.BlockSpec((tc, D_RNN), lambda b, t: (b * nchunk + t, rg_blk)),
                  pl.BlockSpec((CONV_WIDTH, D_RNN), lambda b, t: (0, 0)),
                  vec, wspec, wspec, vec, vec, vec],
        out_specs=[pl.BlockSpec((tc, D_RNN), lambda b, t: (b * nchunk + t, 0)),
                   pl.BlockSpec((None, 1, D_RNN), lambda b, t: (b, 0, 0)),
                   pl.BlockSpec((None, CONV_WIDTH - 1, D_RNN), lambda b, t: (b, 0, 0))],
        out_shape=[jax.ShapeDtypeStruct((nbatch * seq, D_RNN), BF16),
                   jax.ShapeDtypeStruct((nbatch, 1, D_RNN), F32),
                   jax.ShapeDtypeStruct((nbatch, CONV_WIDTH - 1, D_RNN), F32)],
        scratch_shapes=[pltpu.VMEM((tc + SUBLANES, D_RNN), F32),
                        pltpu.VMEM((tc, D_RNN), F32),
                        pltpu.VMEM((tc, D_RNN), F32),
                        pltpu.VMEM((SUBLANES, D_RNN), F32)],
        compiler_params=_params("arbitrary", "arbitrary"),
        name="rnn_prompt",
    )(z, z, conv_w, conv_b.reshape(1, D_RNN), wrg, wig, b_rg.reshape(1, D_RNN),
      b_ig.reshape(1, D_RNN), lam.reshape(1, D_RNN))


def _rnn_step_kernel(rx_ref, rg_ref, c0_ref, c1_ref, c2_ref, h0_ref, cw_ref, cb_ref, wrg_ref, wig_ref,
                     brg_ref, big_ref, lam_ref, o_ref, h_ref):
    x = rx_ref[...]
    xc = (cb_ref[...] + c0_ref[...] * cw_ref[0:1, :] + c1_ref[...] * cw_ref[1:2, :]
          + c2_ref[...] * cw_ref[2:3, :] + x * cw_ref[3:4, :])
    a, u = _lru_coeffs(xc, wrg_ref, wig_ref, brg_ref[...], big_ref[...], lam_ref[...])
    h = a * h0_ref[...] + u
    h_ref[...] = h
    o_ref[...] = (h * jax.nn.gelu(rg_ref[...])).astype(BF16)


def _rnn_step(z, conv_state, h0, conv_w, conv_b, wrg, wig, b_rg, b_ig, lam):
    rows = z.shape[0]
    full = lambda shape: pl.BlockSpec(shape, lambda i: (0,) * len(shape))
    act = full((rows, D_RNN))
    vec = full((1, D_RNN))
    return pl.pallas_call(
        _rnn_step_kernel,
        grid=(1,),
        in_specs=[pl.BlockSpec((rows, D_RNN), lambda i: (0, Z_RX // D_RNN)),
                  pl.BlockSpec((rows, D_RNN), lambda i: (0, Z_RG // D_RNN)),
                  act, act, act, act, full((CONV_WIDTH, D_RNN)), vec,
                  full(wrg.shape), full(wig.shape), vec, vec, vec],
        out_specs=[act, act],
        out_shape=[jax.ShapeDtypeStruct((rows, D_RNN), BF16),
                   jax.ShapeDtypeStruct((rows, D_RNN), F32)],
        compiler_params=_params("arbitrary"),
        name="rnn_step",
    )(z, z, conv_state[:, 0], conv_state[:, 1], conv_state[:, 2], h0, conv_w,
      conv_b.reshape(1, D_RNN), wrg, wig, b_rg.reshape(1, D_RNN), b_ig.reshape(1, D_RNN),
      lam.reshape(1, D_RNN))


def _mix_kernel(oa_ref, or_ref, ga_ref, gr_ref, x_ref, gate_ref, shift_ref, scale_ref, g_ref,
                wpa_ref, wpr_ref, wout_ref, o_ref, h_ref):
    pa = jnp.dot(oa_ref[...].astype(BF16), wpa_ref[...], preferred_element_type=F32)
    pr = jnp.dot(or_ref[...], wpr_ref[...], preferred_element_type=F32)
    mix = jax.nn.sigmoid(ga_ref[...]) * pa + jax.nn.sigmoid(gr_ref[...]) * pr
    x = x_ref[...] + gate_ref[...] * jnp.dot(mix.astype(BF16), wout_ref[...], preferred_element_type=F32)
    o_ref[...] = x
    h_ref[...] = _modulate(x, g_ref[...], shift_ref[...], scale_ref[...]).astype(BF16)


def _mix(o_att, o_rnn, z, x, mod, rows_per_mod, g_next, w_pa, w_pr, w_out, k_gate, k_shift_next, tm):
    m_rows, d = x.shape
    const = lambda shape: pl.BlockSpec(shape, lambda m: (0, 0), pipeline_mode=pl.Buffered(1))
    row = lambda width, col: pl.BlockSpec((tm, width), lambda m: (m, col))
    mspec = lambda k: _mod_spec(mod, tm, rows_per_mod, d, k, 1)
    return pl.pallas_call(
        _mix_kernel,
        grid=(m_rows // tm,),
        in_specs=[row(D_ATTN, 0), row(D_RNN, 0), row(d, Z_GATT // d), row(d, Z_GRNN // d), row(d, 0),
                  mspec(k_gate), mspec(k_shift_next), mspec(k_shift_next + 1), const((1, d)),
                  const(w_pa.shape), const(w_pr.shape), const(w_out.shape)],
        out_specs=[row(d, 0), row(d, 0)],
        out_shape=[jax.ShapeDtypeStruct((m_rows, d), F32), jax.ShapeDtypeStruct((m_rows, d), BF16)],
        compiler_params=_params("arbitrary"),
        name="mix",
    )(o_att, o_rnn, z, z, x, mod, mod, mod, g_next.reshape(1, d), w_pa, w_pr, w_out)


def _block_diag(w, group):
    n, r, _ = w.shape
    eye = jnp.eye(group, dtype=w.dtype)
    wg = w.reshape(n // group, group, r, r)
    return jnp.einsum("ngij,gh->ngihj", wg, eye).reshape(n // group, group * r, group * r)


def _tiles(rows, dff):
    pick = lambda n, prefs: next((t for t in prefs if n % t == 0), n)
    return dict(
        tm_norm=pick(rows, (1024, 512)),
        tm_up=pick(rows, (2048, 1024, 512)), tf=pick(dff, (512,)),
        tm_down=pick(rows, (1024, 512)), tn_down=512,
        tm_z=pick(rows, (1024, 512)), tn_z=D_QKV,
        tm_qkv=pick(rows, (512,)),
        tm_mix=pick(rows, (256,)),
    )


def kernel(x_prompt, x_sample, c_prompt, c_sample, cache_k, cache_v, state_h, state_conv, w_ada, b_ada,
           g_norm_ffn1, g_norm_mix, g_norm_ffn2, ffn1_w1, ffn1_w3, ffn1_w2, ffn2_w1, ffn2_w3, ffn2_w2,
           w_in, g_q, g_k, sinks, conv_w, conv_b, w_rg, b_rg, w_ig, b_ig, lru_lambda, w_pa, w_pr, w_out):
    nb, seq, d = x_prompt.shape
    ns = x_sample.shape[0]
    dff = ffn1_w1.shape[2]
    assert d == D_MODEL and w_ada.shape[0] == 1 and x_sample.shape[1] == 1 and cache_k.shape[2] == WINDOW
    assert seq % WINDOW == 0 and w_in.shape[2] == D_QKV + D_Z

    group = MXU_DIM // RNN_BLOCK
    wrg = _block_diag(w_rg[0], group).astype(BF16)
    wig = _block_diag(w_ig[0], group).astype(BF16)
    wpa, wpr, wout = w_pa[0].astype(BF16), w_pr[0].astype(BF16), w_out[0].astype(BF16)
    gqk = jnp.concatenate([jnp.tile(g_q[0], N_Q_HEADS), jnp.tile(g_k[0], N_KV_HEADS)]).reshape(1, D_QK)
    head_of = jnp.arange(D_QK) // HEAD_DIM
    onehot = (head_of[:, None] == jnp.arange(LANES)[None, :]).astype(F32)
    head_reduce = (onehot / HEAD_DIM).astype(BF16)
    head_expand = jnp.concatenate([onehot.T, onehot.T], axis=0).astype(BF16)
    inv = ROPE_THETA ** (-jnp.arange(HEAD_DIM // 2, dtype=F32) * 2.0 / HEAD_DIM)
    inv_row = jnp.tile(inv, LANES // (HEAD_DIM // 2)).reshape(1, LANES)
    sink_vec = sinks[0]
    lam = lru_lambda[0]

    pad_rows = (-(nb + ns)) % SUBLANES
    c_all = jnp.concatenate([c_prompt, c_sample, jnp.zeros((pad_rows, d), F32)], axis=0)
    mod_all = _ada(c_all, w_ada[0], b_ada[0], 1024)
    mod_p = mod_all[:nb].reshape(nb, 1, N_MOD * d)
    mod_s = mod_all[nb:nb + ns]

    wts = dict(f1w1=ffn1_w1, f1w3=ffn1_w3, f1w2=ffn1_w2, f2w1=ffn2_w1, f2w3=ffn2_w3, f2w2=ffn2_w2, wz=w_in)

    def trunk(x, mod, rows_per_mod, cs, sn, table_blocks, t):
        h1 = _modulated_norm(x, mod, rows_per_mod, g_norm_ffn1[0], 0, t["tm_norm"])
        g1, wts["f1w1"], wts["f1w3"] = _ffn_up(h1, wts["f1w1"], wts["f1w3"], t["tm_up"], t["tf"])
        x1, wts["f1w2"] = _ffn_down(g1, wts["f1w2"], x, mod, rows_per_mod, 2, t["tm_down"], t["tn_down"])
        hm, q, kx, vx, kf, vf = _inproj_qkv(x1, mod, rows_per_mod, g_norm_mix[0], 3, w_in, cs, sn, gqk,
                                            head_reduce, head_expand, t["tm_qkv"], table_blocks)
        z, wts["wz"] = _inproj_z(hm, wts["wz"], t["tm_z"], t["tn_z"])
        return x1, z, q, kx, vx, kf, vf

    def tail(o_att, o_rnn, z, x1, mod, rows_per_mod, t):
        x2, h2 = _mix(o_att, o_rnn, z, x1, mod, rows_per_mod, g_norm_ffn2[0], wpa, wpr, wout, 5, 6, t["tm_mix"])
        g2, wts["f2w1"], wts["f2w3"] = _ffn_up(h2, wts["f2w1"], wts["f2w3"], t["tm_up"], t["tf"])
        y, wts["f2w2"] = _ffn_down(g2, wts["f2w2"], x2, mod, rows_per_mod, 8, t["tm_down"], t["tn_down"])
        return y

    tp = _tiles(nb * seq, dff)
    tp = {k: (min(v, seq) if k.startswith("tm") else v) for k, v in tp.items()}
    cs_p, sn_p = _rope_tables(inv_row, 0, seq, tp["tm_qkv"])
    x1, z, q, kx, vx, kf, vf = trunk(x_prompt.reshape(nb * seq, d), mod_p, seq, cs_p, sn_p,
                                     seq // tp["tm_qkv"], tp)
    nblk = seq // WINDOW
    o_att = _attn(sink_vec, q, kx, vx, nb, nblk, next(n for n in (4, 2, 1) if nblk % n == 0))
    o_rnn, h_p, conv_p = _rnn_prompt(z, nb, seq, conv_w[0], conv_b[0], wrg, wig, b_rg[0], b_ig[0], lam,
                                     min(512, seq))
    y_p = tail(o_att, o_rnn, z, x1, mod_p, seq, tp)

    last = lambda a: a.reshape(nb, seq, D_KV)[:, seq - WINDOW:].reshape(1, nb, WINDOW, N_KV_HEADS, HEAD_DIM)
    k_prompt, v_prompt = last(kf), last(vf)

    ts = _tiles(ns, dff)
    cs_1, sn_1 = _rope_tables(inv_row, PAST_LEN, SUBLANES, SUBLANES)
    cs_s = jnp.broadcast_to(cs_1[0:1], (ns, LANES))
    sn_s = jnp.broadcast_to(sn_1[0:1], (ns, LANES))
    x1s, zs, qs, kxs, vxs, kfs, vfs = trunk(x_sample.reshape(ns, d), mod_s, 1, cs_s, sn_s, 1, ts)
    nsub = 2 * SUBLANES
    assert ns % nsub == 0
    cache_rows = ns * WINDOW * N_KV_HEADS
    o_att_s = _attn_step(sink_vec, qs, kfs, vfs, cache_k.reshape(cache_rows, HEAD_DIM),
                         cache_v.reshape(cache_rows, HEAD_DIM), nsub)
    conv_s_in = state_conv[0]
    o_rnn_s, h_s = _rnn_step(zs, conv_s_in, state_h[0], conv_w[0], conv_b[0], wrg, wig, b_rg[0], b_ig[0], lam)
    y_s = tail(o_att_s, o_rnn_s, zs, x1s, mod_s, 1, ts)

    k_sample = kfs.reshape(1, ns, 1, N_KV_HEADS, HEAD_DIM)
    v_sample = vfs.reshape(1, ns, 1, N_KV_HEADS, HEAD_DIM)
    conv_sample = jnp.concatenate([conv_s_in[:, 1:], zs[:, None, Z_RX:Z_RX + D_RNN]], axis=1)[None]

    return (y_p.reshape(nb, seq, d), y_s.reshape(ns, 1, d), k_prompt, v_prompt, k_sample, v_sample,
            h_p.reshape(1, nb, D_RNN), h_s[None], conv_p[None], conv_sample)
```

```python
import functools

import jax
import jax.numpy as jnp
from jax import lax
from jax.experimental import pallas as pl
from jax.experimental.pallas import tpu as pltpu

F32 = jnp.float32
BF16 = jnp.bfloat16

D_MODEL = 2048
HEAD_DIM = 64
N_Q_HEADS = 16
N_KV_HEADS = 4
Q_PER_KV = N_Q_HEADS // N_KV_HEADS
D_ATTN = N_Q_HEADS * HEAD_DIM
D_KV = N_KV_HEADS * HEAD_DIM
D_QK = D_ATTN + D_KV
D_QKV = D_QK + D_KV
WINDOW = 128
ROPE_THETA = 10000.0
D_RNN = 1024
N_RNN_BLOCKS = 16
RNN_BLOCK = D_RNN // N_RNN_BLOCKS
CONV_WIDTH = 4
LRU_C = 8.0
N_MOD = 9
EPS = 1e-6
PAST_LEN = 16384

LANES = 128
SUBLANES = 8
MXU_DIM = 256
VMEM_LIMIT = 56 * 1024 * 1024

Z_RX = 0
Z_RG = Z_RX + D_RNN
Z_GATT = Z_RG + D_RNN
Z_GRNN = Z_GATT + D_MODEL
D_Z = Z_GRNN + D_MODEL
DX = 2 * N_KV_HEADS * LANES


def _params(*sem):
    return pltpu.CompilerParams(dimension_semantics=sem, vmem_limit_bytes=VMEM_LIMIT)


def _silu(x):
    return x * jax.nn.sigmoid(x)


def _modulate(x, g, shift, scale):
    ms = jnp.mean(x * x, axis=-1, keepdims=True)
    return (x * lax.rsqrt(ms + EPS) * g) * (1.0 + scale) + shift


def _mod_spec(mod, tm, rows_per_mod, width, col, ngrid):
    colf = col if callable(col) else (lambda *idx: col)
    if mod.ndim == 3:
        return pl.BlockSpec((None, 1, width), lambda *idx: ((idx[ngrid - 1] * tm) // rows_per_mod, 0, colf(*idx)))
    return pl.BlockSpec((tm, width), lambda *idx: (idx[ngrid - 1], colf(*idx)))


def _ada_kernel(c_ref, w_ref, b_ref, o_ref):
    s = _silu(c_ref[...]).astype(BF16)
    o_ref[...] = jnp.dot(s, w_ref[...].astype(BF16), preferred_element_type=F32) + b_ref[...]


def _ada(c_all, w_ada, b_ada, tn):
    rows, d = c_all.shape
    n = w_ada.shape[1]
    return pl.pallas_call(
        _ada_kernel,
        grid=(n // tn,),
        in_specs=[pl.BlockSpec((rows, d), lambda j: (0, 0)),
                  pl.BlockSpec((d, tn), lambda j: (0, j)),
                  pl.BlockSpec((1, tn), lambda j: (0, j))],
        out_specs=pl.BlockSpec((rows, tn), lambda j: (0, j)),
        out_shape=jax.ShapeDtypeStruct((rows, n), F32),
        compiler_params=_params("arbitrary"),
        name="ada",
    )(c_all, w_ada, b_ada.reshape(1, n))


def _rope_kernel(inv_ref, cs_ref, sn_ref, *, base, blk):
    r = pl.program_id(0)
    pos = (base + r * blk + lax.broadcasted_iota(jnp.int32, (blk, LANES), 0)).astype(F32)
    ang = pos * inv_ref[...]
    lane = lax.broadcasted_iota(jnp.int32, (blk, LANES), 1)
    first_half = (lane & (HEAD_DIM - 1)) < HEAD_DIM // 2
    sn = jnp.sin(ang)
    cs_ref[...] = jnp.cos(ang)
    sn_ref[...] = jnp.where(first_half, -sn, sn)


def _rope_tables(inv_row, base, rows, blk):
    out = jax.ShapeDtypeStruct((rows, LANES), F32)
    return pl.pallas_call(
        functools.partial(_rope_kernel, base=base, blk=blk),
        grid=(rows // blk,),
        in_specs=[pl.BlockSpec((1, LANES), lambda r: (0, 0))],
        out_specs=[pl.BlockSpec((blk, LANES), lambda r: (r, 0))] * 2,
        out_shape=[out, out],
        compiler_params=_params("arbitrary"),
        name="rope_tables",
    )(inv_row)


def _modulate_kernel(x_ref, shift_ref, scale_ref, g_ref, h_ref):
    group = 2 * SUBLANES
    per_row = shift_ref.shape[0] > 1

    def body(r, carry):
        rows = pl.ds(pl.multiple_of(r * group, group), group)
        shift = shift_ref[rows, :] if per_row else shift_ref[...]
        scale = scale_ref[rows, :] if per_row else scale_ref[...]
        h_ref[rows, :] = _modulate(x_ref[rows, :], g_ref[...], shift, scale).astype(BF16)
        return carry

    lax.fori_loop(0, x_ref.shape[0] // group, body, 0, unroll=8)


def _modulated_norm(x, mod, rows_per_mod, g, k_shift, tm):
    m_rows, d = x.shape
    return pl.pallas_call(
        _modulate_kernel,
        grid=(m_rows // tm,),
        in_specs=[pl.BlockSpec((tm, d), lambda m: (m, 0)),
                  _mod_spec(mod, tm, rows_per_mod, d, k_shift, 1),
                  _mod_spec(mod, tm, rows_per_mod, d, k_shift + 1, 1),
                  pl.BlockSpec((1, d), lambda m: (0, 0))],
        out_specs=pl.BlockSpec((tm, d), lambda m: (m, 0)),
        out_shape=jax.ShapeDtypeStruct((m_rows, d), BF16),
        compiler_params=_params("arbitrary"),
        name="modnorm",
    )(x, mod, mod, g.reshape(1, d))


HBM = pl.BlockSpec(memory_space=pl.ANY)


def _is_param(w):
    return w.dtype == F32


def _stream_weight_tiles(weights, tn, first):
    t, nt = pl.program_id(0), pl.num_programs(0)
    m, nm = pl.program_id(1), pl.num_programs(1)
    cols = lambda tile: pl.ds(pl.multiple_of(tile * tn, LANES), tn)
    fetch = lambda w_hbm, stage_ref, sem, tile: pltpu.make_async_copy(
        w_hbm.at[0, :, cols(first + tile)], stage_ref, sem.at[0])
    put = lambda wb_ref, wout_hbm, sem, tile: pltpu.make_async_copy(wb_ref, wout_hbm.at[:, cols(tile)], sem.at[1])

    @pl.when(m == 0)
    def _():
        for w_hbm, stage_ref, wb_ref, wout_hbm, sem in weights:
            @pl.when(t == 0)
            def _():
                fetch(w_hbm, stage_ref, sem, 0).start()

            fetch(w_hbm, stage_ref, sem, t).wait()

            @pl.when(t > 0)
            def _():
                put(wb_ref, wout_hbm, sem, t - 1).wait()

            wb_ref[...] = stage_ref[...].astype(BF16)
            put(wb_ref, wout_hbm, sem, t).start()

            @pl.when(t + 1 < nt)
            def _():
                fetch(w_hbm, stage_ref, sem, t + 1).start()

    @pl.when((t == nt - 1) & (m == nm - 1))
    def _():
        for w_hbm, stage_ref, wb_ref, wout_hbm, sem in weights:
            put(wb_ref, wout_hbm, sem, t).wait()


def _weight_scratch(rows, tn):
    return [pltpu.VMEM((rows, tn), F32), pltpu.VMEM((rows, tn), BF16), pltpu.SemaphoreType.DMA((2,))]


def _row_groups(tm):
    n = max(1, tm // 1024)
    return [slice(r * (tm // n), (r + 1) * (tm // n)) for r in range(n)]


def _ffn_up_rows(h_ref, w1b_ref, w3b_ref, g_ref):
    for rows in _row_groups(h_ref.shape[0]):
        h = h_ref[rows, :]
        a = jnp.dot(h, w1b_ref[...], preferred_element_type=F32)
        b = jnp.dot(h, w3b_ref[...], preferred_element_type=F32)
        g_ref[rows, :] = (_silu(a) * b).astype(BF16)


def _ffn_up_param_kernel(h_ref, w1_hbm, w3_hbm, g_ref, w1o_hbm, w3o_hbm,
                         s1_ref, w1b_ref, sem1, s3_ref, w3b_ref, sem3, *, tf):
    _stream_weight_tiles(((w1_hbm, s1_ref, w1b_ref, w1o_hbm, sem1), (w3_hbm, s3_ref, w3b_ref, w3o_hbm, sem3)),
                         tf, 0)
    _ffn_up_rows(h_ref, w1b_ref, w3b_ref, g_ref)


def _ffn_up(h, w1, w3, tm, tf):
    m_rows, d = h.shape
    dff = w1.shape[-1]
    grid = (dff // tf, m_rows // tm)
    h_spec = pl.BlockSpec((tm, d), lambda f, m: (m, 0))
    g_spec = pl.BlockSpec((tm, tf), lambda f, m: (m, f))
    g_shape = jax.ShapeDtypeStruct((m_rows, dff), BF16)
    if not _is_param(w1):
        w_spec = pl.BlockSpec((d, tf), lambda f, m: (0, f))
        g = pl.pallas_call(
            _ffn_up_rows, grid=grid, in_specs=[h_spec, w_spec, w_spec], out_specs=g_spec, out_shape=g_shape,
            compiler_params=_params("arbitrary", "arbitrary"), name="ffn_up_rows",
        )(h, w1, w3)
        return g, w1, w3
    w_shape = jax.ShapeDtypeStruct((d, dff), BF16)
    return pl.pallas_call(
        functools.partial(_ffn_up_param_kernel, tf=tf),
        grid=grid,
        in_specs=[h_spec, HBM, HBM],
        out_specs=[g_spec, HBM, HBM],
        out_shape=[g_shape, w_shape, w_shape],
        scratch_shapes=_weight_scratch(d, tf) + _weight_scratch(d, tf),
        compiler_params=_params("arbitrary", "arbitrary"),
        name="ffn_up",
    )(h, w1, w3)


def _ffn_down_rows(g_ref, w2b_ref, x_ref, gate_ref, o_ref):
    for rows in _row_groups(g_ref.shape[0]):
        acc = jnp.dot(g_ref[rows, :], w2b_ref[...], preferred_element_type=F32)
        gate = gate_ref[...] if gate_ref.shape[0] == 1 else gate_ref[rows, :]
        o_ref[rows, :] = x_ref[rows, :] + 0.5 * gate * acc


def _ffn_down_param_kernel(g_ref, w2_hbm, x_ref, gate_ref, o_ref, w2o_hbm, s2_ref, w2b_ref, sem2, *, tn):
    _stream_weight_tiles(((w2_hbm, s2_ref, w2b_ref, w2o_hbm, sem2),), tn, 0)
    _ffn_down_rows(g_ref, w2b_ref, x_ref, gate_ref, o_ref)


def _ffn_down(g, w2, x, mod, rows_per_mod, k_gate, tm, tn):
    m_rows, d = x.shape
    dff = g.shape[1]
    grid = (d // tn, m_rows // tm)
    gate_col = lambda n, m: k_gate * (d // tn) + n
    g_spec = pl.BlockSpec((tm, dff), lambda n, m: (m, 0))
    x_spec = pl.BlockSpec((tm, tn), lambda n, m: (m, n))
    gate_spec = _mod_spec(mod, tm, rows_per_mod, tn, gate_col, 2)
    o_shape = jax.ShapeDtypeStruct((m_rows, d), F32)
    if not _is_param(w2):
        out = pl.pallas_call(
            _ffn_down_rows, grid=grid,
            in_specs=[g_spec, pl.BlockSpec((dff, tn), lambda n, m: (0, n)), x_spec, gate_spec],
            out_specs=x_spec, out_shape=o_shape,
            compiler_params=_params("arbitrary", "arbitrary"), name="ffn_down_rows",
        )(g, w2, x, mod)
        return out, w2
    return pl.pallas_call(
        functools.partial(_ffn_down_param_kernel, tn=tn),
        grid=grid,
        in_specs=[g_spec, HBM, x_spec, gate_spec],
        out_specs=[x_spec, HBM],
        out_shape=[o_shape, jax.ShapeDtypeStruct((dff, d), BF16)],
        scratch_shapes=_weight_scratch(dff, tn),
        compiler_params=_params("arbitrary", "arbitrary"),
        name="ffn_down",
    )(g, w2, x, mod)


def _inproj_z_rows(h_ref, wb_ref, z_ref):
    for rows in _row_groups(h_ref.shape[0]):
        z_ref[rows, :] = jnp.dot(h_ref[rows, :], wb_ref[...], preferred_element_type=F32)


def _inproj_z_param_kernel(h_ref, w_hbm, z_ref, wo_hbm, s_ref, wb_ref, sem, *, tn, first):
    _stream_weight_tiles(((w_hbm, s_ref, wb_ref, wo_hbm, sem),), tn, first)
    _inproj_z_rows(h_ref, wb_ref, z_ref)


def _inproj_z(h, w, tm, tn):
    m_rows, d = h.shape
    grid = (D_Z // tn, m_rows // tm)
    h_spec = pl.BlockSpec((tm, d), lambda n, m: (m, 0))
    z_spec = pl.BlockSpec((tm, tn), lambda n, m: (m, n))
    z_shape = jax.ShapeDtypeStruct((m_rows, D_Z), F32)
    if not _is_param(w):
        z = pl.pallas_call(
            _inproj_z_rows, grid=grid, in_specs=[h_spec, pl.BlockSpec((d, tn), lambda n, m: (0, n))],
            out_specs=z_spec, out_shape=z_shape,
            compiler_params=_params("arbitrary", "arbitrary"), name="inproj_z_rows",
        )(h, w)
        return z, w
    return pl.pallas_call(
        functools.partial(_inproj_z_param_kernel, tn=tn, first=D_QKV // tn),
        grid=grid,
        in_specs=[h_spec, HBM],
        out_specs=[z_spec, HBM],
        out_shape=[z_shape, jax.ShapeDtypeStruct((d, D_Z), BF16)],
        scratch_shapes=_weight_scratch(d, tn),
        compiler_params=_params("arbitrary", "arbitrary"),
        name="inproj_z",
    )(h, w)


def _store_head_pairs(ref, rows, chunk, c):
    lo = lax.broadcasted_iota(jnp.int32, chunk.shape, 1) < HEAD_DIM
    swapped = pltpu.roll(chunk, HEAD_DIM, 1)
    zero = jnp.zeros_like(chunk)
    cols = (jnp.where(lo, chunk, zero), jnp.where(lo, zero, swapped),
            jnp.where(lo, swapped, zero), jnp.where(lo, zero, chunk))
    for i, col in enumerate(cols):
        ref[rows, (4 * c + i) * LANES:(4 * c + i + 1) * LANES] = col.astype(ref.dtype)


def _side_chunk_rows(side, steps):
    total = sum(w.shape[1] for w in side)
    rows = total // steps
    ok = rows * steps == total and rows % (2 * SUBLANES) == 0 and all(w.shape[1] % rows == 0 for w in side)
    return rows if ok else None


def _round_side_matrices(mats, stage_ref, ostage_ref, sem):
    m = pl.program_id(0)
    nrows = stage_ref.shape[1]
    chunks = [(w, wb, r0) for w, wb in mats for r0 in range(0, w.shape[1], nrows)]

    def fetch(c):
        w, _, r0 = chunks[c]
        return pltpu.make_async_copy(w.at[0, pl.ds(r0, nrows), :], stage_ref.at[c % 2], sem.at[c % 2])

    def put(c):
        _, wb, r0 = chunks[c]
        return pltpu.make_async_copy(ostage_ref.at[c % 2], wb.at[pl.ds(r0, nrows), :], sem.at[2 + c % 2])

    last = len(chunks) - 1
    for c in range(len(chunks)):
        @pl.when(m == c)
        def _():
            if c == 0:
                fetch(0).start()
            if c < last:
                fetch(c + 1).start()
            fetch(c).wait()
            if c >= 2:
                put(c - 2).wait()
            ostage_ref[c % 2] = stage_ref[c % 2].astype(BF16)
            put(c).start()
            if c == last:
                if c >= 1:
                    put(c - 1).wait()
                put(c).wait()


def _inproj_qkv_kernel(*refs, nsplit, nside):
    (x_ref, shift_ref, scale_ref, g_ref, w_ref, cs_ref, sn_ref, gqk_ref, red_ref, exp_ref) = refs[:10]
    side_in = refs[10:10 + nside]
    h_ref, q_ref, kx_ref, vx_ref, kf_ref, vf_ref = refs[10 + nside:16 + nside]
    side_out = refs[16 + nside:16 + 2 * nside]
    wb_ref = refs[16 + 2 * nside]
    if nside:
        _round_side_matrices(list(zip(side_in, side_out)), *refs[17 + 2 * nside:])

    @pl.when(pl.program_id(0) == 0)
    def _():
        wb_ref[...] = w_ref[...].astype(BF16)

    step = x_ref.shape[0] // nsplit
    for r in range(nsplit):
        rows = slice(r * step, (r + 1) * step)
        shift = shift_ref[...] if shift_ref.shape[0] == 1 else shift_ref[rows, :]
        scale = scale_ref[...] if scale_ref.shape[0] == 1 else scale_ref[rows, :]
        h = _modulate(x_ref[rows, :], g_ref[...], shift, scale).astype(BF16)
        h_ref[rows, :] = h
        acc = jnp.dot(h, wb_ref[...], preferred_element_type=F32)
        qk = acc[:, :D_QK]
        hm = jnp.dot((qk * qk).astype(BF16), red_ref[...], preferred_element_type=F32)
        hi = hm.astype(BF16)
        lo = (hm - hi.astype(F32)).astype(BF16)
        ms = jnp.dot(jnp.concatenate([hi, lo], axis=1), exp_ref[...], preferred_element_type=F32)
        y = qk * lax.rsqrt(ms + EPS) * gqk_ref[...]
        cs = cs_ref[rows, :]
        sn = sn_ref[rows, :]
        lane = lax.broadcasted_iota(jnp.int32, cs.shape, 1)
        first_half = (lane & (HEAD_DIM - 1)) < HEAD_DIM // 2
        for c in range(D_QK // LANES):
            yc = y[:, c * LANES:(c + 1) * LANES]
            partner = jnp.where(first_half,
                                pltpu.roll(yc, LANES - HEAD_DIM // 2, 1),
                                pltpu.roll(yc, HEAD_DIM // 2, 1))
            rot = yc * cs + partner * sn
            if c < D_ATTN // LANES:
                q_ref[rows, c * LANES:(c + 1) * LANES] = (rot * (HEAD_DIM ** -0.5)).astype(BF16)
            else:
                kf_ref[rows, c * LANES - D_ATTN:(c + 1) * LANES - D_ATTN] = rot
                _store_head_pairs(kx_ref, rows, rot, c - D_ATTN // LANES)
        v = acc[:, D_QK:]
        vf_ref[rows, :] = v
        for c in range(D_KV // LANES):
            _store_head_pairs(vx_ref, rows, v[:, c * LANES:(c + 1) * LANES], c)


def _inproj_qkv(x, mod, rows_per_mod, g, k_shift, w_in, cs, sn, gqk, head_reduce, head_expand, tm, table_blocks,
                side=()):
    m_rows, d = x.shape
    steps = m_rows // tm
    row = lambda width: pl.BlockSpec((tm, width), lambda m: (m, 0))
    table = pl.BlockSpec((tm, LANES), lambda m: (m % table_blocks, 0))
    nsplit = 2 if tm % (2 * MXU_DIM) == 0 else 1
    side_scratch = []
    if side:
        width = side[0].shape[2]
        chunk = _side_chunk_rows(side, steps)
        assert chunk is not None and all(w.shape[2] == width for w in side)
        side_scratch = [pltpu.VMEM((2, chunk, width), F32), pltpu.VMEM((2, chunk, width), BF16),
                        pltpu.SemaphoreType.DMA((4,))]
    return pl.pallas_call(
        functools.partial(_inproj_qkv_kernel, nsplit=nsplit, nside=len(side)),
        grid=(steps,),
        in_specs=[row(d),
                  _mod_spec(mod, tm, rows_per_mod, d, k_shift, 1),
                  _mod_spec(mod, tm, rows_per_mod, d, k_shift + 1, 1),
                  pl.BlockSpec((1, d), lambda m: (0, 0)),
                  pl.BlockSpec((None, d, D_QKV), lambda m: (0, 0, 0), pipeline_mode=pl.Buffered(1)),
                  table, table,
                  pl.BlockSpec((1, D_QK), lambda m: (0, 0)),
                  pl.BlockSpec(head_reduce.shape, lambda m: (0, 0)),
                  pl.BlockSpec(head_expand.shape, lambda m: (0, 0))] + [HBM] * len(side),
        out_specs=[row(d), row(D_ATTN), row(DX), row(DX), row(D_KV), row(D_KV)] + [HBM] * len(side),
        out_shape=[jax.ShapeDtypeStruct((m_rows, d), BF16),
                   jax.ShapeDtypeStruct((m_rows, D_ATTN), BF16),
                   jax.ShapeDtypeStruct((m_rows, DX), BF16),
                   jax.ShapeDtypeStruct((m_rows, DX), BF16),
                   jax.ShapeDtypeStruct((m_rows, D_KV), F32),
                   jax.ShapeDtypeStruct((m_rows, D_KV), F32)]
        + [jax.ShapeDtypeStruct(w.shape[1:], BF16) for w in side],
        scratch_shapes=[pltpu.VMEM((d, D_QKV), BF16)] + side_scratch,
        compiler_params=_params("arbitrary"),
        name="inproj_qkv",
    )(x, mod, mod, g.reshape(1, d), w_in, cs, sn, gqk, head_reduce, head_expand, *side)


def _attn_bias(bias_ref, tq, past_off):
    nk = 2 * WINDOW
    ri = lax.broadcasted_iota(jnp.int32, (2 * tq, 2 * nk), 0) & (tq - 1)
    kj = lax.broadcasted_iota(jnp.int32, (2 * tq, 2 * nk), 1) & (nk - 1)
    visible = ((kj < WINDOW) & (kj > ri + past_off)) | ((kj >= WINDOW) & (kj - WINDOW <= ri))
    bias_ref[...] = jnp.where(visible, 0.0, -jnp.inf)


def _attn_blocks(sink_ref, bias_of, q_ref, past_of, kc_ref, vc_ref, nsub, tq, store):
    nk = 2 * WINDOW
    rows = 2 * tq
    first_rows = lax.broadcasted_iota(jnp.int32, (rows, 1), 0) < tq
    lo_lanes = lax.broadcasted_iota(jnp.int32, (rows, LANES), 1) < HEAD_DIM
    ones_lo = (lax.broadcasted_iota(jnp.int32, (nk, LANES), 1) < HEAD_DIM).astype(BF16)
    ones_hi = (1 - ones_lo.astype(F32)).astype(BF16)
    nt = (((1,), (1,)), ((), ()))

    def keys(which, s, col):
        cols = slice(col * LANES, (col + 1) * LANES)
        past = past_of(s)
        parts = [past[which][past[2]:past[2] + WINDOW, cols], (kc_ref, vc_ref)[which][s * tq:(s + 1) * tq, cols]]
        if tq < WINDOW:
            parts.append(jnp.zeros((WINDOW - tq, LANES), BF16))
        return parts

    for s in range(nsub):
        for g in range(N_KV_HEADS):
            c0, c1 = 2 * g, 2 * g + 1
            q4 = jnp.concatenate([q_ref[s * tq:(s + 1) * tq, c0 * LANES:(c0 + 1) * LANES],
                                  q_ref[s * tq:(s + 1) * tq, c1 * LANES:(c1 + 1) * LANES]], axis=0)
            kk = jnp.concatenate(keys(0, s, c0) + keys(0, s, c1), axis=0)
            sc = lax.dot_general(q4, kk, nt, preferred_element_type=F32) + bias_of(s)[...]
            sk_lo = jnp.where(first_rows, sink_ref[4 * g], sink_ref[4 * g + 2])
            sk_hi = jnp.where(first_rows, sink_ref[4 * g + 1], sink_ref[4 * g + 3])
            m_lo = jnp.maximum(jnp.max(sc[:, :nk], axis=-1, keepdims=True), sk_lo)
            m_hi = jnp.maximum(jnp.max(sc[:, nk:], axis=-1, keepdims=True), sk_hi)
            p = jnp.concatenate([jnp.exp(sc[:, :nk] - m_lo), jnp.exp(sc[:, nk:] - m_hi)], axis=1).astype(BF16)
            vv = jnp.concatenate(
                [jnp.concatenate([jnp.concatenate(keys(1, s, c0), axis=0), ones_lo], axis=1),
                 jnp.concatenate([jnp.concatenate(keys(1, s, c1), axis=0), ones_hi], axis=1)],
                axis=0)
            o = jnp.dot(p, vv, preferred_element_type=F32)
            denom = o[:, LANES:] + jnp.where(lo_lanes, jnp.exp(sk_lo - m_lo), jnp.exp(sk_hi - m_hi))
            store(s, c0, c1, o[:, :LANES] / denom)


def _attn_kernel(sink_ref, q_ref, kp_ref, kc_ref, vp_ref, vc_ref, o_ref, bias0_ref, bias_ref, *, nq):
    tq = WINDOW
    _attn_bias(bias0_ref, tq, jnp.where(pl.program_id(1) > 0, 0, WINDOW))
    if nq > 1:
        _attn_bias(bias_ref, tq, 0)

    def store(s, c0, c1, out):
        o_ref[s * tq:(s + 1) * tq, c0 * LANES:(c0 + 1) * LANES] = out[:tq].astype(BF16)
        o_ref[s * tq:(s + 1) * tq, c1 * LANES:(c1 + 1) * LANES] = out[tq:].astype(BF16)

    past_of = lambda s: (kp_ref, vp_ref, 0) if s == 0 else (kc_ref, vc_ref, (s - 1) * WINDOW)
    bias_of = lambda s: bias0_ref if s == 0 else bias_ref
    _attn_blocks(sink_ref, bias_of, q_ref, past_of, kc_ref, vc_ref, nq, tq, store)


def _attn(sinks, q, kx, vx, nbatch, nblk, nq):
    steps = nblk // nq
    cur = lambda b, n: (b * steps + n, 0)
    past = lambda b, n: (b * nblk + jnp.maximum(n * nq - 1, 0), 0)
    kv_cur = pl.BlockSpec((nq * WINDOW, DX), cur)
    kv_past = pl.BlockSpec((WINDOW, DX), past)
    bias = pltpu.VMEM((2 * WINDOW, 4 * WINDOW), F32)
    return pl.pallas_call(
        functools.partial(_attn_kernel, nq=nq),
        grid=(nbatch, steps),
        in_specs=[pl.BlockSpec(memory_space=pltpu.SMEM), pl.BlockSpec((nq * WINDOW, D_ATTN), cur),
                  kv_past, kv_cur, kv_past, kv_cur],
        out_specs=pl.BlockSpec((nq * WINDOW, D_ATTN), cur),
        out_shape=jax.ShapeDtypeStruct(q.shape, BF16),
        scratch_shapes=[bias, bias],
        compiler_params=_params("arbitrary", "arbitrary"),
        name="attn",
    )(sinks, q, kx, kx, vx, vx)


def _rows_from_lanes(row, nrows):
    out_rows = max(nrows, SUBLANES)
    rep = jnp.broadcast_to(row, (out_rows, row.shape[1]))
    ridx = lax.broadcasted_iota(jnp.int32, (out_rows, HEAD_DIM), 0)
    out = jnp.zeros((out_rows, HEAD_DIM), row.dtype)
    for i in range(nrows):
        out = jnp.where(ridx == i, rep[:, i * HEAD_DIM:(i + 1) * HEAD_DIM], out)
    return out


def _attn_step_kernel(sink_ref, q_ref, kn_ref, vn_ref, ck_ref, cv_ref, o_ref, *, nsub):
    grp = lax.broadcasted_iota(jnp.int32, (N_Q_HEADS, HEAD_DIM), 0) >> 2
    bias = jnp.where(lax.broadcasted_iota(jnp.int32, (N_Q_HEADS, WINDOW), 1) >= 1, 0.0, -jnp.inf)
    hcol = lax.broadcasted_iota(jnp.int32, (N_Q_HEADS, 1), 0)
    sk = jnp.zeros((N_Q_HEADS, 1), F32)
    for h in range(N_Q_HEADS):
        sk = jnp.where(hcol == h, sink_ref[h], sk)
    nt = (((1,), (1,)), ((), ()))
    qf = q_ref[...].astype(F32)

    def own_group(x):
        out = jnp.zeros((N_Q_HEADS, HEAD_DIM), F32)
        for g in range(N_KV_HEADS):
            out = jnp.where(grp == g, x[:, g * HEAD_DIM:(g + 1) * HEAD_DIM], out)
        return out

    for s in range(nsub):
        qh = _rows_from_lanes(qf[s:s + 1, :], N_Q_HEADS)
        zero = jnp.zeros_like(qh)
        q_bd = jnp.concatenate([jnp.where(grp == g, qh, zero) for g in range(N_KV_HEADS)], axis=1)
        kt = ck_ref[s * D_KV:(s + 1) * D_KV, :].astype(BF16)
        vt = cv_ref[s * D_KV:(s + 1) * D_KV, :].astype(BF16)
        sc = jnp.dot(q_bd.astype(BF16), kt, preferred_element_type=F32) + bias
        s_new = jnp.sum(q_bd * kn_ref[s:s + 1, :], axis=-1, keepdims=True)
        m = jnp.maximum(jnp.maximum(jnp.max(sc, axis=-1, keepdims=True), s_new), sk)
        p = jnp.exp(sc - m)
        p_new = jnp.exp(s_new - m)
        denom = jnp.sum(p, axis=-1, keepdims=True) + p_new + jnp.exp(sk - m)
        o_all = lax.dot_general(p.astype(BF16), vt, nt, preferred_element_type=F32)
        v_new = own_group(jnp.broadcast_to(vn_ref[s:s + 1, :], (N_Q_HEADS, D_KV)))
        o = (own_group(o_all) + p_new * v_new) / denom
        for c in range(N_Q_HEADS // 2):
            o_ref[s:s + 1, c * LANES:(c + 1) * LANES] = jnp.concatenate(
                [o[2 * c:2 * c + 1, :], o[2 * c + 1:2 * c + 2, :]], axis=1)


def _attn_step(sinks, q, k_new, v_new, cache_k, cache_v, nsub):
    ns = q.shape[0]
    row = lambda width: pl.BlockSpec((nsub, width), lambda b: (b, 0))
    cache = pl.BlockSpec((nsub * D_KV, WINDOW), lambda b: (b, 0))
    return pl.pallas_call(
        functools.partial(_attn_step_kernel, nsub=nsub),
        grid=(ns // nsub,),
        in_specs=[pl.BlockSpec(memory_space=pltpu.SMEM), row(D_ATTN), row(D_KV), row(D_KV), cache, cache],
        out_specs=row(D_ATTN),
        out_shape=jax.ShapeDtypeStruct((ns, D_ATTN), F32),
        compiler_params=_params("arbitrary"),
        name="attn_step",
    )(sinks, q, k_new, v_new, cache_k, cache_v)


def _softplus(x):
    return jnp.maximum(x, 0.0) + jnp.log1p(jnp.exp(-jnp.abs(x)))


def _lru_coeffs(xc, wrg_ref, wig_ref, brg, big, lam):
    xb = xc.astype(BF16)
    ngroups = D_RNN // MXU_DIM
    rs, igs = [], []
    for c in range(ngroups):
        xg = xb[:, c * MXU_DIM:(c + 1) * MXU_DIM]
        rs.append(jnp.dot(xg, wrg_ref[c], preferred_element_type=F32))
        igs.append(jnp.dot(xg, wig_ref[c], preferred_element_type=F32))
    r = jax.nn.sigmoid(jnp.concatenate(rs, axis=1) + brg)
    ig = jax.nn.sigmoid(jnp.concatenate(igs, axis=1) + big)
    log_a = -LRU_C * r * _softplus(-lam)
    a = jnp.exp(log_a)
    one_minus_a2 = -jnp.tanh(log_a) * (1.0 + a * a)
    u = jnp.sqrt(one_minus_a2) * (ig * xc)
    return a, u


def _rnn_prompt_kernel(rx_ref, rg_ref, cw_ref, cb_ref, wrg_ref, wig_ref, brg_ref, big_ref, lam_ref,
                       o_ref, hlast_ref, conv_ref, xs_ref, a_ref, h_ref, carry_ref, *, tc):
    t = pl.program_id(1)
    pad = SUBLANES

    @pl.when(t == 0)
    def _():
        xs_ref[0:pad, :] = jnp.zeros((pad, D_RNN), F32)
        carry_ref[...] = jnp.zeros_like(carry_ref)

    x = rx_ref[...]
    xs_ref[pad:pad + tc, :] = x
    xc = cb_ref[...] + x * cw_ref[CONV_WIDTH - 1:CONV_WIDTH, :]
    for k in range(1, CONV_WIDTH):
        xc = xc + xs_ref[pad - k:pad - k + tc, :] * cw_ref[CONV_WIDTH - 1 - k:CONV_WIDTH - k, :]
    tail = xs_ref[tc:tc + pad, :]
    xs_ref[0:pad, :] = tail
    conv_ref[...] = tail[pad - (CONV_WIDTH - 1):, :]

    a, u = _lru_coeffs(xc, wrg_ref, wig_ref, brg_ref[...], big_ref[...], lam_ref[...])
    a_ref[...] = a
    h_ref[...] = u

    row = lax.broadcasted_iota(jnp.int32, (SUBLANES, D_RNN), 0)

    def body(r, carry):
        off = pl.multiple_of(r * SUBLANES, SUBLANES)
        av = a_ref[pl.ds(off, SUBLANES), :]
        hv = h_ref[pl.ds(off, SUBLANES), :]
        for sft in (1, 2, 4):
            keep = row >= sft
            a_sh = jnp.where(keep, pltpu.roll(av, sft, 0), 1.0)
            h_sh = jnp.where(keep, pltpu.roll(hv, sft, 0), 0.0)
            hv = av * h_sh + hv
            av = av * a_sh
        hv = hv + av * carry
        h_ref[pl.ds(off, SUBLANES), :] = hv
        return jnp.broadcast_to(hv[SUBLANES - 1:SUBLANES, :], (SUBLANES, D_RNN))

    carry = lax.fori_loop(0, tc // SUBLANES, body, carry_ref[...])
    carry_ref[...] = carry
    hlast_ref[...] = carry[0:1, :]
    o_ref[...] = (h_ref[...] * jax.nn.gelu(rg_ref[...])).astype(BF16)


def _rnn_prompt(z, nbatch, seq, conv_w, conv_b, wrg, wig, b_rg, b_ig, lam, tc):
    nchunk = seq // tc
    rx_blk = Z_RX // D_RNN
    rg_blk = Z_RG // D_RNN
    vec = pl.BlockSpec((1, D_RNN), lambda b, t: (0, 0))
    wspec = pl.BlockSpec(wrg.shape, lambda b, t: (0, 0, 0))
    return pl.pallas_call(
        functools.partial(_rnn_prompt_kernel, tc=tc),
        grid=(nbatch, nchunk),
        in_specs=[pl.BlockSpec((tc, D_RNN), lambda b, t: (b * nchunk + t, rx_blk)),
                  pl.BlockSpec((tc, D_RNN), lambda b, t: (b * nchunk + t, rg_blk)),
                  pl.BlockSpec((CONV_WIDTH, D_RNN), lambda b, t: (0, 0)),
                  vec, wspec, wspec, vec, vec, vec],
        out_specs=[pl.BlockSpec((tc, D_RNN), lambda b, t: (b * nchunk + t, 0)),
                   pl.BlockSpec((None, 1, D_RNN), lambda b, t: (b, 0, 0)),
                   pl.BlockSpec((None, CONV_WIDTH - 1, D_RNN), lambda b, t: (b, 0, 0))],
        out_shape=[jax.ShapeDtypeStruct((nbatch * seq, D_RNN), BF16),
                   jax.ShapeDtypeStruct((nbatch, 1, D_RNN), F32),
                   jax.ShapeDtypeStruct((nbatch, CONV_WIDTH - 1, D_RNN), F32)],
        scratch_shapes=[pltpu.VMEM((tc + SUBLANES, D_RNN), F32),
                        pltpu.VMEM((tc, D_RNN), F32),
                        pltpu.VMEM((tc, D_RNN), F32),
                        pltpu.VMEM((SUBLANES, D_RNN), F32)],
        compiler_params=_params("arbitrary", "arbitrary"),
        name="rnn_prompt",
    )(z, z, conv_w, conv_b.reshape(1, D_RNN), wrg, wig, b_rg.reshape(1, D_RNN),
      b_ig.reshape(1, D_RNN), lam.reshape(1, D_RNN))


def _rnn_step_kernel(rx_ref, rg_ref, c0_ref, c1_ref, c2_ref, h0_ref, cw_ref, cb_ref, wrg_ref, wig_ref,
                     brg_ref, big_ref, lam_ref, o_ref, h_ref):
    x = rx_ref[...]
    xc = (cb_ref[...] + c0_ref[...] * cw_ref[0:1, :] + c1_ref[...] * cw_ref[1:2, :]
          + c2_ref[...] * cw_ref[2:3, :] + x * cw_ref[3:4, :])
    a, u = _lru_coeffs(xc, wrg_ref, wig_ref, brg_ref[...], big_ref[...], lam_ref[...])
    h = a * h0_ref[...] + u
    h_ref[...] = h
    o_ref[...] = (h * jax.nn.gelu(rg_ref[...])).astype(BF16)


def _rnn_step(z, conv_state, h0, conv_w, conv_b, wrg, wig, b_rg, b_ig, lam):
    rows = z.shape[0]
    full = lambda shape: pl.BlockSpec(shape, lambda i: (0,) * len(shape))
    act = full((rows, D_RNN))
    vec = full((1, D_RNN))
    return pl.pallas_call(
        _rnn_step_kernel,
        grid=(1,),
        in_specs=[pl.BlockSpec((rows, D_RNN), lambda i: (0, Z_RX // D_RNN)),
                  pl.BlockSpec((rows, D_RNN), lambda i: (0, Z_RG // D_RNN)),
                  act, act, act, act, full((CONV_WIDTH, D_RNN)), vec,
                  full(wrg.shape), full(wig.shape), vec, vec, vec],
        out_specs=[act, act],
        out_shape=[jax.ShapeDtypeStruct((rows, D_RNN), BF16),
                   jax.ShapeDtypeStruct((rows, D_RNN), F32)],
        compiler_params=_params("arbitrary"),
        name="rnn_step",
    )(z, z, conv_state[:, 0], conv_state[:, 1], conv_state[:, 2], h0, conv_w,
      conv_b.reshape(1, D_RNN), wrg, wig, b_rg.reshape(1, D_RNN), b_ig.reshape(1, D_RNN),
      lam.reshape(1, D_RNN))


def _mix_kernel(oa_ref, or_ref, ga_ref, gr_ref, x_ref, gate_ref, shift_ref, scale_ref, g_ref,
                wpa_ref, wpr_ref, wout_ref, o_ref, h_ref):
    pa = jnp.dot(oa_ref[...].astype(BF16), wpa_ref[...], preferred_element_type=F32)
    pr = jnp.dot(or_ref[...], wpr_ref[...], preferred_element_type=F32)
    mix = jax.nn.sigmoid(ga_ref[...]) * pa + jax.nn.sigmoid(gr_ref[...]) * pr
    x = x_ref[...] + gate_ref[...] * jnp.dot(mix.astype(BF16), wout_ref[...], preferred_element_type=F32)
    o_ref[...] = x
    h_ref[...] = _modulate(x, g_ref[...], shift_ref[...], scale_ref[...]).astype(BF16)


def _mix(o_att, o_rnn, z, x, mod, rows_per_mod, g_next, w_pa, w_pr, w_out, k_gate, k_shift_next, tm):
    m_rows, d = x.shape
    const = lambda shape: pl.BlockSpec(shape, lambda m: (0, 0), pipeline_mode=pl.Buffered(1))
    row = lambda width, col: pl.BlockSpec((tm, width), lambda m: (m, col))
    mspec = lambda k: _mod_spec(mod, tm, rows_per_mod, d, k, 1)
    return pl.pallas_call(
        _mix_kernel,
        grid=(m_rows // tm,),
        in_specs=[row(D_ATTN, 0), row(D_RNN, 0), row(d, Z_GATT // d), row(d, Z_GRNN // d), row(d, 0),
                  mspec(k_gate), mspec(k_shift_next), mspec(k_shift_next + 1), const((1, d)),
                  const(w_pa.shape), const(w_pr.shape), const(w_out.shape)],
        out_specs=[row(d, 0), row(d, 0)],
        out_shape=[jax.ShapeDtypeStruct((m_rows, d), F32), jax.ShapeDtypeStruct((m_rows, d), BF16)],
        compiler_params=_params("arbitrary"),
        name="mix",
    )(o_att, o_rnn, z, z, x, mod, mod, mod, g_next.reshape(1, d), w_pa, w_pr, w_out)


def _block_diag(w, group):
    n, r, _ = w.shape
    eye = jnp.eye(group, dtype=w.dtype)
    wg = w.reshape(n // group, group, r, r)
    return jnp.einsum("ngij,gh->ngihj", wg, eye).reshape(n // group, group * r, group * r)


def _tiles(rows, dff):
    pick = lambda n, prefs: next((t for t in prefs if n % t == 0), n)
    return dict(
        tm_norm=pick(rows, (1024, 512)),
        tm_up=pick(rows, (2048, 1024, 512)), tf=pick(dff, (512,)),
        tm_down=pick(rows, (1024, 512)), tn_down=512,
        tm_z=pick(rows, (1024, 512)), tn_z=D_QKV,
        tm_qkv=pick(rows, (512,)),
        tm_mix=pick(rows, (256,)),
    )


def kernel(x_prompt, x_sample, c_prompt, c_sample, cache_k, cache_v, state_h, state_conv, w_ada, b_ada,
           g_norm_ffn1, g_norm_mix, g_norm_ffn2, ffn1_w1, ffn1_w3, ffn1_w2, ffn2_w1, ffn2_w3, ffn2_w2,
           w_in, g_q, g_k, sinks, conv_w, conv_b, w_rg, b_rg, w_ig, b_ig, lru_lambda, w_pa, w_pr, w_out):
    nb, seq, d = x_prompt.shape
    ns = x_sample.shape[0]
    dff = ffn1_w1.shape[2]
    assert d == D_MODEL and w_ada.shape[0] == 1 and x_sample.shape[1] == 1 and cache_k.shape[2] == WINDOW
    assert seq % WINDOW == 0 and w_in.shape[2] == D_QKV + D_Z

    group = MXU_DIM // RNN_BLOCK
    wrg = _block_diag(w_rg[0], group).astype(BF16)
    wig = _block_diag(w_ig[0], group).astype(BF16)
    gqk = jnp.concatenate([jnp.tile(g_q[0], N_Q_HEADS), jnp.tile(g_k[0], N_KV_HEADS)]).reshape(1, D_QK)
    head_of = jnp.arange(D_QK) // HEAD_DIM
    onehot = (head_of[:, None] == jnp.arange(LANES)[None, :]).astype(F32)
    head_reduce = (onehot / HEAD_DIM).astype(BF16)
    head_expand = jnp.concatenate([onehot.T, onehot.T], axis=0).astype(BF16)
    inv = ROPE_THETA ** (-jnp.arange(HEAD_DIM // 2, dtype=F32) * 2.0 / HEAD_DIM)
    inv_row = jnp.tile(inv, LANES // (HEAD_DIM // 2)).reshape(1, LANES)
    sink_vec = sinks[0]
    lam = lru_lambda[0]

    pad_rows = (-(nb + ns)) % SUBLANES
    c_all = jnp.concatenate([c_prompt, c_sample, jnp.zeros((pad_rows, d), F32)], axis=0)
    mod_all = _ada(c_all, w_ada[0], b_ada[0], 1024)
    mod_p = mod_all[:nb].reshape(nb, 1, N_MOD * d)
    mod_s = mod_all[nb:nb + ns]

    wts = dict(f1w1=ffn1_w1, f1w3=ffn1_w3, f1w2=ffn1_w2, f2w1=ffn2_w1, f2w3=ffn2_w3, f2w2=ffn2_w2, wz=w_in)

    def trunk(x, mod, rows_per_mod, cs, sn, table_blocks, t):
        h1 = _modulated_norm(x, mod, rows_per_mod, g_norm_ffn1[0], 0, t["tm_norm"])
        g1, wts["f1w1"], wts["f1w3"] = _ffn_up(h1, wts["f1w1"], wts["f1w3"], t["tm_up"], t["tf"])
        x1, wts["f1w2"] = _ffn_down(g1, wts["f1w2"], x, mod, rows_per_mod, 2, t["tm_down"], t["tn_down"])
        side = ()
        if "mix" not in wts:
            side = (w_pa, w_pr, w_out)
            if _side_chunk_rows(side, x.shape[0] // t["tm_qkv"]) is None:
                side, wts["mix"] = (), tuple(w[0].astype(BF16) for w in side)
        hm, q, kx, vx, kf, vf, *rounded = _inproj_qkv(x1, mod, rows_per_mod, g_norm_mix[0], 3, w_in, cs, sn, gqk,
                                                      head_reduce, head_expand, t["tm_qkv"], table_blocks, side)
        if side:
            wts["mix"] = tuple(rounded)
        z, wts["wz"] = _inproj_z(hm, wts["wz"], t["tm_z"], t["tn_z"])
        return x1, z, q, kx, vx, kf, vf

    def tail(o_att, o_rnn, z, x1, mod, rows_per_mod, t):
        x2, h2 = _mix(o_att, o_rnn, z, x1, mod, rows_per_mod, g_norm_ffn2[0], *wts["mix"], 5, 6, t["tm_mix"])
        g2, wts["f2w1"], wts["f2w3"] = _ffn_up(h2, wts["f2w1"], wts["f2w3"], t["tm_up"], t["tf"])
        y, wts["f2w2"] = _ffn_down(g2, wts["f2w2"], x2, mod, rows_per_mod, 8, t["tm_down"], t["tn_down"])
        return y

    tp = _tiles(nb * seq, dff)
    tp = {k: (min(v, seq) if k.startswith("tm") else v) for k, v in tp.items()}
    cs_p, sn_p = _rope_tables(inv_row, 0, seq, tp["tm_qkv"])
    x1, z, q, kx, vx, kf, vf = trunk(x_prompt.reshape(nb * seq, d), mod_p, seq, cs_p, sn_p,
                                     seq // tp["tm_qkv"], tp)
    nblk = seq // WINDOW
    o_att = _attn(sink_vec, q, kx, vx, nb, nblk, next(n for n in (4, 2, 1) if nblk % n == 0))
    o_rnn, h_p, conv_p = _rnn_prompt(z, nb, seq, conv_w[0], conv_b[0], wrg, wig, b_rg[0], b_ig[0], lam,
                                     min(512, seq))
    y_p = tail(o_att, o_rnn, z, x1, mod_p, seq, tp)

    last = lambda a: a.reshape(nb, seq, D_KV)[:, seq - WINDOW:].reshape(1, nb, WINDOW, N_KV_HEADS, HEAD_DIM)
    k_prompt, v_prompt = last(kf), last(vf)

    ts = _tiles(ns, dff)
    cs_1, sn_1 = _rope_tables(inv_row, PAST_LEN, SUBLANES, SUBLANES)
    cs_s = jnp.broadcast_to(cs_1[0:1], (ns, LANES))
    sn_s = jnp.broadcast_to(sn_1[0:1], (ns, LANES))
    x1s, zs, qs, kxs, vxs, kfs, vfs = trunk(x_sample.reshape(ns, d), mod_s, 1, cs_s, sn_s, 1, ts)
    nsub = 2 * SUBLANES
    assert ns % nsub == 0
    key_minor = lambda c: jnp.transpose(c[0], (0, 2, 3, 1)).reshape(ns * D_KV, WINDOW)
    o_att_s = _attn_step(sink_vec, qs, kfs, vfs, key_minor(cache_k), key_minor(cache_v), nsub)
    conv_s_in = state_conv[0]
    o_rnn_s, h_s = _rnn_step(zs, conv_s_in, state_h[0], conv_w[0], conv_b[0], wrg, wig, b_rg[0], b_ig[0], lam)
    y_s = tail(o_att_s, o_rnn_s, zs, x1s, mod_s, 1, ts)

    k_sample = kfs.reshape(1, ns, 1, N_KV_HEADS, HEAD_DIM)
    v_sample = vfs.reshape(1, ns, 1, N_KV_HEADS, HEAD_DIM)
    conv_sample = jnp.concatenate([conv_s_in[:, 1:], zs[:, None, Z_RX:Z_RX + D_RNN]], axis=1)[None]

    return (y_p.reshape(nb, seq, d), y_s.reshape(ns, 1, d), k_prompt, v_prompt, k_sample, v_sample,
            h_p.reshape(1, nb, D_RNN), h_s[None], conv_p[None], conv_sample)
```

```python
import functools

import jax
import jax.numpy as jnp
from jax import lax
from jax.experimental import pallas as pl
from jax.experimental.pallas import tpu as pltpu

F32 = jnp.float32
BF16 = jnp.bfloat16

D_MODEL = 2048
HEAD_DIM = 64
N_Q_HEADS = 16
N_KV_HEADS = 4
Q_PER_KV = N_Q_HEADS // N_KV_HEADS
D_ATTN = N_Q_HEADS * HEAD_DIM
D_KV = N_KV_HEADS * HEAD_DIM
D_QK = D_ATTN + D_KV
D_QKV = D_QK + D_KV
WINDOW = 128
ROPE_THETA = 10000.0
D_RNN = 1024
N_RNN_BLOCKS = 16
RNN_BLOCK = D_RNN // N_RNN_BLOCKS
CONV_WIDTH = 4
LRU_C = 8.0
N_MOD = 9
EPS = 1e-6
PAST_LEN = 16384

LANES = 128
SUBLANES = 8
MXU_DIM = 256
VMEM_LIMIT = 56 * 1024 * 1024

Z_RX = 0
Z_RG = Z_RX + D_RNN
Z_GATT = Z_RG + D_RNN
Z_GRNN = Z_GATT + D_MODEL
D_Z = Z_GRNN + D_MODEL
DX = 2 * N_KV_HEADS * LANES


def _params(*sem):
    return pltpu.CompilerParams(dimension_semantics=sem, vmem_limit_bytes=VMEM_LIMIT)


def _silu(x):
    return x * jax.nn.sigmoid(x)


def _modulate(x, g, shift, scale):
    ms = jnp.mean(x * x, axis=-1, keepdims=True)
    return (x * lax.rsqrt(ms + EPS) * g) * (1.0 + scale) + shift


def _mod_spec(mod, tm, rows_per_mod, width, col, ngrid):
    colf = col if callable(col) else (lambda *idx: col)
    if mod.ndim == 3:
        return pl.BlockSpec((None, 1, width), lambda *idx: ((idx[ngrid - 1] * tm) // rows_per_mod, 0, colf(*idx)))
    return pl.BlockSpec((tm, width), lambda *idx: (idx[ngrid - 1], colf(*idx)))


def _ada_kernel(c_ref, w_ref, b_ref, o_ref):
    s = _silu(c_ref[...]).astype(BF16)
    o_ref[...] = jnp.dot(s, w_ref[...].astype(BF16), preferred_element_type=F32) + b_ref[...]


def _ada(c_all, w_ada, b_ada, tn):
    rows, d = c_all.shape
    n = w_ada.shape[1]
    return pl.pallas_call(
        _ada_kernel,
        grid=(n // tn,),
        in_specs=[pl.BlockSpec((rows, d), lambda j: (0, 0)),
                  pl.BlockSpec((d, tn), lambda j: (0, j)),
                  pl.BlockSpec((1, tn), lambda j: (0, j))],
        out_specs=pl.BlockSpec((rows, tn), lambda j: (0, j)),
        out_shape=jax.ShapeDtypeStruct((rows, n), F32),
        compiler_params=_params("arbitrary"),
        name="ada",
    )(c_all, w_ada, b_ada.reshape(1, n))


def _rope_kernel(inv_ref, cs_ref, sn_ref, *, base, blk):
    r = pl.program_id(0)
    pos = (base + r * blk + lax.broadcasted_iota(jnp.int32, (blk, LANES), 0)).astype(F32)
    ang = pos * inv_ref[...]
    lane = lax.broadcasted_iota(jnp.int32, (blk, LANES), 1)
    first_half = (lane & (HEAD_DIM - 1)) < HEAD_DIM // 2
    sn = jnp.sin(ang)
    cs_ref[...] = jnp.cos(ang)
    sn_ref[...] = jnp.where(first_half, -sn, sn)


def _rope_tables(inv_row, base, rows, blk):
    out = jax.ShapeDtypeStruct((rows, LANES), F32)
    return pl.pallas_call(
        functools.partial(_rope_kernel, base=base, blk=blk),
        grid=(rows // blk,),
        in_specs=[pl.BlockSpec((1, LANES), lambda r: (0, 0))],
        out_specs=[pl.BlockSpec((blk, LANES), lambda r: (r, 0))] * 2,
        out_shape=[out, out],
        compiler_params=_params("arbitrary"),
        name="rope_tables",
    )(inv_row)


def _modulate_kernel(x_ref, shift_ref, scale_ref, g_ref, h_ref):
    group = 2 * SUBLANES
    per_row = shift_ref.shape[0] > 1

    def body(r, carry):
        rows = pl.ds(pl.multiple_of(r * group, group), group)
        shift = shift_ref[rows, :] if per_row else shift_ref[...]
        scale = scale_ref[rows, :] if per_row else scale_ref[...]
        h_ref[rows, :] = _modulate(x_ref[rows, :], g_ref[...], shift, scale).astype(BF16)
        return carry

    lax.fori_loop(0, x_ref.shape[0] // group, body, 0, unroll=8)


def _modulated_norm(x, mod, rows_per_mod, g, k_shift, tm):
    m_rows, d = x.shape
    return pl.pallas_call(
        _modulate_kernel,
        grid=(m_rows // tm,),
        in_specs=[pl.BlockSpec((tm, d), lambda m: (m, 0)),
                  _mod_spec(mod, tm, rows_per_mod, d, k_shift, 1),
                  _mod_spec(mod, tm, rows_per_mod, d, k_shift + 1, 1),
                  pl.BlockSpec((1, d), lambda m: (0, 0))],
        out_specs=pl.BlockSpec((tm, d), lambda m: (m, 0)),
        out_shape=jax.ShapeDtypeStruct((m_rows, d), BF16),
        compiler_params=_params("arbitrary"),
        name="modnorm",
    )(x, mod, mod, g.reshape(1, d))


HBM = pl.BlockSpec(memory_space=pl.ANY)


def _is_param(w):
    return w.dtype == F32


def _stream_weight_tiles(weights, tn, first):
    t, nt = pl.program_id(0), pl.num_programs(0)
    m, nm = pl.program_id(1), pl.num_programs(1)
    cols = lambda tile: pl.ds(pl.multiple_of(tile * tn, LANES), tn)
    fetch = lambda w_hbm, stage_ref, sem, tile: pltpu.make_async_copy(
        w_hbm.at[0, :, cols(first + tile)], stage_ref, sem.at[0])
    put = lambda wb_ref, wout_hbm, sem, tile: pltpu.make_async_copy(wb_ref, wout_hbm.at[:, cols(tile)], sem.at[1])

    @pl.when(m == 0)
    def _():
        for w_hbm, stage_ref, wb_ref, wout_hbm, sem in weights:
            @pl.when(t == 0)
            def _():
                fetch(w_hbm, stage_ref, sem, 0).start()

            fetch(w_hbm, stage_ref, sem, t).wait()

            @pl.when(t > 0)
            def _():
                put(wb_ref, wout_hbm, sem, t - 1).wait()

            wb_ref[...] = stage_ref[...].astype(BF16)
            put(wb_ref, wout_hbm, sem, t).start()

            @pl.when(t + 1 < nt)
            def _():
                fetch(w_hbm, stage_ref, sem, t + 1).start()

    @pl.when((t == nt - 1) & (m == nm - 1))
    def _():
        for w_hbm, stage_ref, wb_ref, wout_hbm, sem in weights:
            put(wb_ref, wout_hbm, sem, t).wait()


def _weight_scratch(rows, tn):
    return [pltpu.VMEM((rows, tn), F32), pltpu.VMEM((rows, tn), BF16), pltpu.SemaphoreType.DMA((2,))]


def _row_groups(tm):
    n = max(1, tm // 1024)
    return [slice(r * (tm // n), (r + 1) * (tm // n)) for r in range(n)]


def _ffn_up_rows(h_ref, w1b_ref, w3b_ref, g_ref):
    for rows in _row_groups(h_ref.shape[0]):
        h = h_ref[rows, :]
        a = jnp.dot(h, w1b_ref[...], preferred_element_type=F32)
        b = jnp.dot(h, w3b_ref[...], preferred_element_type=F32)
        g_ref[rows, :] = (_silu(a) * b).astype(BF16)


def _ffn_up_param_kernel(h_ref, w1_hbm, w3_hbm, g_ref, w1o_hbm, w3o_hbm,
                         s1_ref, w1b_ref, sem1, s3_ref, w3b_ref, sem3, *, tf):
    _stream_weight_tiles(((w1_hbm, s1_ref, w1b_ref, w1o_hbm, sem1), (w3_hbm, s3_ref, w3b_ref, w3o_hbm, sem3)),
                         tf, 0)
    _ffn_up_rows(h_ref, w1b_ref, w3b_ref, g_ref)


def _ffn_up(h, w1, w3, tm, tf):
    m_rows, d = h.shape
    dff = w1.shape[-1]
    grid = (dff // tf, m_rows // tm)
    h_spec = pl.BlockSpec((tm, d), lambda f, m: (m, 0))
    g_spec = pl.BlockSpec((tm, tf), lambda f, m: (m, f))
    g_shape = jax.ShapeDtypeStruct((m_rows, dff), BF16)
    if not _is_param(w1):
        w_spec = pl.BlockSpec((d, tf), lambda f, m: (0, f))
        g = pl.pallas_call(
            _ffn_up_rows, grid=grid, in_specs=[h_spec, w_spec, w_spec], out_specs=g_spec, out_shape=g_shape,
            compiler_params=_params("arbitrary", "arbitrary"), name="ffn_up_rows",
        )(h, w1, w3)
        return g, w1, w3
    w_shape = jax.ShapeDtypeStruct((d, dff), BF16)
    return pl.pallas_call(
        functools.partial(_ffn_up_param_kernel, tf=tf),
        grid=grid,
        in_specs=[h_spec, HBM, HBM],
        out_specs=[g_spec, HBM, HBM],
        out_shape=[g_shape, w_shape, w_shape],
        scratch_shapes=_weight_scratch(d, tf) + _weight_scratch(d, tf),
        compiler_params=_params("arbitrary", "arbitrary"),
        name="ffn_up",
    )(h, w1, w3)


def _ffn_down_rows(g_ref, w2b_ref, x_ref, gate_ref, o_ref):
    for rows in _row_groups(g_ref.shape[0]):
        acc = jnp.dot(g_ref[rows, :], w2b_ref[...], preferred_element_type=F32)
        gate = gate_ref[...] if gate_ref.shape[0] == 1 else gate_ref[rows, :]
        o_ref[rows, :] = x_ref[rows, :] + 0.5 * gate * acc


def _ffn_down_param_kernel(g_ref, w2_hbm, x_ref, gate_ref, o_ref, w2o_hbm, s2_ref, w2b_ref, sem2, *, tn):
    _stream_weight_tiles(((w2_hbm, s2_ref, w2b_ref, w2o_hbm, sem2),), tn, 0)
    _ffn_down_rows(g_ref, w2b_ref, x_ref, gate_ref, o_ref)


def _ffn_down(g, w2, x, mod, rows_per_mod, k_gate, tm, tn):
    m_rows, d = x.shape
    dff = g.shape[1]
    grid = (d // tn, m_rows // tm)
    gate_col = lambda n, m: k_gate * (d // tn) + n
    g_spec = pl.BlockSpec((tm, dff), lambda n, m: (m, 0))
    x_spec = pl.BlockSpec((tm, tn), lambda n, m: (m, n))
    gate_spec = _mod_spec(mod, tm, rows_per_mod, tn, gate_col, 2)
    o_shape = jax.ShapeDtypeStruct((m_rows, d), F32)
    if not _is_param(w2):
        out = pl.pallas_call(
            _ffn_down_rows, grid=grid,
            in_specs=[g_spec, pl.BlockSpec((dff, tn), lambda n, m: (0, n)), x_spec, gate_spec],
            out_specs=x_spec, out_shape=o_shape,
            compiler_params=_params("arbitrary", "arbitrary"), name="ffn_down_rows",
        )(g, w2, x, mod)
        return out, w2
    return pl.pallas_call(
        functools.partial(_ffn_down_param_kernel, tn=tn),
        grid=grid,
        in_specs=[g_spec, HBM, x_spec, gate_spec],
        out_specs=[x_spec, HBM],
        out_shape=[o_shape, jax.ShapeDtypeStruct((dff, d), BF16)],
        scratch_shapes=_weight_scratch(dff, tn),
        compiler_params=_params("arbitrary", "arbitrary"),
        name="ffn_down",
    )(g, w2, x, mod)


def _inproj_z_rows(h_ref, wb_ref, z_ref):
    for rows in _row_groups(h_ref.shape[0]):
        z_ref[rows, :] = jnp.dot(h_ref[rows, :], wb_ref[...], preferred_element_type=F32)


def _inproj_z_param_kernel(h_ref, w_hbm, z_ref, wo_hbm, s_ref, wb_ref, sem, *, tn, first):
    _stream_weight_tiles(((w_hbm, s_ref, wb_ref, wo_hbm, sem),), tn, first)
    _inproj_z_rows(h_ref, wb_ref, z_ref)


def _inproj_z(h, w, tm, tn):
    m_rows, d = h.shape
    grid = (D_Z // tn, m_rows // tm)
    h_spec = pl.BlockSpec((tm, d), lambda n, m: (m, 0))
    z_spec = pl.BlockSpec((tm, tn), lambda n, m: (m, n))
    z_shape = jax.ShapeDtypeStruct((m_rows, D_Z), F32)
    if not _is_param(w):
        z = pl.pallas_call(
            _inproj_z_rows, grid=grid, in_specs=[h_spec, pl.BlockSpec((d, tn), lambda n, m: (0, n))],
            out_specs=z_spec, out_shape=z_shape,
            compiler_params=_params("arbitrary", "arbitrary"), name="inproj_z_rows",
        )(h, w)
        return z, w
    return pl.pallas_call(
        functools.partial(_inproj_z_param_kernel, tn=tn, first=D_QKV // tn),
        grid=grid,
        in_specs=[h_spec, HBM],
        out_specs=[z_spec, HBM],
        out_shape=[z_shape, jax.ShapeDtypeStruct((d, D_Z), BF16)],
        scratch_shapes=_weight_scratch(d, tn),
        compiler_params=_params("arbitrary", "arbitrary"),
        name="inproj_z",
    )(h, w)


def _store_head_pairs(ref, rows, chunk, c):
    lo = lax.broadcasted_iota(jnp.int32, chunk.shape, 1) < HEAD_DIM
    swapped = pltpu.roll(chunk, HEAD_DIM, 1)
    zero = jnp.zeros_like(chunk)
    cols = (jnp.where(lo, chunk, zero), jnp.where(lo, zero, swapped),
            jnp.where(lo, swapped, zero), jnp.where(lo, zero, chunk))
    for i, col in enumerate(cols):
        ref[rows, (4 * c + i) * LANES:(4 * c + i + 1) * LANES] = col.astype(ref.dtype)


def _side_chunk_rows(side, steps):
    total = sum(w.shape[1] for w in side)
    rows = total // steps
    ok = rows * steps == total and rows % (2 * SUBLANES) == 0 and all(w.shape[1] % rows == 0 for w in side)
    return rows if ok else None


def _round_side_matrices(mats, stage_ref, ostage_ref, sem):
    m = pl.program_id(0)
    nrows = stage_ref.shape[1]
    chunks = [(w, wb, r0) for w, wb in mats for r0 in range(0, w.shape[1], nrows)]

    def fetch(c):
        w, _, r0 = chunks[c]
        return pltpu.make_async_copy(w.at[0, pl.ds(r0, nrows), :], stage_ref.at[c % 2], sem.at[c % 2])

    def put(c):
        _, wb, r0 = chunks[c]
        return pltpu.make_async_copy(ostage_ref.at[c % 2], wb.at[pl.ds(r0, nrows), :], sem.at[2 + c % 2])

    last = len(chunks) - 1
    for c in range(len(chunks)):
        @pl.when(m == c)
        def _():
            if c == 0:
                fetch(0).start()
            if c < last:
                fetch(c + 1).start()
            fetch(c).wait()
            if c >= 2:
                put(c - 2).wait()
            ostage_ref[c % 2] = stage_ref[c % 2].astype(BF16)
            put(c).start()
            if c == last:
                if c >= 1:
                    put(c - 1).wait()
                put(c).wait()


def _inproj_qkv_kernel(*refs, nsplit, nside):
    (x_ref, shift_ref, scale_ref, g_ref, w_ref, cs_ref, sn_ref, gqk_ref, red_ref, exp_ref) = refs[:10]
    side_in = refs[10:10 + nside]
    h_ref, q_ref, kx_ref, vx_ref, kf_ref, vf_ref = refs[10 + nside:16 + nside]
    side_out = refs[16 + nside:16 + 2 * nside]
    wb_ref = refs[16 + 2 * nside]
    if nside:
        _round_side_matrices(list(zip(side_in, side_out)), *refs[17 + 2 * nside:])

    @pl.when(pl.program_id(0) == 0)
    def _():
        wb_ref[...] = w_ref[...].astype(BF16)

    step = x_ref.shape[0] // nsplit
    for r in range(nsplit):
        rows = slice(r * step, (r + 1) * step)
        shift = shift_ref[...] if shift_ref.shape[0] == 1 else shift_ref[rows, :]
        scale = scale_ref[...] if scale_ref.shape[0] == 1 else scale_ref[rows, :]
        h = _modulate(x_ref[rows, :], g_ref[...], shift, scale).astype(BF16)
        h_ref[rows, :] = h
        acc = jnp.dot(h, wb_ref[...], preferred_element_type=F32)
        qk = acc[:, :D_QK]
        hm = jnp.dot((qk * qk).astype(BF16), red_ref[...], preferred_element_type=F32)
        hi = hm.astype(BF16)
        lo = (hm - hi.astype(F32)).astype(BF16)
        ms = jnp.dot(jnp.concatenate([hi, lo], axis=1), exp_ref[...], preferred_element_type=F32)
        y = qk * lax.rsqrt(ms + EPS) * gqk_ref[...]
        cs = cs_ref[rows, :]
        sn = sn_ref[rows, :]
        lane = lax.broadcasted_iota(jnp.int32, cs.shape, 1)
        first_half = (lane & (HEAD_DIM - 1)) < HEAD_DIM // 2
        for c in range(D_QK // LANES):
            yc = y[:, c * LANES:(c + 1) * LANES]
            partner = jnp.where(first_half,
                                pltpu.roll(yc, LANES - HEAD_DIM // 2, 1),
                                pltpu.roll(yc, HEAD_DIM // 2, 1))
            rot = yc * cs + partner * sn
            if c < D_ATTN // LANES:
                q_ref[rows, c * LANES:(c + 1) * LANES] = (rot * (HEAD_DIM ** -0.5)).astype(BF16)
            else:
                kf_ref[rows, c * LANES - D_ATTN:(c + 1) * LANES - D_ATTN] = rot
                _store_head_pairs(kx_ref, rows, rot, c - D_ATTN // LANES)
        v = acc[:, D_QK:]
        vf_ref[rows, :] = v
        for c in range(D_KV // LANES):
            _store_head_pairs(vx_ref, rows, v[:, c * LANES:(c + 1) * LANES], c)


def _inproj_qkv(x, mod, rows_per_mod, g, k_shift, w_in, cs, sn, gqk, head_reduce, head_expand, tm, table_blocks,
                side=()):
    m_rows, d = x.shape
    steps = m_rows // tm
    row = lambda width: pl.BlockSpec((tm, width), lambda m: (m, 0))
    table = pl.BlockSpec((tm, LANES), lambda m: (m % table_blocks, 0))
    nsplit = 2 if tm % (2 * MXU_DIM) == 0 else 1
    side_scratch = []
    if side:
        width = side[0].shape[2]
        chunk = _side_chunk_rows(side, steps)
        assert chunk is not None and all(w.shape[2] == width for w in side)
        side_scratch = [pltpu.VMEM((2, chunk, width), F32), pltpu.VMEM((2, chunk, width), BF16),
                        pltpu.SemaphoreType.DMA((4,))]
    return pl.pallas_call(
        functools.partial(_inproj_qkv_kernel, nsplit=nsplit, nside=len(side)),
        grid=(steps,),
        in_specs=[row(d),
                  _mod_spec(mod, tm, rows_per_mod, d, k_shift, 1),
                  _mod_spec(mod, tm, rows_per_mod, d, k_shift + 1, 1),
                  pl.BlockSpec((1, d), lambda m: (0, 0)),
                  pl.BlockSpec((None, d, D_QKV), lambda m: (0, 0, 0), pipeline_mode=pl.Buffered(1)),
                  table, table,
                  pl.BlockSpec((1, D_QK), lambda m: (0, 0)),
                  pl.BlockSpec(head_reduce.shape, lambda m: (0, 0)),
                  pl.BlockSpec(head_expand.shape, lambda m: (0, 0))] + [HBM] * len(side),
        out_specs=[row(d), row(D_ATTN), row(DX), row(DX), row(D_KV), row(D_KV)] + [HBM] * len(side),
        out_shape=[jax.ShapeDtypeStruct((m_rows, d), BF16),
                   jax.ShapeDtypeStruct((m_rows, D_ATTN), BF16),
                   jax.ShapeDtypeStruct((m_rows, DX), BF16),
                   jax.ShapeDtypeStruct((m_rows, DX), BF16),
                   jax.ShapeDtypeStruct((m_rows, D_KV), F32),
                   jax.ShapeDtypeStruct((m_rows, D_KV), F32)]
        + [jax.ShapeDtypeStruct(w.shape[1:], BF16) for w in side],
        scratch_shapes=[pltpu.VMEM((d, D_QKV), BF16)] + side_scratch,
        compiler_params=_params("arbitrary"),
        name="inproj_qkv",
    )(x, mod, mod, g.reshape(1, d), w_in, cs, sn, gqk, head_reduce, head_expand, *side)


def _attn_bias(bias_ref, tq, past_off):
    nk = 2 * WINDOW
    ri = lax.broadcasted_iota(jnp.int32, (2 * tq, 2 * nk), 0) & (tq - 1)
    kj = lax.broadcasted_iota(jnp.int32, (2 * tq, 2 * nk), 1) & (nk - 1)
    visible = ((kj < WINDOW) & (kj > ri + past_off)) | ((kj >= WINDOW) & (kj - WINDOW <= ri))
    bias_ref[...] = jnp.where(visible, 0.0, -jnp.inf)


def _attn_blocks(sink_ref, bias_of, q_ref, past_of, kc_ref, vc_ref, nsub, tq, store):
    nk = 2 * WINDOW
    rows = 2 * tq
    first_rows = lax.broadcasted_iota(jnp.int32, (rows, 1), 0) < tq
    lo_lanes = lax.broadcasted_iota(jnp.int32, (rows, LANES), 1) < HEAD_DIM
    ones_lo = (lax.broadcasted_iota(jnp.int32, (nk, LANES), 1) < HEAD_DIM).astype(BF16)
    ones_hi = (1 - ones_lo.astype(F32)).astype(BF16)
    nt = (((1,), (1,)), ((), ()))

    def keys(which, s, col):
        cols = slice(col * LANES, (col + 1) * LANES)
        past = past_of(s)
        parts = [past[which][past[2]:past[2] + WINDOW, cols], (kc_ref, vc_ref)[which][s * tq:(s + 1) * tq, cols]]
        if tq < WINDOW:
            parts.append(jnp.zeros((WINDOW - tq, LANES), BF16))
        return parts

    for s in range(nsub):
        for g in range(N_KV_HEADS):
            c0, c1 = 2 * g, 2 * g + 1
            q4 = jnp.concatenate([q_ref[s * tq:(s + 1) * tq, c0 * LANES:(c0 + 1) * LANES],
                                  q_ref[s * tq:(s + 1) * tq, c1 * LANES:(c1 + 1) * LANES]], axis=0)
            kk = jnp.concatenate(keys(0, s, c0) + keys(0, s, c1), axis=0)
            sc = lax.dot_general(q4, kk, nt, preferred_element_type=F32) + bias_of(s)[...]
            sk_lo = jnp.where(first_rows, sink_ref[4 * g], sink_ref[4 * g + 2])
            sk_hi = jnp.where(first_rows, sink_ref[4 * g + 1], sink_ref[4 * g + 3])
            m_lo = jnp.maximum(jnp.max(sc[:, :nk], axis=-1, keepdims=True), sk_lo)
            m_hi = jnp.maximum(jnp.max(sc[:, nk:], axis=-1, keepdims=True), sk_hi)
            p = jnp.concatenate([jnp.exp(sc[:, :nk] - m_lo), jnp.exp(sc[:, nk:] - m_hi)], axis=1).astype(BF16)
            vv = jnp.concatenate(
                [jnp.concatenate([jnp.concatenate(keys(1, s, c0), axis=0), ones_lo], axis=1),
                 jnp.concatenate([jnp.concatenate(keys(1, s, c1), axis=0), ones_hi], axis=1)],
                axis=0)
            o = jnp.dot(p, vv, preferred_element_type=F32)
            denom = o[:, LANES:] + jnp.where(lo_lanes, jnp.exp(sk_lo - m_lo), jnp.exp(sk_hi - m_hi))
            store(s, c0, c1, o[:, :LANES] / denom)


def _attn_kernel(sink_ref, q_ref, kp_ref, kc_ref, vp_ref, vc_ref, o_ref, bias0_ref, bias_ref, *, nq):
    tq = WINDOW
    _attn_bias(bias0_ref, tq, jnp.where(pl.program_id(1) > 0, 0, WINDOW))
    if nq > 1:
        _attn_bias(bias_ref, tq, 0)

    def store(s, c0, c1, out):
        o_ref[s * tq:(s + 1) * tq, c0 * LANES:(c0 + 1) * LANES] = out[:tq].astype(BF16)
        o_ref[s * tq:(s + 1) * tq, c1 * LANES:(c1 + 1) * LANES] = out[tq:].astype(BF16)

    past_of = lambda s: (kp_ref, vp_ref, 0) if s == 0 else (kc_ref, vc_ref, (s - 1) * WINDOW)
    bias_of = lambda s: bias0_ref if s == 0 else bias_ref
    _attn_blocks(sink_ref, bias_of, q_ref, past_of, kc_ref, vc_ref, nq, tq, store)


def _attn(sinks, q, kx, vx, nbatch, nblk, nq):
    steps = nblk // nq
    cur = lambda b, n: (b * steps + n, 0)
    past = lambda b, n: (b * nblk + jnp.maximum(n * nq - 1, 0), 0)
    kv_cur = pl.BlockSpec((nq * WINDOW, DX), cur)
    kv_past = pl.BlockSpec((WINDOW, DX), past)
    bias = pltpu.VMEM((2 * WINDOW, 4 * WINDOW), F32)
    return pl.pallas_call(
        functools.partial(_attn_kernel, nq=nq),
        grid=(nbatch, steps),
        in_specs=[pl.BlockSpec(memory_space=pltpu.SMEM), pl.BlockSpec((nq * WINDOW, D_ATTN), cur),
                  kv_past, kv_cur, kv_past, kv_cur],
        out_specs=pl.BlockSpec((nq * WINDOW, D_ATTN), cur),
        out_shape=jax.ShapeDtypeStruct(q.shape, BF16),
        scratch_shapes=[bias, bias],
        compiler_params=_params("arbitrary", "arbitrary"),
        name="attn",
    )(sinks, q, kx, kx, vx, vx)


def _rows_from_lanes(row, nrows):
    out_rows = max(nrows, SUBLANES)
    rep = jnp.broadcast_to(row, (out_rows, row.shape[1]))
    ridx = lax.broadcasted_iota(jnp.int32, (out_rows, HEAD_DIM), 0)
    out = jnp.zeros((out_rows, HEAD_DIM), row.dtype)
    for i in range(nrows):
        out = jnp.where(ridx == i, rep[:, i * HEAD_DIM:(i + 1) * HEAD_DIM], out)
    return out


def _attn_step_kernel(sink_ref, q_ref, kn_ref, vn_ref, ck_ref, cv_ref, o_ref, *, nsub):
    grp = lax.broadcasted_iota(jnp.int32, (N_Q_HEADS, HEAD_DIM), 0) >> 2
    bias = jnp.where(lax.broadcasted_iota(jnp.int32, (N_Q_HEADS, WINDOW), 1) >= 1, 0.0, -jnp.inf)
    hcol = lax.broadcasted_iota(jnp.int32, (N_Q_HEADS, 1), 0)
    sk = jnp.zeros((N_Q_HEADS, 1), F32)
    for h in range(N_Q_HEADS):
        sk = jnp.where(hcol == h, sink_ref[h], sk)
    nt = (((1,), (1,)), ((), ()))
    qf = q_ref[...].astype(F32)

    def own_group(x):
        out = jnp.zeros((N_Q_HEADS, HEAD_DIM), F32)
        for g in range(N_KV_HEADS):
            out = jnp.where(grp == g, x[:, g * HEAD_DIM:(g + 1) * HEAD_DIM], out)
        return out

    for s in range(nsub):
        qh = _rows_from_lanes(qf[s:s + 1, :], N_Q_HEADS)
        zero = jnp.zeros_like(qh)
        q_bd = jnp.concatenate([jnp.where(grp == g, qh, zero) for g in range(N_KV_HEADS)], axis=1)
        kt = ck_ref[s * D_KV:(s + 1) * D_KV, :].astype(BF16)
        vt = cv_ref[s * D_KV:(s + 1) * D_KV, :].astype(BF16)
        sc = jnp.dot(q_bd.astype(BF16), kt, preferred_element_type=F32) + bias
        s_new = jnp.sum(q_bd * kn_ref[s:s + 1, :], axis=-1, keepdims=True)
        m = jnp.maximum(jnp.maximum(jnp.max(sc, axis=-1, keepdims=True), s_new), sk)
        p = jnp.exp(sc - m)
        p_new = jnp.exp(s_new - m)
        denom = jnp.sum(p, axis=-1, keepdims=True) + p_new + jnp.exp(sk - m)
        o_all = lax.dot_general(p.astype(BF16), vt, nt, preferred_element_type=F32)
        v_new = own_group(jnp.broadcast_to(vn_ref[s:s + 1, :], (N_Q_HEADS, D_KV)))
        o = (own_group(o_all) + p_new * v_new) / denom
        for c in range(N_Q_HEADS // 2):
            o_ref[s:s + 1, c * LANES:(c + 1) * LANES] = jnp.concatenate(
                [o[2 * c:2 * c + 1, :], o[2 * c + 1:2 * c + 2, :]], axis=1)


def _attn_step(sinks, q, k_new, v_new, cache_k, cache_v, nsub):
    ns = q.shape[0]
    row = lambda width: pl.BlockSpec((nsub, width), lambda b: (b, 0))
    cache = pl.BlockSpec((nsub * D_KV, WINDOW), lambda b: (b, 0))
    return pl.pallas_call(
        functools.partial(_attn_step_kernel, nsub=nsub),
        grid=(ns // nsub,),
        in_specs=[pl.BlockSpec(memory_space=pltpu.SMEM), row(D_ATTN), row(D_KV), row(D_KV), cache, cache],
        out_specs=row(D_ATTN),
        out_shape=jax.ShapeDtypeStruct((ns, D_ATTN), F32),
        compiler_params=_params("arbitrary"),
        name="attn_step",
    )(sinks, q, k_new, v_new, cache_k, cache_v)


def _softplus(x):
    return jnp.maximum(x, 0.0) + jnp.log1p(jnp.exp(-jnp.abs(x)))


def _lru_coeffs(xc, wrg_ref, wig_ref, brg, big, lam):
    xb = xc.astype(BF16)
    ngroups = D_RNN // MXU_DIM
    rs, igs = [], []
    for c in range(ngroups):
        xg = xb[:, c * MXU_DIM:(c + 1) * MXU_DIM]
        rs.append(jnp.dot(xg, wrg_ref[c], preferred_element_type=F32))
        igs.append(jnp.dot(xg, wig_ref[c], preferred_element_type=F32))
    r = jax.nn.sigmoid(jnp.concatenate(rs, axis=1) + brg)
    ig = jax.nn.sigmoid(jnp.concatenate(igs, axis=1) + big)
    log_a = -LRU_C * r * _softplus(-lam)
    a = jnp.exp(log_a)
    one_minus_a2 = -jnp.tanh(log_a) * (1.0 + a * a)
    u = jnp.sqrt(one_minus_a2) * (ig * xc)
    return a, u


def _rnn_prompt_kernel(rx_ref, rg_ref, cw_ref, cb_ref, wrg_ref, wig_ref, brg_ref, big_ref, lam_ref,
                       o_ref, hlast_ref, conv_ref, xs_ref, a_ref, h_ref, carry_ref, *, tc):
    t = pl.program_id(1)
    pad = SUBLANES

    @pl.when(t == 0)
    def _():
        xs_ref[0:pad, :] = jnp.zeros((pad, D_RNN), F32)
        carry_ref[...] = jnp.zeros_like(carry_ref)

    x = rx_ref[...]
    xs_ref[pad:pad + tc, :] = x
    xc = cb_ref[...] + x * cw_ref[CONV_WIDTH - 1:CONV_WIDTH, :]
    for k in range(1, CONV_WIDTH):
        xc = xc + xs_ref[pad - k:pad - k + tc, :] * cw_ref[CONV_WIDTH - 1 - k:CONV_WIDTH - k, :]
    tail = xs_ref[tc:tc + pad, :]
    xs_ref[0:pad, :] = tail
    conv_ref[...] = tail[pad - (CONV_WIDTH - 1):, :]

    a, u = _lru_coeffs(xc, wrg_ref, wig_ref, brg_ref[...], big_ref[...], lam_ref[...])
    a_ref[...] = a
    h_ref[...] = u

    row = lax.broadcasted_iota(jnp.int32, (SUBLANES, D_RNN), 0)

    def body(r, carry):
        off = pl.multiple_of(r * SUBLANES, SUBLANES)
        av = a_ref[pl.ds(off, SUBLANES), :]
        hv = h_ref[pl.ds(off, SUBLANES), :]
        for sft in (1, 2, 4):
            keep = row >= sft
            a_sh = jnp.where(keep, pltpu.roll(av, sft, 0), 1.0)
            h_sh = jnp.where(keep, pltpu.roll(hv, sft, 0), 0.0)
            hv = av * h_sh + hv
            av = av * a_sh
        hv = hv + av * carry
        h_ref[pl.ds(off, SUBLANES), :] = hv
        return jnp.broadcast_to(hv[SUBLANES - 1:SUBLANES, :], (SUBLANES, D_RNN))

    carry = lax.fori_loop(0, tc // SUBLANES, body, carry_ref[...])
    carry_ref[...] = carry
    hlast_ref[...] = carry[0:1, :]
    o_ref[...] = (h_ref[...] * jax.nn.gelu(rg_ref[...])).astype(BF16)


def _rnn_prompt(z, nbatch, seq, conv_w, conv_b, wrg, wig, b_rg, b_ig, lam, tc):
    nchunk = seq // tc
    rx_blk = Z_RX // D_RNN
    rg_blk = Z_RG // D_RNN
    vec = pl.BlockSpec((1, D_RNN), lambda b, t: (0, 0))
    wspec = pl.BlockSpec(wrg.shape, lambda b, t: (0, 0, 0))
    return pl.pallas_call(
        functools.partial(_rnn_prompt_kernel, tc=tc),
        grid=(nbatch, nchunk),
        in_specs=[pl.BlockSpec((tc, D_RNN), lambda b, t: (b * nchunk + t, rx_blk)),
                  pl.BlockSpec((tc, D_RNN), lambda b, t: (b * nchunk + t, rg_blk)),
                  pl.BlockSpec((CONV_WIDTH, D_RNN), lambda b, t: (0, 0)),
                  vec, wspec, wspec, vec, vec, vec],
        out_specs=[pl.BlockSpec((tc, D_RNN), lambda b, t: (b * nchunk + t, 0)),
                   pl.BlockSpec((None, 1, D_RNN), lambda b, t: (b, 0, 0)),
                   pl.BlockSpec((None, CONV_WIDTH - 1, D_RNN), lambda b, t: (b, 0, 0))],
        out_shape=[jax.ShapeDtypeStruct((nbatch * seq, D_RNN), BF16),
                   jax.ShapeDtypeStruct((nbatch, 1, D_RNN), F32),
                   jax.ShapeDtypeStruct((nbatch, CONV_WIDTH - 1, D_RNN), F32)],
        scratch_shapes=[pltpu.VMEM((tc + SUBLANES, D_RNN), F32),
                        pltpu.VMEM((tc, D_RNN), F32),
                        pltpu.VMEM((tc, D_RNN), F32),
                        pltpu.VMEM((SUBLANES, D_RNN), F32)],
        compiler_params=_params("arbitrary", "arbitrary"),
        name="rnn_prompt",
    )(z, z, conv_w, conv_b.reshape(1, D_RNN), wrg, wig, b_rg.reshape(1, D_RNN),
      b_ig.reshape(1, D_RNN), lam.reshape(1, D_RNN))


def _rnn_step_kernel(rx_ref, rg_ref, c0_ref, c1_ref, c2_ref, h0_ref, cw_ref, cb_ref, wrg_ref, wig_ref,
                     brg_ref, big_ref, lam_ref, o_ref, h_ref):
    x = rx_ref[...]
    xc = (cb_ref[...] + c0_ref[...] * cw_ref[0:1, :] + c1_ref[...] * cw_ref[1:2, :]
          + c2_ref[...] * cw_ref[2:3, :] + x * cw_ref[3:4, :])
    a, u = _lru_coeffs(xc, wrg_ref, wig_ref, brg_ref[...], big_ref[...], lam_ref[...])
    h = a * h0_ref[...] + u
    h_ref[...] = h
    o_ref[...] = (h * jax.nn.gelu(rg_ref[...])).astype(BF16)


def _rnn_step(z, conv_state, h0, conv_w, conv_b, wrg, wig, b_rg, b_ig, lam):
    rows = z.shape[0]
    full = lambda shape: pl.BlockSpec(shape, lambda i: (0,) * len(shape))
    act = full((rows, D_RNN))
    vec = full((1, D_RNN))
    return pl.pallas_call(
        _rnn_step_kernel,
        grid=(1,),
        in_specs=[pl.BlockSpec((rows, D_RNN), lambda i: (0, Z_RX // D_RNN)),
                  pl.BlockSpec((rows, D_RNN), lambda i: (0, Z_RG // D_RNN)),
                  act, act, act, act, full((CONV_WIDTH, D_RNN)), vec,
                  full(wrg.shape), full(wig.shape), vec, vec, vec],
        out_specs=[act, act],
        out_shape=[jax.ShapeDtypeStruct((rows, D_RNN), BF16),
                   jax.ShapeDtypeStruct((rows, D_RNN), F32)],
        compiler_params=_params("arbitrary"),
        name="rnn_step",
    )(z, z, conv_state[:, 0], conv_state[:, 1], conv_state[:, 2], h0, conv_w,
      conv_b.reshape(1, D_RNN), wrg, wig, b_rg.reshape(1, D_RNN), b_ig.reshape(1, D_RNN),
      lam.reshape(1, D_RNN))


def _mix_kernel(oa_ref, or_ref, ga_ref, gr_ref, x_ref, gate_ref, shift_ref, scale_ref, g_ref,
                wpa_ref, wpr_ref, wout_ref, o_ref, h_ref):
    pa = jnp.dot(oa_ref[...].astype(BF16), wpa_ref[...], preferred_element_type=F32)
    pr = jnp.dot(or_ref[...], wpr_ref[...], preferred_element_type=F32)
    mix = jax.nn.sigmoid(ga_ref[...]) * pa + jax.nn.sigmoid(gr_ref[...]) * pr
    x = x_ref[...] + gate_ref[...] * jnp.dot(mix.astype(BF16), wout_ref[...], preferred_element_type=F32)
    o_ref[...] = x
    h_ref[...] = _modulate(x, g_ref[...], shift_ref[...], scale_ref[...]).astype(BF16)


def _mix(o_att, o_rnn, z, x, mod, rows_per_mod, g_next, w_pa, w_pr, w_out, k_gate, k_shift_next, tm):
    m_rows, d = x.shape
    const = lambda shape: pl.BlockSpec(shape, lambda m: (0, 0), pipeline_mode=pl.Buffered(1))
    row = lambda width, col: pl.BlockSpec((tm, width), lambda m: (m, col))
    mspec = lambda k: _mod_spec(mod, tm, rows_per_mod, d, k, 1)
    return pl.pallas_call(
        _mix_kernel,
        grid=(m_rows // tm,),
        in_specs=[row(D_ATTN, 0), row(D_RNN, 0), row(d, Z_GATT // d), row(d, Z_GRNN // d), row(d, 0),
                  mspec(k_gate), mspec(k_shift_next), mspec(k_shift_next + 1), const((1, d)),
                  const(w_pa.shape), const(w_pr.shape), const(w_out.shape)],
        out_specs=[row(d, 0), row(d, 0)],
        out_shape=[jax.ShapeDtypeStruct((m_rows, d), F32), jax.ShapeDtypeStruct((m_rows, d), BF16)],
        compiler_params=_params("arbitrary"),
        name="mix",
    )(o_att, o_rnn, z, z, x, mod, mod, mod, g_next.reshape(1, d), w_pa, w_pr, w_out)


def _block_diag(w, group):
    n, r, _ = w.shape
    eye = jnp.eye(group, dtype=w.dtype)
    wg = w.reshape(n // group, group, r, r)
    return jnp.einsum("ngij,gh->ngihj", wg, eye).reshape(n // group, group * r, group * r)


def _tiles(rows, dff):
    pick = lambda n, prefs: next((t for t in prefs if n % t == 0), n)
    few_rows = rows <= 2 * LANES
    return dict(
        tm_norm=pick(rows, (1024, 512)),
        tm_up=pick(rows, (2048, 1024, 512)), tf=pick(dff, (1408, 512) if few_rows else (512,)),
        tm_down=pick(rows, (1024, 512)), tn_down=1024 if few_rows else 512,
        tm_z=pick(rows, (1024, 512)), tn_z=D_QKV,
        tm_qkv=pick(rows, (512,)),
        tm_mix=pick(rows, (256,)),
    )


def kernel(x_prompt, x_sample, c_prompt, c_sample, cache_k, cache_v, state_h, state_conv, w_ada, b_ada,
           g_norm_ffn1, g_norm_mix, g_norm_ffn2, ffn1_w1, ffn1_w3, ffn1_w2, ffn2_w1, ffn2_w3, ffn2_w2,
           w_in, g_q, g_k, sinks, conv_w, conv_b, w_rg, b_rg, w_ig, b_ig, lru_lambda, w_pa, w_pr, w_out):
    nb, seq, d = x_prompt.shape
    ns = x_sample.shape[0]
    dff = ffn1_w1.shape[2]
    assert d == D_MODEL and w_ada.shape[0] == 1 and x_sample.shape[1] == 1 and cache_k.shape[2] == WINDOW
    assert seq % WINDOW == 0 and w_in.shape[2] == D_QKV + D_Z

    group = MXU_DIM // RNN_BLOCK
    wrg = _block_diag(w_rg[0], group).astype(BF16)
    wig = _block_diag(w_ig[0], group).astype(BF16)
    gqk = jnp.concatenate([jnp.tile(g_q[0], N_Q_HEADS), jnp.tile(g_k[0], N_KV_HEADS)]).reshape(1, D_QK)
    head_of = jnp.arange(D_QK) // HEAD_DIM
    onehot = (head_of[:, None] == jnp.arange(LANES)[None, :]).astype(F32)
    head_reduce = (onehot / HEAD_DIM).astype(BF16)
    head_expand = jnp.concatenate([onehot.T, onehot.T], axis=0).astype(BF16)
    inv = ROPE_THETA ** (-jnp.arange(HEAD_DIM // 2, dtype=F32) * 2.0 / HEAD_DIM)
    inv_row = jnp.tile(inv, LANES // (HEAD_DIM // 2)).reshape(1, LANES)
    sink_vec = sinks[0]
    lam = lru_lambda[0]

    pad_rows = (-(nb + ns)) % SUBLANES
    c_all = jnp.concatenate([c_prompt, c_sample, jnp.zeros((pad_rows, d), F32)], axis=0)
    mod_all = _ada(c_all, w_ada[0], b_ada[0], 1024)
    mod_p = mod_all[:nb].reshape(nb, 1, N_MOD * d)
    mod_s = mod_all[nb:nb + ns]

    wts = dict(f1w1=ffn1_w1, f1w3=ffn1_w3, f1w2=ffn1_w2, f2w1=ffn2_w1, f2w3=ffn2_w3, f2w2=ffn2_w2, wz=w_in)

    def trunk(x, mod, rows_per_mod, cs, sn, table_blocks, t):
        h1 = _modulated_norm(x, mod, rows_per_mod, g_norm_ffn1[0], 0, t["tm_norm"])
        g1, wts["f1w1"], wts["f1w3"] = _ffn_up(h1, wts["f1w1"], wts["f1w3"], t["tm_up"], t["tf"])
        x1, wts["f1w2"] = _ffn_down(g1, wts["f1w2"], x, mod, rows_per_mod, 2, t["tm_down"], t["tn_down"])
        side = ()
        if "mix" not in wts:
            side = (w_pa, w_pr, w_out)
            if _side_chunk_rows(side, x.shape[0] // t["tm_qkv"]) is None:
                side, wts["mix"] = (), tuple(w[0].astype(BF16) for w in side)
        hm, q, kx, vx, kf, vf, *rounded = _inproj_qkv(x1, mod, rows_per_mod, g_norm_mix[0], 3, w_in, cs, sn, gqk,
                                                      head_reduce, head_expand, t["tm_qkv"], table_blocks, side)
        if side:
            wts["mix"] = tuple(rounded)
        z, wts["wz"] = _inproj_z(hm, wts["wz"], t["tm_z"], t["tn_z"])
        return x1, z, q, kx, vx, kf, vf

    def tail(o_att, o_rnn, z, x1, mod, rows_per_mod, t):
        x2, h2 = _mix(o_att, o_rnn, z, x1, mod, rows_per_mod, g_norm_ffn2[0], *wts["mix"], 5, 6, t["tm_mix"])
        g2, wts["f2w1"], wts["f2w3"] = _ffn_up(h2, wts["f2w1"], wts["f2w3"], t["tm_up"], t["tf"])
        y, wts["f2w2"] = _ffn_down(g2, wts["f2w2"], x2, mod, rows_per_mod, 8, t["tm_down"], t["tn_down"])
        return y

    tp = _tiles(nb * seq, dff)
    tp = {k: (min(v, seq) if k.startswith("tm") else v) for k, v in tp.items()}
    cs_p, sn_p = _rope_tables(inv_row, 0, seq, tp["tm_qkv"])
    x1, z, q, kx, vx, kf, vf = trunk(x_prompt.reshape(nb * seq, d), mod_p, seq, cs_p, sn_p,
                                     seq // tp["tm_qkv"], tp)
    nblk = seq // WINDOW
    o_att = _attn(sink_vec, q, kx, vx, nb, nblk, next(n for n in (4, 2, 1) if nblk % n == 0))
    o_rnn, h_p, conv_p = _rnn_prompt(z, nb, seq, conv_w[0], conv_b[0], wrg, wig, b_rg[0], b_ig[0], lam,
                                     min(512, seq))
    y_p = tail(o_att, o_rnn, z, x1, mod_p, seq, tp)

    last = lambda a: a.reshape(nb, seq, D_KV)[:, seq - WINDOW:].reshape(1, nb, WINDOW, N_KV_HEADS, HEAD_DIM)
    k_prompt, v_prompt = last(kf), last(vf)

    ts = _tiles(ns, dff)
    cs_1, sn_1 = _rope_tables(inv_row, PAST_LEN, SUBLANES, SUBLANES)
    cs_s = jnp.broadcast_to(cs_1[0:1], (ns, LANES))
    sn_s = jnp.broadcast_to(sn_1[0:1], (ns, LANES))
    x1s, zs, qs, kxs, vxs, kfs, vfs = trunk(x_sample.reshape(ns, d), mod_s, 1, cs_s, sn_s, 1, ts)
    nsub = 2 * SUBLANES
    assert ns % nsub == 0
    key_minor = lambda c: jnp.transpose(c[0], (0, 2, 3, 1)).reshape(ns * D_KV, WINDOW)
    o_att_s = _attn_step(sink_vec, qs, kfs, vfs, key_minor(cache_k), key_minor(cache_v), nsub)
    conv_s_in = state_conv[0]
    o_rnn_s, h_s = _rnn_step(zs, conv_s_in, state_h[0], conv_w[0], conv_b[0], wrg, wig, b_rg[0], b_ig[0], lam)
    y_s = tail(o_att_s, o_rnn_s, zs, x1s, mod_s, 1, ts)

    k_sample = kfs.reshape(1, ns, 1, N_KV_HEADS, HEAD_DIM)
    v_sample = vfs.reshape(1, ns, 1, N_KV_HEADS, HEAD_DIM)
    conv_sample = jnp.concatenate([conv_s_in[:, 1:], zs[:, None, Z_RX:Z_RX + D_RNN]], axis=1)[None]

    return (y_p.reshape(nb, seq, d), y_s.reshape(ns, 1, d), k_prompt, v_prompt, k_sample, v_sample,
            h_p.reshape(1, nb, D_RNN), h_s[None], conv_p[None], conv_sample)
```

```python
import functools

import jax
import jax.numpy as jnp
from jax import lax
from jax.experimental import pallas as pl
from jax.experimental.pallas import tpu as pltpu

F32 = jnp.float32
BF16 = jnp.bfloat16

D_MODEL = 2048
HEAD_DIM = 64
N_Q_HEADS = 16
N_KV_HEADS = 4
Q_PER_KV = N_Q_HEADS // N_KV_HEADS
D_ATTN = N_Q_HEADS * HEAD_DIM
D_KV = N_KV_HEADS * HEAD_DIM
D_QK = D_ATTN + D_KV
D_QKV = D_QK + D_KV
WINDOW = 128
ROPE_THETA = 10000.0
D_RNN = 1024
N_RNN_BLOCKS = 16
RNN_BLOCK = D_RNN // N_RNN_BLOCKS
CONV_WIDTH = 4
LRU_C = 8.0
N_MOD = 9
EPS = 1e-6
PAST_LEN = 16384

LANES = 128
SUBLANES = 8
MXU_DIM = 256
VMEM_LIMIT = 56 * 1024 * 1024

Z_RX = 0
Z_RG = Z_RX + D_RNN
Z_GATT = Z_RG + D_RNN
Z_GRNN = Z_GATT + D_MODEL
D_Z = Z_GRNN + D_MODEL
DX = 2 * N_KV_HEADS * LANES


def _params(*sem):
    return pltpu.CompilerParams(dimension_semantics=sem, vmem_limit_bytes=VMEM_LIMIT)


def _silu(x):
    return x * jax.nn.sigmoid(x)


def _modulate(x, g, shift, scale):
    ms = jnp.mean(x * x, axis=-1, keepdims=True)
    return (x * lax.rsqrt(ms + EPS) * g) * (1.0 + scale) + shift


def _mod_spec(mod, tm, rows_per_mod, width, col, ngrid):
    colf = col if callable(col) else (lambda *idx: col)
    if mod.ndim == 3:
        return pl.BlockSpec((None, 1, width), lambda *idx: ((idx[ngrid - 1] * tm) // rows_per_mod, 0, colf(*idx)))
    return pl.BlockSpec((tm, width), lambda *idx: (idx[ngrid - 1], colf(*idx)))


def _ada_kernel(c_ref, w_ref, b_ref, o_ref):
    s = _silu(c_ref[...]).astype(BF16)
    o_ref[...] = jnp.dot(s, w_ref[...].astype(BF16), preferred_element_type=F32) + b_ref[...]


def _ada(c_all, w_ada, b_ada, tn):
    rows, d = c_all.shape
    n = w_ada.shape[1]
    return pl.pallas_call(
        _ada_kernel,
        grid=(n // tn,),
        in_specs=[pl.BlockSpec((rows, d), lambda j: (0, 0)),
                  pl.BlockSpec((d, tn), lambda j: (0, j)),
                  pl.BlockSpec((1, tn), lambda j: (0, j))],
        out_specs=pl.BlockSpec((rows, tn), lambda j: (0, j)),
        out_shape=jax.ShapeDtypeStruct((rows, n), F32),
        compiler_params=_params("arbitrary"),
        name="ada",
    )(c_all, w_ada, b_ada.reshape(1, n))


ROPE_SPAN = 64


def _rope_kernel(inv_ref, cs_ref, sn_ref, *, base, blk):
    r = pl.program_id(0)
    inv = inv_ref[...]
    lane = lax.broadcasted_iota(jnp.int32, (1, LANES), 1)
    sign = jnp.where((lane & (HEAD_DIM - 1)) < HEAD_DIM // 2, -1.0, 1.0)
    rows = lambda n, scale: (scale * lax.broadcasted_iota(jnp.int32, (n, LANES), 0)).astype(F32)
    if blk % ROPE_SPAN:
        ang = (rows(blk, 1) + (base + r * blk).astype(F32)) * inv
        cs_ref[...] = jnp.cos(ang)
        sn_ref[...] = jnp.sin(ang) * sign
        return
    ncoarse = blk // ROPE_SPAN
    coarse = (rows(max(ncoarse, SUBLANES), ROPE_SPAN) + (base + r * blk).astype(F32)) * inv
    fine = rows(ROPE_SPAN, 1) * inv
    ca, sa, cb, sb = jnp.cos(coarse), jnp.sin(coarse), jnp.cos(fine), jnp.sin(fine)
    for i in range(ncoarse):
        span = slice(i * ROPE_SPAN, (i + 1) * ROPE_SPAN)
        cs_ref[span, :] = ca[i:i + 1] * cb - sa[i:i + 1] * sb
        sn_ref[span, :] = (sa[i:i + 1] * cb + ca[i:i + 1] * sb) * sign


def _rope_tables(inv_row, base, rows, blk):
    out = jax.ShapeDtypeStruct((rows, LANES), F32)
    return pl.pallas_call(
        functools.partial(_rope_kernel, base=base, blk=blk),
        grid=(rows // blk,),
        in_specs=[pl.BlockSpec((1, LANES), lambda r: (0, 0))],
        out_specs=[pl.BlockSpec((blk, LANES), lambda r: (r, 0))] * 2,
        out_shape=[out, out],
        compiler_params=_params("arbitrary"),
        name="rope_tables",
    )(inv_row)


def _modulate_kernel(x_ref, shift_ref, scale_ref, g_ref, h_ref):
    group = 2 * SUBLANES
    per_row = shift_ref.shape[0] > 1

    def body(r, carry):
        rows = pl.ds(pl.multiple_of(r * group, group), group)
        shift = shift_ref[rows, :] if per_row else shift_ref[...]
        scale = scale_ref[rows, :] if per_row else scale_ref[...]
        h_ref[rows, :] = _modulate(x_ref[rows, :], g_ref[...], shift, scale).astype(BF16)
        return carry

    lax.fori_loop(0, x_ref.shape[0] // group, body, 0, unroll=8)


def _modulated_norm(x, mod, rows_per_mod, g, k_shift, tm):
    m_rows, d = x.shape
    return pl.pallas_call(
        _modulate_kernel,
        grid=(m_rows // tm,),
        in_specs=[pl.BlockSpec((tm, d), lambda m: (m, 0)),
                  _mod_spec(mod, tm, rows_per_mod, d, k_shift, 1),
                  _mod_spec(mod, tm, rows_per_mod, d, k_shift + 1, 1),
                  pl.BlockSpec((1, d), lambda m: (0, 0))],
        out_specs=pl.BlockSpec((tm, d), lambda m: (m, 0)),
        out_shape=jax.ShapeDtypeStruct((m_rows, d), BF16),
        compiler_params=_params("arbitrary"),
        name="modnorm",
    )(x, mod, mod, g.reshape(1, d))


HBM = pl.BlockSpec(memory_space=pl.ANY)


def _is_param(w):
    return w.dtype == F32


def _stream_weight_tiles(weights, tn, first):
    t, nt = pl.program_id(0), pl.num_programs(0)
    m, nm = pl.program_id(1), pl.num_programs(1)
    cols = lambda tile: pl.ds(pl.multiple_of(tile * tn, LANES), tn)
    fetch = lambda w_hbm, stage_ref, sem, tile: pltpu.make_async_copy(
        w_hbm.at[0, :, cols(first + tile)], stage_ref, sem.at[0])
    put = lambda wb_ref, wout_hbm, sem, tile: pltpu.make_async_copy(wb_ref, wout_hbm.at[:, cols(tile)], sem.at[1])

    @pl.when(m == 0)
    def _():
        for w_hbm, stage_ref, wb_ref, wout_hbm, sem in weights:
            @pl.when(t == 0)
            def _():
                fetch(w_hbm, stage_ref, sem, 0).start()

            fetch(w_hbm, stage_ref, sem, t).wait()

            @pl.when(t > 0)
            def _():
                put(wb_ref, wout_hbm, sem, t - 1).wait()

            wb_ref[...] = stage_ref[...].astype(BF16)
            put(wb_ref, wout_hbm, sem, t).start()

            @pl.when(t + 1 < nt)
            def _():
                fetch(w_hbm, stage_ref, sem, t + 1).start()

    @pl.when((t == nt - 1) & (m == nm - 1))
    def _():
        for w_hbm, stage_ref, wb_ref, wout_hbm, sem in weights:
            put(wb_ref, wout_hbm, sem, t).wait()


def _weight_scratch(rows, tn):
    return [pltpu.VMEM((rows, tn), F32), pltpu.VMEM((rows, tn), BF16), pltpu.SemaphoreType.DMA((2,))]


def _row_groups(tm):
    n = max(1, tm // 1024)
    return [slice(r * (tm // n), (r + 1) * (tm // n)) for r in range(n)]


def _ffn_up_rows(h_ref, w1b_ref, w3b_ref, g_ref):
    for rows in _row_groups(h_ref.shape[0]):
        h = h_ref[rows, :]
        a = jnp.dot(h, w1b_ref[...], preferred_element_type=F32)
        b = jnp.dot(h, w3b_ref[...], preferred_element_type=F32)
        g_ref[rows, :] = (_silu(a) * b).astype(BF16)


def _ffn_up_param_kernel(h_ref, w1_hbm, w3_hbm, g_ref, w1o_hbm, w3o_hbm,
                         s1_ref, w1b_ref, sem1, s3_ref, w3b_ref, sem3, *, tf):
    _stream_weight_tiles(((w1_hbm, s1_ref, w1b_ref, w1o_hbm, sem1), (w3_hbm, s3_ref, w3b_ref, w3o_hbm, sem3)),
                         tf, 0)
    _ffn_up_rows(h_ref, w1b_ref, w3b_ref, g_ref)


def _ffn_up(h, w1, w3, tm, tf):
    m_rows, d = h.shape
    dff = w1.shape[-1]
    grid = (dff // tf, m_rows // tm)
    h_spec = pl.BlockSpec((tm, d), lambda f, m: (m, 0))
    g_spec = pl.BlockSpec((tm, tf), lambda f, m: (m, f))
    g_shape = jax.ShapeDtypeStruct((m_rows, dff), BF16)
    if not _is_param(w1):
        w_spec = pl.BlockSpec((d, tf), lambda f, m: (0, f))
        g = pl.pallas_call(
            _ffn_up_rows, grid=grid, in_specs=[h_spec, w_spec, w_spec], out_specs=g_spec, out_shape=g_shape,
            compiler_params=_params("arbitrary", "arbitrary"), name="ffn_up_rows",
        )(h, w1, w3)
        return g, w1, w3
    w_shape = jax.ShapeDtypeStruct((d, dff), BF16)
    return pl.pallas_call(
        functools.partial(_ffn_up_param_kernel, tf=tf),
        grid=grid,
        in_specs=[h_spec, HBM, HBM],
        out_specs=[g_spec, HBM, HBM],
        out_shape=[g_shape, w_shape, w_shape],
        scratch_shapes=_weight_scratch(d, tf) + _weight_scratch(d, tf),
        compiler_params=_params("arbitrary", "arbitrary"),
        name="ffn_up",
    )(h, w1, w3)


def _ffn_down_rows(g_ref, w2b_ref, x_ref, gate_ref, o_ref):
    for rows in _row_groups(g_ref.shape[0]):
        acc = jnp.dot(g_ref[rows, :], w2b_ref[...], preferred_element_type=F32)
        gate = gate_ref[...] if gate_ref.shape[0] == 1 else gate_ref[rows, :]
        o_ref[rows, :] = x_ref[rows, :] + 0.5 * gate * acc


def _ffn_down_param_kernel(g_ref, w2_hbm, x_ref, gate_ref, o_ref, w2o_hbm, s2_ref, w2b_ref, sem2, *, tn):
    _stream_weight_tiles(((w2_hbm, s2_ref, w2b_ref, w2o_hbm, sem2),), tn, 0)
    _ffn_down_rows(g_ref, w2b_ref, x_ref, gate_ref, o_ref)


def _ffn_down(g, w2, x, mod, rows_per_mod, k_gate, tm, tn):
    m_rows, d = x.shape
    dff = g.shape[1]
    grid = (d // tn, m_rows // tm)
    gate_col = lambda n, m: k_gate * (d // tn) + n
    g_spec = pl.BlockSpec((tm, dff), lambda n, m: (m, 0))
    x_spec = pl.BlockSpec((tm, tn), lambda n, m: (m, n))
    gate_spec = _mod_spec(mod, tm, rows_per_mod, tn, gate_col, 2)
    o_shape = jax.ShapeDtypeStruct((m_rows, d), F32)
    if not _is_param(w2):
        out = pl.pallas_call(
            _ffn_down_rows, grid=grid,
            in_specs=[g_spec, pl.BlockSpec((dff, tn), lambda n, m: (0, n)), x_spec, gate_spec],
            out_specs=x_spec, out_shape=o_shape,
            compiler_params=_params("arbitrary", "arbitrary"), name="ffn_down_rows",
        )(g, w2, x, mod)
        return out, w2
    return pl.pallas_call(
        functools.partial(_ffn_down_param_kernel, tn=tn),
        grid=grid,
        in_specs=[g_spec, HBM, x_spec, gate_spec],
        out_specs=[x_spec, HBM],
        out_shape=[o_shape, jax.ShapeDtypeStruct((dff, d), BF16)],
        scratch_shapes=_weight_scratch(dff, tn),
        compiler_params=_params("arbitrary", "arbitrary"),
        name="ffn_down",
    )(g, w2, x, mod)


def _inproj_z_rows(h_ref, wb_ref, z_ref):
    for rows in _row_groups(h_ref.shape[0]):
        z_ref[rows, :] = jnp.dot(h_ref[rows, :], wb_ref[...], preferred_element_type=F32)


def _inproj_z_param_kernel(h_ref, w_hbm, z_ref, wo_hbm, s_ref, wb_ref, sem, *, tn, first):
    _stream_weight_tiles(((w_hbm, s_ref, wb_ref, wo_hbm, sem),), tn, first)
    _inproj_z_rows(h_ref, wb_ref, z_ref)


def _inproj_z(h, w, tm, tn):
    m_rows, d = h.shape
    grid = (D_Z // tn, m_rows // tm)
    h_spec = pl.BlockSpec((tm, d), lambda n, m: (m, 0))
    z_spec = pl.BlockSpec((tm, tn), lambda n, m: (m, n))
    z_shape = jax.ShapeDtypeStruct((m_rows, D_Z), F32)
    if not _is_param(w):
        z = pl.pallas_call(
            _inproj_z_rows, grid=grid, in_specs=[h_spec, pl.BlockSpec((d, tn), lambda n, m: (0, n))],
            out_specs=z_spec, out_shape=z_shape,
            compiler_params=_params("arbitrary", "arbitrary"), name="inproj_z_rows",
        )(h, w)
        return z, w
    return pl.pallas_call(
        functools.partial(_inproj_z_param_kernel, tn=tn, first=D_QKV // tn),
        grid=grid,
        in_specs=[h_spec, HBM],
        out_specs=[z_spec, HBM],
        out_shape=[z_shape, jax.ShapeDtypeStruct((d, D_Z), BF16)],
        scratch_shapes=_weight_scratch(d, tn),
        compiler_params=_params("arbitrary", "arbitrary"),
        name="inproj_z",
    )(h, w)


def _store_head_pairs(ref, rows, chunk, c):
    lo = lax.broadcasted_iota(jnp.int32, chunk.shape, 1) < HEAD_DIM
    swapped = pltpu.roll(chunk, HEAD_DIM, 1)
    zero = jnp.zeros_like(chunk)
    cols = (jnp.where(lo, chunk, zero), jnp.where(lo, zero, swapped),
            jnp.where(lo, swapped, zero), jnp.where(lo, zero, chunk))
    for i, col in enumerate(cols):
        ref[rows, (4 * c + i) * LANES:(4 * c + i + 1) * LANES] = col.astype(ref.dtype)


def _side_chunk_rows(side, steps):
    total = sum(w.shape[1] for w in side)
    rows = total // steps
    ok = rows * steps == total and rows % (2 * SUBLANES) == 0 and all(w.shape[1] % rows == 0 for w in side)
    return rows if ok else None


def _round_side_matrices(mats, stage_ref, ostage_ref, sem):
    m = pl.program_id(0)
    nrows = stage_ref.shape[1]
    chunks = [(w, wb, r0) for w, wb in mats for r0 in range(0, w.shape[1], nrows)]

    def fetch(c):
        w, _, r0 = chunks[c]
        return pltpu.make_async_copy(w.at[0, pl.ds(r0, nrows), :], stage_ref.at[c % 2], sem.at[c % 2])

    def put(c):
        _, wb, r0 = chunks[c]
        return pltpu.make_async_copy(ostage_ref.at[c % 2], wb.at[pl.ds(r0, nrows), :], sem.at[2 + c % 2])

    last = len(chunks) - 1
    for c in range(len(chunks)):
        @pl.when(m == c)
        def _():
            if c == 0:
                fetch(0).start()
            if c < last:
                fetch(c + 1).start()
            fetch(c).wait()
            if c >= 2:
                put(c - 2).wait()

    ostage_ref[m % 2] = stage_ref[m % 2].astype(BF16)

    def finish():
        for c in range(len(chunks)):
            @pl.when(m == c)
            def _():
                put(c).start()
                if c == last:
                    if c >= 1:
                        put(c - 1).wait()
                    put(c).wait()

    return finish


def _inproj_qkv_kernel(*refs, nsplit, nside):
    (x_ref, shift_ref, scale_ref, g_ref, w_ref, cs_ref, sn_ref, gqk_ref, red_ref, exp_ref) = refs[:10]
    side_in = refs[10:10 + nside]
    h_ref, q_ref, kx_ref, vx_ref, kf_ref, vf_ref = refs[10 + nside:16 + nside]
    side_out = refs[16 + nside:16 + 2 * nside]
    wb_ref = refs[16 + 2 * nside]

    @pl.when(pl.program_id(0) == 0)
    def _():
        wb_ref[...] = w_ref[...].astype(BF16)

    finish_side = _round_side_matrices(list(zip(side_in, side_out)), *refs[17 + 2 * nside:]) if nside else None

    step = x_ref.shape[0] // nsplit
    for r in range(nsplit):
        rows = slice(r * step, (r + 1) * step)
        shift = shift_ref[...] if shift_ref.shape[0] == 1 else shift_ref[rows, :]
        scale = scale_ref[...] if scale_ref.shape[0] == 1 else scale_ref[rows, :]
        h = _modulate(x_ref[rows, :], g_ref[...], shift, scale).astype(BF16)
        h_ref[rows, :] = h
        acc = jnp.dot(h, wb_ref[...], preferred_element_type=F32)
        qk = acc[:, :D_QK]
        hm = jnp.dot((qk * qk).astype(BF16), red_ref[...], preferred_element_type=F32)
        hi = hm.astype(BF16)
        lo = (hm - hi.astype(F32)).astype(BF16)
        ms = jnp.dot(jnp.concatenate([hi, lo], axis=1), exp_ref[...], preferred_element_type=F32)
        y = qk * lax.rsqrt(ms + EPS) * gqk_ref[...]
        cs = cs_ref[rows, :]
        sn = sn_ref[rows, :]
        lane = lax.broadcasted_iota(jnp.int32, cs.shape, 1)
        first_half = (lane & (HEAD_DIM - 1)) < HEAD_DIM // 2
        for c in range(D_QK // LANES):
            yc = y[:, c * LANES:(c + 1) * LANES]
            partner = jnp.where(first_half,
                                pltpu.roll(yc, LANES - HEAD_DIM // 2, 1),
                                pltpu.roll(yc, HEAD_DIM // 2, 1))
            rot = yc * cs + partner * sn
            if c < D_ATTN // LANES:
                q_ref[rows, c * LANES:(c + 1) * LANES] = (rot * (HEAD_DIM ** -0.5)).astype(BF16)
            else:
                kf_ref[rows, c * LANES - D_ATTN:(c + 1) * LANES - D_ATTN] = rot
                _store_head_pairs(kx_ref, rows, rot, c - D_ATTN // LANES)
        v = acc[:, D_QK:]
        vf_ref[rows, :] = v
        for c in range(D_KV // LANES):
            _store_head_pairs(vx_ref, rows, v[:, c * LANES:(c + 1) * LANES], c)
    if finish_side is not None:
        finish_side()


def _inproj_qkv(x, mod, rows_per_mod, g, k_shift, w_in, cs, sn, gqk, head_reduce, head_expand, tm, table_blocks,
                side=()):
    m_rows, d = x.shape
    steps = m_rows // tm
    row = lambda width: pl.BlockSpec((tm, width), lambda m: (m, 0))
    table = pl.BlockSpec((tm, LANES), lambda m: (m % table_blocks, 0))
    nsplit = 2 if tm % (2 * MXU_DIM) == 0 else 1
    side_scratch = []
    if side:
        width = side[0].shape[2]
        chunk = _side_chunk_rows(side, steps)
        assert chunk is not None and all(w.shape[2] == width for w in side)
        side_scratch = [pltpu.VMEM((2, chunk, width), F32), pltpu.VMEM((2, chunk, width), BF16),
                        pltpu.SemaphoreType.DMA((4,))]
    return pl.pallas_call(
        functools.partial(_inproj_qkv_kernel, nsplit=nsplit, nside=len(side)),
        grid=(steps,),
        in_specs=[row(d),
                  _mod_spec(mod, tm, rows_per_mod, d, k_shift, 1),
                  _mod_spec(mod, tm, rows_per_mod, d, k_shift + 1, 1),
                  pl.BlockSpec((1, d), lambda m: (0, 0)),
                  pl.BlockSpec((None, d, D_QKV), lambda m: (0, 0, 0), pipeline_mode=pl.Buffered(1)),
                  table, table,
                  pl.BlockSpec((1, D_QK), lambda m: (0, 0)),
                  pl.BlockSpec(head_reduce.shape, lambda m: (0, 0)),
                  pl.BlockSpec(head_expand.shape, lambda m: (0, 0))] + [HBM] * len(side),
        out_specs=[row(d), row(D_ATTN), row(DX), row(DX), row(D_KV), row(D_KV)] + [HBM] * len(side),
        out_shape=[jax.ShapeDtypeStruct((m_rows, d), BF16),
                   jax.ShapeDtypeStruct((m_rows, D_ATTN), BF16),
                   jax.ShapeDtypeStruct((m_rows, DX), BF16),
                   jax.ShapeDtypeStruct((m_rows, DX), BF16),
                   jax.ShapeDtypeStruct((m_rows, D_KV), F32),
                   jax.ShapeDtypeStruct((m_rows, D_KV), F32)]
        + [jax.ShapeDtypeStruct(w.shape[1:], BF16) for w in side],
        scratch_shapes=[pltpu.VMEM((d, D_QKV), BF16)] + side_scratch,
        compiler_params=_params("arbitrary"),
        name="inproj_qkv",
    )(x, mod, mod, g.reshape(1, d), w_in, cs, sn, gqk, head_reduce, head_expand, *side)


def _attn_bias(bias_ref, tq, past_off):
    nk = 2 * WINDOW
    ri = lax.broadcasted_iota(jnp.int32, (2 * tq, 2 * nk), 0) & (tq - 1)
    kj = lax.broadcasted_iota(jnp.int32, (2 * tq, 2 * nk), 1) & (nk - 1)
    visible = ((kj < WINDOW) & (kj > ri + past_off)) | ((kj >= WINDOW) & (kj - WINDOW <= ri))
    bias_ref[...] = jnp.where(visible, 0.0, -jnp.inf)


def _attn_blocks(sink_ref, bias_of, q_ref, past_of, kc_ref, vc_ref, nsub, tq, store):
    nk = 2 * WINDOW
    rows = 2 * tq
    first_rows = lax.broadcasted_iota(jnp.int32, (rows, 1), 0) < tq
    lo_lanes = lax.broadcasted_iota(jnp.int32, (rows, LANES), 1) < HEAD_DIM
    ones_lo = (lax.broadcasted_iota(jnp.int32, (nk, LANES), 1) < HEAD_DIM).astype(BF16)
    ones_hi = (1 - ones_lo.astype(F32)).astype(BF16)
    nt = (((1,), (1,)), ((), ()))

    def keys(which, s, col):
        cols = slice(col * LANES, (col + 1) * LANES)
        past = past_of(s)
        return [past[which][past[2]:past[2] + WINDOW, cols], (kc_ref, vc_ref)[which][s * tq:(s + 1) * tq, cols]]

    for s in range(nsub):
        for g in range(N_KV_HEADS):
            c0, c1 = 2 * g, 2 * g + 1
            q4 = jnp.concatenate([q_ref[s * tq:(s + 1) * tq, c0 * LANES:(c0 + 1) * LANES],
                                  q_ref[s * tq:(s + 1) * tq, c1 * LANES:(c1 + 1) * LANES]], axis=0)
            kk = jnp.concatenate(keys(0, s, c0) + keys(0, s, c1), axis=0)
            sc = lax.dot_general(q4, kk, nt, preferred_element_type=F32) + bias_of(s)[...]
            sk_lo = jnp.where(first_rows, sink_ref[4 * g], sink_ref[4 * g + 2])
            sk_hi = jnp.where(first_rows, sink_ref[4 * g + 1], sink_ref[4 * g + 3])
            m_lo = jnp.maximum(jnp.max(sc[:, :nk], axis=-1, keepdims=True), sk_lo)
            m_hi = jnp.maximum(jnp.max(sc[:, nk:], axis=-1, keepdims=True), sk_hi)
            p = jnp.concatenate([jnp.exp(sc[:, :nk] - m_lo), jnp.exp(sc[:, nk:] - m_hi)], axis=1).astype(BF16)
            vv = jnp.concatenate(
                [jnp.concatenate([jnp.concatenate(keys(1, s, c0), axis=0), ones_lo], axis=1),
                 jnp.concatenate([jnp.concatenate(keys(1, s, c1), axis=0), ones_hi], axis=1)],
                axis=0)
            o = jnp.dot(p, vv, preferred_element_type=F32)
            denom = o[:, LANES:] + jnp.where(lo_lanes, jnp.exp(sk_lo - m_lo), jnp.exp(sk_hi - m_hi))
            store(s, c0, c1, o[:, :LANES] / denom)


def _attn_kernel(sink_ref, q_ref, kp_ref, kc_ref, vp_ref, vc_ref, o_ref, bias0_ref, bias_ref, *, nq):
    tq = WINDOW
    _attn_bias(bias0_ref, tq, jnp.where(pl.program_id(1) > 0, 0, WINDOW))
    if nq > 1:
        _attn_bias(bias_ref, tq, 0)

    def store(s, c0, c1, out):
        o_ref[s * tq:(s + 1) * tq, c0 * LANES:(c0 + 1) * LANES] = out[:tq].astype(BF16)
        o_ref[s * tq:(s + 1) * tq, c1 * LANES:(c1 + 1) * LANES] = out[tq:].astype(BF16)

    past_of = lambda s: (kp_ref, vp_ref, 0) if s == 0 else (kc_ref, vc_ref, (s - 1) * WINDOW)
    bias_of = lambda s: bias0_ref if s == 0 else bias_ref
    _attn_blocks(sink_ref, bias_of, q_ref, past_of, kc_ref, vc_ref, nq, tq, store)


def _attn(sinks, q, kx, vx, nbatch, nblk, nq):
    steps = nblk // nq
    cur = lambda b, n: (b * steps + n, 0)
    past = lambda b, n: (b * nblk + jnp.maximum(n * nq - 1, 0), 0)
    kv_cur = pl.BlockSpec((nq * WINDOW, DX), cur)
    kv_past = pl.BlockSpec((WINDOW, DX), past)
    bias = pltpu.VMEM((2 * WINDOW, 4 * WINDOW), F32)
    return pl.pallas_call(
        functools.partial(_attn_kernel, nq=nq),
        grid=(nbatch, steps),
        in_specs=[pl.BlockSpec(memory_space=pltpu.SMEM), pl.BlockSpec((nq * WINDOW, D_ATTN), cur),
                  kv_past, kv_cur, kv_past, kv_cur],
        out_specs=pl.BlockSpec((nq * WINDOW, D_ATTN), cur),
        out_shape=jax.ShapeDtypeStruct(q.shape, BF16),
        scratch_shapes=[bias, bias],
        compiler_params=_params("arbitrary", "arbitrary"),
        name="attn",
    )(sinks, q, kx, kx, vx, vx)


def _rows_from_lanes(row, nrows):
    out_rows = max(nrows, SUBLANES)
    rep = jnp.broadcast_to(row, (out_rows, row.shape[1]))
    ridx = lax.broadcasted_iota(jnp.int32, (out_rows, HEAD_DIM), 0)
    out = jnp.zeros((out_rows, HEAD_DIM), row.dtype)
    for i in range(nrows):
        out = jnp.where(ridx == i, rep[:, i * HEAD_DIM:(i + 1) * HEAD_DIM], out)
    return out


def _attn_step_kernel(sink_ref, q_ref, kn_ref, vn_ref, ck_ref, cv_ref, o_ref, *, nsub):
    grp = lax.broadcasted_iota(jnp.int32, (N_Q_HEADS, HEAD_DIM), 0) >> 2
    bias = jnp.where(lax.broadcasted_iota(jnp.int32, (N_Q_HEADS, WINDOW), 1) >= 1, 0.0, -jnp.inf)
    hcol = lax.broadcasted_iota(jnp.int32, (N_Q_HEADS, 1), 0)
    sk = jnp.zeros((N_Q_HEADS, 1), F32)
    for h in range(N_Q_HEADS):
        sk = jnp.where(hcol == h, sink_ref[h], sk)
    nt = (((1,), (1,)), ((), ()))
    qf = q_ref[...].astype(F32)

    def own_group(x):
        out = jnp.zeros((N_Q_HEADS, HEAD_DIM), F32)
        for g in range(N_KV_HEADS):
            out = jnp.where(grp == g, x[:, g * HEAD_DIM:(g + 1) * HEAD_DIM], out)
        return out

    for s in range(nsub):
        qh = _rows_from_lanes(qf[s:s + 1, :], N_Q_HEADS)
        zero = jnp.zeros_like(qh)
        q_bd = jnp.concatenate([jnp.where(grp == g, qh, zero) for g in range(N_KV_HEADS)], axis=1)
        kt = ck_ref[s * D_KV:(s + 1) * D_KV, :].astype(BF16)
        vt = cv_ref[s * D_KV:(s + 1) * D_KV, :].astype(BF16)
        sc = jnp.dot(q_bd.astype(BF16), kt, preferred_element_type=F32) + bias
        s_new = jnp.sum(q_bd * kn_ref[s:s + 1, :], axis=-1, keepdims=True)
        m = jnp.maximum(jnp.maximum(jnp.max(sc, axis=-1, keepdims=True), s_new), sk)
        p = jnp.exp(sc - m)
        p_new = jnp.exp(s_new - m)
        denom = jnp.sum(p, axis=-1, keepdims=True) + p_new + jnp.exp(sk - m)
        o_all = lax.dot_general(p.astype(BF16), vt, nt, preferred_element_type=F32)
        v_new = own_group(jnp.broadcast_to(vn_ref[s:s + 1, :], (N_Q_HEADS, D_KV)))
        o = (own_group(o_all) + p_new * v_new) / denom
        for c in range(N_Q_HEADS // 2):
            o_ref[s:s + 1, c * LANES:(c + 1) * LANES] = jnp.concatenate(
                [o[2 * c:2 * c + 1, :], o[2 * c + 1:2 * c + 2, :]], axis=1)


def _attn_step(sinks, q, k_new, v_new, cache_k, cache_v, nsub):
    ns = q.shape[0]
    row = lambda width: pl.BlockSpec((nsub, width), lambda b: (b, 0))
    cache = pl.BlockSpec((nsub * D_KV, WINDOW), lambda b: (b, 0))
    return pl.pallas_call(
        functools.partial(_attn_step_kernel, nsub=nsub),
        grid=(ns // nsub,),
        in_specs=[pl.BlockSpec(memory_space=pltpu.SMEM), row(D_ATTN), row(D_KV), row(D_KV), cache, cache],
        out_specs=row(D_ATTN),
        out_shape=jax.ShapeDtypeStruct((ns, D_ATTN), F32),
        compiler_params=_params("arbitrary"),
        name="attn_step",
    )(sinks, q, k_new, v_new, cache_k, cache_v)


def _softplus(x):
    return jnp.maximum(x, 0.0) + jnp.log1p(jnp.exp(-jnp.abs(x)))


def _lru_coeffs(xc, wrg_ref, wig_ref, brg, big, lam):
    xb = xc.astype(BF16)
    ngroups = D_RNN // MXU_DIM
    rs, igs = [], []
    for c in range(ngroups):
        xg = xb[:, c * MXU_DIM:(c + 1) * MXU_DIM]
        rs.append(jnp.dot(xg, wrg_ref[c], preferred_element_type=F32))
        igs.append(jnp.dot(xg, wig_ref[c], preferred_element_type=F32))
    r = jax.nn.sigmoid(jnp.concatenate(rs, axis=1) + brg)
    ig = jax.nn.sigmoid(jnp.concatenate(igs, axis=1) + big)
    log_a = -LRU_C * r * _softplus(-lam)
    a = jnp.exp(log_a)
    one_minus_a2 = -jnp.tanh(log_a) * (1.0 + a * a)
    u = jnp.sqrt(one_minus_a2) * (ig * xc)
    return a, u


def _rnn_prompt_kernel(rx_ref, rg_ref, cw_ref, cb_ref, wrg_ref, wig_ref, brg_ref, big_ref, lam_ref,
                       o_ref, hlast_ref, conv_ref, xs_ref, a_ref, h_ref, carry_ref, *, tc):
    t = pl.program_id(1)
    pad = SUBLANES

    @pl.when(t == 0)
    def _():
        xs_ref[0:pad, :] = jnp.zeros((pad, D_RNN), F32)
        carry_ref[...] = jnp.zeros_like(carry_ref)

    x = rx_ref[...]
    xs_ref[pad:pad + tc, :] = x
    xc = cb_ref[...] + x * cw_ref[CONV_WIDTH - 1:CONV_WIDTH, :]
    for k in range(1, CONV_WIDTH):
        xc = xc + xs_ref[pad - k:pad - k + tc, :] * cw_ref[CONV_WIDTH - 1 - k:CONV_WIDTH - k, :]
    tail = xs_ref[tc:tc + pad, :]
    xs_ref[0:pad, :] = tail
    conv_ref[...] = tail[pad - (CONV_WIDTH - 1):, :]

    a, u = _lru_coeffs(xc, wrg_ref, wig_ref, brg_ref[...], big_ref[...], lam_ref[...])
    a_ref[...] = a
    h_ref[...] = u

    row = lax.broadcasted_iota(jnp.int32, (SUBLANES, D_RNN), 0)

    def body(r, carry):
        off = pl.multiple_of(r * SUBLANES, SUBLANES)
        av = a_ref[pl.ds(off, SUBLANES), :]
        hv = h_ref[pl.ds(off, SUBLANES), :]
        for sft in (1, 2, 4):
            keep = row >= sft
            a_sh = jnp.where(keep, pltpu.roll(av, sft, 0), 1.0)
            h_sh = jnp.where(keep, pltpu.roll(hv, sft, 0), 0.0)
            hv = av * h_sh + hv
            av = av * a_sh
        hv = hv + av * carry
        h_ref[pl.ds(off, SUBLANES), :] = hv
        return jnp.broadcast_to(hv[SUBLANES - 1:SUBLANES, :], (SUBLANES, D_RNN))

    carry = lax.fori_loop(0, tc // SUBLANES, body, carry_ref[...])
    carry_ref[...] = carry
    hlast_ref[...] = carry[0:1, :]
    o_ref[...] = (h_ref[...] * jax.nn.gelu(rg_ref[...])).astype(BF16)


def _rnn_prompt(z, nbatch, seq, conv_w, conv_b, wrg, wig, b_rg, b_ig, lam, tc):
    nchunk = seq // tc
    rx_blk = Z_RX // D_RNN
    rg_blk = Z_RG // D_RNN
    vec = pl.BlockSpec((1, D_RNN), lambda b, t: (0, 0))
    wspec = pl.BlockSpec(wrg.shape, lambda b, t: (0, 0, 0))
    return pl.pallas_call(
        functools.partial(_rnn_prompt_kernel, tc=tc),
        grid=(nbatch, nchunk),
        in_specs=[pl.BlockSpec((tc, D_RNN), lambda b, t: (b * nchunk + t, rx_blk)),
                  pl.BlockSpec((tc, D_RNN), lambda b, t: (b * nchunk + t, rg_blk)),
                  pl.BlockSpec((CONV_WIDTH, D_RNN), lambda b, t: (0, 0)),
                  vec, wspec, wspec, vec, vec, vec],
        out_specs=[pl.BlockSpec((tc, D_RNN), lambda b, t: (b * nchunk + t, 0)),
                   pl.BlockSpec((None, 1, D_RNN), lambda b, t: (b, 0, 0)),
                   pl.BlockSpec((None, CONV_WIDTH - 1, D_RNN), lambda b, t: (b, 0, 0))],
        out_shape=[jax.ShapeDtypeStruct((nbatch * seq, D_RNN), BF16),
                   jax.ShapeDtypeStruct((nbatch, 1, D_RNN), F32),
                   jax.ShapeDtypeStruct((nbatch, CONV_WIDTH - 1, D_RNN), F32)],
        scratch_shapes=[pltpu.VMEM((tc + SUBLANES, D_RNN), F32),
                        pltpu.VMEM((tc, D_RNN), F32),
                        pltpu.VMEM((tc, D_RNN), F32),
                        pltpu.VMEM((SUBLANES, D_RNN), F32)],
        compiler_params=_params("arbitrary", "arbitrary"),
        name="rnn_prompt",
    )(z, z, conv_w, conv_b.reshape(1, D_RNN), wrg, wig, b_rg.reshape(1, D_RNN),
      b_ig.reshape(1, D_RNN), lam.reshape(1, D_RNN))


def _rnn_step_kernel(rx_ref, rg_ref, c0_ref, c1_ref, c2_ref, h0_ref, cw_ref, cb_ref, wrg_ref, wig_ref,
                     brg_ref, big_ref, lam_ref, o_ref, h_ref):
    x = rx_ref[...]
    xc = (cb_ref[...] + c0_ref[...] * cw_ref[0:1, :] + c1_ref[...] * cw_ref[1:2, :]
          + c2_ref[...] * cw_ref[2:3, :] + x * cw_ref[3:4, :])
    a, u = _lru_coeffs(xc, wrg_ref, wig_ref, brg_ref[...], big_ref[...], lam_ref[...])
    h = a * h0_ref[...] + u
    h_ref[...] = h
    o_ref[...] = (h * jax.nn.gelu(rg_ref[...])).astype(BF16)


def _rnn_step(z, conv_state, h0, conv_w, conv_b, wrg, wig, b_rg, b_ig, lam):
    rows = z.shape[0]
    full = lambda shape: pl.BlockSpec(shape, lambda i: (0,) * len(shape))
    act = full((rows, D_RNN))
    vec = full((1, D_RNN))
    return pl.pallas_call(
        _rnn_step_kernel,
        grid=(1,),
        in_specs=[pl.BlockSpec((rows, D_RNN), lambda i: (0, Z_RX // D_RNN)),
                  pl.BlockSpec((rows, D_RNN), lambda i: (0, Z_RG // D_RNN)),
                  act, act, act, act, full((CONV_WIDTH, D_RNN)), vec,
                  full(wrg.shape), full(wig.shape), vec, vec, vec],
        out_specs=[act, act],
        out_shape=[jax.ShapeDtypeStruct((rows, D_RNN), BF16),
                   jax.ShapeDtypeStruct((rows, D_RNN), F32)],
        compiler_params=_params("arbitrary"),
        name="rnn_step",
    )(z, z, conv_state[:, 0], conv_state[:, 1], conv_state[:, 2], h0, conv_w,
      conv_b.reshape(1, D_RNN), wrg, wig, b_rg.reshape(1, D_RNN), b_ig.reshape(1, D_RNN),
      lam.reshape(1, D_RNN))


def _mix_kernel(oa_ref, or_ref, ga_ref, gr_ref, x_ref, gate_ref, shift_ref, scale_ref, g_ref,
                wpa_ref, wpr_ref, wout_ref, o_ref, h_ref):
    pa = jnp.dot(oa_ref[...].astype(BF16), wpa_ref[...], preferred_element_type=F32)
    pr = jnp.dot(or_ref[...], wpr_ref[...], preferred_element_type=F32)
    mix = jax.nn.sigmoid(ga_ref[...]) * pa + jax.nn.sigmoid(gr_ref[...]) * pr
    x = x_ref[...] + gate_ref[...] * jnp.dot(mix.astype(BF16), wout_ref[...], preferred_element_type=F32)
    o_ref[...] = x
    h_ref[...] = _modulate(x, g_ref[...], shift_ref[...], scale_ref[...]).astype(BF16)


def _mix(o_att, o_rnn, z, x, mod, rows_per_mod, g_next, w_pa, w_pr, w_out, k_gate, k_shift_next, tm):
    m_rows, d = x.shape
    const = lambda shape: pl.BlockSpec(shape, lambda m: (0, 0), pipeline_mode=pl.Buffered(1))
    row = lambda width, col: pl.BlockSpec((tm, width), lambda m: (m, col))
    mspec = lambda k: _mod_spec(mod, tm, rows_per_mod, d, k, 1)
    return pl.pallas_call(
        _mix_kernel,
        grid=(m_rows // tm,),
        in_specs=[row(D_ATTN, 0), row(D_RNN, 0), row(d, Z_GATT // d), row(d, Z_GRNN // d), row(d, 0),
                  mspec(k_gate), mspec(k_shift_next), mspec(k_shift_next + 1), const((1, d)),
                  const(w_pa.shape), const(w_pr.shape), const(w_out.shape)],
        out_specs=[row(d, 0), row(d, 0)],
        out_shape=[jax.ShapeDtypeStruct((m_rows, d), F32), jax.ShapeDtypeStruct((m_rows, d), BF16)],
        compiler_params=_params("arbitrary"),
        name="mix",
    )(o_att, o_rnn, z, z, x, mod, mod, mod, g_next.reshape(1, d), w_pa, w_pr, w_out)


def _block_diag(w, group):
    n, r, _ = w.shape
    eye = jnp.eye(group, dtype=w.dtype)
    wg = w.reshape(n // group, group, r, r)
    return jnp.einsum("ngij,gh->ngihj", wg, eye).reshape(n // group, group * r, group * r)


def _tiles(rows, dff):
    pick = lambda n, prefs: next((t for t in prefs if n % t == 0), n)
    return dict(
        tm_norm=pick(rows, (1024, 512)),
        tm_up=pick(rows, (2048, 1024, 512)), tf=pick(dff, (512,)),
        tm_down=pick(rows, (1024, 512)), tn_down=512,
        tm_z=pick(rows, (1024, 512)), tn_z=D_QKV,
        tm_qkv=pick(rows, (512,)),
        tm_mix=pick(rows, (256,)),
    )


def kernel(x_prompt, x_sample, c_prompt, c_sample, cache_k, cache_v, state_h, state_conv, w_ada, b_ada,
           g_norm_ffn1, g_norm_mix, g_norm_ffn2, ffn1_w1, ffn1_w3, ffn1_w2, ffn2_w1, ffn2_w3, ffn2_w2,
           w_in, g_q, g_k, sinks, conv_w, conv_b, w_rg, b_rg, w_ig, b_ig, lru_lambda, w_pa, w_pr, w_out):
    nb, seq, d = x_prompt.shape
    ns = x_sample.shape[0]
    dff = ffn1_w1.shape[2]
    assert d == D_MODEL and w_ada.shape[0] == 1 and x_sample.shape[1] == 1 and cache_k.shape[2] == WINDOW
    assert seq % WINDOW == 0 and w_in.shape[2] == D_QKV + D_Z

    group = MXU_DIM // RNN_BLOCK
    wrg = _block_diag(w_rg[0], group).astype(BF16)
    wig = _block_diag(w_ig[0], group).astype(BF16)
    gqk = jnp.concatenate([jnp.tile(g_q[0], N_Q_HEADS), jnp.tile(g_k[0], N_KV_HEADS)]).reshape(1, D_QK)
    head_of = jnp.arange(D_QK) // HEAD_DIM
    onehot = (head_of[:, None] == jnp.arange(LANES)[None, :]).astype(F32)
    head_reduce = (onehot / HEAD_DIM).astype(BF16)
    head_expand = jnp.concatenate([onehot.T, onehot.T], axis=0).astype(BF16)
    inv = ROPE_THETA ** (-jnp.arange(HEAD_DIM // 2, dtype=F32) * 2.0 / HEAD_DIM)
    inv_row = jnp.tile(inv, LANES // (HEAD_DIM // 2)).reshape(1, LANES)
    sink_vec = sinks[0]
    lam = lru_lambda[0]

    pad_rows = (-(nb + ns)) % SUBLANES
    c_all = jnp.concatenate([c_prompt, c_sample, jnp.zeros((pad_rows, d), F32)], axis=0)
    mod_all = _ada(c_all, w_ada[0], b_ada[0], 1024)
    mod_p = mod_all[:nb].reshape(nb, 1, N_MOD * d)
    mod_s = mod_all[nb:nb + ns]

    wts = dict(f1w1=ffn1_w1, f1w3=ffn1_w3, f1w2=ffn1_w2, f2w1=ffn2_w1, f2w3=ffn2_w3, f2w2=ffn2_w2, wz=w_in)

    def trunk(x, mod, rows_per_mod, cs, sn, table_blocks, t):
        h1 = _modulated_norm(x, mod, rows_per_mod, g_norm_ffn1[0], 0, t["tm_norm"])
        g1, wts["f1w1"], wts["f1w3"] = _ffn_up(h1, wts["f1w1"], wts["f1w3"], t["tm_up"], t["tf"])
        x1, wts["f1w2"] = _ffn_down(g1, wts["f1w2"], x, mod, rows_per_mod, 2, t["tm_down"], t["tn_down"])
        side = ()
        if "mix" not in wts:
            side = (w_pa, w_pr, w_out)
            if _side_chunk_rows(side, x.shape[0] // t["tm_qkv"]) is None:
                side, wts["mix"] = (), tuple(w[0].astype(BF16) for w in side)
        hm, q, kx, vx, kf, vf, *rounded = _inproj_qkv(x1, mod, rows_per_mod, g_norm_mix[0], 3, w_in, cs, sn, gqk,
                                                      head_reduce, head_expand, t["tm_qkv"], table_blocks, side)
        if side:
            wts["mix"] = tuple(rounded)
        z, wts["wz"] = _inproj_z(hm, wts["wz"], t["tm_z"], t["tn_z"])
        return x1, z, q, kx, vx, kf, vf

    def tail(o_att, o_rnn, z, x1, mod, rows_per_mod, t):
        x2, h2 = _mix(o_att, o_rnn, z, x1, mod, rows_per_mod, g_norm_ffn2[0], *wts["mix"], 5, 6, t["tm_mix"])
        g2, wts["f2w1"], wts["f2w3"] = _ffn_up(h2, wts["f2w1"], wts["f2w3"], t["tm_up"], t["tf"])
        y, wts["f2w2"] = _ffn_down(g2, wts["f2w2"], x2, mod, rows_per_mod, 8, t["tm_down"], t["tn_down"])
        return y

    tp = _tiles(nb * seq, dff)
    tp = {k: (min(v, seq) if k.startswith("tm") else v) for k, v in tp.items()}
    cs_p, sn_p = _rope_tables(inv_row, 0, seq, tp["tm_qkv"])
    x1, z, q, kx, vx, kf, vf = trunk(x_prompt.reshape(nb * seq, d), mod_p, seq, cs_p, sn_p,
                                     seq // tp["tm_qkv"], tp)
    nblk = seq // WINDOW
    o_att = _attn(sink_vec, q, kx, vx, nb, nblk, next(n for n in (4, 2, 1) if nblk % n == 0))
    o_rnn, h_p, conv_p = _rnn_prompt(z, nb, seq, conv_w[0], conv_b[0], wrg, wig, b_rg[0], b_ig[0], lam,
                                     min(512, seq))
    y_p = tail(o_att, o_rnn, z, x1, mod_p, seq, tp)

    last = lambda a: a.reshape(nb, seq, D_KV)[:, seq - WINDOW:].reshape(1, nb, WINDOW, N_KV_HEADS, HEAD_DIM)
    k_prompt, v_prompt = last(kf), last(vf)

    ts = _tiles(ns, dff)
    cs_1, sn_1 = _rope_tables(inv_row, PAST_LEN, SUBLANES, SUBLANES)
    cs_s = jnp.broadcast_to(cs_1[0:1], (ns, LANES))
    sn_s = jnp.broadcast_to(sn_1[0:1], (ns, LANES))
    x1s, zs, qs, kxs, vxs, kfs, vfs = trunk(x_sample.reshape(ns, d), mod_s, 1, cs_s, sn_s, 1, ts)
    nsub = 2 * SUBLANES
    assert ns % nsub == 0
    key_minor = lambda c: jnp.transpose(c[0], (0, 2, 3, 1)).reshape(ns * D_KV, WINDOW)
    o_att_s = _attn_step(sink_vec, qs, kfs, vfs, key_minor(cache_k), key_minor(cache_v), nsub)
    conv_s_in = state_conv[0]
    o_rnn_s, h_s = _rnn_step(zs, conv_s_in, state_h[0], conv_w[0], conv_b[0], wrg, wig, b_rg[0], b_ig[0], lam)
    y_s = tail(o_att_s, o_rnn_s, zs, x1s, mod_s, 1, ts)

    k_sample = kfs.reshape(1, ns, 1, N_KV_HEADS, HEAD_DIM)
    v_sample = vfs.reshape(1, ns, 1, N_KV_HEADS, HEAD_DIM)
    conv_sample = jnp.concatenate([conv_s_in[:, 1:], zs[:, None, Z_RX:Z_RX + D_RNN]], axis=1)[None]

    return (y_p.reshape(nb, seq, d), y_s.reshape(ns, 1, d), k_prompt, v_prompt, k_sample, v_sample,
            h_p.reshape(1, nb, D_RNN), h_s[None], conv_p[None], conv_sample)
```

```python
import functools

import jax
import jax.numpy as jnp
from jax import lax
from jax.experimental import pallas as pl
from jax.experimental.pallas import tpu as pltpu

F32 = jnp.float32
BF16 = jnp.bfloat16

D_MODEL = 2048
HEAD_DIM = 64
N_Q_HEADS = 16
N_KV_HEADS = 4
Q_PER_KV = N_Q_HEADS // N_KV_HEADS
D_ATTN = N_Q_HEADS * HEAD_DIM
D_KV = N_KV_HEADS * HEAD_DIM
D_QK = D_ATTN + D_KV
D_QKV = D_QK + D_KV
WINDOW = 128
ROPE_THETA = 10000.0
D_RNN = 1024
N_RNN_BLOCKS = 16
RNN_BLOCK = D_RNN // N_RNN_BLOCKS
CONV_WIDTH = 4
LRU_C = 8.0
N_MOD = 9
EPS = 1e-6
PAST_LEN = 16384

LANES = 128
SUBLANES = 8
MXU_DIM = 256
VMEM_LIMIT = 56 * 1024 * 1024

Z_RX = 0
Z_RG = Z_RX + D_RNN
Z_GATT = Z_RG + D_RNN
Z_GRNN = Z_GATT + D_MODEL
D_Z = Z_GRNN + D_MODEL
DX = 2 * N_KV_HEADS * LANES
LOG2E = 1.4426950408889634
Q_SCALE = LOG2E * HEAD_DIM ** -0.5


def _params(*sem):
    return pltpu.CompilerParams(dimension_semantics=sem, vmem_limit_bytes=VMEM_LIMIT)


def _silu(x):
    return x * jax.nn.sigmoid(x)


def _modulate(x, g, shift, scale):
    ms = jnp.mean(x * x, axis=-1, keepdims=True)
    return (x * lax.rsqrt(ms + EPS) * g) * (1.0 + scale) + shift


def _mod_spec(mod, tm, rows_per_mod, width, col, ngrid):
    colf = col if callable(col) else (lambda *idx: col)
    if mod.ndim == 3:
        return pl.BlockSpec((None, 1, width), lambda *idx: ((idx[ngrid - 1] * tm) // rows_per_mod, 0, colf(*idx)))
    return pl.BlockSpec((tm, width), lambda *idx: (idx[ngrid - 1], colf(*idx)))


def _ada_kernel(c_ref, w_ref, b_ref, o_ref):
    s = _silu(c_ref[...]).astype(BF16)
    o_ref[...] = jnp.dot(s, w_ref[...].astype(BF16), preferred_element_type=F32) + b_ref[...]


def _ada(c_all, w_ada, b_ada, tn):
    rows, d = c_all.shape
    n = w_ada.shape[1]
    return pl.pallas_call(
        _ada_kernel,
        grid=(n // tn,),
        in_specs=[pl.BlockSpec((rows, d), lambda j: (0, 0)),
                  pl.BlockSpec((d, tn), lambda j: (0, j)),
                  pl.BlockSpec((1, tn), lambda j: (0, j))],
        out_specs=pl.BlockSpec((rows, tn), lambda j: (0, j)),
        out_shape=jax.ShapeDtypeStruct((rows, n), F32),
        compiler_params=_params("arbitrary"),
        name="ada",
    )(c_all, w_ada, b_ada.reshape(1, n))


ROPE_SPAN = 64


def _rope_kernel(inv_ref, cs_ref, sn_ref, *, base, blk):
    r = pl.program_id(0)
    inv = inv_ref[...]
    lane = lax.broadcasted_iota(jnp.int32, (1, LANES), 1)
    sign = jnp.where((lane & (HEAD_DIM - 1)) < HEAD_DIM // 2, -1.0, 1.0)
    rows = lambda n, scale: (scale * lax.broadcasted_iota(jnp.int32, (n, LANES), 0)).astype(F32)
    if blk % ROPE_SPAN:
        ang = (rows(blk, 1) + (base + r * blk).astype(F32)) * inv
        cs_ref[...] = jnp.cos(ang)
        sn_ref[...] = jnp.sin(ang) * sign
        return
    ncoarse = blk // ROPE_SPAN
    coarse = (rows(max(ncoarse, SUBLANES), ROPE_SPAN) + (base + r * blk).astype(F32)) * inv
    fine = rows(ROPE_SPAN, 1) * inv
    ca, sa, cb, sb = jnp.cos(coarse), jnp.sin(coarse), jnp.cos(fine), jnp.sin(fine)
    for i in range(ncoarse):
        span = slice(i * ROPE_SPAN, (i + 1) * ROPE_SPAN)
        cs_ref[span, :] = ca[i:i + 1] * cb - sa[i:i + 1] * sb
        sn_ref[span, :] = (sa[i:i + 1] * cb + ca[i:i + 1] * sb) * sign


def _rope_tables(inv_row, base, rows, blk):
    out = jax.ShapeDtypeStruct((rows, LANES), F32)
    return pl.pallas_call(
        functools.partial(_rope_kernel, base=base, blk=blk),
        grid=(rows // blk,),
        in_specs=[pl.BlockSpec((1, LANES), lambda r: (0, 0))],
        out_specs=[pl.BlockSpec((blk, LANES), lambda r: (r, 0))] * 2,
        out_shape=[out, out],
        compiler_params=_params("arbitrary"),
        name="rope_tables",
    )(inv_row)


def _modulate_kernel(x_ref, shift_ref, scale_ref, g_ref, h_ref):
    group = 2 * SUBLANES
    per_row = shift_ref.shape[0] > 1

    def body(r, carry):
        rows = pl.ds(pl.multiple_of(r * group, group), group)
        shift = shift_ref[rows, :] if per_row else shift_ref[...]
        scale = scale_ref[rows, :] if per_row else scale_ref[...]
        h_ref[rows, :] = _modulate(x_ref[rows, :], g_ref[...], shift, scale).astype(BF16)
        return carry

    lax.fori_loop(0, x_ref.shape[0] // group, body, 0, unroll=8)


def _modulated_norm(x, mod, rows_per_mod, g, k_shift, tm):
    m_rows, d = x.shape
    return pl.pallas_call(
        _modulate_kernel,
        grid=(m_rows // tm,),
        in_specs=[pl.BlockSpec((tm, d), lambda m: (m, 0)),
                  _mod_spec(mod, tm, rows_per_mod, d, k_shift, 1),
                  _mod_spec(mod, tm, rows_per_mod, d, k_shift + 1, 1),
                  pl.BlockSpec((1, d), lambda m: (0, 0))],
        out_specs=pl.BlockSpec((tm, d), lambda m: (m, 0)),
        out_shape=jax.ShapeDtypeStruct((m_rows, d), BF16),
        compiler_params=_params("arbitrary"),
        name="modnorm",
    )(x, mod, mod, g.reshape(1, d))


HBM = pl.BlockSpec(memory_space=pl.ANY)


def _is_param(w):
    return w.dtype == F32


def _stream_weight_tiles(weights, tn, first):
    t, nt = pl.program_id(0), pl.num_programs(0)
    m, nm = pl.program_id(1), pl.num_programs(1)
    cols = lambda tile: pl.ds(pl.multiple_of(tile * tn, LANES), tn)
    fetch = lambda w_hbm, stage_ref, sem, tile: pltpu.make_async_copy(
        w_hbm.at[0, :, cols(first + tile)], stage_ref, sem.at[0])
    put = lambda wb_ref, wout_hbm, sem, tile: pltpu.make_async_copy(wb_ref, wout_hbm.at[:, cols(tile)], sem.at[1])

    @pl.when(m == 0)
    def _():
        for w_hbm, stage_ref, wb_ref, wout_hbm, sem in weights:
            @pl.when(t == 0)
            def _():
                fetch(w_hbm, stage_ref, sem, 0).start()

            fetch(w_hbm, stage_ref, sem, t).wait()

            @pl.when(t > 0)
            def _():
                put(wb_ref, wout_hbm, sem, t - 1).wait()

            wb_ref[...] = stage_ref[...].astype(BF16)
            put(wb_ref, wout_hbm, sem, t).start()

            @pl.when(t + 1 < nt)
            def _():
                fetch(w_hbm, stage_ref, sem, t + 1).start()

    @pl.when((t == nt - 1) & (m == nm - 1))
    def _():
        for w_hbm, stage_ref, wb_ref, wout_hbm, sem in weights:
            put(wb_ref, wout_hbm, sem, t).wait()


def _weight_scratch(rows, tn):
    return [pltpu.VMEM((rows, tn), F32), pltpu.VMEM((rows, tn), BF16), pltpu.SemaphoreType.DMA((2,))]


def _row_groups(tm):
    n = max(1, tm // 1024)
    return [slice(r * (tm // n), (r + 1) * (tm // n)) for r in range(n)]


def _ffn_up_rows(h_ref, w1b_ref, w3b_ref, g_ref):
    for rows in _row_groups(h_ref.shape[0]):
        h = h_ref[rows, :]
        a = jnp.dot(h, w1b_ref[...], preferred_element_type=F32)
        b = jnp.dot(h, w3b_ref[...], preferred_element_type=F32)
        g_ref[rows, :] = (_silu(a) * b).astype(BF16)


def _ffn_up_param_kernel(h_ref, w1_hbm, w3_hbm, g_ref, w1o_hbm, w3o_hbm,
                         s1_ref, w1b_ref, sem1, s3_ref, w3b_ref, sem3, *, tf):
    _stream_weight_tiles(((w1_hbm, s1_ref, w1b_ref, w1o_hbm, sem1), (w3_hbm, s3_ref, w3b_ref, w3o_hbm, sem3)),
                         tf, 0)
    _ffn_up_rows(h_ref, w1b_ref, w3b_ref, g_ref)


def _ffn_up(h, w1, w3, tm, tf):
    m_rows, d = h.shape
    dff = w1.shape[-1]
    grid = (dff // tf, m_rows // tm)
    h_spec = pl.BlockSpec((tm, d), lambda f, m: (m, 0))
    g_spec = pl.BlockSpec((tm, tf), lambda f, m: (m, f))
    g_shape = jax.ShapeDtypeStruct((m_rows, dff), BF16)
    if not _is_param(w1):
        w_spec = pl.BlockSpec((d, tf), lambda f, m: (0, f))
        g = pl.pallas_call(
            _ffn_up_rows, grid=grid, in_specs=[h_spec, w_spec, w_spec], out_specs=g_spec, out_shape=g_shape,
            compiler_params=_params("arbitrary", "arbitrary"), name="ffn_up_rows",
        )(h, w1, w3)
        return g, w1, w3
    w_shape = jax.ShapeDtypeStruct((d, dff), BF16)
    return pl.pallas_call(
        functools.partial(_ffn_up_param_kernel, tf=tf),
        grid=grid,
        in_specs=[h_spec, HBM, HBM],
        out_specs=[g_spec, HBM, HBM],
        out_shape=[g_shape, w_shape, w_shape],
        scratch_shapes=_weight_scratch(d, tf) + _weight_scratch(d, tf),
        compiler_params=_params("arbitrary", "arbitrary"),
        name="ffn_up",
    )(h, w1, w3)


def _ffn_down_rows(g_ref, w2b_ref, x_ref, gate_ref, o_ref):
    for rows in _row_groups(g_ref.shape[0]):
        acc = jnp.dot(g_ref[rows, :], w2b_ref[...], preferred_element_type=F32)
        gate = gate_ref[...] if gate_ref.shape[0] == 1 else gate_ref[rows, :]
        o_ref[rows, :] = x_ref[rows, :] + 0.5 * gate * acc


def _ffn_down_param_kernel(g_ref, w2_hbm, x_ref, gate_ref, o_ref, w2o_hbm, s2_ref, w2b_ref, sem2, *, tn):
    _stream_weight_tiles(((w2_hbm, s2_ref, w2b_ref, w2o_hbm, sem2),), tn, 0)
    _ffn_down_rows(g_ref, w2b_ref, x_ref, gate_ref, o_ref)


def _ffn_down(g, w2, x, mod, rows_per_mod, k_gate, tm, tn):
    m_rows, d = x.shape
    dff = g.shape[1]
    grid = (d // tn, m_rows // tm)
    gate_col = lambda n, m: k_gate * (d // tn) + n
    g_spec = pl.BlockSpec((tm, dff), lambda n, m: (m, 0))
    x_spec = pl.BlockSpec((tm, tn), lambda n, m: (m, n))
    gate_spec = _mod_spec(mod, tm, rows_per_mod, tn, gate_col, 2)
    o_shape = jax.ShapeDtypeStruct((m_rows, d), F32)
    if not _is_param(w2):
        out = pl.pallas_call(
            _ffn_down_rows, grid=grid,
            in_specs=[g_spec, pl.BlockSpec((dff, tn), lambda n, m: (0, n)), x_spec, gate_spec],
            out_specs=x_spec, out_shape=o_shape,
            compiler_params=_params("arbitrary", "arbitrary"), name="ffn_down_rows",
        )(g, w2, x, mod)
        return out, w2
    return pl.pallas_call(
        functools.partial(_ffn_down_param_kernel, tn=tn),
        grid=grid,
        in_specs=[g_spec, HBM, x_spec, gate_spec],
        out_specs=[x_spec, HBM],
        out_shape=[o_shape, jax.ShapeDtypeStruct((dff, d), BF16)],
        scratch_shapes=_weight_scratch(dff, tn),
        compiler_params=_params("arbitrary", "arbitrary"),
        name="ffn_down",
    )(g, w2, x, mod)


def _inproj_z_rows(h_ref, wb_ref, z_ref):
    for rows in _row_groups(h_ref.shape[0]):
        z_ref[rows, :] = jnp.dot(h_ref[rows, :], wb_ref[...], preferred_element_type=F32)


def _inproj_z_param_kernel(h_ref, w_hbm, z_ref, wo_hbm, s_ref, wb_ref, sem, *, tn, first):
    _stream_weight_tiles(((w_hbm, s_ref, wb_ref, wo_hbm, sem),), tn, first)
    _inproj_z_rows(h_ref, wb_ref, z_ref)


def _inproj_z(h, w, tm, tn):
    m_rows, d = h.shape
    grid = (D_Z // tn, m_rows // tm)
    h_spec = pl.BlockSpec((tm, d), lambda n, m: (m, 0))
    z_spec = pl.BlockSpec((tm, tn), lambda n, m: (m, n))
    z_shape = jax.ShapeDtypeStruct((m_rows, D_Z), F32)
    if not _is_param(w):
        z = pl.pallas_call(
            _inproj_z_rows, grid=grid, in_specs=[h_spec, pl.BlockSpec((d, tn), lambda n, m: (0, n))],
            out_specs=z_spec, out_shape=z_shape,
            compiler_params=_params("arbitrary", "arbitrary"), name="inproj_z_rows",
        )(h, w)
        return z, w
    return pl.pallas_call(
        functools.partial(_inproj_z_param_kernel, tn=tn, first=D_QKV // tn),
        grid=grid,
        in_specs=[h_spec, HBM],
        out_specs=[z_spec, HBM],
        out_shape=[z_shape, jax.ShapeDtypeStruct((d, D_Z), BF16)],
        scratch_shapes=_weight_scratch(d, tn),
        compiler_params=_params("arbitrary", "arbitrary"),
        name="inproj_z",
    )(h, w)


def _store_head_pairs(ref, rows, chunk, c):
    lo = lax.broadcasted_iota(jnp.int32, chunk.shape, 1) < HEAD_DIM
    swapped = pltpu.roll(chunk, HEAD_DIM, 1)
    zero = jnp.zeros_like(chunk)
    cols = (jnp.where(lo, chunk, zero), jnp.where(lo, zero, swapped),
            jnp.where(lo, swapped, zero), jnp.where(lo, zero, chunk))
    for i, col in enumerate(cols):
        ref[rows, (4 * c + i) * LANES:(4 * c + i + 1) * LANES] = col.astype(ref.dtype)


def _side_chunk_rows(side, steps):
    total = sum(w.shape[1] for w in side)
    rows = total // steps
    ok = rows * steps == total and rows % (2 * SUBLANES) == 0 and all(w.shape[1] % rows == 0 for w in side)
    return rows if ok else None


def _round_side_matrices(mats, stage_ref, ostage_ref, sem):
    m = pl.program_id(0)
    nrows = stage_ref.shape[1]
    chunks = [(w, wb, r0) for w, wb in mats for r0 in range(0, w.shape[1], nrows)]

    def fetch(c):
        w, _, r0 = chunks[c]
        return pltpu.make_async_copy(w.at[0, pl.ds(r0, nrows), :], stage_ref.at[c % 2], sem.at[c % 2])

    def put(c):
        _, wb, r0 = chunks[c]
        return pltpu.make_async_copy(ostage_ref.at[c % 2], wb.at[pl.ds(r0, nrows), :], sem.at[2 + c % 2])

    last = len(chunks) - 1
    for c in range(len(chunks)):
        @pl.when(m == c)
        def _():
            if c == 0:
                fetch(0).start()
            if c < last:
                fetch(c + 1).start()
            fetch(c).wait()
            if c >= 2:
                put(c - 2).wait()

    ostage_ref[m % 2] = stage_ref[m % 2].astype(BF16)

    def finish():
        for c in range(len(chunks)):
            @pl.when(m == c)
            def _():
                put(c).start()
                if c == last:
                    if c >= 1:
                        put(c - 1).wait()
                    put(c).wait()

    return finish


def _inproj_qkv_kernel(*refs, nsplit, nside):
    (x_ref, shift_ref, scale_ref, g_ref, w_ref, cs_ref, sn_ref, gqk_ref, red_ref, exp_ref) = refs[:10]
    side_in = refs[10:10 + nside]
    h_ref, q_ref, kx_ref, vx_ref, kf_ref, vf_ref = refs[10 + nside:16 + nside]
    side_out = refs[16 + nside:16 + 2 * nside]
    wb_ref = refs[16 + 2 * nside]

    @pl.when(pl.program_id(0) == 0)
    def _():
        wb_ref[...] = w_ref[...].astype(BF16)

    finish_side = _round_side_matrices(list(zip(side_in, side_out)), *refs[17 + 2 * nside:]) if nside else None

    step = x_ref.shape[0] // nsplit
    for r in range(nsplit):
        rows = slice(r * step, (r + 1) * step)
        shift = shift_ref[...] if shift_ref.shape[0] == 1 else shift_ref[rows, :]
        scale = scale_ref[...] if scale_ref.shape[0] == 1 else scale_ref[rows, :]
        h = _modulate(x_ref[rows, :], g_ref[...], shift, scale).astype(BF16)
        h_ref[rows, :] = h
        acc = jnp.dot(h, wb_ref[...], preferred_element_type=F32)
        qk = acc[:, :D_QK]
        hm = jnp.dot((qk * qk).astype(BF16), red_ref[...], preferred_element_type=F32)
        hi = hm.astype(BF16)
        lo = (hm - hi.astype(F32)).astype(BF16)
        ms = jnp.dot(jnp.concatenate([hi, lo], axis=1), exp_ref[...], preferred_element_type=F32)
        y = qk * lax.rsqrt(ms + EPS) * gqk_ref[...]
        cs = cs_ref[rows, :]
        sn = sn_ref[rows, :]
        lane = lax.broadcasted_iota(jnp.int32, cs.shape, 1)
        first_half = (lane & (HEAD_DIM - 1)) < HEAD_DIM // 2
        for c in range(D_QK // LANES):
            yc = y[:, c * LANES:(c + 1) * LANES]
            partner = jnp.where(first_half,
                                pltpu.roll(yc, LANES - HEAD_DIM // 2, 1),
                                pltpu.roll(yc, HEAD_DIM // 2, 1))
            rot = yc * cs + partner * sn
            if c < D_ATTN // LANES:
                q_ref[rows, c * LANES:(c + 1) * LANES] = (rot * Q_SCALE).astype(BF16)
            else:
                kf_ref[rows, c * LANES - D_ATTN:(c + 1) * LANES - D_ATTN] = rot
                _store_head_pairs(kx_ref, rows, rot, c - D_ATTN // LANES)
        v = acc[:, D_QK:]
        vf_ref[rows, :] = v
        for c in range(D_KV // LANES):
            _store_head_pairs(vx_ref, rows, v[:, c * LANES:(c + 1) * LANES], c)
    if finish_side is not None:
        finish_side()


def _inproj_qkv(x, mod, rows_per_mod, g, k_shift, w_in, cs, sn, gqk, head_reduce, head_expand, tm, table_blocks,
                side=()):
    m_rows, d = x.shape
    steps = m_rows // tm
    row = lambda width: pl.BlockSpec((tm, width), lambda m: (m, 0))
    table = pl.BlockSpec((tm, LANES), lambda m: (m % table_blocks, 0))
    nsplit = 2 if tm % (2 * MXU_DIM) == 0 else 1
    side_scratch = []
    if side:
        width = side[0].shape[2]
        chunk = _side_chunk_rows(side, steps)
        assert chunk is not None and all(w.shape[2] == width for w in side)
        side_scratch = [pltpu.VMEM((2, chunk, width), F32), pltpu.VMEM((2, chunk, width), BF16),
                        pltpu.SemaphoreType.DMA((4,))]
    return pl.pallas_call(
        functools.partial(_inproj_qkv_kernel, nsplit=nsplit, nside=len(side)),
        grid=(steps,),
        in_specs=[row(d),
                  _mod_spec(mod, tm, rows_per_mod, d, k_shift, 1),
                  _mod_spec(mod, tm, rows_per_mod, d, k_shift + 1, 1),
                  pl.BlockSpec((1, d), lambda m: (0, 0)),
                  pl.BlockSpec((None, d, D_QKV), lambda m: (0, 0, 0), pipeline_mode=pl.Buffered(1)),
                  table, table,
                  pl.BlockSpec((1, D_QK), lambda m: (0, 0)),
                  pl.BlockSpec(head_reduce.shape, lambda m: (0, 0)),
                  pl.BlockSpec(head_expand.shape, lambda m: (0, 0))] + [HBM] * len(side),
        out_specs=[row(d), row(D_ATTN), row(DX), row(DX), row(D_KV), row(D_KV)] + [HBM] * len(side),
        out_shape=[jax.ShapeDtypeStruct((m_rows, d), BF16),
                   jax.ShapeDtypeStruct((m_rows, D_ATTN), BF16),
                   jax.ShapeDtypeStruct((m_rows, DX), BF16),
                   jax.ShapeDtypeStruct((m_rows, DX), BF16),
                   jax.ShapeDtypeStruct((m_rows, D_KV), F32),
                   jax.ShapeDtypeStruct((m_rows, D_KV), F32)]
        + [jax.ShapeDtypeStruct(w.shape[1:], BF16) for w in side],
        scratch_shapes=[pltpu.VMEM((d, D_QKV), BF16)] + side_scratch,
        compiler_params=_params("arbitrary"),
        name="inproj_qkv",
    )(x, mod, mod, g.reshape(1, d), w_in, cs, sn, gqk, head_reduce, head_expand, *side)


def _attn_bias(bias_ref, tq, past_off):
    nk = 2 * WINDOW
    ri = lax.broadcasted_iota(jnp.int32, (2 * tq, 2 * nk), 0) & (tq - 1)
    kj = lax.broadcasted_iota(jnp.int32, (2 * tq, 2 * nk), 1) & (nk - 1)
    visible = ((kj < WINDOW) & (kj > ri + past_off)) | ((kj >= WINDOW) & (kj - WINDOW <= ri))
    bias_ref[...] = jnp.where(visible, 0.0, -jnp.inf)


def _attn_blocks(sink_ref, bias_of, q_ref, past_of, kc_ref, vc_ref, nsub, tq, store):
    nk = 2 * WINDOW
    rows = 2 * tq
    first_rows = lax.broadcasted_iota(jnp.int32, (rows, 1), 0) < tq
    lo_lanes = lax.broadcasted_iota(jnp.int32, (rows, LANES), 1) < HEAD_DIM
    ones_lo = (lax.broadcasted_iota(jnp.int32, (nk, LANES), 1) < HEAD_DIM).astype(BF16)
    ones_hi = (1 - ones_lo.astype(F32)).astype(BF16)
    nt = (((1,), (1,)), ((), ()))

    def keys(which, s, col):
        cols = slice(col * LANES, (col + 1) * LANES)
        past = past_of(s)
        return [past[which][past[2]:past[2] + WINDOW, cols], (kc_ref, vc_ref)[which][s * tq:(s + 1) * tq, cols]]

    for s in range(nsub):
        for g in range(N_KV_HEADS):
            c0, c1 = 2 * g, 2 * g + 1
            q4 = jnp.concatenate([q_ref[s * tq:(s + 1) * tq, c0 * LANES:(c0 + 1) * LANES],
                                  q_ref[s * tq:(s + 1) * tq, c1 * LANES:(c1 + 1) * LANES]], axis=0)
            kk = jnp.concatenate(keys(0, s, c0) + keys(0, s, c1), axis=0)
            sc = lax.dot_general(q4, kk, nt, preferred_element_type=F32) + bias_of(s)[...]
            sk_lo = jnp.where(first_rows, sink_ref[4 * g], sink_ref[4 * g + 2]) * LOG2E
            sk_hi = jnp.where(first_rows, sink_ref[4 * g + 1], sink_ref[4 * g + 3]) * LOG2E
            m_lo = jnp.maximum(jnp.max(sc[:, :nk], axis=-1, keepdims=True), sk_lo)
            m_hi = jnp.maximum(jnp.max(sc[:, nk:], axis=-1, keepdims=True), sk_hi)
            p = jnp.concatenate([jnp.exp2(sc[:, :nk] - m_lo), jnp.exp2(sc[:, nk:] - m_hi)], axis=1).astype(BF16)
            vv = jnp.concatenate(
                [jnp.concatenate([jnp.concatenate(keys(1, s, c0), axis=0), ones_lo], axis=1),
                 jnp.concatenate([jnp.concatenate(keys(1, s, c1), axis=0), ones_hi], axis=1)],
                axis=0)
            o = jnp.dot(p, vv, preferred_element_type=F32)
            denom = o[:, LANES:] + jnp.where(lo_lanes, jnp.exp2(sk_lo - m_lo), jnp.exp2(sk_hi - m_hi))
            store(s, c0, c1, o[:, :LANES] / denom)


def _attn_kernel(sink_ref, q_ref, kp_ref, kc_ref, vp_ref, vc_ref, o_ref, bias0_ref, bias_ref, *, nq):
    tq = WINDOW
    _attn_bias(bias0_ref, tq, jnp.where(pl.program_id(1) > 0, 0, WINDOW))
    if nq > 1:
        _attn_bias(bias_ref, tq, 0)

    def store(s, c0, c1, out):
        o_ref[s * tq:(s + 1) * tq, c0 * LANES:(c0 + 1) * LANES] = out[:tq].astype(BF16)
        o_ref[s * tq:(s + 1) * tq, c1 * LANES:(c1 + 1) * LANES] = out[tq:].astype(BF16)

    past_of = lambda s: (kp_ref, vp_ref, 0) if s == 0 else (kc_ref, vc_ref, (s - 1) * WINDOW)
    bias_of = lambda s: bias0_ref if s == 0 else bias_ref
    _attn_blocks(sink_ref, bias_of, q_ref, past_of, kc_ref, vc_ref, nq, tq, store)


def _attn(sinks, q, kx, vx, nbatch, nblk, nq):
    steps = nblk // nq
    cur = lambda b, n: (b * steps + n, 0)
    past = lambda b, n: (b * nblk + jnp.maximum(n * nq - 1, 0), 0)
    kv_cur = pl.BlockSpec((nq * WINDOW, DX), cur)
    kv_past = pl.BlockSpec((WINDOW, DX), past)
    bias = pltpu.VMEM((2 * WINDOW, 4 * WINDOW), F32)
    return pl.pallas_call(
        functools.partial(_attn_kernel, nq=nq),
        grid=(nbatch, steps),
        in_specs=[pl.BlockSpec(memory_space=pltpu.SMEM), pl.BlockSpec((nq * WINDOW, D_ATTN), cur),
                  kv_past, kv_cur, kv_past, kv_cur],
        out_specs=pl.BlockSpec((nq * WINDOW, D_ATTN), cur),
        out_shape=jax.ShapeDtypeStruct(q.shape, BF16),
        scratch_shapes=[bias, bias],
        compiler_params=_params("arbitrary", "arbitrary"),
        name="attn",
    )(sinks, q, kx, kx, vx, vx)


def _rows_from_lanes(row, nrows):
    out_rows = max(nrows, SUBLANES)
    rep = jnp.broadcast_to(row, (out_rows, row.shape[1]))
    ridx = lax.broadcasted_iota(jnp.int32, (out_rows, HEAD_DIM), 0)
    out = jnp.zeros((out_rows, HEAD_DIM), row.dtype)
    for i in range(nrows):
        out = jnp.where(ridx == i, rep[:, i * HEAD_DIM:(i + 1) * HEAD_DIM], out)
    return out


def _attn_step_kernel(sink_ref, q_ref, kn_ref, vn_ref, ck_ref, cv_ref, o_ref, *, nsub):
    grp = lax.broadcasted_iota(jnp.int32, (N_Q_HEADS, HEAD_DIM), 0) >> 2
    bias = jnp.where(lax.broadcasted_iota(jnp.int32, (N_Q_HEADS, WINDOW), 1) >= 1, 0.0, -jnp.inf)
    hcol = lax.broadcasted_iota(jnp.int32, (N_Q_HEADS, 1), 0)
    sk = jnp.zeros((N_Q_HEADS, 1), F32)
    for h in range(N_Q_HEADS):
        sk = jnp.where(hcol == h, sink_ref[h] * LOG2E, sk)
    nt = (((1,), (1,)), ((), ()))
    qf = q_ref[...].astype(F32)

    def own_group(x):
        out = jnp.zeros((N_Q_HEADS, HEAD_DIM), F32)
        for g in range(N_KV_HEADS):
            out = jnp.where(grp == g, x[:, g * HEAD_DIM:(g + 1) * HEAD_DIM], out)
        return out

    for s in range(nsub):
        qh = _rows_from_lanes(qf[s:s + 1, :], N_Q_HEADS)
        zero = jnp.zeros_like(qh)
        q_bd = jnp.concatenate([jnp.where(grp == g, qh, zero) for g in range(N_KV_HEADS)], axis=1)
        kt = ck_ref[s * D_KV:(s + 1) * D_KV, :].astype(BF16)
        vt = cv_ref[s * D_KV:(s + 1) * D_KV, :].astype(BF16)
        sc = jnp.dot(q_bd.astype(BF16), kt, preferred_element_type=F32) + bias
        s_new = jnp.sum(q_bd * kn_ref[s:s + 1, :], axis=-1, keepdims=True)
        m = jnp.maximum(jnp.maximum(jnp.max(sc, axis=-1, keepdims=True), s_new), sk)
        p = jnp.exp2(sc - m)
        p_new = jnp.exp2(s_new - m)
        denom = jnp.sum(p, axis=-1, keepdims=True) + p_new + jnp.exp2(sk - m)
        o_all = lax.dot_general(p.astype(BF16), vt, nt, preferred_element_type=F32)
        v_new = own_group(jnp.broadcast_to(vn_ref[s:s + 1, :], (N_Q_HEADS, D_KV)))
        o = (own_group(o_all) + p_new * v_new) / denom
        for c in range(N_Q_HEADS // 2):
            o_ref[s:s + 1, c * LANES:(c + 1) * LANES] = jnp.concatenate(
                [o[2 * c:2 * c + 1, :], o[2 * c + 1:2 * c + 2, :]], axis=1)


def _attn_step(sinks, q, k_new, v_new, cache_k, cache_v, nsub):
    ns = q.shape[0]
    row = lambda width: pl.BlockSpec((nsub, width), lambda b: (b, 0))
    cache = pl.BlockSpec((nsub * D_KV, WINDOW), lambda b: (b, 0))
    return pl.pallas_call(
        functools.partial(_attn_step_kernel, nsub=nsub),
        grid=(ns // nsub,),
        in_specs=[pl.BlockSpec(memory_space=pltpu.SMEM), row(D_ATTN), row(D_KV), row(D_KV), cache, cache],
        out_specs=row(D_ATTN),
        out_shape=jax.ShapeDtypeStruct((ns, D_ATTN), F32),
        compiler_params=_params("arbitrary"),
        name="attn_step",
    )(sinks, q, k_new, v_new, cache_k, cache_v)


def _softplus(x):
    return jnp.maximum(x, 0.0) + jnp.log1p(jnp.exp(-jnp.abs(x)))


def _lru_coeffs(xc, wrg_ref, wig_ref, brg, big, lam):
    xb = xc.astype(BF16)
    ngroups = D_RNN // MXU_DIM
    rs, igs = [], []
    for c in range(ngroups):
        xg = xb[:, c * MXU_DIM:(c + 1) * MXU_DIM]
        rs.append(jnp.dot(xg, wrg_ref[c], preferred_element_type=F32))
        igs.append(jnp.dot(xg, wig_ref[c], preferred_element_type=F32))
    r = jax.nn.sigmoid(jnp.concatenate(rs, axis=1) + brg)
    ig = jax.nn.sigmoid(jnp.concatenate(igs, axis=1) + big)
    log_a = -LRU_C * r * _softplus(-lam)
    a = jnp.exp(log_a)
    one_minus_a2 = -jnp.tanh(log_a) * (1.0 + a * a)
    root = jnp.where(one_minus_a2 > 0.0, one_minus_a2 * lax.rsqrt(one_minus_a2), 0.0)
    u = root * (ig * xc)
    return a, u


def _rnn_prompt_kernel(rx_ref, rg_ref, cw_ref, cb_ref, wrg_ref, wig_ref, brg_ref, big_ref, lam_ref,
                       o_ref, hlast_ref, conv_ref, xs_ref, a_ref, h_ref, carry_ref, *, tc):
    t = pl.program_id(1)
    pad = SUBLANES

    @pl.when(t == 0)
    def _():
        xs_ref[0:pad, :] = jnp.zeros((pad, D_RNN), F32)
        carry_ref[...] = jnp.zeros_like(carry_ref)

    x = rx_ref[...]
    xs_ref[pad:pad + tc, :] = x
    xc = cb_ref[...] + x * cw_ref[CONV_WIDTH - 1:CONV_WIDTH, :]
    for k in range(1, CONV_WIDTH):
        xc = xc + xs_ref[pad - k:pad - k + tc, :] * cw_ref[CONV_WIDTH - 1 - k:CONV_WIDTH - k, :]
    tail = xs_ref[tc:tc + pad, :]
    xs_ref[0:pad, :] = tail
    conv_ref[...] = tail[pad - (CONV_WIDTH - 1):, :]

    a, u = _lru_coeffs(xc, wrg_ref, wig_ref, brg_ref[...], big_ref[...], lam_ref[...])
    a_ref[...] = a
    h_ref[...] = u

    row = lax.broadcasted_iota(jnp.int32, (SUBLANES, D_RNN), 0)

    def body(r, carry):
        off = pl.multiple_of(r * SUBLANES, SUBLANES)
        av = a_ref[pl.ds(off, SUBLANES), :]
        hv = h_ref[pl.ds(off, SUBLANES), :]
        hv = hv + jnp.where(row == 0, av * carry, 0.0)
        for sft in (1, 2, 4):
            keep = row >= sft
            h_sh = jnp.where(keep, pltpu.roll(hv, sft, 0), 0.0)
            hv = av * h_sh + hv
            if sft < SUBLANES // 2:
                av = av * jnp.where(keep, pltpu.roll(av, sft, 0), 1.0)
        h_ref[pl.ds(off, SUBLANES), :] = hv
        return jnp.broadcast_to(hv[SUBLANES - 1:SUBLANES, :], (SUBLANES, D_RNN))

    carry = lax.fori_loop(0, tc // SUBLANES, body, carry_ref[...])
    carry_ref[...] = carry
    hlast_ref[...] = carry[0:1, :]
    o_ref[...] = (h_ref[...] * jax.nn.gelu(rg_ref[...])).astype(BF16)


def _rnn_prompt(z, nbatch, seq, conv_w, conv_b, wrg, wig, b_rg, b_ig, lam, tc):
    nchunk = seq // tc
    rx_blk = Z_RX // D_RNN
    rg_blk = Z_RG // D_RNN
    vec = pl.BlockSpec((1, D_RNN), lambda b, t: (0, 0))
    wspec = pl.BlockSpec(wrg.shape, lambda b, t: (0, 0, 0))
    return pl.pallas_call(
        functools.partial(_rnn_prompt_kernel, tc=tc),
        grid=(nbatch, nchunk),
        in_specs=[pl.BlockSpec((tc, D_RNN), lambda b, t: (b * nchunk + t, rx_blk)),
                  pl.BlockSpec((tc, D_RNN), lambda b, t: (b * nchunk + t, rg_blk)),
                  pl.BlockSpec((CONV_WIDTH, D_RNN), lambda b, t: (0, 0)),
                  vec, wspec, wspec, vec, vec, vec],
        out_specs=[pl.BlockSpec((tc, D_RNN), lambda b, t: (b * nchunk + t, 0)),
                   pl.BlockSpec((None, 1, D_RNN), lambda b, t: (b, 0, 0)),
                   pl.BlockSpec((None, CONV_WIDTH - 1, D_RNN), lambda b, t: (b, 0, 0))],
        out_shape=[jax.ShapeDtypeStruct((nbatch * seq, D_RNN), BF16),
                   jax.ShapeDtypeStruct((nbatch, 1, D_RNN), F32),
                   jax.ShapeDtypeStruct((nbatch, CONV_WIDTH - 1, D_RNN), F32)],
        scratch_shapes=[pltpu.VMEM((tc + SUBLANES, D_RNN), F32),
                        pltpu.VMEM((tc, D_RNN), F32),
                        pltpu.VMEM((tc, D_RNN), F32),
                        pltpu.VMEM((SUBLANES, D_RNN), F32)],
        compiler_params=_params("arbitrary", "arbitrary"),
        name="rnn_prompt",
    )(z, z, conv_w, conv_b.reshape(1, D_RNN), wrg, wig, b_rg.reshape(1, D_RNN),
      b_ig.reshape(1, D_RNN), lam.reshape(1, D_RNN))


def _rnn_step_kernel(rx_ref, rg_ref, c0_ref, c1_ref, c2_ref, h0_ref, cw_ref, cb_ref, wrg_ref, wig_ref,
                     brg_ref, big_ref, lam_ref, o_ref, h_ref):
    x = rx_ref[...]
    xc = (cb_ref[...] + c0_ref[...] * cw_ref[0:1, :] + c1_ref[...] * cw_ref[1:2, :]
          + c2_ref[...] * cw_ref[2:3, :] + x * cw_ref[3:4, :])
    a, u = _lru_coeffs(xc, wrg_ref, wig_ref, brg_ref[...], big_ref[...], lam_ref[...])
    h = a * h0_ref[...] + u
    h_ref[...] = h
    o_ref[...] = (h * jax.nn.gelu(rg_ref[...])).astype(BF16)


def _rnn_step(z, conv_state, h0, conv_w, conv_b, wrg, wig, b_rg, b_ig, lam):
    rows = z.shape[0]
    full = lambda shape: pl.BlockSpec(shape, lambda i: (0,) * len(shape))
    act = full((rows, D_RNN))
    vec = full((1, D_RNN))
    return pl.pallas_call(
        _rnn_step_kernel,
        grid=(1,),
        in_specs=[pl.BlockSpec((rows, D_RNN), lambda i: (0, Z_RX // D_RNN)),
                  pl.BlockSpec((rows, D_RNN), lambda i: (0, Z_RG // D_RNN)),
                  act, act, act, act, full((CONV_WIDTH, D_RNN)), vec,
                  full(wrg.shape), full(wig.shape), vec, vec, vec],
        out_specs=[act, act],
        out_shape=[jax.ShapeDtypeStruct((rows, D_RNN), BF16),
                   jax.ShapeDtypeStruct((rows, D_RNN), F32)],
        compiler_params=_params("arbitrary"),
        name="rnn_step",
    )(z, z, conv_state[:, 0], conv_state[:, 1], conv_state[:, 2], h0, conv_w,
      conv_b.reshape(1, D_RNN), wrg, wig, b_rg.reshape(1, D_RNN), b_ig.reshape(1, D_RNN),
      lam.reshape(1, D_RNN))


def _mix_kernel(oa_ref, or_ref, ga_ref, gr_ref, x_ref, gate_ref, shift_ref, scale_ref, g_ref,
                wpa_ref, wpr_ref, wout_ref, o_ref, h_ref):
    pa = jnp.dot(oa_ref[...].astype(BF16), wpa_ref[...], preferred_element_type=F32)
    pr = jnp.dot(or_ref[...], wpr_ref[...], preferred_element_type=F32)
    mix = jax.nn.sigmoid(ga_ref[...]) * pa + jax.nn.sigmoid(gr_ref[...]) * pr
    x = x_ref[...] + gate_ref[...] * jnp.dot(mix.astype(BF16), wout_ref[...], preferred_element_type=F32)
    o_ref[...] = x
    h_ref[...] = _modulate(x, g_ref[...], shift_ref[...], scale_ref[...]).astype(BF16)


def _mix(o_att, o_rnn, z, x, mod, rows_per_mod, g_next, w_pa, w_pr, w_out, k_gate, k_shift_next, tm):
    m_rows, d = x.shape
    const = lambda shape: pl.BlockSpec(shape, lambda m: (0, 0), pipeline_mode=pl.Buffered(1))
    row = lambda width, col: pl.BlockSpec((tm, width), lambda m: (m, col))
    mspec = lambda k: _mod_spec(mod, tm, rows_per_mod, d, k, 1)
    return pl.pallas_call(
        _mix_kernel,
        grid=(m_rows // tm,),
        in_specs=[row(D_ATTN, 0), row(D_RNN, 0), row(d, Z_GATT // d), row(d, Z_GRNN // d), row(d, 0),
                  mspec(k_gate), mspec(k_shift_next), mspec(k_shift_next + 1), const((1, d)),
                  const(w_pa.shape), const(w_pr.shape), const(w_out.shape)],
        out_specs=[row(d, 0), row(d, 0)],
        out_shape=[jax.ShapeDtypeStruct((m_rows, d), F32), jax.ShapeDtypeStruct((m_rows, d), BF16)],
        compiler_params=_params("arbitrary"),
        name="mix",
    )(o_att, o_rnn, z, z, x, mod, mod, mod, g_next.reshape(1, d), w_pa, w_pr, w_out)


def _block_diag(w, group):
    n, r, _ = w.shape
    eye = jnp.eye(group, dtype=w.dtype)
    wg = w.reshape(n // group, group, r, r)
    return jnp.einsum("ngij,gh->ngihj", wg, eye).reshape(n // group, group * r, group * r)


def _tiles(rows, dff):
    pick = lambda n, prefs: next((t for t in prefs if n % t == 0), n)
    return dict(
        tm_norm=pick(rows, (1024, 512)),
        tm_up=pick(rows, (2048, 1024, 512)), tf=pick(dff, (512,)),
        tm_down=pick(rows, (1024, 512)), tn_down=512,
        tm_z=pick(rows, (1024, 512)), tn_z=D_QKV,
        tm_qkv=pick(rows, (512,)),
        tm_mix=pick(rows, (256,)),
    )


def kernel(x_prompt, x_sample, c_prompt, c_sample, cache_k, cache_v, state_h, state_conv, w_ada, b_ada,
           g_norm_ffn1, g_norm_mix, g_norm_ffn2, ffn1_w1, ffn1_w3, ffn1_w2, ffn2_w1, ffn2_w3, ffn2_w2,
           w_in, g_q, g_k, sinks, conv_w, conv_b, w_rg, b_rg, w_ig, b_ig, lru_lambda, w_pa, w_pr, w_out):
    nb, seq, d = x_prompt.shape
    ns = x_sample.shape[0]
    dff = ffn1_w1.shape[2]
    assert d == D_MODEL and w_ada.shape[0] == 1 and x_sample.shape[1] == 1 and cache_k.shape[2] == WINDOW
    assert seq % WINDOW == 0 and w_in.shape[2] == D_QKV + D_Z

    group = MXU_DIM // RNN_BLOCK
    wrg = _block_diag(w_rg[0], group).astype(BF16)
    wig = _block_diag(w_ig[0], group).astype(BF16)
    gqk = jnp.concatenate([jnp.tile(g_q[0], N_Q_HEADS), jnp.tile(g_k[0], N_KV_HEADS)]).reshape(1, D_QK)
    head_of = jnp.arange(D_QK) // HEAD_DIM
    onehot = (head_of[:, None] == jnp.arange(LANES)[None, :]).astype(F32)
    head_reduce = (onehot / HEAD_DIM).astype(BF16)
    head_expand = jnp.concatenate([onehot.T, onehot.T], axis=0).astype(BF16)
    inv = ROPE_THETA ** (-jnp.arange(HEAD_DIM // 2, dtype=F32) * 2.0 / HEAD_DIM)
    inv_row = jnp.tile(inv, LANES // (HEAD_DIM // 2)).reshape(1, LANES)
    sink_vec = sinks[0]
    lam = lru_lambda[0]

    pad_rows = (-(nb + ns)) % SUBLANES
    c_all = jnp.concatenate([c_prompt, c_sample, jnp.zeros((pad_rows, d), F32)], axis=0)
    mod_all = _ada(c_all, w_ada[0], b_ada[0], 1024)
    mod_p = mod_all[:nb].reshape(nb, 1, N_MOD * d)
    mod_s = mod_all[nb:nb + ns]

    wts = dict(f1w1=ffn1_w1, f1w3=ffn1_w3, f1w2=ffn1_w2, f2w1=ffn2_w1, f2w3=ffn2_w3, f2w2=ffn2_w2, wz=w_in)

    def trunk(x, mod, rows_per_mod, cs, sn, table_blocks, t):
        h1 = _modulated_norm(x, mod, rows_per_mod, g_norm_ffn1[0], 0, t["tm_norm"])
        g1, wts["f1w1"], wts["f1w3"] = _ffn_up(h1, wts["f1w1"], wts["f1w3"], t["tm_up"], t["tf"])
        x1, wts["f1w2"] = _ffn_down(g1, wts["f1w2"], x, mod, rows_per_mod, 2, t["tm_down"], t["tn_down"])
        side = ()
        if "mix" not in wts:
            side = (w_pa, w_pr, w_out)
            if _side_chunk_rows(side, x.shape[0] // t["tm_qkv"]) is None:
                side, wts["mix"] = (), tuple(w[0].astype(BF16) for w in side)
        hm, q, kx, vx, kf, vf, *rounded = _inproj_qkv(x1, mod, rows_per_mod, g_norm_mix[0], 3, w_in, cs, sn, gqk,
                                                      head_reduce, head_expand, t["tm_qkv"], table_blocks, side)
        if side:
            wts["mix"] = tuple(rounded)
        z, wts["wz"] = _inproj_z(hm, wts["wz"], t["tm_z"], t["tn_z"])
        return x1, z, q, kx, vx, kf, vf

    def tail(o_att, o_rnn, z, x1, mod, rows_per_mod, t):
        x2, h2 = _mix(o_att, o_rnn, z, x1, mod, rows_per_mod, g_norm_ffn2[0], *wts["mix"], 5, 6, t["tm_mix"])
        g2, wts["f2w1"], wts["f2w3"] = _ffn_up(h2, wts["f2w1"], wts["f2w3"], t["tm_up"], t["tf"])
        y, wts["f2w2"] = _ffn_down(g2, wts["f2w2"], x2, mod, rows_per_mod, 8, t["tm_down"], t["tn_down"])
        return y

    tp = _tiles(nb * seq, dff)
    tp = {k: (min(v, seq) if k.startswith("tm") else v) for k, v in tp.items()}
    cs_p, sn_p = _rope_tables(inv_row, 0, seq, seq)
    x1, z, q, kx, vx, kf, vf = trunk(x_prompt.reshape(nb * seq, d), mod_p, seq, cs_p, sn_p,
                                     seq // tp["tm_qkv"], tp)
    nblk = seq // WINDOW
    o_att = _attn(sink_vec, q, kx, vx, nb, nblk, next(n for n in (4, 2, 1) if nblk % n == 0))
    o_rnn, h_p, conv_p = _rnn_prompt(z, nb, seq, conv_w[0], conv_b[0], wrg, wig, b_rg[0], b_ig[0], lam,
                                     min(512, seq))
    y_p = tail(o_att, o_rnn, z, x1, mod_p, seq, tp)

    last = lambda a: a.reshape(nb, seq, D_KV)[:, seq - WINDOW:].reshape(1, nb, WINDOW, N_KV_HEADS, HEAD_DIM)
    k_prompt, v_prompt = last(kf), last(vf)

    ts = _tiles(ns, dff)
    cs_1, sn_1 = _rope_tables(inv_row, PAST_LEN, SUBLANES, SUBLANES)
    cs_s = jnp.broadcast_to(cs_1[0:1], (ns, LANES))
    sn_s = jnp.broadcast_to(sn_1[0:1], (ns, LANES))
    x1s, zs, qs, kxs, vxs, kfs, vfs = trunk(x_sample.reshape(ns, d), mod_s, 1, cs_s, sn_s, 1, ts)
    nsub = 2 * SUBLANES
    assert ns % nsub == 0
    key_minor = lambda c: jnp.transpose(c[0], (0, 2, 3, 1)).reshape(ns * D_KV, WINDOW)
    o_att_s = _attn_step(sink_vec, qs, kfs, vfs, key_minor(cache_k), key_minor(cache_v), nsub)
    conv_s_in = state_conv[0]
    o_rnn_s, h_s = _rnn_step(zs, conv_s_in, state_h[0], conv_w[0], conv_b[0], wrg, wig, b_rg[0], b_ig[0], lam)
    y_s = tail(o_att_s, o_rnn_s, zs, x1s, mod_s, 1, ts)

    k_sample = kfs.reshape(1, ns, 1, N_KV_HEADS, HEAD_DIM)
    v_sample = vfs.reshape(1, ns, 1, N_KV_HEADS, HEAD_DIM)
    conv_sample = jnp.concatenate([conv_s_in[:, 1:], zs[:, None, Z_RX:Z_RX + D_RNN]], axis=1)[None]

    return (y_p.reshape(nb, seq, d), y_s.reshape(ns, 1, d), k_prompt, v_prompt, k_sample, v_sample,
            h_p.reshape(1, nb, D_RNN), h_s[None], conv_p[None], conv_sample)
```

```python
import functools

import jax
import jax.numpy as jnp
from jax import lax
from jax.experimental import pallas as pl
from jax.experimental.pallas import tpu as pltpu

F32 = jnp.float32
BF16 = jnp.bfloat16

D_MODEL = 2048
HEAD_DIM = 64
N_Q_HEADS = 16
N_KV_HEADS = 4
Q_PER_KV = N_Q_HEADS // N_KV_HEADS
D_ATTN = N_Q_HEADS * HEAD_DIM
D_KV = N_KV_HEADS * HEAD_DIM
D_QK = D_ATTN + D_KV
D_QKV = D_QK + D_KV
WINDOW = 128
ROPE_THETA = 10000.0
D_RNN = 1024
N_RNN_BLOCKS = 16
RNN_BLOCK = D_RNN // N_RNN_BLOCKS
CONV_WIDTH = 4
LRU_C = 8.0
N_MOD = 9
EPS = 1e-6
PAST_LEN = 16384

LANES = 128
SUBLANES = 8
MXU_DIM = 256
VMEM_LIMIT = 56 * 1024 * 1024

Z_RX = 0
Z_RG = Z_RX + D_RNN
Z_GATT = Z_RG + D_RNN
Z_GRNN = Z_GATT + D_MODEL
D_Z = Z_GRNN + D_MODEL
DX = 2 * N_KV_HEADS * LANES
LOG2E = 1.4426950408889634
Q_SCALE = LOG2E * HEAD_DIM ** -0.5


def _params(*sem):
    return pltpu.CompilerParams(dimension_semantics=sem, vmem_limit_bytes=VMEM_LIMIT)


def _silu(x):
    return x * jax.nn.sigmoid(x)


def _modulate(x, g, shift, scale):
    ms = jnp.mean(x * x, axis=-1, keepdims=True)
    return (x * lax.rsqrt(ms + EPS) * g) * (1.0 + scale) + shift


def _mod_spec(mod, tm, rows_per_mod, width, col, ngrid):
    colf = col if callable(col) else (lambda *idx: col)
    if mod.ndim == 3:
        return pl.BlockSpec((None, 1, width), lambda *idx: ((idx[ngrid - 1] * tm) // rows_per_mod, 0, colf(*idx)))
    return pl.BlockSpec((tm, width), lambda *idx: (idx[ngrid - 1], colf(*idx)))


PREFETCH_SLOTS = 4


def _prefetched_tile(w_hbm, buf_ref, sem, tn, ntiles):
    t = pl.program_id(0)
    slots = buf_ref.shape[0]
    ahead = slots - 1

    def copy(tile):
        cols = pl.ds(pl.multiple_of(tile * tn, LANES), tn)
        return pltpu.make_async_copy(w_hbm.at[:, cols], buf_ref.at[tile % slots], sem.at[tile % slots])

    @pl.when(t == 0)
    def _():
        for k in range(min(ahead, ntiles)):
            copy(k).start()

    @pl.when(t + ahead < ntiles)
    def _():
        copy(t + ahead).start()

    copy(t).wait()
    return t % slots


def _prefetch_scratch(rows, tn, dtype):
    return [pltpu.VMEM((PREFETCH_SLOTS, rows, tn), dtype), pltpu.SemaphoreType.DMA((PREFETCH_SLOTS,))]


def _ada_kernel(c_ref, w_hbm, b_ref, o_ref, buf_ref, sem, *, tn, ntiles):
    slot = _prefetched_tile(w_hbm, buf_ref, sem, tn, ntiles)
    s = _silu(c_ref[...]).astype(BF16)
    o_ref[...] = jnp.dot(s, buf_ref[slot].astype(BF16), preferred_element_type=F32) + b_ref[...]


def _ada(c_all, w_ada, b_ada, tn):
    rows, d = c_all.shape
    n = w_ada.shape[1]
    return pl.pallas_call(
        functools.partial(_ada_kernel, tn=tn, ntiles=n // tn),
        grid=(n // tn,),
        in_specs=[pl.BlockSpec((rows, d), lambda j: (0, 0)),
                  pl.BlockSpec(memory_space=pl.ANY),
                  pl.BlockSpec((1, tn), lambda j: (0, j))],
        out_specs=pl.BlockSpec((rows, tn), lambda j: (0, j)),
        out_shape=jax.ShapeDtypeStruct((rows, n), F32),
        scratch_shapes=_prefetch_scratch(d, tn, F32),
        compiler_params=_params("arbitrary"),
        name="ada",
    )(c_all, w_ada, b_ada.reshape(1, n))


ROPE_SPAN = 64


def _rope_kernel(inv_ref, cs_ref, sn_ref, *, base, blk):
    r = pl.program_id(0)
    inv = inv_ref[...]
    lane = lax.broadcasted_iota(jnp.int32, (1, LANES), 1)
    sign = jnp.where((lane & (HEAD_DIM - 1)) < HEAD_DIM // 2, -1.0, 1.0)
    rows = lambda n, scale: (scale * lax.broadcasted_iota(jnp.int32, (n, LANES), 0)).astype(F32)
    if blk % ROPE_SPAN:
        ang = (rows(blk, 1) + (base + r * blk).astype(F32)) * inv
        cs_ref[...] = jnp.cos(ang)
        sn_ref[...] = jnp.sin(ang) * sign
        return
    ncoarse = blk // ROPE_SPAN
    coarse = (rows(max(ncoarse, SUBLANES), ROPE_SPAN) + (base + r * blk).astype(F32)) * inv
    fine = rows(ROPE_SPAN, 1) * inv
    ca, sa, cb, sb = jnp.cos(coarse), jnp.sin(coarse), jnp.cos(fine), jnp.sin(fine)
    for i in range(ncoarse):
        span = slice(i * ROPE_SPAN, (i + 1) * ROPE_SPAN)
        cs_ref[span, :] = ca[i:i + 1] * cb - sa[i:i + 1] * sb
        sn_ref[span, :] = (sa[i:i + 1] * cb + ca[i:i + 1] * sb) * sign


def _rope_tables(inv_row, base, rows, blk):
    out = jax.ShapeDtypeStruct((rows, LANES), F32)
    return pl.pallas_call(
        functools.partial(_rope_kernel, base=base, blk=blk),
        grid=(rows // blk,),
        in_specs=[pl.BlockSpec((1, LANES), lambda r: (0, 0))],
        out_specs=[pl.BlockSpec((blk, LANES), lambda r: (r, 0))] * 2,
        out_shape=[out, out],
        compiler_params=_params("arbitrary"),
        name="rope_tables",
    )(inv_row)


def _modulate_kernel(x_ref, shift_ref, scale_ref, g_ref, h_ref):
    group = 2 * SUBLANES
    per_row = shift_ref.shape[0] > 1

    def body(r, carry):
        rows = pl.ds(pl.multiple_of(r * group, group), group)
        shift = shift_ref[rows, :] if per_row else shift_ref[...]
        scale = scale_ref[rows, :] if per_row else scale_ref[...]
        h_ref[rows, :] = _modulate(x_ref[rows, :], g_ref[...], shift, scale).astype(BF16)
        return carry

    lax.fori_loop(0, x_ref.shape[0] // group, body, 0, unroll=8)


def _modulated_norm(x, mod, rows_per_mod, g, k_shift, tm):
    m_rows, d = x.shape
    return pl.pallas_call(
        _modulate_kernel,
        grid=(m_rows // tm,),
        in_specs=[pl.BlockSpec((tm, d), lambda m: (m, 0)),
                  _mod_spec(mod, tm, rows_per_mod, d, k_shift, 1),
                  _mod_spec(mod, tm, rows_per_mod, d, k_shift + 1, 1),
                  pl.BlockSpec((1, d), lambda m: (0, 0))],
        out_specs=pl.BlockSpec((tm, d), lambda m: (m, 0)),
        out_shape=jax.ShapeDtypeStruct((m_rows, d), BF16),
        compiler_params=_params("arbitrary"),
        name="modnorm",
    )(x, mod, mod, g.reshape(1, d))


HBM = pl.BlockSpec(memory_space=pl.ANY)


def _is_param(w):
    return w.dtype == F32


def _stream_weight_tiles(weights, tn, first):
    t, nt = pl.program_id(0), pl.num_programs(0)
    m, nm = pl.program_id(1), pl.num_programs(1)
    cols = lambda tile: pl.ds(pl.multiple_of(tile * tn, LANES), tn)
    fetch = lambda w_hbm, stage_ref, sem, tile: pltpu.make_async_copy(
        w_hbm.at[0, :, cols(first + tile)], stage_ref, sem.at[0])
    put = lambda wb_ref, wout_hbm, sem, tile: pltpu.make_async_copy(wb_ref, wout_hbm.at[:, cols(tile)], sem.at[1])

    @pl.when(m == 0)
    def _():
        for w_hbm, stage_ref, wb_ref, wout_hbm, sem in weights:
            @pl.when(t == 0)
            def _():
                fetch(w_hbm, stage_ref, sem, 0).start()

            fetch(w_hbm, stage_ref, sem, t).wait()

            @pl.when(t > 0)
            def _():
                put(wb_ref, wout_hbm, sem, t - 1).wait()

            wb_ref[...] = stage_ref[...].astype(BF16)
            put(wb_ref, wout_hbm, sem, t).start()

            @pl.when(t + 1 < nt)
            def _():
                fetch(w_hbm, stage_ref, sem, t + 1).start()

    @pl.when((t == nt - 1) & (m == nm - 1))
    def _():
        for w_hbm, stage_ref, wb_ref, wout_hbm, sem in weights:
            put(wb_ref, wout_hbm, sem, t).wait()


def _weight_scratch(rows, tn):
    return [pltpu.VMEM((rows, tn), F32), pltpu.VMEM((rows, tn), BF16), pltpu.SemaphoreType.DMA((2,))]


def _row_groups(tm):
    n = max(1, tm // 1024)
    return [slice(r * (tm // n), (r + 1) * (tm // n)) for r in range(n)]


def _ffn_up_rows(h_ref, w1b_ref, w3b_ref, g_ref):
    for rows in _row_groups(h_ref.shape[0]):
        h = h_ref[rows, :]
        a = jnp.dot(h, w1b_ref[...], preferred_element_type=F32)
        b = jnp.dot(h, w3b_ref[...], preferred_element_type=F32)
        g_ref[rows, :] = (_silu(a) * b).astype(BF16)


def _ffn_up_param_kernel(h_ref, w1_hbm, w3_hbm, g_ref, w1o_hbm, w3o_hbm,
                         s1_ref, w1b_ref, sem1, s3_ref, w3b_ref, sem3, *, tf):
    _stream_weight_tiles(((w1_hbm, s1_ref, w1b_ref, w1o_hbm, sem1), (w3_hbm, s3_ref, w3b_ref, w3o_hbm, sem3)),
                         tf, 0)
    _ffn_up_rows(h_ref, w1b_ref, w3b_ref, g_ref)


def _ffn_up(h, w1, w3, tm, tf):
    m_rows, d = h.shape
    dff = w1.shape[-1]
    grid = (dff // tf, m_rows // tm)
    h_spec = pl.BlockSpec((tm, d), lambda f, m: (m, 0))
    g_spec = pl.BlockSpec((tm, tf), lambda f, m: (m, f))
    g_shape = jax.ShapeDtypeStruct((m_rows, dff), BF16)
    if not _is_param(w1):
        assert grid[1] == 1

        def rounded_kernel(h_ref, w1_hbm, w3_hbm, g_ref, b1_ref, sem1, b3_ref, sem3):
            s1 = _prefetched_tile(w1_hbm, b1_ref, sem1, tf, grid[0])
            s3 = _prefetched_tile(w3_hbm, b3_ref, sem3, tf, grid[0])
            _ffn_up_rows(h_ref, b1_ref.at[s1], b3_ref.at[s3], g_ref)

        g = pl.pallas_call(
            rounded_kernel, grid=grid, in_specs=[h_spec, HBM, HBM], out_specs=g_spec, out_shape=g_shape,
            scratch_shapes=_prefetch_scratch(d, tf, BF16) + _prefetch_scratch(d, tf, BF16),
            compiler_params=_params("arbitrary", "arbitrary"), name="ffn_up_rows",
        )(h, w1, w3)
        return g, w1, w3
    w_shape = jax.ShapeDtypeStruct((d, dff), BF16)
    return pl.pallas_call(
        functools.partial(_ffn_up_param_kernel, tf=tf),
        grid=grid,
        in_specs=[h_spec, HBM, HBM],
        out_specs=[g_spec, HBM, HBM],
        out_shape=[g_shape, w_shape, w_shape],
        scratch_shapes=_weight_scratch(d, tf) + _weight_scratch(d, tf),
        compiler_params=_params("arbitrary", "arbitrary"),
        name="ffn_up",
    )(h, w1, w3)


def _ffn_down_rows(g_ref, w2b_ref, x_ref, gate_ref, o_ref):
    for rows in _row_groups(g_ref.shape[0]):
        acc = jnp.dot(g_ref[rows, :], w2b_ref[...], preferred_element_type=F32)
        gate = gate_ref[...] if gate_ref.shape[0] == 1 else gate_ref[rows, :]
        o_ref[rows, :] = x_ref[rows, :] + 0.5 * gate * acc


def _ffn_down_param_kernel(g_ref, w2_hbm, x_ref, gate_ref, o_ref, w2o_hbm, s2_ref, w2b_ref, sem2, *, tn):
    _stream_weight_tiles(((w2_hbm, s2_ref, w2b_ref, w2o_hbm, sem2),), tn, 0)
    _ffn_down_rows(g_ref, w2b_ref, x_ref, gate_ref, o_ref)


def _ffn_down(g, w2, x, mod, rows_per_mod, k_gate, tm, tn):
    m_rows, d = x.shape
    dff = g.shape[1]
    grid = (d // tn, m_rows // tm)
    gate_col = lambda n, m: k_gate * (d // tn) + n
    g_spec = pl.BlockSpec((tm, dff), lambda n, m: (m, 0))
    x_spec = pl.BlockSpec((tm, tn), lambda n, m: (m, n))
    gate_spec = _mod_spec(mod, tm, rows_per_mod, tn, gate_col, 2)
    o_shape = jax.ShapeDtypeStruct((m_rows, d), F32)
    if not _is_param(w2):
        assert grid[1] == 1

        def rounded_kernel(g_ref, w2_hbm, x_ref, gate_ref, o_ref, b2_ref, sem2):
            s2 = _prefetched_tile(w2_hbm, b2_ref, sem2, tn, grid[0])
            _ffn_down_rows(g_ref, b2_ref.at[s2], x_ref, gate_ref, o_ref)

        out = pl.pallas_call(
            rounded_kernel, grid=grid, in_specs=[g_spec, HBM, x_spec, gate_spec],
            out_specs=x_spec, out_shape=o_shape, scratch_shapes=_prefetch_scratch(dff, tn, BF16),
            compiler_params=_params("arbitrary", "arbitrary"), name="ffn_down_rows",
        )(g, w2, x, mod)
        return out, w2
    return pl.pallas_call(
        functools.partial(_ffn_down_param_kernel, tn=tn),
        grid=grid,
        in_specs=[g_spec, HBM, x_spec, gate_spec],
        out_specs=[x_spec, HBM],
        out_shape=[o_shape, jax.ShapeDtypeStruct((dff, d), BF16)],
        scratch_shapes=_weight_scratch(dff, tn),
        compiler_params=_params("arbitrary", "arbitrary"),
        name="ffn_down",
    )(g, w2, x, mod)


def _inproj_z_rows(h_ref, wb_ref, z_ref):
    for rows in _row_groups(h_ref.shape[0]):
        z_ref[rows, :] = jnp.dot(h_ref[rows, :], wb_ref[...], preferred_element_type=F32)


def _inproj_z_param_kernel(h_ref, w_hbm, z_ref, wo_hbm, s_ref, wb_ref, sem, *, tn, first):
    _stream_weight_tiles(((w_hbm, s_ref, wb_ref, wo_hbm, sem),), tn, first)
    _inproj_z_rows(h_ref, wb_ref, z_ref)


def _inproj_z(h, w, tm, tn):
    m_rows, d = h.shape
    grid = (D_Z // tn, m_rows // tm)
    h_spec = pl.BlockSpec((tm, d), lambda n, m: (m, 0))
    z_spec = pl.BlockSpec((tm, tn), lambda n, m: (m, n))
    z_shape = jax.ShapeDtypeStruct((m_rows, D_Z), F32)
    if not _is_param(w):
        assert grid[1] == 1

        def rounded_kernel(h_ref, w_hbm, z_ref, b_ref, sem):
            s = _prefetched_tile(w_hbm, b_ref, sem, tn, grid[0])
            _inproj_z_rows(h_ref, b_ref.at[s], z_ref)

        z = pl.pallas_call(
            rounded_kernel, grid=grid, in_specs=[h_spec, HBM], out_specs=z_spec, out_shape=z_shape,
            scratch_shapes=_prefetch_scratch(d, tn, BF16),
            compiler_params=_params("arbitrary", "arbitrary"), name="inproj_z_rows",
        )(h, w)
        return z, w
    return pl.pallas_call(
        functools.partial(_inproj_z_param_kernel, tn=tn, first=D_QKV // tn),
        grid=grid,
        in_specs=[h_spec, HBM],
        out_specs=[z_spec, HBM],
        out_shape=[z_shape, jax.ShapeDtypeStruct((d, D_Z), BF16)],
        scratch_shapes=_weight_scratch(d, tn),
        compiler_params=_params("arbitrary", "arbitrary"),
        name="inproj_z",
    )(h, w)


def _store_head_pairs(ref, rows, chunk, c):
    lo = lax.broadcasted_iota(jnp.int32, chunk.shape, 1) < HEAD_DIM
    swapped = pltpu.roll(chunk, HEAD_DIM, 1)
    zero = jnp.zeros_like(chunk)
    cols = (jnp.where(lo, chunk, zero), jnp.where(lo, zero, swapped),
            jnp.where(lo, swapped, zero), jnp.where(lo, zero, chunk))
    for i, col in enumerate(cols):
        ref[rows, (4 * c + i) * LANES:(4 * c + i + 1) * LANES] = col.astype(ref.dtype)


def _side_chunk_rows(side, steps):
    total = sum(w.shape[1] for w in side)
    rows = total // steps
    ok = rows * steps == total and rows % (2 * SUBLANES) == 0 and all(w.shape[1] % rows == 0 for w in side)
    return rows if ok else None


def _round_side_matrices(mats, stage_ref, ostage_ref, sem):
    m = pl.program_id(0)
    nrows = stage_ref.shape[1]
    chunks = [(w, wb, r0) for w, wb in mats for r0 in range(0, w.shape[1], nrows)]

    def fetch(c):
        w, _, r0 = chunks[c]
        return pltpu.make_async_copy(w.at[0, pl.ds(r0, nrows), :], stage_ref.at[c % 2], sem.at[c % 2])

    def put(c):
        _, wb, r0 = chunks[c]
        return pltpu.make_async_copy(ostage_ref.at[c % 2], wb.at[pl.ds(r0, nrows), :], sem.at[2 + c % 2])

    last = len(chunks) - 1
    for c in range(len(chunks)):
        @pl.when(m == c)
        def _():
            if c == 0:
                fetch(0).start()
            if c < last:
                fetch(c + 1).start()
            fetch(c).wait()
            if c >= 2:
                put(c - 2).wait()

    ostage_ref[m % 2] = stage_ref[m % 2].astype(BF16)

    def finish():
        for c in range(len(chunks)):
            @pl.when(m == c)
            def _():
                put(c).start()
                if c == last:
                    if c >= 1:
                        put(c - 1).wait()
                    put(c).wait()

    return finish


def _inproj_qkv_kernel(*refs, nsplit, nside):
    (x_ref, shift_ref, scale_ref, g_ref, w_ref, cs_ref, sn_ref, gqk_ref, red_ref, exp_ref) = refs[:10]
    side_in = refs[10:10 + nside]
    h_ref, q_ref, kx_ref, vx_ref, kf_ref, vf_ref = refs[10 + nside:16 + nside]
    side_out = refs[16 + nside:16 + 2 * nside]
    wb_ref = refs[16 + 2 * nside]

    @pl.when(pl.program_id(0) == 0)
    def _():
        wb_ref[...] = w_ref[...].astype(BF16)

    finish_side = _round_side_matrices(list(zip(side_in, side_out)), *refs[17 + 2 * nside:]) if nside else None

    step = x_ref.shape[0] // nsplit
    for r in range(nsplit):
        rows = slice(r * step, (r + 1) * step)
        shift = shift_ref[...] if shift_ref.shape[0] == 1 else shift_ref[rows, :]
        scale = scale_ref[...] if scale_ref.shape[0] == 1 else scale_ref[rows, :]
        h = _modulate(x_ref[rows, :], g_ref[...], shift, scale).astype(BF16)
        h_ref[rows, :] = h
        acc = jnp.dot(h, wb_ref[...], preferred_element_type=F32)
        qk = acc[:, :D_QK]
        hm = jnp.dot((qk * qk).astype(BF16), red_ref[...], preferred_element_type=F32)
        hi = hm.astype(BF16)
        lo = (hm - hi.astype(F32)).astype(BF16)
        ms = jnp.dot(jnp.concatenate([hi, lo], axis=1), exp_ref[...], preferred_element_type=F32)
        y = qk * lax.rsqrt(ms + EPS) * gqk_ref[...]
        cs = cs_ref[rows, :]
        sn = sn_ref[rows, :]
        lane = lax.broadcasted_iota(jnp.int32, cs.shape, 1)
        first_half = (lane & (HEAD_DIM - 1)) < HEAD_DIM // 2
        for c in range(D_QK // LANES):
            yc = y[:, c * LANES:(c + 1) * LANES]
            partner = jnp.where(first_half,
                                pltpu.roll(yc, LANES - HEAD_DIM // 2, 1),
                                pltpu.roll(yc, HEAD_DIM // 2, 1))
            rot = yc * cs + partner * sn
            if c < D_ATTN // LANES:
                q_ref[rows, c * LANES:(c + 1) * LANES] = (rot * Q_SCALE).astype(BF16)
            else:
                kf_ref[rows, c * LANES - D_ATTN:(c + 1) * LANES - D_ATTN] = rot
                _store_head_pairs(kx_ref, rows, rot, c - D_ATTN // LANES)
        v = acc[:, D_QK:]
        vf_ref[rows, :] = v
        for c in range(D_KV // LANES):
            _store_head_pairs(vx_ref, rows, v[:, c * LANES:(c + 1) * LANES], c)
    if finish_side is not None:
        finish_side()


def _inproj_qkv(x, mod, rows_per_mod, g, k_shift, w_in, cs, sn, gqk, head_reduce, head_expand, tm, table_blocks,
                side=()):
    m_rows, d = x.shape
    steps = m_rows // tm
    row = lambda width: pl.BlockSpec((tm, width), lambda m: (m, 0))
    table = pl.BlockSpec((tm, LANES), lambda m: (m % table_blocks, 0))
    nsplit = 2 if tm % (2 * MXU_DIM) == 0 else 1
    side_scratch = []
    if side:
        width = side[0].shape[2]
        chunk = _side_chunk_rows(side, steps)
        assert chunk is not None and all(w.shape[2] == width for w in side)
        side_scratch = [pltpu.VMEM((2, chunk, width), F32), pltpu.VMEM((2, chunk, width), BF16),
                        pltpu.SemaphoreType.DMA((4,))]
    return pl.pallas_call(
        functools.partial(_inproj_qkv_kernel, nsplit=nsplit, nside=len(side)),
        grid=(steps,),
        in_specs=[row(d),
                  _mod_spec(mod, tm, rows_per_mod, d, k_shift, 1),
                  _mod_spec(mod, tm, rows_per_mod, d, k_shift + 1, 1),
                  pl.BlockSpec((1, d), lambda m: (0, 0)),
                  pl.BlockSpec((None, d, D_QKV), lambda m: (0, 0, 0), pipeline_mode=pl.Buffered(1)),
                  table, table,
                  pl.BlockSpec((1, D_QK), lambda m: (0, 0)),
                  pl.BlockSpec(head_reduce.shape, lambda m: (0, 0)),
                  pl.BlockSpec(head_expand.shape, lambda m: (0, 0))] + [HBM] * len(side),
        out_specs=[row(d), row(D_ATTN), row(DX), row(DX), row(D_KV), row(D_KV)] + [HBM] * len(side),
        out_shape=[jax.ShapeDtypeStruct((m_rows, d), BF16),
                   jax.ShapeDtypeStruct((m_rows, D_ATTN), BF16),
                   jax.ShapeDtypeStruct((m_rows, DX), BF16),
                   jax.ShapeDtypeStruct((m_rows, DX), BF16),
                   jax.ShapeDtypeStruct((m_rows, D_KV), F32),
                   jax.ShapeDtypeStruct((m_rows, D_KV), F32)]
        + [jax.ShapeDtypeStruct(w.shape[1:], BF16) for w in side],
        scratch_shapes=[pltpu.VMEM((d, D_QKV), BF16)] + side_scratch,
        compiler_params=_params("arbitrary"),
        name="inproj_qkv",
    )(x, mod, mod, g.reshape(1, d), w_in, cs, sn, gqk, head_reduce, head_expand, *side)


def _attn_bias(bias_ref, tq, past_off):
    nk = 2 * WINDOW
    ri = lax.broadcasted_iota(jnp.int32, (2 * tq, 2 * nk), 0) & (tq - 1)
    kj = lax.broadcasted_iota(jnp.int32, (2 * tq, 2 * nk), 1) & (nk - 1)
    visible = ((kj < WINDOW) & (kj > ri + past_off)) | ((kj >= WINDOW) & (kj - WINDOW <= ri))
    bias_ref[...] = jnp.where(visible, 0.0, -jnp.inf)


def _attn_blocks(sink_ref, bias_of, q_ref, past_of, kc_ref, vc_ref, nsub, tq, store):
    nk = 2 * WINDOW
    rows = 2 * tq
    first_rows = lax.broadcasted_iota(jnp.int32, (rows, 1), 0) < tq
    lo_lanes = lax.broadcasted_iota(jnp.int32, (rows, LANES), 1) < HEAD_DIM
    ones_lo = (lax.broadcasted_iota(jnp.int32, (nk, LANES), 1) < HEAD_DIM).astype(BF16)
    ones_hi = (1 - ones_lo.astype(F32)).astype(BF16)
    nt = (((1,), (1,)), ((), ()))

    def keys(which, s, col):
        cols = slice(col * LANES, (col + 1) * LANES)
        past = past_of(s)
        return [past[which][past[2]:past[2] + WINDOW, cols], (kc_ref, vc_ref)[which][s * tq:(s + 1) * tq, cols]]

    for s in range(nsub):
        for g in range(N_KV_HEADS):
            c0, c1 = 2 * g, 2 * g + 1
            q4 = jnp.concatenate([q_ref[s * tq:(s + 1) * tq, c0 * LANES:(c0 + 1) * LANES],
                                  q_ref[s * tq:(s + 1) * tq, c1 * LANES:(c1 + 1) * LANES]], axis=0)
            kk = jnp.concatenate(keys(0, s, c0) + keys(0, s, c1), axis=0)
            sc = lax.dot_general(q4, kk, nt, preferred_element_type=F32) + bias_of(s)[...]
            sk_lo = jnp.where(first_rows, sink_ref[4 * g], sink_ref[4 * g + 2]) * LOG2E
            sk_hi = jnp.where(first_rows, sink_ref[4 * g + 1], sink_ref[4 * g + 3]) * LOG2E
            m_lo = jnp.maximum(jnp.max(sc[:, :nk], axis=-1, keepdims=True), sk_lo)
            m_hi = jnp.maximum(jnp.max(sc[:, nk:], axis=-1, keepdims=True), sk_hi)
            p = jnp.concatenate([jnp.exp2(sc[:, :nk] - m_lo), jnp.exp2(sc[:, nk:] - m_hi)], axis=1).astype(BF16)
            vv = jnp.concatenate(
                [jnp.concatenate([jnp.concatenate(keys(1, s, c0), axis=0), ones_lo], axis=1),
                 jnp.concatenate([jnp.concatenate(keys(1, s, c1), axis=0), ones_hi], axis=1)],
                axis=0)
            o = jnp.dot(p, vv, preferred_element_type=F32)
            denom = o[:, LANES:] + jnp.where(lo_lanes, jnp.exp2(sk_lo - m_lo), jnp.exp2(sk_hi - m_hi))
            store(s, c0, c1, o[:, :LANES] / denom)


def _attn_kernel(sink_ref, q_ref, kp_ref, kc_ref, vp_ref, vc_ref, o_ref, bias0_ref, bias_ref, *, nq):
    tq = WINDOW
    _attn_bias(bias0_ref, tq, jnp.where(pl.program_id(1) > 0, 0, WINDOW))
    if nq > 1:
        _attn_bias(bias_ref, tq, 0)

    def store(s, c0, c1, out):
        o_ref[s * tq:(s + 1) * tq, c0 * LANES:(c0 + 1) * LANES] = out[:tq].astype(BF16)
        o_ref[s * tq:(s + 1) * tq, c1 * LANES:(c1 + 1) * LANES] = out[tq:].astype(BF16)

    past_of = lambda s: (kp_ref, vp_ref, 0) if s == 0 else (kc_ref, vc_ref, (s - 1) * WINDOW)
    bias_of = lambda s: bias0_ref if s == 0 else bias_ref
    _attn_blocks(sink_ref, bias_of, q_ref, past_of, kc_ref, vc_ref, nq, tq, store)


def _attn(sinks, q, kx, vx, nbatch, nblk, nq):
    steps = nblk // nq
    cur = lambda b, n: (b * steps + n, 0)
    past = lambda b, n: (b * nblk + jnp.maximum(n * nq - 1, 0), 0)
    kv_cur = pl.BlockSpec((nq * WINDOW, DX), cur)
    kv_past = pl.BlockSpec((WINDOW, DX), past)
    bias = pltpu.VMEM((2 * WINDOW, 4 * WINDOW), F32)
    return pl.pallas_call(
        functools.partial(_attn_kernel, nq=nq),
        grid=(nbatch, steps),
        in_specs=[pl.BlockSpec(memory_space=pltpu.SMEM), pl.BlockSpec((nq * WINDOW, D_ATTN), cur),
                  kv_past, kv_cur, kv_past, kv_cur],
        out_specs=pl.BlockSpec((nq * WINDOW, D_ATTN), cur),
        out_shape=jax.ShapeDtypeStruct(q.shape, BF16),
        scratch_shapes=[bias, bias],
        compiler_params=_params("arbitrary", "arbitrary"),
        name="attn",
    )(sinks, q, kx, kx, vx, vx)


def _rows_from_lanes(row, nrows):
    out_rows = max(nrows, SUBLANES)
    rep = jnp.broadcast_to(row, (out_rows, row.shape[1]))
    ridx = lax.broadcasted_iota(jnp.int32, (out_rows, HEAD_DIM), 0)
    out = jnp.zeros((out_rows, HEAD_DIM), row.dtype)
    for i in range(nrows):
        out = jnp.where(ridx == i, rep[:, i * HEAD_DIM:(i + 1) * HEAD_DIM], out)
    return out


def _attn_step_kernel(sink_ref, q_ref, kn_ref, vn_ref, ck_ref, cv_ref, o_ref, *, nsub):
    grp = lax.broadcasted_iota(jnp.int32, (N_Q_HEADS, HEAD_DIM), 0) >> 2
    bias = jnp.where(lax.broadcasted_iota(jnp.int32, (N_Q_HEADS, WINDOW), 1) >= 1, 0.0, -jnp.inf)
    hcol = lax.broadcasted_iota(jnp.int32, (N_Q_HEADS, 1), 0)
    sk = jnp.zeros((N_Q_HEADS, 1), F32)
    for h in range(N_Q_HEADS):
        sk = jnp.where(hcol == h, sink_ref[h] * LOG2E, sk)
    nt = (((1,), (1,)), ((), ()))
    qf = q_ref[...].astype(F32)

    def own_group(x):
        out = jnp.zeros((N_Q_HEADS, HEAD_DIM), F32)
        for g in range(N_KV_HEADS):
            out = jnp.where(grp == g, x[:, g * HEAD_DIM:(g + 1) * HEAD_DIM], out)
        return out

    for s in range(nsub):
        qh = _rows_from_lanes(qf[s:s + 1, :], N_Q_HEADS)
        zero = jnp.zeros_like(qh)
        q_bd = jnp.concatenate([jnp.where(grp == g, qh, zero) for g in range(N_KV_HEADS)], axis=1)
        kt = ck_ref[s * D_KV:(s + 1) * D_KV, :].astype(BF16)
        vt = cv_ref[s * D_KV:(s + 1) * D_KV, :].astype(BF16)
        sc = jnp.dot(q_bd.astype(BF16), kt, preferred_element_type=F32) + bias
        s_new = jnp.sum(q_bd * kn_ref[s:s + 1, :], axis=-1, keepdims=True)
        m = jnp.maximum(jnp.maximum(jnp.max(sc, axis=-1, keepdims=True), s_new), sk)
        p = jnp.exp2(sc - m)
        p_new = jnp.exp2(s_new - m)
        denom = jnp.sum(p, axis=-1, keepdims=True) + p_new + jnp.exp2(sk - m)
        o_all = lax.dot_general(p.astype(BF16), vt, nt, preferred_element_type=F32)
        v_new = own_group(jnp.broadcast_to(vn_ref[s:s + 1, :], (N_Q_HEADS, D_KV)))
        o = (own_group(o_all) + p_new * v_new) / denom
        for c in range(N_Q_HEADS // 2):
            o_ref[s:s + 1, c * LANES:(c + 1) * LANES] = jnp.concatenate(
                [o[2 * c:2 * c + 1, :], o[2 * c + 1:2 * c + 2, :]], axis=1)


def _attn_step(sinks, q, k_new, v_new, cache_k, cache_v, nsub):
    ns = q.shape[0]
    row = lambda width: pl.BlockSpec((nsub, width), lambda b: (b, 0))
    cache = pl.BlockSpec((nsub * D_KV, WINDOW), lambda b: (b, 0))
    return pl.pallas_call(
        functools.partial(_attn_step_kernel, nsub=nsub),
        grid=(ns // nsub,),
        in_specs=[pl.BlockSpec(memory_space=pltpu.SMEM), row(D_ATTN), row(D_KV), row(D_KV), cache, cache],
        out_specs=row(D_ATTN),
        out_shape=jax.ShapeDtypeStruct((ns, D_ATTN), F32),
        compiler_params=_params("arbitrary"),
        name="attn_step",
    )(sinks, q, k_new, v_new, cache_k, cache_v)


def _softplus(x):
    return jnp.maximum(x, 0.0) + jnp.log1p(jnp.exp(-jnp.abs(x)))


def _lru_coeffs(xc, wrg_ref, wig_ref, brg, big, lam):
    xb = xc.astype(BF16)
    ngroups = D_RNN // MXU_DIM
    rs, igs = [], []
    for c in range(ngroups):
        xg = xb[:, c * MXU_DIM:(c + 1) * MXU_DIM]
        rs.append(jnp.dot(xg, wrg_ref[c], preferred_element_type=F32))
        igs.append(jnp.dot(xg, wig_ref[c], preferred_element_type=F32))
    r = jax.nn.sigmoid(jnp.concatenate(rs, axis=1) + brg)
    ig = jax.nn.sigmoid(jnp.concatenate(igs, axis=1) + big)
    log_a = -LRU_C * r * _softplus(-lam)
    a = jnp.exp(log_a)
    one_minus_a2 = -jnp.tanh(log_a) * (1.0 + a * a)
    root = jnp.where(one_minus_a2 > 0.0, one_minus_a2 * lax.rsqrt(one_minus_a2), 0.0)
    u = root * (ig * xc)
    return a, u


def _rnn_prompt_kernel(rx_ref, rg_ref, cw_ref, cb_ref, wrg_ref, wig_ref, brg_ref, big_ref, lam_ref,
                       o_ref, hlast_ref, conv_ref, xs_ref, a_ref, h_ref, carry_ref, *, tc):
    t = pl.program_id(1)
    pad = SUBLANES

    @pl.when(t == 0)
    def _():
        xs_ref[0:pad, :] = jnp.zeros((pad, D_RNN), F32)
        carry_ref[...] = jnp.zeros_like(carry_ref)

    x = rx_ref[...]
    xs_ref[pad:pad + tc, :] = x
    xc = cb_ref[...] + x * cw_ref[CONV_WIDTH - 1:CONV_WIDTH, :]
    for k in range(1, CONV_WIDTH):
        xc = xc + xs_ref[pad - k:pad - k + tc, :] * cw_ref[CONV_WIDTH - 1 - k:CONV_WIDTH - k, :]
    tail = xs_ref[tc:tc + pad, :]
    xs_ref[0:pad, :] = tail
    conv_ref[...] = tail[pad - (CONV_WIDTH - 1):, :]

    a, u = _lru_coeffs(xc, wrg_ref, wig_ref, brg_ref[...], big_ref[...], lam_ref[...])
    a_ref[...] = a
    h_ref[...] = u

    row = lax.broadcasted_iota(jnp.int32, (SUBLANES, D_RNN), 0)

    def body(r, carry):
        off = pl.multiple_of(r * SUBLANES, SUBLANES)
        av = a_ref[pl.ds(off, SUBLANES), :]
        hv = h_ref[pl.ds(off, SUBLANES), :]
        hv = hv + jnp.where(row == 0, av * carry, 0.0)
        for sft in (1, 2, 4):
            keep = row >= sft
            h_sh = jnp.where(keep, pltpu.roll(hv, sft, 0), 0.0)
            hv = av * h_sh + hv
            if sft < SUBLANES // 2:
                av = av * jnp.where(keep, pltpu.roll(av, sft, 0), 1.0)
        h_ref[pl.ds(off, SUBLANES), :] = hv
        return jnp.broadcast_to(hv[SUBLANES - 1:SUBLANES, :], (SUBLANES, D_RNN))

    carry = lax.fori_loop(0, tc // SUBLANES, body, carry_ref[...])
    carry_ref[...] = carry
    hlast_ref[...] = carry[0:1, :]
    o_ref[...] = (h_ref[...] * jax.nn.gelu(rg_ref[...])).astype(BF16)


def _rnn_prompt(z, nbatch, seq, conv_w, conv_b, wrg, wig, b_rg, b_ig, lam, tc):
    nchunk = seq // tc
    rx_blk = Z_RX // D_RNN
    rg_blk = Z_RG // D_RNN
    vec = pl.BlockSpec((1, D_RNN), lambda b, t: (0, 0))
    wspec = pl.BlockSpec(wrg.shape, lambda b, t: (0, 0, 0))
    return pl.pallas_call(
        functools.partial(_rnn_prompt_kernel, tc=tc),
        grid=(nbatch, nchunk),
        in_specs=[pl.BlockSpec((tc, D_RNN), lambda b, t: (b * nchunk + t, rx_blk)),
                  pl.BlockSpec((tc, D_RNN), lambda b, t: (b * nchunk + t, rg_blk)),
                  pl.BlockSpec((CONV_WIDTH, D_RNN), lambda b, t: (0, 0)),
                  vec, wspec, wspec, vec, vec, vec],
        out_specs=[pl.BlockSpec((tc, D_RNN), lambda b, t: (b * nchunk + t, 0)),
                   pl.BlockSpec((None, 1, D_RNN), lambda b, t: (b, 0, 0)),
                   pl.BlockSpec((None, CONV_WIDTH - 1, D_RNN), lambda b, t: (b, 0, 0))],
        out_shape=[jax.ShapeDtypeStruct((nbatch * seq, D_RNN), BF16),
                   jax.ShapeDtypeStruct((nbatch, 1, D_RNN), F32),
                   jax.ShapeDtypeStruct((nbatch, CONV_WIDTH - 1, D_RNN), F32)],
        scratch_shapes=[pltpu.VMEM((tc + SUBLANES, D_RNN), F32),
                        pltpu.VMEM((tc, D_RNN), F32),
                        pltpu.VMEM((tc, D_RNN), F32),
                        pltpu.VMEM((SUBLANES, D_RNN), F32)],
        compiler_params=_params("arbitrary", "arbitrary"),
        name="rnn_prompt",
    )(z, z, conv_w, conv_b.reshape(1, D_RNN), wrg, wig, b_rg.reshape(1, D_RNN),
      b_ig.reshape(1, D_RNN), lam.reshape(1, D_RNN))


def _rnn_step_kernel(rx_ref, rg_ref, c0_ref, c1_ref, c2_ref, h0_ref, cw_ref, cb_ref, wrg_ref, wig_ref,
                     brg_ref, big_ref, lam_ref, o_ref, h_ref):
    x = rx_ref[...]
    xc = (cb_ref[...] + c0_ref[...] * cw_ref[0:1, :] + c1_ref[...] * cw_ref[1:2, :]
          + c2_ref[...] * cw_ref[2:3, :] + x * cw_ref[3:4, :])
    a, u = _lru_coeffs(xc, wrg_ref, wig_ref, brg_ref[...], big_ref[...], lam_ref[...])
    h = a * h0_ref[...] + u
    h_ref[...] = h
    o_ref[...] = (h * jax.nn.gelu(rg_ref[...])).astype(BF16)


def _rnn_step(z, conv_state, h0, conv_w, conv_b, wrg, wig, b_rg, b_ig, lam):
    rows = z.shape[0]
    full = lambda shape: pl.BlockSpec(shape, lambda i: (0,) * len(shape))
    act = full((rows, D_RNN))
    vec = full((1, D_RNN))
    return pl.pallas_call(
        _rnn_step_kernel,
        grid=(1,),
        in_specs=[pl.BlockSpec((rows, D_RNN), lambda i: (0, Z_RX // D_RNN)),
                  pl.BlockSpec((rows, D_RNN), lambda i: (0, Z_RG // D_RNN)),
                  act, act, act, act, full((CONV_WIDTH, D_RNN)), vec,
                  full(wrg.shape), full(wig.shape), vec, vec, vec],
        out_specs=[act, act],
        out_shape=[jax.ShapeDtypeStruct((rows, D_RNN), BF16),
                   jax.ShapeDtypeStruct((rows, D_RNN), F32)],
        compiler_params=_params("arbitrary"),
        name="rnn_step",
    )(z, z, conv_state[:, 0], conv_state[:, 1], conv_state[:, 2], h0, conv_w,
      conv_b.reshape(1, D_RNN), wrg, wig, b_rg.reshape(1, D_RNN), b_ig.reshape(1, D_RNN),
      lam.reshape(1, D_RNN))


def _mix_kernel(oa_ref, or_ref, ga_ref, gr_ref, x_ref, gate_ref, shift_ref, scale_ref, g_ref,
                wpa_ref, wpr_ref, wout_ref, o_ref, h_ref):
    pa = jnp.dot(oa_ref[...].astype(BF16), wpa_ref[...], preferred_element_type=F32)
    pr = jnp.dot(or_ref[...], wpr_ref[...], preferred_element_type=F32)
    mix = jax.nn.sigmoid(ga_ref[...]) * pa + jax.nn.sigmoid(gr_ref[...]) * pr
    x = x_ref[...] + gate_ref[...] * jnp.dot(mix.astype(BF16), wout_ref[...], preferred_element_type=F32)
    o_ref[...] = x
    h_ref[...] = _modulate(x, g_ref[...], shift_ref[...], scale_ref[...]).astype(BF16)


def _mix(o_att, o_rnn, z, x, mod, rows_per_mod, g_next, w_pa, w_pr, w_out, k_gate, k_shift_next, tm):
    m_rows, d = x.shape
    const = lambda shape: pl.BlockSpec(shape, lambda m: (0, 0), pipeline_mode=pl.Buffered(1))
    row = lambda width, col: pl.BlockSpec((tm, width), lambda m: (m, col))
    mspec = lambda k: _mod_spec(mod, tm, rows_per_mod, d, k, 1)
    return pl.pallas_call(
        _mix_kernel,
        grid=(m_rows // tm,),
        in_specs=[row(D_ATTN, 0), row(D_RNN, 0), row(d, Z_GATT // d), row(d, Z_GRNN // d), row(d, 0),
                  mspec(k_gate), mspec(k_shift_next), mspec(k_shift_next + 1), const((1, d)),
                  const(w_pa.shape), const(w_pr.shape), const(w_out.shape)],
        out_specs=[row(d, 0), row(d, 0)],
        out_shape=[jax.ShapeDtypeStruct((m_rows, d), F32), jax.ShapeDtypeStruct((m_rows, d), BF16)],
        compiler_params=_params("arbitrary"),
        name="mix",
    )(o_att, o_rnn, z, z, x, mod, mod, mod, g_next.reshape(1, d), w_pa, w_pr, w_out)


def _block_diag(w, group):
    n, r, _ = w.shape
    eye = jnp.eye(group, dtype=w.dtype)
    wg = w.reshape(n // group, group, r, r)
    return jnp.einsum("ngij,gh->ngihj", wg, eye).reshape(n // group, group * r, group * r)


def _tiles(rows, dff):
    pick = lambda n, prefs: next((t for t in prefs if n % t == 0), n)
    return dict(
        tm_norm=pick(rows, (1024, 512)),
        tm_up=pick(rows, (2048, 1024, 512)), tf=pick(dff, (512,)),
        tm_down=pick(rows, (1024, 512)), tn_down=512,
        tm_z=pick(rows, (1024, 512)), tn_z=D_QKV,
        tm_qkv=pick(rows, (512,)),
        tm_mix=pick(rows, (256,)),
    )


def kernel(x_prompt, x_sample, c_prompt, c_sample, cache_k, cache_v, state_h, state_conv, w_ada, b_ada,
           g_norm_ffn1, g_norm_mix, g_norm_ffn2, ffn1_w1, ffn1_w3, ffn1_w2, ffn2_w1, ffn2_w3, ffn2_w2,
           w_in, g_q, g_k, sinks, conv_w, conv_b, w_rg, b_rg, w_ig, b_ig, lru_lambda, w_pa, w_pr, w_out):
    nb, seq, d = x_prompt.shape
    ns = x_sample.shape[0]
    dff = ffn1_w1.shape[2]
    assert d == D_MODEL and w_ada.shape[0] == 1 and x_sample.shape[1] == 1 and cache_k.shape[2] == WINDOW
    assert seq % WINDOW == 0 and w_in.shape[2] == D_QKV + D_Z

    group = MXU_DIM // RNN_BLOCK
    wrg = _block_diag(w_rg[0], group).astype(BF16)
    wig = _block_diag(w_ig[0], group).astype(BF16)
    gqk = jnp.concatenate([jnp.tile(g_q[0], N_Q_HEADS), jnp.tile(g_k[0], N_KV_HEADS)]).reshape(1, D_QK)
    head_of = jnp.arange(D_QK) // HEAD_DIM
    onehot = (head_of[:, None] == jnp.arange(LANES)[None, :]).astype(F32)
    head_reduce = (onehot / HEAD_DIM).astype(BF16)
    head_expand = jnp.concatenate([onehot.T, onehot.T], axis=0).astype(BF16)
    inv = ROPE_THETA ** (-jnp.arange(HEAD_DIM // 2, dtype=F32) * 2.0 / HEAD_DIM)
    inv_row = jnp.tile(inv, LANES // (HEAD_DIM // 2)).reshape(1, LANES)
    sink_vec = sinks[0]
    lam = lru_lambda[0]

    pad_rows = (-(nb + ns)) % SUBLANES
    c_all = jnp.concatenate([c_prompt, c_sample, jnp.zeros((pad_rows, d), F32)], axis=0)
    mod_all = _ada(c_all, w_ada[0], b_ada[0], 1024)
    mod_p = mod_all[:nb].reshape(nb, 1, N_MOD * d)
    mod_s = mod_all[nb:nb + ns]

    wts = dict(f1w1=ffn1_w1, f1w3=ffn1_w3, f1w2=ffn1_w2, f2w1=ffn2_w1, f2w3=ffn2_w3, f2w2=ffn2_w2, wz=w_in)

    def trunk(x, mod, rows_per_mod, cs, sn, table_blocks, t):
        h1 = _modulated_norm(x, mod, rows_per_mod, g_norm_ffn1[0], 0, t["tm_norm"])
        g1, wts["f1w1"], wts["f1w3"] = _ffn_up(h1, wts["f1w1"], wts["f1w3"], t["tm_up"], t["tf"])
        x1, wts["f1w2"] = _ffn_down(g1, wts["f1w2"], x, mod, rows_per_mod, 2, t["tm_down"], t["tn_down"])
        side = ()
        if "mix" not in wts:
            side = (w_pa, w_pr, w_out)
            if _side_chunk_rows(side, x.shape[0] // t["tm_qkv"]) is None:
                side, wts["mix"] = (), tuple(w[0].astype(BF16) for w in side)
        hm, q, kx, vx, kf, vf, *rounded = _inproj_qkv(x1, mod, rows_per_mod, g_norm_mix[0], 3, w_in, cs, sn, gqk,
                                                      head_reduce, head_expand, t["tm_qkv"], table_blocks, side)
        if side:
            wts["mix"] = tuple(rounded)
        z, wts["wz"] = _inproj_z(hm, wts["wz"], t["tm_z"], t["tn_z"])
        return x1, z, q, kx, vx, kf, vf

    def tail(o_att, o_rnn, z, x1, mod, rows_per_mod, t):
        x2, h2 = _mix(o_att, o_rnn, z, x1, mod, rows_per_mod, g_norm_ffn2[0], *wts["mix"], 5, 6, t["tm_mix"])
        g2, wts["f2w1"], wts["f2w3"] = _ffn_up(h2, wts["f2w1"], wts["f2w3"], t["tm_up"], t["tf"])
        y, wts["f2w2"] = _ffn_down(g2, wts["f2w2"], x2, mod, rows_per_mod, 8, t["tm_down"], t["tn_down"])
        return y

    tp = _tiles(nb * seq, dff)
    tp = {k: (min(v, seq) if k.startswith("tm") else v) for k, v in tp.items()}
    cs_p, sn_p = _rope_tables(inv_row, 0, seq, seq)
    x1, z, q, kx, vx, kf, vf = trunk(x_prompt.reshape(nb * seq, d), mod_p, seq, cs_p, sn_p,
                                     seq // tp["tm_qkv"], tp)
    nblk = seq // WINDOW
    o_att = _attn(sink_vec, q, kx, vx, nb, nblk, next(n for n in (8, 4, 2, 1) if nblk % n == 0))
    o_rnn, h_p, conv_p = _rnn_prompt(z, nb, seq, conv_w[0], conv_b[0], wrg, wig, b_rg[0], b_ig[0], lam,
                                     min(1024, seq))
    y_p = tail(o_att, o_rnn, z, x1, mod_p, seq, tp)

    last = lambda a: a.reshape(nb, seq, D_KV)[:, seq - WINDOW:].reshape(1, nb, WINDOW, N_KV_HEADS, HEAD_DIM)
    k_prompt, v_prompt = last(kf), last(vf)

    ts = _tiles(ns, dff)
    cs_1, sn_1 = _rope_tables(inv_row, PAST_LEN, SUBLANES, SUBLANES)
    cs_s = jnp.broadcast_to(cs_1[0:1], (ns, LANES))
    sn_s = jnp.broadcast_to(sn_1[0:1], (ns, LANES))
    x1s, zs, qs, kxs, vxs, kfs, vfs = trunk(x_sample.reshape(ns, d), mod_s, 1, cs_s, sn_s, 1, ts)
    nsub = 2 * SUBLANES
    assert ns % nsub == 0
    key_minor = lambda c: jnp.transpose(c[0], (0, 2, 3, 1)).reshape(ns * D_KV, WINDOW)
    o_att_s = _attn_step(sink_vec, qs, kfs, vfs, key_minor(cache_k), key_minor(cache_v), nsub)
    conv_s_in = state_conv[0]
    o_rnn_s, h_s = _rnn_step(zs, conv_s_in, state_h[0], conv_w[0], conv_b[0], wrg, wig, b_rg[0], b_ig[0], lam)
    y_s = tail(o_att_s, o_rnn_s, zs, x1s, mod_s, 1, ts)

    k_sample = kfs.reshape(1, ns, 1, N_KV_HEADS, HEAD_DIM)
    v_sample = vfs.reshape(1, ns, 1, N_KV_HEADS, HEAD_DIM)
    conv_sample = jnp.concatenate([conv_s_in[:, 1:], zs[:, None, Z_RX:Z_RX + D_RNN]], axis=1)[None]

    return (y_p.reshape(nb, seq, d), y_s.reshape(ns, 1, d), k_prompt, v_prompt, k_sample, v_sample,
            h_p.reshape(1, nb, D_RNN), h_s[None], conv_p[None], conv_sample)
```

```python
import functools

import jax
import jax.numpy as jnp
from jax import lax
from jax.experimental import pallas as pl
from jax.experimental.pallas import tpu as pltpu

F32 = jnp.float32
BF16 = jnp.bfloat16

D_MODEL = 2048
HEAD_DIM = 64
N_Q_HEADS = 16
N_KV_HEADS = 4
Q_PER_KV = N_Q_HEADS // N_KV_HEADS
D_ATTN = N_Q_HEADS * HEAD_DIM
D_KV = N_KV_HEADS * HEAD_DIM
D_QK = D_ATTN + D_KV
D_QKV = D_QK + D_KV
WINDOW = 128
ROPE_THETA = 10000.0
D_RNN = 1024
N_RNN_BLOCKS = 16
RNN_BLOCK = D_RNN // N_RNN_BLOCKS
CONV_WIDTH = 4
LRU_C = 8.0
N_MOD = 9
EPS = 1e-6
PAST_LEN = 16384

LANES = 128
SUBLANES = 8
MXU_DIM = 256
VMEM_LIMIT = 56 * 1024 * 1024

Z_RX = 0
Z_RG = Z_RX + D_RNN
Z_GATT = Z_RG + D_RNN
Z_GRNN = Z_GATT + D_MODEL
D_Z = Z_GRNN + D_MODEL
DX = 2 * N_KV_HEADS * LANES
LOG2E = 1.4426950408889634
Q_SCALE = LOG2E * HEAD_DIM ** -0.5


def _params(*sem):
    return pltpu.CompilerParams(dimension_semantics=sem, vmem_limit_bytes=VMEM_LIMIT)


def _silu(x):
    return x * jax.nn.sigmoid(x)


def _modulate(x, g, shift, scale):
    ms = jnp.mean(x * x, axis=-1, keepdims=True)
    return (x * lax.rsqrt(ms + EPS) * g) * (1.0 + scale) + shift


def _mod_spec(mod, tm, rows_per_mod, width, col, ngrid):
    colf = col if callable(col) else (lambda *idx: col)
    if mod.ndim == 3:
        return pl.BlockSpec((None, 1, width), lambda *idx: ((idx[ngrid - 1] * tm) // rows_per_mod, 0, colf(*idx)))
    return pl.BlockSpec((tm, width), lambda *idx: (idx[ngrid - 1], colf(*idx)))


PREFETCH_SLOTS = 3


def _prefetched_tile(w_hbm, buf_ref, sem, tn, ntiles):
    t = pl.program_id(0)
    slots = buf_ref.shape[0]
    ahead = slots - 1

    def copy(tile):
        cols = pl.ds(pl.multiple_of(tile * tn, LANES), tn)
        return pltpu.make_async_copy(w_hbm.at[:, cols], buf_ref.at[tile % slots], sem.at[tile % slots])

    @pl.when(t == 0)
    def _():
        for k in range(min(ahead, ntiles)):
            copy(k).start()

    @pl.when(t + ahead < ntiles)
    def _():
        copy(t + ahead).start()

    copy(t).wait()
    return t % slots


def _prefetch_scratch(rows, tn, dtype):
    return [pltpu.VMEM((PREFETCH_SLOTS, rows, tn), dtype), pltpu.SemaphoreType.DMA((PREFETCH_SLOTS,))]


def _ada_kernel(c_ref, w_hbm, b_ref, o_ref, buf_ref, sem, *, tn, ntiles):
    slot = _prefetched_tile(w_hbm, buf_ref, sem, tn, ntiles)
    s = _silu(c_ref[...]).astype(BF16)
    o_ref[...] = jnp.dot(s, buf_ref[slot].astype(BF16), preferred_element_type=F32) + b_ref[...]


def _ada(c_all, w_ada, b_ada, tn):
    rows, d = c_all.shape
    n = w_ada.shape[1]
    return pl.pallas_call(
        functools.partial(_ada_kernel, tn=tn, ntiles=n // tn),
        grid=(n // tn,),
        in_specs=[pl.BlockSpec((rows, d), lambda j: (0, 0)),
                  pl.BlockSpec(memory_space=pl.ANY),
                  pl.BlockSpec((1, tn), lambda j: (0, j))],
        out_specs=pl.BlockSpec((rows, tn), lambda j: (0, j)),
        out_shape=jax.ShapeDtypeStruct((rows, n), F32),
        scratch_shapes=_prefetch_scratch(d, tn, F32),
        compiler_params=_params("arbitrary"),
        name="ada",
    )(c_all, w_ada, b_ada.reshape(1, n))


ROPE_SPAN = 64


def _rope_kernel(inv_ref, cs_ref, sn_ref, *, base, blk):
    r = pl.program_id(0)
    inv = inv_ref[...]
    lane = lax.broadcasted_iota(jnp.int32, (1, LANES), 1)
    sign = jnp.where((lane & (HEAD_DIM - 1)) < HEAD_DIM // 2, -1.0, 1.0)
    rows = lambda n, scale: (scale * lax.broadcasted_iota(jnp.int32, (n, LANES), 0)).astype(F32)
    if blk % ROPE_SPAN:
        ang = (rows(blk, 1) + (base + r * blk).astype(F32)) * inv
        cs_ref[...] = jnp.cos(ang)
        sn_ref[...] = jnp.sin(ang) * sign
        return
    ncoarse = blk // ROPE_SPAN
    coarse = (rows(max(ncoarse, SUBLANES), ROPE_SPAN) + (base + r * blk).astype(F32)) * inv
    fine = rows(ROPE_SPAN, 1) * inv
    ca, sa, cb, sb = jnp.cos(coarse), jnp.sin(coarse), jnp.cos(fine), jnp.sin(fine)
    for i in range(ncoarse):
        span = slice(i * ROPE_SPAN, (i + 1) * ROPE_SPAN)
        cs_ref[span, :] = ca[i:i + 1] * cb - sa[i:i + 1] * sb
        sn_ref[span, :] = (sa[i:i + 1] * cb + ca[i:i + 1] * sb) * sign


def _rope_tables(inv_row, base, rows, blk):
    out = jax.ShapeDtypeStruct((rows, LANES), F32)
    return pl.pallas_call(
        functools.partial(_rope_kernel, base=base, blk=blk),
        grid=(rows // blk,),
        in_specs=[pl.BlockSpec((1, LANES), lambda r: (0, 0))],
        out_specs=[pl.BlockSpec((blk, LANES), lambda r: (r, 0))] * 2,
        out_shape=[out, out],
        compiler_params=_params("arbitrary"),
        name="rope_tables",
    )(inv_row)


def _modulate_kernel(x_ref, shift_ref, scale_ref, g_ref, h_ref):
    group = 2 * SUBLANES
    per_row = shift_ref.shape[0] > 1

    def body(r, carry):
        rows = pl.ds(pl.multiple_of(r * group, group), group)
        shift = shift_ref[rows, :] if per_row else shift_ref[...]
        scale = scale_ref[rows, :] if per_row else scale_ref[...]
        h_ref[rows, :] = _modulate(x_ref[rows, :], g_ref[...], shift, scale).astype(BF16)
        return carry

    lax.fori_loop(0, x_ref.shape[0] // group, body, 0, unroll=8)


def _modulated_norm(x, mod, rows_per_mod, g, k_shift, tm):
    m_rows, d = x.shape
    return pl.pallas_call(
        _modulate_kernel,
        grid=(m_rows // tm,),
        in_specs=[pl.BlockSpec((tm, d), lambda m: (m, 0)),
                  _mod_spec(mod, tm, rows_per_mod, d, k_shift, 1),
                  _mod_spec(mod, tm, rows_per_mod, d, k_shift + 1, 1),
                  pl.BlockSpec((1, d), lambda m: (0, 0))],
        out_specs=pl.BlockSpec((tm, d), lambda m: (m, 0)),
        out_shape=jax.ShapeDtypeStruct((m_rows, d), BF16),
        compiler_params=_params("arbitrary"),
        name="modnorm",
    )(x, mod, mod, g.reshape(1, d))


HBM = pl.BlockSpec(memory_space=pl.ANY)


def _is_param(w):
    return w.dtype == F32


def _stream_weight_tiles(weights, tn, first):
    t, nt = pl.program_id(0), pl.num_programs(0)
    m, nm = pl.program_id(1), pl.num_programs(1)
    cols = lambda tile: pl.ds(pl.multiple_of(tile * tn, LANES), tn)
    fetch = lambda w_hbm, stage_ref, sem, tile: pltpu.make_async_copy(
        w_hbm.at[0, :, cols(first + tile)], stage_ref, sem.at[0])
    put = lambda wb_ref, wout_hbm, sem, tile: pltpu.make_async_copy(wb_ref, wout_hbm.at[:, cols(tile)], sem.at[1])

    @pl.when(m == 0)
    def _():
        for w_hbm, stage_ref, wb_ref, wout_hbm, sem in weights:
            @pl.when(t == 0)
            def _():
                fetch(w_hbm, stage_ref, sem, 0).start()

            fetch(w_hbm, stage_ref, sem, t).wait()

            @pl.when(t > 0)
            def _():
                put(wb_ref, wout_hbm, sem, t - 1).wait()

            wb_ref[...] = stage_ref[...].astype(BF16)
            put(wb_ref, wout_hbm, sem, t).start()

            @pl.when(t + 1 < nt)
            def _():
                fetch(w_hbm, stage_ref, sem, t + 1).start()

    @pl.when((t == nt - 1) & (m == nm - 1))
    def _():
        for w_hbm, stage_ref, wb_ref, wout_hbm, sem in weights:
            put(wb_ref, wout_hbm, sem, t).wait()


def _weight_scratch(rows, tn):
    return [pltpu.VMEM((rows, tn), F32), pltpu.VMEM((rows, tn), BF16), pltpu.SemaphoreType.DMA((2,))]


def _row_groups(tm):
    n = max(1, tm // 1024)
    return [slice(r * (tm // n), (r + 1) * (tm // n)) for r in range(n)]


def _ffn_up_rows(h_ref, w1b_ref, w3b_ref, g_ref):
    for rows in _row_groups(h_ref.shape[0]):
        h = h_ref[rows, :]
        a = jnp.dot(h, w1b_ref[...], preferred_element_type=F32)
        b = jnp.dot(h, w3b_ref[...], preferred_element_type=F32)
        g_ref[rows, :] = (_silu(a) * b).astype(BF16)


def _ffn_up_param_kernel(h_ref, w1_hbm, w3_hbm, g_ref, w1o_hbm, w3o_hbm,
                         s1_ref, w1b_ref, sem1, s3_ref, w3b_ref, sem3, *, tf):
    _stream_weight_tiles(((w1_hbm, s1_ref, w1b_ref, w1o_hbm, sem1), (w3_hbm, s3_ref, w3b_ref, w3o_hbm, sem3)),
                         tf, 0)
    _ffn_up_rows(h_ref, w1b_ref, w3b_ref, g_ref)


def _ffn_up(h, w1, w3, tm, tf):
    m_rows, d = h.shape
    dff = w1.shape[-1]
    grid = (dff // tf, m_rows // tm)
    h_spec = pl.BlockSpec((tm, d), lambda f, m: (m, 0))
    g_spec = pl.BlockSpec((tm, tf), lambda f, m: (m, f))
    g_shape = jax.ShapeDtypeStruct((m_rows, dff), BF16)
    if not _is_param(w1):
        assert grid[1] == 1

        def rounded_kernel(h_ref, w1_hbm, w3_hbm, g_ref, b1_ref, sem1, b3_ref, sem3):
            s1 = _prefetched_tile(w1_hbm, b1_ref, sem1, tf, grid[0])
            s3 = _prefetched_tile(w3_hbm, b3_ref, sem3, tf, grid[0])
            _ffn_up_rows(h_ref, b1_ref.at[s1], b3_ref.at[s3], g_ref)

        g = pl.pallas_call(
            rounded_kernel, grid=grid, in_specs=[h_spec, HBM, HBM], out_specs=g_spec, out_shape=g_shape,
            scratch_shapes=_prefetch_scratch(d, tf, BF16) + _prefetch_scratch(d, tf, BF16),
            compiler_params=_params("arbitrary", "arbitrary"), name="ffn_up_rows",
        )(h, w1, w3)
        return g, w1, w3
    w_shape = jax.ShapeDtypeStruct((d, dff), BF16)
    return pl.pallas_call(
        functools.partial(_ffn_up_param_kernel, tf=tf),
        grid=grid,
        in_specs=[h_spec, HBM, HBM],
        out_specs=[g_spec, HBM, HBM],
        out_shape=[g_shape, w_shape, w_shape],
        scratch_shapes=_weight_scratch(d, tf) + _weight_scratch(d, tf),
        compiler_params=_params("arbitrary", "arbitrary"),
        name="ffn_up",
    )(h, w1, w3)


def _ffn_down_rows(g_ref, w2b_ref, x_ref, gate_ref, o_ref):
    for rows in _row_groups(g_ref.shape[0]):
        acc = jnp.dot(g_ref[rows, :], w2b_ref[...], preferred_element_type=F32)
        gate = gate_ref[...] if gate_ref.shape[0] == 1 else gate_ref[rows, :]
        o_ref[rows, :] = x_ref[rows, :] + 0.5 * gate * acc


def _ffn_down_param_kernel(g_ref, w2_hbm, x_ref, gate_ref, o_ref, w2o_hbm, s2_ref, w2b_ref, sem2, *, tn):
    _stream_weight_tiles(((w2_hbm, s2_ref, w2b_ref, w2o_hbm, sem2),), tn, 0)
    _ffn_down_rows(g_ref, w2b_ref, x_ref, gate_ref, o_ref)


def _ffn_down(g, w2, x, mod, rows_per_mod, k_gate, tm, tn):
    m_rows, d = x.shape
    dff = g.shape[1]
    grid = (d // tn, m_rows // tm)
    gate_col = lambda n, m: k_gate * (d // tn) + n
    g_spec = pl.BlockSpec((tm, dff), lambda n, m: (m, 0))
    x_spec = pl.BlockSpec((tm, tn), lambda n, m: (m, n))
    gate_spec = _mod_spec(mod, tm, rows_per_mod, tn, gate_col, 2)
    o_shape = jax.ShapeDtypeStruct((m_rows, d), F32)
    if not _is_param(w2):
        assert grid[1] == 1

        def rounded_kernel(g_ref, w2_hbm, x_ref, gate_ref, o_ref, b2_ref, sem2):
            s2 = _prefetched_tile(w2_hbm, b2_ref, sem2, tn, grid[0])
            _ffn_down_rows(g_ref, b2_ref.at[s2], x_ref, gate_ref, o_ref)

        out = pl.pallas_call(
            rounded_kernel, grid=grid, in_specs=[g_spec, HBM, x_spec, gate_spec],
            out_specs=x_spec, out_shape=o_shape, scratch_shapes=_prefetch_scratch(dff, tn, BF16),
            compiler_params=_params("arbitrary", "arbitrary"), name="ffn_down_rows",
        )(g, w2, x, mod)
        return out, w2
    return pl.pallas_call(
        functools.partial(_ffn_down_param_kernel, tn=tn),
        grid=grid,
        in_specs=[g_spec, HBM, x_spec, gate_spec],
        out_specs=[x_spec, HBM],
        out_shape=[o_shape, jax.ShapeDtypeStruct((dff, d), BF16)],
        scratch_shapes=_weight_scratch(dff, tn),
        compiler_params=_params("arbitrary", "arbitrary"),
        name="ffn_down",
    )(g, w2, x, mod)


def _inproj_z_rows(h_ref, wb_ref, z_ref):
    for rows in _row_groups(h_ref.shape[0]):
        z_ref[rows, :] = jnp.dot(h_ref[rows, :], wb_ref[...], preferred_element_type=F32)


def _inproj_z_param_kernel(h_ref, w_hbm, z_ref, wo_hbm, s_ref, wb_ref, sem, *, tn, first):
    _stream_weight_tiles(((w_hbm, s_ref, wb_ref, wo_hbm, sem),), tn, first)
    _inproj_z_rows(h_ref, wb_ref, z_ref)


def _inproj_z(h, w, tm, tn):
    m_rows, d = h.shape
    grid = (D_Z // tn, m_rows // tm)
    h_spec = pl.BlockSpec((tm, d), lambda n, m: (m, 0))
    z_spec = pl.BlockSpec((tm, tn), lambda n, m: (m, n))
    z_shape = jax.ShapeDtypeStruct((m_rows, D_Z), F32)
    if not _is_param(w):
        assert grid[1] == 1

        def rounded_kernel(h_ref, w_hbm, z_ref, b_ref, sem):
            s = _prefetched_tile(w_hbm, b_ref, sem, tn, grid[0])
            _inproj_z_rows(h_ref, b_ref.at[s], z_ref)

        z = pl.pallas_call(
            rounded_kernel, grid=grid, in_specs=[h_spec, HBM], out_specs=z_spec, out_shape=z_shape,
            scratch_shapes=_prefetch_scratch(d, tn, BF16),
            compiler_params=_params("arbitrary", "arbitrary"), name="inproj_z_rows",
        )(h, w)
        return z, w
    return pl.pallas_call(
        functools.partial(_inproj_z_param_kernel, tn=tn, first=D_QKV // tn),
        grid=grid,
        in_specs=[h_spec, HBM],
        out_specs=[z_spec, HBM],
        out_shape=[z_shape, jax.ShapeDtypeStruct((d, D_Z), BF16)],
        scratch_shapes=_weight_scratch(d, tn),
        compiler_params=_params("arbitrary", "arbitrary"),
        name="inproj_z",
    )(h, w)


def _store_head_pairs(ref, rows, chunk, c):
    lo = lax.broadcasted_iota(jnp.int32, chunk.shape, 1) < HEAD_DIM
    swapped = pltpu.roll(chunk, HEAD_DIM, 1)
    zero = jnp.zeros_like(chunk)
    cols = (jnp.where(lo, chunk, zero), jnp.where(lo, zero, swapped),
            jnp.where(lo, swapped, zero), jnp.where(lo, zero, chunk))
    for i, col in enumerate(cols):
        ref[rows, (4 * c + i) * LANES:(4 * c + i + 1) * LANES] = col.astype(ref.dtype)


def _side_chunk_rows(side, steps):
    total = sum(w.shape[1] for w in side)
    rows = total // steps
    ok = rows * steps == total and rows % (2 * SUBLANES) == 0 and all(w.shape[1] % rows == 0 for w in side)
    return rows if ok else None


def _round_side_matrices(mats, stage_ref, ostage_ref, sem):
    m = pl.program_id(0)
    nrows = stage_ref.shape[1]
    chunks = [(w, wb, r0) for w, wb in mats for r0 in range(0, w.shape[1], nrows)]

    def fetch(c):
        w, _, r0 = chunks[c]
        return pltpu.make_async_copy(w.at[0, pl.ds(r0, nrows), :], stage_ref.at[c % 2], sem.at[c % 2])

    def put(c):
        _, wb, r0 = chunks[c]
        return pltpu.make_async_copy(ostage_ref.at[c % 2], wb.at[pl.ds(r0, nrows), :], sem.at[2 + c % 2])

    last = len(chunks) - 1
    for c in range(len(chunks)):
        @pl.when(m == c)
        def _():
            if c == 0:
                fetch(0).start()
            if c < last:
                fetch(c + 1).start()
            fetch(c).wait()
            if c >= 2:
                put(c - 2).wait()

    ostage_ref[m % 2] = stage_ref[m % 2].astype(BF16)

    def finish():
        for c in range(len(chunks)):
            @pl.when(m == c)
            def _():
                put(c).start()
                if c == last:
                    if c >= 1:
                        put(c - 1).wait()
                    put(c).wait()

    return finish


def _inproj_qkv_kernel(*refs, nsplit, nside):
    (x_ref, shift_ref, scale_ref, g_ref, w_ref, cs_ref, sn_ref, gqk_ref, red_ref, exp_ref) = refs[:10]
    side_in = refs[10:10 + nside]
    h_ref, q_ref, kvx_ref, kvf_ref = refs[10 + nside:14 + nside]
    side_out = refs[14 + nside:14 + 2 * nside]
    wb_ref = refs[14 + 2 * nside]

    @pl.when(pl.program_id(0) == 0)
    def _():
        wb_ref[...] = w_ref[...].astype(BF16)

    finish_side = _round_side_matrices(list(zip(side_in, side_out)), *refs[15 + 2 * nside:]) if nside else None

    step = x_ref.shape[0] // nsplit
    for r in range(nsplit):
        rows = slice(r * step, (r + 1) * step)
        shift = shift_ref[...] if shift_ref.shape[0] == 1 else shift_ref[rows, :]
        scale = scale_ref[...] if scale_ref.shape[0] == 1 else scale_ref[rows, :]
        h = _modulate(x_ref[rows, :], g_ref[...], shift, scale).astype(BF16)
        h_ref[rows, :] = h
        acc = jnp.dot(h, wb_ref[...], preferred_element_type=F32)
        qk = acc[:, :D_QK]
        hm = jnp.dot((qk * qk).astype(BF16), red_ref[...], preferred_element_type=F32)
        hi = hm.astype(BF16)
        lo = (hm - hi.astype(F32)).astype(BF16)
        ms = jnp.dot(jnp.concatenate([hi, lo], axis=1), exp_ref[...], preferred_element_type=F32)
        y = qk * lax.rsqrt(ms + EPS) * gqk_ref[...]
        cs = cs_ref[rows, :]
        sn = sn_ref[rows, :]
        lane = lax.broadcasted_iota(jnp.int32, cs.shape, 1)
        first_half = (lane & (HEAD_DIM - 1)) < HEAD_DIM // 2
        for c in range(D_QK // LANES):
            yc = y[:, c * LANES:(c + 1) * LANES]
            partner = jnp.where(first_half,
                                pltpu.roll(yc, LANES - HEAD_DIM // 2, 1),
                                pltpu.roll(yc, HEAD_DIM // 2, 1))
            rot = yc * cs + partner * sn
            if c < D_ATTN // LANES:
                q_ref[rows, c * LANES:(c + 1) * LANES] = (rot * Q_SCALE).astype(BF16)
            else:
                kvf_ref[rows, c * LANES - D_ATTN:(c + 1) * LANES - D_ATTN] = rot
                _store_head_pairs(kvx_ref, rows, rot, c - D_ATTN // LANES)
        v = acc[:, D_QK:]
        kvf_ref[rows, D_KV:] = v
        for c in range(D_KV // LANES):
            _store_head_pairs(kvx_ref, rows, v[:, c * LANES:(c + 1) * LANES], D_KV // LANES + c)
    if finish_side is not None:
        finish_side()


def _inproj_qkv(x, mod, rows_per_mod, g, k_shift, w_in, cs, sn, gqk, head_reduce, head_expand, tm, table_blocks,
                side=()):
    m_rows, d = x.shape
    steps = m_rows // tm
    row = lambda width: pl.BlockSpec((tm, width), lambda m: (m, 0))
    table = pl.BlockSpec((tm, LANES), lambda m: (m % table_blocks, 0))
    nsplit = 2 if tm % (2 * MXU_DIM) == 0 else 1
    side_scratch = []
    if side:
        width = side[0].shape[2]
        chunk = _side_chunk_rows(side, steps)
        assert chunk is not None and all(w.shape[2] == width for w in side)
        side_scratch = [pltpu.VMEM((2, chunk, width), F32), pltpu.VMEM((2, chunk, width), BF16),
                        pltpu.SemaphoreType.DMA((4,))]
    return pl.pallas_call(
        functools.partial(_inproj_qkv_kernel, nsplit=nsplit, nside=len(side)),
        grid=(steps,),
        in_specs=[row(d),
                  _mod_spec(mod, tm, rows_per_mod, d, k_shift, 1),
                  _mod_spec(mod, tm, rows_per_mod, d, k_shift + 1, 1),
                  pl.BlockSpec((1, d), lambda m: (0, 0)),
                  pl.BlockSpec((None, d, D_QKV), lambda m: (0, 0, 0), pipeline_mode=pl.Buffered(1)),
                  table, table,
                  pl.BlockSpec((1, D_QK), lambda m: (0, 0)),
                  pl.BlockSpec(head_reduce.shape, lambda m: (0, 0)),
                  pl.BlockSpec(head_expand.shape, lambda m: (0, 0))] + [HBM] * len(side),
        out_specs=[row(d), row(D_ATTN), row(2 * DX), row(2 * D_KV)] + [HBM] * len(side),
        out_shape=[jax.ShapeDtypeStruct((m_rows, d), BF16),
                   jax.ShapeDtypeStruct((m_rows, D_ATTN), BF16),
                   jax.ShapeDtypeStruct((m_rows, 2 * DX), BF16),
                   jax.ShapeDtypeStruct((m_rows, 2 * D_KV), F32)]
        + [jax.ShapeDtypeStruct(w.shape[1:], BF16) for w in side],
        scratch_shapes=[pltpu.VMEM((d, D_QKV), BF16)] + side_scratch,
        compiler_params=_params("arbitrary"),
        name="inproj_qkv",
    )(x, mod, mod, g.reshape(1, d), w_in, cs, sn, gqk, head_reduce, head_expand, *side)


def _attn_bias(bias_ref, tq, past_off):
    nk = 2 * WINDOW
    ri = lax.broadcasted_iota(jnp.int32, (2 * tq, 2 * nk), 0) & (tq - 1)
    kj = lax.broadcasted_iota(jnp.int32, (2 * tq, 2 * nk), 1) & (nk - 1)
    visible = ((kj < WINDOW) & (kj > ri + past_off)) | ((kj >= WINDOW) & (kj - WINDOW <= ri))
    bias_ref[...] = jnp.where(visible, 0.0, -jnp.inf)


def _attn_blocks(sink_ref, bias_of, q_ref, past_of, kvc_ref, nsub, tq, store):
    nk = 2 * WINDOW
    rows = 2 * tq
    first_rows = lax.broadcasted_iota(jnp.int32, (rows, 1), 0) < tq
    lo_lanes = lax.broadcasted_iota(jnp.int32, (rows, LANES), 1) < HEAD_DIM
    ones_lo = (lax.broadcasted_iota(jnp.int32, (nk, LANES), 1) < HEAD_DIM).astype(BF16)
    ones_hi = (1 - ones_lo.astype(F32)).astype(BF16)
    nt = (((1,), (1,)), ((), ()))

    def keys(which, s, col):
        cols = slice(which * DX + col * LANES, which * DX + (col + 1) * LANES)
        past_ref, first = past_of(s)
        return [past_ref[first:first + WINDOW, cols], kvc_ref[s * tq:(s + 1) * tq, cols]]

    for s in range(nsub):
        for g in range(N_KV_HEADS):
            c0, c1 = 2 * g, 2 * g + 1
            q4 = jnp.concatenate([q_ref[s * tq:(s + 1) * tq, c0 * LANES:(c0 + 1) * LANES],
                                  q_ref[s * tq:(s + 1) * tq, c1 * LANES:(c1 + 1) * LANES]], axis=0)
            kk = jnp.concatenate(keys(0, s, c0) + keys(0, s, c1), axis=0)
            sc = lax.dot_general(q4, kk, nt, preferred_element_type=F32) + bias_of(s)[...]
            sk_lo = jnp.where(first_rows, sink_ref[4 * g], sink_ref[4 * g + 2]) * LOG2E
            sk_hi = jnp.where(first_rows, sink_ref[4 * g + 1], sink_ref[4 * g + 3]) * LOG2E
            m_lo = jnp.maximum(jnp.max(sc[:, :nk], axis=-1, keepdims=True), sk_lo)
            m_hi = jnp.maximum(jnp.max(sc[:, nk:], axis=-1, keepdims=True), sk_hi)
            p = jnp.concatenate([jnp.exp2(sc[:, :nk] - m_lo), jnp.exp2(sc[:, nk:] - m_hi)], axis=1).astype(BF16)
            vv = jnp.concatenate(
                [jnp.concatenate([jnp.concatenate(keys(1, s, c0), axis=0), ones_lo], axis=1),
                 jnp.concatenate([jnp.concatenate(keys(1, s, c1), axis=0), ones_hi], axis=1)],
                axis=0)
            o = jnp.dot(p, vv, preferred_element_type=F32)
            denom = o[:, LANES:] + jnp.where(lo_lanes, jnp.exp2(sk_lo - m_lo), jnp.exp2(sk_hi - m_hi))
            store(s, c0, c1, o[:, :LANES] / denom)


def _attn_kernel(sink_ref, q_ref, kvp_ref, kvc_ref, o_ref, bias0_ref, bias_ref, *, nq):
    tq = WINDOW
    _attn_bias(bias0_ref, tq, jnp.where(pl.program_id(1) > 0, 0, WINDOW))
    if nq > 1:
        _attn_bias(bias_ref, tq, 0)

    def store(s, c0, c1, out):
        o_ref[s * tq:(s + 1) * tq, c0 * LANES:(c0 + 1) * LANES] = out[:tq].astype(BF16)
        o_ref[s * tq:(s + 1) * tq, c1 * LANES:(c1 + 1) * LANES] = out[tq:].astype(BF16)

    past_of = lambda s: (kvp_ref, 0) if s == 0 else (kvc_ref, (s - 1) * WINDOW)
    bias_of = lambda s: bias0_ref if s == 0 else bias_ref
    _attn_blocks(sink_ref, bias_of, q_ref, past_of, kvc_ref, nq, tq, store)


def _attn(sinks, q, kvx, nbatch, nblk, nq):
    steps = nblk // nq
    cur = lambda b, n: (b * steps + n, 0)
    past = lambda b, n: (b * nblk + jnp.maximum(n * nq - 1, 0), 0)
    bias = pltpu.VMEM((2 * WINDOW, 4 * WINDOW), F32)
    return pl.pallas_call(
        functools.partial(_attn_kernel, nq=nq),
        grid=(nbatch, steps),
        in_specs=[pl.BlockSpec(memory_space=pltpu.SMEM), pl.BlockSpec((nq * WINDOW, D_ATTN), cur),
                  pl.BlockSpec((WINDOW, 2 * DX), past), pl.BlockSpec((nq * WINDOW, 2 * DX), cur)],
        out_specs=pl.BlockSpec((nq * WINDOW, D_ATTN), cur),
        out_shape=jax.ShapeDtypeStruct(q.shape, BF16),
        scratch_shapes=[bias, bias],
        compiler_params=_params("arbitrary", "arbitrary"),
        name="attn",
    )(sinks, q, kvx, kvx)


def _rows_from_lanes(row, nrows):
    out_rows = max(nrows, SUBLANES)
    rep = jnp.broadcast_to(row, (out_rows, row.shape[1]))
    ridx = lax.broadcasted_iota(jnp.int32, (out_rows, HEAD_DIM), 0)
    out = jnp.zeros((out_rows, HEAD_DIM), row.dtype)
    for i in range(nrows):
        out = jnp.where(ridx == i, rep[:, i * HEAD_DIM:(i + 1) * HEAD_DIM], out)
    return out


def _attn_step_kernel(sink_ref, q_ref, kvn_ref, ck_ref, cv_ref, o_ref, *, nsub):
    grp = lax.broadcasted_iota(jnp.int32, (N_Q_HEADS, HEAD_DIM), 0) >> 2
    bias = jnp.where(lax.broadcasted_iota(jnp.int32, (N_Q_HEADS, WINDOW), 1) >= 1, 0.0, -jnp.inf)
    hcol = lax.broadcasted_iota(jnp.int32, (N_Q_HEADS, 1), 0)
    sk = jnp.zeros((N_Q_HEADS, 1), F32)
    for h in range(N_Q_HEADS):
        sk = jnp.where(hcol == h, sink_ref[h] * LOG2E, sk)
    nt = (((1,), (1,)), ((), ()))
    qf = q_ref[...].astype(F32)

    def own_group(x):
        out = jnp.zeros((N_Q_HEADS, HEAD_DIM), F32)
        for g in range(N_KV_HEADS):
            out = jnp.where(grp == g, x[:, g * HEAD_DIM:(g + 1) * HEAD_DIM], out)
        return out

    for s in range(nsub):
        qh = _rows_from_lanes(qf[s:s + 1, :], N_Q_HEADS)
        zero = jnp.zeros_like(qh)
        q_bd = jnp.concatenate([jnp.where(grp == g, qh, zero) for g in range(N_KV_HEADS)], axis=1)
        kt = ck_ref[s * D_KV:(s + 1) * D_KV, :].astype(BF16)
        vt = cv_ref[s * D_KV:(s + 1) * D_KV, :].astype(BF16)
        sc = jnp.dot(q_bd.astype(BF16), kt, preferred_element_type=F32) + bias
        s_new = jnp.sum(q_bd * kvn_ref[s:s + 1, :D_KV], axis=-1, keepdims=True)
        m = jnp.maximum(jnp.maximum(jnp.max(sc, axis=-1, keepdims=True), s_new), sk)
        p = jnp.exp2(sc - m)
        p_new = jnp.exp2(s_new - m)
        denom = jnp.sum(p, axis=-1, keepdims=True) + p_new + jnp.exp2(sk - m)
        o_all = lax.dot_general(p.astype(BF16), vt, nt, preferred_element_type=F32)
        v_new = own_group(jnp.broadcast_to(kvn_ref[s:s + 1, D_KV:], (N_Q_HEADS, D_KV)))
        o = (own_group(o_all) + p_new * v_new) / denom
        for c in range(N_Q_HEADS // 2):
            o_ref[s:s + 1, c * LANES:(c + 1) * LANES] = jnp.concatenate(
                [o[2 * c:2 * c + 1, :], o[2 * c + 1:2 * c + 2, :]], axis=1)


def _attn_step(sinks, q, kv_new, cache_k, cache_v, nsub):
    ns = q.shape[0]
    row = lambda width: pl.BlockSpec((nsub, width), lambda b: (b, 0))
    cache = pl.BlockSpec((nsub * D_KV, WINDOW), lambda b: (b, 0))
    return pl.pallas_call(
        functools.partial(_attn_step_kernel, nsub=nsub),
        grid=(ns // nsub,),
        in_specs=[pl.BlockSpec(memory_space=pltpu.SMEM), row(D_ATTN), row(2 * D_KV), cache, cache],
        out_specs=row(D_ATTN),
        out_shape=jax.ShapeDtypeStruct((ns, D_ATTN), F32),
        compiler_params=_params("arbitrary"),
        name="attn_step",
    )(sinks, q, kv_new, cache_k, cache_v)


def _softplus(x):
    return jnp.maximum(x, 0.0) + jnp.log1p(jnp.exp(-jnp.abs(x)))


def _lru_coeffs(xc, wrg_ref, wig_ref, brg, big, lam):
    xb = xc.astype(BF16)
    ngroups = D_RNN // MXU_DIM
    rs, igs = [], []
    for c in range(ngroups):
        xg = xb[:, c * MXU_DIM:(c + 1) * MXU_DIM]
        rs.append(jnp.dot(xg, wrg_ref[c], preferred_element_type=F32))
        igs.append(jnp.dot(xg, wig_ref[c], preferred_element_type=F32))
    r = jax.nn.sigmoid(jnp.concatenate(rs, axis=1) + brg)
    ig = jax.nn.sigmoid(jnp.concatenate(igs, axis=1) + big)
    log_a = -LRU_C * r * _softplus(-lam)
    a = jnp.exp(log_a)
    one_minus_a2 = -jnp.tanh(log_a) * (1.0 + a * a)
    root = jnp.where(one_minus_a2 > 0.0, one_minus_a2 * lax.rsqrt(one_minus_a2), 0.0)
    u = root * (ig * xc)
    return a, u


def _rnn_prompt_kernel(rx_ref, rg_ref, cw_ref, cb_ref, wrg_ref, wig_ref, brg_ref, big_ref, lam_ref,
                       o_ref, hlast_ref, conv_ref, xs_ref, a_ref, h_ref, carry_ref, *, tc):
    t = pl.program_id(1)
    pad = SUBLANES

    @pl.when(t == 0)
    def _():
        xs_ref[0:pad, :] = jnp.zeros((pad, D_RNN), F32)
        carry_ref[...] = jnp.zeros_like(carry_ref)

    x = rx_ref[...]
    xs_ref[pad:pad + tc, :] = x
    xc = cb_ref[...] + x * cw_ref[CONV_WIDTH - 1:CONV_WIDTH, :]
    for k in range(1, CONV_WIDTH):
        xc = xc + xs_ref[pad - k:pad - k + tc, :] * cw_ref[CONV_WIDTH - 1 - k:CONV_WIDTH - k, :]
    tail = xs_ref[tc:tc + pad, :]
    xs_ref[0:pad, :] = tail
    conv_ref[...] = tail[pad - (CONV_WIDTH - 1):, :]

    a, u = _lru_coeffs(xc, wrg_ref, wig_ref, brg_ref[...], big_ref[...], lam_ref[...])
    a_ref[...] = a
    h_ref[...] = u

    row = lax.broadcasted_iota(jnp.int32, (SUBLANES, D_RNN), 0)

    def body(r, carry):
        off = pl.multiple_of(r * SUBLANES, SUBLANES)
        av = a_ref[pl.ds(off, SUBLANES), :]
        hv = h_ref[pl.ds(off, SUBLANES), :]
        hv = hv + jnp.where(row == 0, av * carry, 0.0)
        for sft in (1, 2, 4):
            keep = row >= sft
            h_sh = jnp.where(keep, pltpu.roll(hv, sft, 0), 0.0)
            hv = av * h_sh + hv
            if sft < SUBLANES // 2:
                av = av * jnp.where(keep, pltpu.roll(av, sft, 0), 1.0)
        h_ref[pl.ds(off, SUBLANES), :] = hv
        return jnp.broadcast_to(hv[SUBLANES - 1:SUBLANES, :], (SUBLANES, D_RNN))

    carry = lax.fori_loop(0, tc // SUBLANES, body, carry_ref[...])
    carry_ref[...] = carry
    hlast_ref[...] = carry[0:1, :]
    o_ref[...] = (h_ref[...] * jax.nn.gelu(rg_ref[...])).astype(BF16)


def _rnn_prompt(z, nbatch, seq, conv_w, conv_b, wrg, wig, b_rg, b_ig, lam, tc):
    nchunk = seq // tc
    rx_blk = Z_RX // D_RNN
    rg_blk = Z_RG // D_RNN
    vec = pl.BlockSpec((1, D_RNN), lambda b, t: (0, 0))
    wspec = pl.BlockSpec(wrg.shape, lambda b, t: (0, 0, 0))
    return pl.pallas_call(
        functools.partial(_rnn_prompt_kernel, tc=tc),
        grid=(nbatch, nchunk),
        in_specs=[pl.BlockSpec((tc, D_RNN), lambda b, t: (b * nchunk + t, rx_blk)),
                  pl.BlockSpec((tc, D_RNN), lambda b, t: (b * nchunk + t, rg_blk)),
                  pl.BlockSpec((CONV_WIDTH, D_RNN), lambda b, t: (0, 0)),
                  vec, wspec, wspec, vec, vec, vec],
        out_specs=[pl.BlockSpec((tc, D_RNN), lambda b, t: (b * nchunk + t, 0)),
                   pl.BlockSpec((None, 1, D_RNN), lambda b, t: (b, 0, 0)),
                   pl.BlockSpec((None, CONV_WIDTH - 1, D_RNN), lambda b, t: (b, 0, 0))],
        out_shape=[jax.ShapeDtypeStruct((nbatch * seq, D_RNN), BF16),
                   jax.ShapeDtypeStruct((nbatch, 1, D_RNN), F32),
                   jax.ShapeDtypeStruct((nbatch, CONV_WIDTH - 1, D_RNN), F32)],
        scratch_shapes=[pltpu.VMEM((tc + SUBLANES, D_RNN), F32),
                        pltpu.VMEM((tc, D_RNN), F32),
                        pltpu.VMEM((tc, D_RNN), F32),
                        pltpu.VMEM((SUBLANES, D_RNN), F32)],
        compiler_params=_params("arbitrary", "arbitrary"),
        name="rnn_prompt",
    )(z, z, conv_w, conv_b.reshape(1, D_RNN), wrg, wig, b_rg.reshape(1, D_RNN),
      b_ig.reshape(1, D_RNN), lam.reshape(1, D_RNN))


def _rnn_step_kernel(rx_ref, rg_ref, c0_ref, c1_ref, c2_ref, h0_ref, cw_ref, cb_ref, wrg_ref, wig_ref,
                     brg_ref, big_ref, lam_ref, o_ref, h_ref):
    x = rx_ref[...]
    xc = (cb_ref[...] + c0_ref[...] * cw_ref[0:1, :] + c1_ref[...] * cw_ref[1:2, :]
          + c2_ref[...] * cw_ref[2:3, :] + x * cw_ref[3:4, :])
    a, u = _lru_coeffs(xc, wrg_ref, wig_ref, brg_ref[...], big_ref[...], lam_ref[...])
    h = a * h0_ref[...] + u
    h_ref[...] = h
    o_ref[...] = (h * jax.nn.gelu(rg_ref[...])).astype(BF16)


def _rnn_step(z, conv_state, h0, conv_w, conv_b, wrg, wig, b_rg, b_ig, lam):
    rows = z.shape[0]
    full = lambda shape: pl.BlockSpec(shape, lambda i: (0,) * len(shape))
    act = full((rows, D_RNN))
    vec = full((1, D_RNN))
    return pl.pallas_call(
        _rnn_step_kernel,
        grid=(1,),
        in_specs=[pl.BlockSpec((rows, D_RNN), lambda i: (0, Z_RX // D_RNN)),
                  pl.BlockSpec((rows, D_RNN), lambda i: (0, Z_RG // D_RNN)),
                  act, act, act, act, full((CONV_WIDTH, D_RNN)), vec,
                  full(wrg.shape), full(wig.shape), vec, vec, vec],
        out_specs=[act, act],
        out_shape=[jax.ShapeDtypeStruct((rows, D_RNN), BF16),
                   jax.ShapeDtypeStruct((rows, D_RNN), F32)],
        compiler_params=_params("arbitrary"),
        name="rnn_step",
    )(z, z, conv_state[:, 0], conv_state[:, 1], conv_state[:, 2], h0, conv_w,
      conv_b.reshape(1, D_RNN), wrg, wig, b_rg.reshape(1, D_RNN), b_ig.reshape(1, D_RNN),
      lam.reshape(1, D_RNN))


def _mix_kernel(oa_ref, or_ref, ga_ref, gr_ref, x_ref, gate_ref, shift_ref, scale_ref, g_ref,
                wpa_ref, wpr_ref, wout_ref, o_ref, h_ref):
    pa = jnp.dot(oa_ref[...].astype(BF16), wpa_ref[...], preferred_element_type=F32)
    pr = jnp.dot(or_ref[...], wpr_ref[...], preferred_element_type=F32)
    mix = jax.nn.sigmoid(ga_ref[...]) * pa + jax.nn.sigmoid(gr_ref[...]) * pr
    x = x_ref[...] + gate_ref[...] * jnp.dot(mix.astype(BF16), wout_ref[...], preferred_element_type=F32)
    o_ref[...] = x
    h_ref[...] = _modulate(x, g_ref[...], shift_ref[...], scale_ref[...]).astype(BF16)


def _mix(o_att, o_rnn, z, x, mod, rows_per_mod, g_next, w_pa, w_pr, w_out, k_gate, k_shift_next, tm):
    m_rows, d = x.shape
    const = lambda shape: pl.BlockSpec(shape, lambda m: (0, 0), pipeline_mode=pl.Buffered(1))
    row = lambda width, col: pl.BlockSpec((tm, width), lambda m: (m, col))
    mspec = lambda k: _mod_spec(mod, tm, rows_per_mod, d, k, 1)
    return pl.pallas_call(
        _mix_kernel,
        grid=(m_rows // tm,),
        in_specs=[row(D_ATTN, 0), row(D_RNN, 0), row(d, Z_GATT // d), row(d, Z_GRNN // d), row(d, 0),
                  mspec(k_gate), mspec(k_shift_next), mspec(k_shift_next + 1), const((1, d)),
                  const(w_pa.shape), const(w_pr.shape), const(w_out.shape)],
        out_specs=[row(d, 0), row(d, 0)],
        out_shape=[jax.ShapeDtypeStruct((m_rows, d), F32), jax.ShapeDtypeStruct((m_rows, d), BF16)],
        compiler_params=_params("arbitrary"),
        name="mix",
    )(o_att, o_rnn, z, z, x, mod, mod, mod, g_next.reshape(1, d), w_pa, w_pr, w_out)


def _block_diag(w, group):
    n, r, _ = w.shape
    eye = jnp.eye(group, dtype=w.dtype)
    wg = w.reshape(n // group, group, r, r)
    return jnp.einsum("ngij,gh->ngihj", wg, eye).reshape(n // group, group * r, group * r)


def _tiles(rows, dff):
    pick = lambda n, prefs: next((t for t in prefs if n % t == 0), n)
    return dict(
        tm_norm=pick(rows, (1024, 512)),
        tm_up=pick(rows, (2048, 1024, 512)), tf=pick(dff, (512,)),
        tm_down=pick(rows, (1024, 512)), tn_down=512,
        tm_z=pick(rows, (1024, 512)), tn_z=D_QKV,
        tm_qkv=pick(rows, (512,)),
        tm_mix=pick(rows, (256,)),
    )


def kernel(x_prompt, x_sample, c_prompt, c_sample, cache_k, cache_v, state_h, state_conv, w_ada, b_ada,
           g_norm_ffn1, g_norm_mix, g_norm_ffn2, ffn1_w1, ffn1_w3, ffn1_w2, ffn2_w1, ffn2_w3, ffn2_w2,
           w_in, g_q, g_k, sinks, conv_w, conv_b, w_rg, b_rg, w_ig, b_ig, lru_lambda, w_pa, w_pr, w_out):
    nb, seq, d = x_prompt.shape
    ns = x_sample.shape[0]
    dff = ffn1_w1.shape[2]
    assert d == D_MODEL and w_ada.shape[0] == 1 and x_sample.shape[1] == 1 and cache_k.shape[2] == WINDOW
    assert seq % WINDOW == 0 and w_in.shape[2] == D_QKV + D_Z

    group = MXU_DIM // RNN_BLOCK
    wrg = _block_diag(w_rg[0], group).astype(BF16)
    wig = _block_diag(w_ig[0], group).astype(BF16)
    gqk = jnp.concatenate([jnp.tile(g_q[0], N_Q_HEADS), jnp.tile(g_k[0], N_KV_HEADS)]).reshape(1, D_QK)
    head_of = jnp.arange(D_QK) // HEAD_DIM
    onehot = (head_of[:, None] == jnp.arange(LANES)[None, :]).astype(F32)
    head_reduce = (onehot / HEAD_DIM).astype(BF16)
    head_expand = jnp.concatenate([onehot.T, onehot.T], axis=0).astype(BF16)
    inv = ROPE_THETA ** (-jnp.arange(HEAD_DIM // 2, dtype=F32) * 2.0 / HEAD_DIM)
    inv_row = jnp.tile(inv, LANES // (HEAD_DIM // 2)).reshape(1, LANES)
    sink_vec = sinks[0]
    lam = lru_lambda[0]

    pad_rows = (-(nb + ns)) % SUBLANES
    c_all = jnp.concatenate([c_prompt, c_sample, jnp.zeros((pad_rows, d), F32)], axis=0)
    mod_all = _ada(c_all, w_ada[0], b_ada[0], 1024)
    mod_p = mod_all[:nb].reshape(nb, 1, N_MOD * d)
    mod_s = mod_all[nb:nb + ns]

    wts = dict(f1w1=ffn1_w1, f1w3=ffn1_w3, f1w2=ffn1_w2, f2w1=ffn2_w1, f2w3=ffn2_w3, f2w2=ffn2_w2, wz=w_in)

    def trunk(x, mod, rows_per_mod, cs, sn, table_blocks, t):
        h1 = _modulated_norm(x, mod, rows_per_mod, g_norm_ffn1[0], 0, t["tm_norm"])
        g1, wts["f1w1"], wts["f1w3"] = _ffn_up(h1, wts["f1w1"], wts["f1w3"], t["tm_up"], t["tf"])
        x1, wts["f1w2"] = _ffn_down(g1, wts["f1w2"], x, mod, rows_per_mod, 2, t["tm_down"], t["tn_down"])
        side = ()
        if "mix" not in wts:
            side = (w_pa, w_pr, w_out)
            if _side_chunk_rows(side, x.shape[0] // t["tm_qkv"]) is None:
                side, wts["mix"] = (), tuple(w[0].astype(BF16) for w in side)
        hm, q, kvx, kvf, *rounded = _inproj_qkv(x1, mod, rows_per_mod, g_norm_mix[0], 3, w_in, cs, sn, gqk,
                                                      head_reduce, head_expand, t["tm_qkv"], table_blocks, side)
        if side:
            wts["mix"] = tuple(rounded)
        z, wts["wz"] = _inproj_z(hm, wts["wz"], t["tm_z"], t["tn_z"])
        return x1, z, q, kvx, kvf

    def tail(o_att, o_rnn, z, x1, mod, rows_per_mod, t):
        x2, h2 = _mix(o_att, o_rnn, z, x1, mod, rows_per_mod, g_norm_ffn2[0], *wts["mix"], 5, 6, t["tm_mix"])
        g2, wts["f2w1"], wts["f2w3"] = _ffn_up(h2, wts["f2w1"], wts["f2w3"], t["tm_up"], t["tf"])
        y, wts["f2w2"] = _ffn_down(g2, wts["f2w2"], x2, mod, rows_per_mod, 8, t["tm_down"], t["tn_down"])
        return y

    tp = _tiles(nb * seq, dff)
    tp = {k: (min(v, seq) if k.startswith("tm") else v) for k, v in tp.items()}
    cs_p, sn_p = _rope_tables(inv_row, 0, seq, seq)
    x1, z, q, kvx, kvf = trunk(x_prompt.reshape(nb * seq, d), mod_p, seq, cs_p, sn_p,
                                     seq // tp["tm_qkv"], tp)
    nblk = seq // WINDOW
    o_att = _attn(sink_vec, q, kvx, nb, nblk, next(n for n in (8, 4, 2, 1) if nblk % n == 0))
    o_rnn, h_p, conv_p = _rnn_prompt(z, nb, seq, conv_w[0], conv_b[0], wrg, wig, b_rg[0], b_ig[0], lam,
                                     min(1024, seq))
    y_p = tail(o_att, o_rnn, z, x1, mod_p, seq, tp)

    kv_last = kvf.reshape(nb, seq, 2 * D_KV)[:, seq - WINDOW:]
    heads = lambda a: a.reshape(1, nb, WINDOW, N_KV_HEADS, HEAD_DIM)
    k_prompt, v_prompt = heads(kv_last[..., :D_KV]), heads(kv_last[..., D_KV:])

    ts = _tiles(ns, dff)
    cs_1, sn_1 = _rope_tables(inv_row, PAST_LEN, SUBLANES, SUBLANES)
    cs_s = jnp.broadcast_to(cs_1[0:1], (ns, LANES))
    sn_s = jnp.broadcast_to(sn_1[0:1], (ns, LANES))
    x1s, zs, qs, _, kvfs = trunk(x_sample.reshape(ns, d), mod_s, 1, cs_s, sn_s, 1, ts)
    nsub = 2 * SUBLANES
    assert ns % nsub == 0
    key_minor = lambda c: jnp.transpose(c[0], (0, 2, 3, 1)).reshape(ns * D_KV, WINDOW)
    o_att_s = _attn_step(sink_vec, qs, kvfs, key_minor(cache_k), key_minor(cache_v), nsub)
    conv_s_in = state_conv[0]
    o_rnn_s, h_s = _rnn_step(zs, conv_s_in, state_h[0], conv_w[0], conv_b[0], wrg, wig, b_rg[0], b_ig[0], lam)
    y_s = tail(o_att_s, o_rnn_s, zs, x1s, mod_s, 1, ts)

    k_sample = kvfs[:, :D_KV].reshape(1, ns, 1, N_KV_HEADS, HEAD_DIM)
    v_sample = kvfs[:, D_KV:].reshape(1, ns, 1, N_KV_HEADS, HEAD_DIM)
    conv_sample = jnp.concatenate([conv_s_in[:, 1:], zs[:, None, Z_RX:Z_RX + D_RNN]], axis=1)[None]

    return (y_p.reshape(nb, seq, d), y_s.reshape(ns, 1, d), k_prompt, v_prompt, k_sample, v_sample,
            h_p.reshape(1, nb, D_RNN), h_s[None], conv_p[None], conv_sample)
```
